```python
import math
import jax, jax.numpy as jnp
from jax import lax
import numpy as np


D_MODEL = 1024
BATCH = 4
SEQ = 8192
DEPTH = 1

MEM_LEN = 256
D_MIX = 2 * D_MODEL
CONV_CH = D_MIX // 2
CONV_GROUPS = 16
CONV_K = 31
SSD_INNER = D_MIX // 2
SSD_HEAD_DIM = 64
SSD_HEADS = SSD_INNER // SSD_HEAD_DIM
SSD_STATE = 128
SSD_GROUPS = 2
SSD_HEADS_PER_GROUP = SSD_HEADS // SSD_GROUPS
SSD_CONV_K = 4
SSD_CHUNK = 128
SSD_BC = SSD_GROUPS * SSD_STATE
SSD_CONV_CH = SSD_INNER + 2 * SSD_BC
IN_SPLIT_POINTS = (CONV_CH, 2 * CONV_CH, 2 * CONV_CH + SSD_INNER, 2 * CONV_CH + 2 * SSD_INNER,
                   2 * CONV_CH + 2 * SSD_INNER + SSD_BC, 2 * CONV_CH + 2 * SSD_INNER + 2 * SSD_BC)
D_IN_PROJ = 2 * CONV_CH + 2 * SSD_INNER + 2 * SSD_BC + SSD_HEADS
XA_HEADS = 4
XA_HEAD_DIM = D_MODEL // XA_HEADS
N_EXPERT_GROUPS = 4
EXPERTS_PER_GROUP = 8
N_EXPERTS = N_EXPERT_GROUPS * EXPERTS_PER_GROUP
TOP_K = 2
D_EXPERT = 512
MOE_BLOCK = 256
RMS_EPS = 1e-6
LN_EPS = 1e-5
DT_MIN = 1e-3
DT_MAX = 1e-1

kernel_name = 'hymba_conformer_ssd_xattn_hiermoe'


def rms_norm(x, g):
    x32 = x.astype(jnp.float32)
    y = x32 * lax.rsqrt(jnp.mean(x32 * x32, axis=-1, keepdims=True) + RMS_EPS)
    return (y * g.astype(jnp.float32)).astype(x.dtype)


def layer_norm(x, g, b):
    x32 = x.astype(jnp.float32)
    mu = jnp.mean(x32, axis=-1, keepdims=True)
    var = jnp.mean(jnp.square(x32 - mu), axis=-1, keepdims=True)
    y = (x32 - mu) * lax.rsqrt(var + LN_EPS)
    return (y * g.astype(jnp.float32) + b.astype(jnp.float32)).astype(x.dtype)


def causal_depthwise_conv(x, w, b):
    k = w.shape[0]
    y = lax.conv_general_dilated(x, w[:, None, :].astype(x.dtype), window_strides=(1,),
                                 padding=[(k - 1, 0)],
                                 dimension_numbers=('NWC', 'WIO', 'NWC'),
                                 feature_group_count=x.shape[-1])
    return y + b.astype(x.dtype)


def ssd_chunk_scan(x, dt, a, bm, cm, d_skip):
    bsz, seqlen = x.shape[0], x.shape[1]
    n_chunks = seqlen // SSD_CHUNK

    def to_chunks(t):
        return jnp.moveaxis(t.reshape((bsz, n_chunks, SSD_CHUNK) + t.shape[2:]), 1, 0)

    causal = jnp.tril(jnp.ones((SSD_CHUNK, SSD_CHUNK), dtype=bool))

    def step(state, inp):
        xc, dtc, bc, cc = inp
        a_cs = jnp.cumsum(dtc * a, axis=1)
        a_t = a_cs.transpose(0, 2, 1)
        seg = a_t[:, :, :, None] - a_t[:, :, None, :]
        decay_mat = jnp.exp(jnp.where(causal, seg, -jnp.inf))
        cb = jnp.repeat(jnp.einsum('btgn,bsgn->bgts', cc, bc), SSD_HEADS_PER_GROUP, axis=1)
        m = cb * decay_mat * dtc.transpose(0, 2, 1)[:, :, None, :]
        y_diag = jnp.einsum('bhts,bshp->bthp', m, xc)
        ch = jnp.repeat(cc, SSD_HEADS_PER_GROUP, axis=2)
        bh = jnp.repeat(bc, SSD_HEADS_PER_GROUP, axis=2)
        y_off = jnp.einsum('bthn,bhpn->bthp', ch, state) * jnp.exp(a_cs)[..., None]
        w_end = jnp.exp(a_cs[:, -1:, :] - a_cs) * dtc
        new_state = (state * jnp.exp(a_cs[:, -1, :])[:, :, None, None]
                     + jnp.einsum('bshn,bsh,bshp->bhpn', bh, w_end, xc))
        return new_state, y_diag + y_off + xc * d_skip[:, None]

    state0 = jnp.zeros((bsz, SSD_HEADS, SSD_HEAD_DIM, SSD_STATE), jnp.float32)
    _, ys = lax.scan(step, state0, (to_chunks(x), to_chunks(dt), to_chunks(bm), to_chunks(cm)))
    return jnp.moveaxis(ys, 0, 1).reshape(bsz, seqlen, SSD_HEADS, SSD_HEAD_DIM)


def parallel_mixer(h, w_in, conv_w, conv_b, ln_g, ln_b, ssd_conv_w, ssd_conv_b,
                   dt_bias, a_log, d_skip, ssd_norm_g, w_out):
    bsz, seqlen, _ = h.shape
    proj = h @ w_in
    c_val, c_gate, z, xs, bs, cs, dt_raw = jnp.split(proj, IN_SPLIT_POINTS, axis=-1)
    u = c_val * jax.nn.sigmoid(c_gate)
    u = causal_depthwise_conv(u, conv_w, conv_b)
    u = jax.nn.silu(layer_norm(u, ln_g, ln_b))
    xbc = jnp.concatenate([xs, bs, cs], axis=-1)
    xbc = jax.nn.silu(causal_depthwise_conv(xbc, ssd_conv_w, ssd_conv_b))
    xs, bs, cs = jnp.split(xbc, (SSD_INNER, SSD_INNER + SSD_BC), axis=-1)
    dt = jax.nn.softplus(dt_raw.astype(jnp.float32) + dt_bias.astype(jnp.float32))
    a = -jnp.exp(a_log.astype(jnp.float32))
    y = ssd_chunk_scan(xs.reshape(bsz, seqlen, SSD_HEADS, SSD_HEAD_DIM).astype(jnp.float32), dt, a,
                       bs.reshape(bsz, seqlen, SSD_GROUPS, SSD_STATE).astype(jnp.float32),
                       cs.reshape(bsz, seqlen, SSD_GROUPS, SSD_STATE).astype(jnp.float32),
                       d_skip.astype(jnp.float32))
    y = y.reshape(bsz, seqlen, SSD_INNER) * jax.nn.silu(z.astype(jnp.float32))
    yg = y.reshape(bsz, seqlen, SSD_GROUPS, SSD_INNER // SSD_GROUPS)
    yg = yg * lax.rsqrt(jnp.mean(yg * yg, axis=-1, keepdims=True) + RMS_EPS)
    y = (yg.reshape(bsz, seqlen, SSD_INNER) * ssd_norm_g.astype(jnp.float32)).astype(h.dtype)
    return jnp.concatenate([u, y], axis=-1) @ w_out


def memory_cross_attention(h, m, w_q, w_k, w_v, w_o):
    bsz, seqlen, _ = h.shape
    q = (h @ w_q).reshape(bsz, seqlen, XA_HEADS, XA_HEAD_DIM)
    k = (m @ w_k).reshape(bsz, MEM_LEN, XA_HEADS, XA_HEAD_DIM)
    v = (m @ w_v).reshape(bsz, MEM_LEN, XA_HEADS, XA_HEAD_DIM)
    s = jnp.einsum('bthd,bshd->bhts', q, k).astype(jnp.float32) * (XA_HEAD_DIM ** -0.5)
    p = jax.nn.softmax(s, axis=-1).astype(h.dtype)
    o = jnp.einsum('bhts,bshd->bthd', p, v).reshape(bsz, seqlen, D_MODEL)
    return o @ w_o


def hierarchical_moe(h, w_rg, b_rg, w_re, b_re, w_gate, w_up, w_down):
    bsz, seqlen, d = h.shape
    n_tok = bsz * seqlen
    hf = h.reshape(n_tok, d)
    p_group = jax.nn.softmax((hf @ w_rg + b_rg).astype(jnp.float32), axis=-1)
    p_top, g_sel = lax.top_k(p_group, 1)
    e_logits = (hf @ w_re + b_re).astype(jnp.float32).reshape(n_tok, N_EXPERT_GROUPS, EXPERTS_PER_GROUP)
    e_logits = jnp.take_along_axis(e_logits, g_sel[:, :, None], axis=1)[:, 0]
    w_top, i_top = lax.top_k(jax.nn.softmax(e_logits, axis=-1), TOP_K)
    w_top = w_top / jnp.sum(w_top, axis=-1, keepdims=True) * p_top
    e_flat = (g_sel * EXPERTS_PER_GROUP + i_top).reshape(-1).astype(jnp.int32)
    w_flat = w_top.reshape(-1)
    tok_flat = jnp.repeat(jnp.arange(n_tok, dtype=jnp.int32), TOP_K)
    n_assign = n_tok * TOP_K
    n_slots = n_assign + N_EXPERTS * MOE_BLOCK
    n_blocks = n_slots // MOE_BLOCK
    order = jnp.argsort(e_flat)
    sorted_e, sorted_tok, sorted_w = e_flat[order], tok_flat[order], w_flat[order]
    counts = jnp.bincount(e_flat, length=N_EXPERTS).astype(jnp.int32)
    cnt_start = jnp.cumsum(counts) - counts
    padded = ((counts + MOE_BLOCK - 1) // MOE_BLOCK) * MOE_BLOCK
    pad_end = jnp.cumsum(padded)
    pad_start = pad_end - padded
    dest = pad_start[sorted_e] + (jnp.arange(n_assign, dtype=jnp.int32) - cnt_start[sorted_e])
    buf_tok = jnp.zeros((n_slots,), jnp.int32).at[dest].set(sorted_tok)
    buf_w = jnp.zeros((n_slots,), h.dtype).at[dest].set(sorted_w.astype(h.dtype))
    block_e = jnp.clip(jnp.searchsorted(pad_end, jnp.arange(n_blocks, dtype=jnp.int32) * MOE_BLOCK,
                                        side='right'), 0, N_EXPERTS - 1).astype(jnp.int32)
    x_blocks = hf[buf_tok].reshape(n_blocks, MOE_BLOCK, d)

    def expert_block(args):
        xb, e = args
        return (jax.nn.silu(xb @ w_gate[e]) * (xb @ w_up[e])) @ w_down[e]

    y_blocks = lax.map(expert_block, (x_blocks, block_e)).reshape(n_slots, d)
    out = jnp.zeros((n_tok, d), h.dtype).at[buf_tok].add(y_blocks * buf_w[:, None])
    return out.reshape(bsz, seqlen, d)


def setup_inputs(seed: int = 0) -> dict:
    key = jax.random.key(seed)
    ks = jax.random.split(key, 32)
    f32 = jnp.float32

    def nrm(k, shape, scale):
        return jax.random.normal(k, shape, f32) * scale

    def gain(k, shape):
        return 1.0 + 0.02 * jax.random.normal(k, shape, f32)

    dt0 = jnp.exp(jax.random.uniform(ks[10], (DEPTH, SSD_HEADS), f32,
                                     math.log(DT_MIN), math.log(DT_MAX)))
    dt_bias = dt0 + jnp.log(-jnp.expm1(-dt0))
    return {
        'x': jax.random.normal(ks[0], (BATCH, SEQ, D_MODEL), f32),
        'mem': jax.random.normal(ks[1], (BATCH, MEM_LEN, D_MODEL), f32),
        'g_mix': gain(ks[2], (DEPTH, D_MODEL)),
        'w_in': nrm(ks[3], (DEPTH, D_MODEL, D_IN_PROJ), D_MODEL ** -0.5),
        'conv_w': nrm(ks[4], (DEPTH, CONV_K, CONV_CH), CONV_K ** -0.5),
        'conv_b': nrm(ks[5], (DEPTH, CONV_CH), 0.02),
        'ln_g': gain(ks[6], (DEPTH, CONV_CH)),
        'ln_b': nrm(ks[7], (DEPTH, CONV_CH), 0.02),
        'ssd_conv_w': nrm(ks[8], (DEPTH, SSD_CONV_K, SSD_CONV_CH), SSD_CONV_K ** -0.5),
        'ssd_conv_b': nrm(ks[9], (DEPTH, SSD_CONV_CH), 0.02),
        'dt_bias': dt_bias,
        'a_log': jnp.log(jax.random.uniform(ks[11], (DEPTH, SSD_HEADS), f32, 1.0, 16.0)),
        'd_skip': gain(ks[12], (DEPTH, SSD_HEADS)),
        'ssd_norm_g': gain(ks[13], (DEPTH, SSD_INNER)),
        'w_out': nrm(ks[14], (DEPTH, D_MIX, D_MODEL), D_MIX ** -0.5),
        'g_xattn': gain(ks[15], (DEPTH, D_MODEL)),
        'g_mem': gain(ks[16], (DEPTH, D_MODEL)),
        'w_q': nrm(ks[17], (DEPTH, D_MODEL, D_MODEL), D_MODEL ** -0.5),
        'w_k': nrm(ks[18], (DEPTH, D_MODEL, D_MODEL), D_MODEL ** -0.5),
        'w_v': nrm(ks[19], (DEPTH, D_MODEL, D_MODEL), D_MODEL ** -0.5),
        'w_o': nrm(ks[20], (DEPTH, D_MODEL, D_MODEL), D_MODEL ** -0.5),
        'g_moe': gain(ks[21], (DEPTH, D_MODEL)),
        'w_router_group': nrm(ks[22], (DEPTH, D_MODEL, N_EXPERT_GROUPS), D_MODEL ** -0.5),
        'b_router_group': nrm(ks[23], (DEPTH, N_EXPERT_GROUPS), 0.01),
        'w_router_expert': nrm(ks[24], (DEPTH, D_MODEL, N_EXPERTS), D_MODEL ** -0.5),
        'b_router_expert': nrm(ks[25], (DEPTH, N_EXPERTS), 0.01),
        'w_gate': nrm(ks[26], (DEPTH, N_EXPERTS, D_MODEL, D_EXPERT), D_MODEL ** -0.5),
        'w_up': nrm(ks[27], (DEPTH, N_EXPERTS, D_MODEL, D_EXPERT), D_MODEL ** -0.5),
        'w_down': nrm(ks[28], (DEPTH, N_EXPERTS, D_EXPERT, D_MODEL), D_EXPERT ** -0.5),
        'g_final': gain(ks[29], (D_MODEL,)),
    }


def reference(x, mem, g_mix, w_in, conv_w, conv_b, ln_g, ln_b, ssd_conv_w, ssd_conv_b,
              dt_bias, a_log, d_skip, ssd_norm_g, w_out, g_xattn, g_mem, w_q, w_k, w_v, w_o,
              g_moe, w_router_group, b_router_group, w_router_expert, b_router_expert,
              w_gate, w_up, w_down, g_final):
    for l in range(DEPTH):
        h = rms_norm(x, g_mix[l])
        x = x + parallel_mixer(h, w_in[l], conv_w[l], conv_b[l], ln_g[l], ln_b[l],
                               ssd_conv_w[l], ssd_conv_b[l], dt_bias[l], a_log[l],
                               d_skip[l], ssd_norm_g[l], w_out[l])
        h = rms_norm(x, g_xattn[l])
        m = rms_norm(mem, g_mem[l])
        x = x + memory_cross_attention(h, m, w_q[l], w_k[l], w_v[l], w_o[l])
        h = rms_norm(x, g_moe[l])
        x = x + hierarchical_moe(h, w_router_group[l], b_router_group[l], w_router_expert[l],
                                 b_router_expert[l], w_gate[l], w_up[l], w_down[l])
    return rms_norm(x, g_final)
```

```python
import functools

import jax
import jax.numpy as jnp
from jax import lax
from jax.experimental import pallas as pl
from jax.experimental.pallas import tpu as pltpu

F32 = jnp.float32
BF16 = jnp.bfloat16

D_MODEL = 1024
CONV_CH = 1024
CONV_K = 31
SSD_INNER = 1024
SSD_HEAD_DIM = 64
SSD_HEADS = 16
SSD_STATE = 128
SSD_GROUPS = 2
SSD_GROUP_W = SSD_INNER // SSD_GROUPS
SSD_CONV_K = 4
SSD_CHUNK = 128
SSD_BC = SSD_GROUPS * SSD_STATE
SSD_CONV_CH = SSD_INNER + 2 * SSD_BC
N_MAIN = 2 * CONV_CH + 2 * SSD_INNER + 2 * SSD_BC
XA_HEADS = 4
XA_HEAD_DIM = 256
MEM_LEN = 256
N_GROUPS = 4
EXPERTS_PER_GROUP = 8
N_EXPERTS = 32
TOP_K = 2
D_EXPERT = 512
MOE_BLOCK = 256
RMS_EPS = 1e-6
LN_EPS = 1e-5

LANES = 128
SUBLANES = 8
VMEM_LIMIT = 56 * 1024 * 1024

TOK_TILE = 512
CONV_TILE = 256
CONV_HALO = 32
CONV_ROWS = 64
SSD_TILE = 512
SSD_HALO = 8
ROUTE_LANES = 128


def _cparams(sem):
    return pltpu.CompilerParams(dimension_semantics=sem, vmem_limit_bytes=VMEM_LIMIT)


def _rms(x, g):
    return x * lax.rsqrt(jnp.mean(x * x, axis=-1, keepdims=True) + RMS_EPS) * g


def _sigmoid(x):
    return 1.0 / (1.0 + jnp.exp(-x))


def _kv_body(m_ref, g_ref, wk_ref, wv_ref, k_ref, v_ref):
    m = _rms(m_ref[0], g_ref[...]).astype(BF16)
    k_ref[0] = jnp.dot(m, wk_ref[...], preferred_element_type=F32).astype(BF16)
    v_ref[0] = jnp.dot(m, wv_ref[...], preferred_element_type=F32).astype(BF16)


def _kv_proj(mem, g_mem, w_k, w_v):
    b, s, d = mem.shape
    w_spec = pl.BlockSpec((d, d), lambda i: (0, 0))
    kv_spec = pl.BlockSpec((1, s, d), lambda i: (i, 0, 0))
    return pl.pallas_call(
        _kv_body,
        grid=(b,),
        in_specs=[kv_spec, pl.BlockSpec((1, d), lambda i: (0, 0)), w_spec, w_spec],
        out_specs=[kv_spec, kv_spec],
        out_shape=[jax.ShapeDtypeStruct((b, s, d), BF16)] * 2,
        compiler_params=_cparams(("arbitrary",)),
        name="kv_proj",
    )(mem, g_mem.reshape(1, d), w_k.astype(BF16), w_v.astype(BF16))


def _inproj_body(x_ref, g_ref, w_ref, wdt_ref, main_ref, dt_ref, *, n_chunk):
    h = _rms(x_ref[...], g_ref[...]).astype(BF16)
    for j in range(N_MAIN // n_chunk):
        sl = slice(j * n_chunk, (j + 1) * n_chunk)
        main_ref[:, sl] = jnp.dot(h, w_ref[:, sl], preferred_element_type=F32).astype(BF16)
    dt_ref[...] = jnp.dot(h, wdt_ref[...], preferred_element_type=F32)


def _in_proj(x2d, g_mix, w_in):
    t, d = x2d.shape
    w_main = w_in[:, :N_MAIN].astype(BF16)
    w_dt = jnp.pad(w_in[:, N_MAIN:], ((0, 0), (0, LANES - SSD_HEADS))).astype(BF16)
    return pl.pallas_call(
        functools.partial(_inproj_body, n_chunk=512),
        grid=(t // TOK_TILE,),
        in_specs=[
            pl.BlockSpec((TOK_TILE, d), lambda i: (i, 0)),
            pl.BlockSpec((1, d), lambda i: (0, 0)),
            pl.BlockSpec((d, N_MAIN), lambda i: (0, 0)),
            pl.BlockSpec((d, LANES), lambda i: (0, 0)),
        ],
        out_specs=[
            pl.BlockSpec((TOK_TILE, N_MAIN), lambda i: (i, 0)),
            pl.BlockSpec((TOK_TILE, LANES), lambda i: (i, 0)),
        ],
        out_shape=[jax.ShapeDtypeStruct((t, N_MAIN), BF16), jax.ShapeDtypeStruct((t, LANES), F32)],
        compiler_params=_cparams(("arbitrary",)),
        name="in_proj",
    )(x2d, g_mix.reshape(1, d), w_main, w_dt)


def _conv_body(val_ref, gate_ref, w_ref, b_ref, lg_ref, lb_ref, o_ref, ubuf_ref, acc_ref):
    @pl.when(pl.program_id(1) == 0)
    def _():
        ubuf_ref[0:CONV_HALO, :] = jnp.zeros((CONV_HALO, CONV_CH), F32)

    val = val_ref[...].astype(F32)
    gate = gate_ref[...].astype(F32)
    ubuf_ref[CONV_HALO:CONV_HALO + CONV_TILE, :] = val * _sigmoid(gate)

    first = CONV_HALO - (CONV_K - 1)
    for cb in range(CONV_CH // LANES):
        cols = slice(cb * LANES, (cb + 1) * LANES)
        for rc in range(CONV_TILE // CONV_ROWS):
            acc = jnp.zeros((CONV_ROWS, LANES), F32)
            for k in range(CONV_K):
                start = rc * CONV_ROWS + first + k
                acc = acc + w_ref[k:k + 1, cols] * ubuf_ref[start:start + CONV_ROWS, cols]
            acc_ref[rc * CONV_ROWS:(rc + 1) * CONV_ROWS, cols] = acc

    ubuf_ref[0:CONV_HALO, :] = ubuf_ref[CONV_TILE:CONV_TILE + CONV_HALO, :]

    u = acc_ref[...] + b_ref[...]
    mu = jnp.mean(u, axis=-1, keepdims=True)
    uc = u - mu
    var = jnp.mean(uc * uc, axis=-1, keepdims=True)
    y = uc * lax.rsqrt(var + LN_EPS) * lg_ref[...] + lb_ref[...]
    o_ref[...] = (y * _sigmoid(y)).astype(BF16)


def _conv_module(main, bsz, seqlen, conv_w, conv_b, ln_g, ln_b):
    nt = seqlen // CONV_TILE
    row = lambda v: v.reshape(1, CONV_CH)
    vec_spec = pl.BlockSpec((1, CONV_CH), lambda b, j: (0, 0))
    return pl.pallas_call(
        _conv_body,
        grid=(bsz, nt),
        in_specs=[
            pl.BlockSpec((CONV_TILE, CONV_CH), lambda b, j: (b * nt + j, 0)),
            pl.BlockSpec((CONV_TILE, CONV_CH), lambda b, j: (b * nt + j, 1)),
            pl.BlockSpec((CONV_K + 1, CONV_CH), lambda b, j: (0, 0)),
            vec_spec, vec_spec, vec_spec,
        ],
        out_specs=pl.BlockSpec((CONV_TILE, CONV_CH), lambda b, j: (b * nt + j, 0)),
        out_shape=jax.ShapeDtypeStruct((bsz * seqlen, CONV_CH), BF16),
        scratch_shapes=[
            pltpu.VMEM((CONV_HALO + CONV_TILE, CONV_CH), F32),
            pltpu.VMEM((CONV_TILE, CONV_CH), F32),
        ],
        compiler_params=_cparams(("arbitrary", "arbitrary")),
        name="conv_module",
    )(main, main, jnp.pad(conv_w, ((0, 1), (0, 0))), row(conv_b), row(ln_g), row(ln_b))


def _ssd_body(xbc_ref, z_ref, dt_ref, cw_ref, cb_ref, dtb_ref, alog_ref, dsk_ref, ng_ref,
              expand_ref, y_ref, xbuf_ref, act_ref, dts_ref, state_ref):
    @pl.when(pl.program_id(1) == 0)
    def _():
        xbuf_ref[0:SSD_HALO, :] = jnp.zeros((SSD_HALO, SSD_CONV_CH), F32)
        state_ref[...] = jnp.zeros(state_ref.shape, F32)

    xbuf_ref[SSD_HALO:SSD_HALO + SSD_TILE, :] = xbc_ref[...].astype(F32)
    first = SSD_HALO - (SSD_CONV_K - 1)
    conv = cb_ref[...] + cw_ref[0:1, :] * xbuf_ref[first:first + SSD_TILE, :]
    for k in range(1, SSD_CONV_K):
        conv = conv + cw_ref[k:k + 1, :] * xbuf_ref[first + k:first + k + SSD_TILE, :]
    act_ref[...] = conv * _sigmoid(conv)
    xbuf_ref[0:SSD_HALO, :] = xbuf_ref[SSD_TILE:SSD_TILE + SSD_HALO, :]

    dt_in = dt_ref[...] + dtb_ref[...]
    dts_ref[...] = jnp.maximum(dt_in, 0.0) + jnp.log1p(jnp.exp(-jnp.abs(dt_in)))

    a_neg = -jnp.exp(alog_ref[...])
    q = SSD_CHUNK
    row_i = lax.broadcasted_iota(jnp.int32, (q, q), 0)
    col_i = lax.broadcasted_iota(jnp.int32, (q, q), 1)
    causal = row_i >= col_i
    tril = causal.astype(F32)
    lane_i = lax.broadcasted_iota(jnp.int32, (q, LANES), 1)
    low_half = lane_i < SSD_HEAD_DIM
    expand = expand_ref[...]

    def chunk(c, carry):
        r0 = pl.multiple_of(c * q, q)
        rows = pl.ds(r0, q)
        dtc = dts_ref[rows, :]
        a_cs = jnp.dot(tril, dtc * a_neg, preferred_element_type=F32,
                       precision=lax.Precision.HIGHEST)
        a_cs_t = a_cs.T
        dt_t = dtc.T
        a_end = a_cs[q - 1:q, :]
        e_exp = jnp.dot(jnp.exp(a_cs).astype(BF16), expand, preferred_element_type=F32)
        w_exp = jnp.dot((jnp.exp(a_end - a_cs) * dtc).astype(BF16), expand,
                        preferred_element_type=F32)
        dec_row = e_exp[q - 1:q, :]
        xc = act_ref[rows, 0:SSD_INNER]
        xw = (xc * w_exp).astype(BF16)
        y_parts = []
        for g in range(SSD_GROUPS):
            b_f = act_ref[rows, SSD_INNER + g * SSD_STATE:SSD_INNER + (g + 1) * SSD_STATE]
            c_f = act_ref[rows, SSD_INNER + SSD_BC + g * SSD_STATE:
                          SSD_INNER + SSD_BC + (g + 1) * SSD_STATE]
            b_g = b_f.astype(BF16)
            c_g = c_f.astype(BF16)
            cb = lax.dot_general(c_g, b_g, (((1,), (1,)), ((), ())),
                                 preferred_element_type=F32)
            gcols = slice(g * SSD_GROUP_W, (g + 1) * SSD_GROUP_W)
            st = state_ref[g]
            y_off = jnp.dot(c_g, st.astype(BF16), preferred_element_type=F32)
            state_ref[g] = st * dec_row[:, gcols] + jnp.dot(
                b_f.T.astype(BF16), xw[:, gcols], preferred_element_type=F32)
            for pair in range(SSD_GROUP_W // LANES):
                ms = []
                for hh in range(2):
                    h = g * (SSD_HEADS // SSD_GROUPS) + 2 * pair + hh
                    seg = a_cs[:, h:h + 1] - a_cs_t[h:h + 1, :]
                    dec = jnp.exp(jnp.where(causal, seg, -jnp.inf))
                    ms.append((cb * dec * dt_t[h:h + 1, :]).astype(BF16))
                lhs = jnp.concatenate(ms, axis=1)
                xp = xc[:, g * SSD_GROUP_W + pair * LANES:g * SSD_GROUP_W + (pair + 1) * LANES]
                rhs = jnp.concatenate([jnp.where(low_half, xp, 0.0),
                                       jnp.where(low_half, 0.0, xp)], axis=0).astype(BF16)
                y_diag = jnp.dot(lhs, rhs, preferred_element_type=F32)
                lo = pair * LANES
                y_parts.append(y_diag + y_off[:, lo:lo + LANES]
                               * e_exp[:, g * SSD_GROUP_W + lo:g * SSD_GROUP_W + lo + LANES])
        y = jnp.concatenate(y_parts, axis=1) + xc * dsk_ref[...]
        z = z_ref[rows, :].astype(F32)
        y = y * (z * _sigmoid(z))
        outs = []
        for g in range(SSD_GROUPS):
            yg = y[:, g * SSD_GROUP_W:(g + 1) * SSD_GROUP_W]
            outs.append(yg * lax.rsqrt(jnp.mean(yg * yg, axis=-1, keepdims=True) + RMS_EPS))
        y_ref[rows, :] = (jnp.concatenate(outs, axis=1) * ng_ref[...]).astype(BF16)
        return carry

    lax.fori_loop(0, SSD_TILE // q, chunk, 0)


def _ssd(main, dt_raw, bsz, seqlen, ssd_conv_w, ssd_conv_b, dt_bias, a_log, d_skip, ssd_norm_g):
    nt = seqlen // SSD_TILE
    pad_h = lambda v: jnp.pad(v, (0, LANES - SSD_HEADS)).reshape(1, LANES)
    expand = (jnp.arange(LANES)[:, None] == (jnp.arange(SSD_INNER) // SSD_HEAD_DIM)[None, :]).astype(BF16)
    const = lambda shape: pl.BlockSpec(shape, lambda b, j: (0, 0))
    xbc_col = (2 * CONV_CH + SSD_INNER) // SSD_CONV_CH
    z_col = (2 * CONV_CH) // SSD_INNER
    return pl.pallas_call(
        _ssd_body,
        grid=(bsz, nt),
        in_specs=[
            pl.BlockSpec((SSD_TILE, SSD_CONV_CH), lambda b, j: (b * nt + j, xbc_col)),
            pl.BlockSpec((SSD_TILE, SSD_INNER), lambda b, j: (b * nt + j, z_col)),
            pl.BlockSpec((SSD_TILE, LANES), lambda b, j: (b * nt + j, 0)),
            const((SSD_CONV_K, SSD_CONV_CH)),
            const((1, SSD_CONV_CH)),
            const((1, LANES)), const((1, LANES)),
            const((1, SSD_INNER)), const((1, SSD_INNER)),
            const((LANES, SSD_INNER)),
        ],
        out_specs=pl.BlockSpec((SSD_TILE, SSD_INNER), lambda b, j: (b * nt + j, 0)),
        out_shape=jax.ShapeDtypeStruct((bsz * seqlen, SSD_INNER), BF16),
        scratch_shapes=[
            pltpu.VMEM((SSD_HALO + SSD_TILE, SSD_CONV_CH), F32),
            pltpu.VMEM((SSD_TILE, SSD_CONV_CH), F32),
            pltpu.VMEM((SSD_TILE, LANES), F32),
            pltpu.VMEM((SSD_GROUPS, SSD_STATE, SSD_GROUP_W), F32),
        ],
        compiler_params=_cparams(("arbitrary", "arbitrary")),
        name="ssd",
    )(main, main, dt_raw, ssd_conv_w, ssd_conv_b.reshape(1, SSD_CONV_CH), pad_h(dt_bias),
      pad_h(a_log), jnp.repeat(d_skip, SSD_HEAD_DIM).reshape(1, SSD_INNER),
      ssd_norm_g.reshape(1, SSD_INNER), expand)


def _outproj_body(x_ref, u_ref, y_ref, wu_ref, wy_ref, o_ref):
    o_ref[...] = (x_ref[...]
                  + jnp.dot(u_ref[...], wu_ref[...], preferred_element_type=F32)
                  + jnp.dot(y_ref[...], wy_ref[...], preferred_element_type=F32))


def _out_proj(x2d, u, y, w_out):
    t, d = x2d.shape
    w = w_out.astype(BF16)
    tile = pl.BlockSpec((TOK_TILE, d), lambda i: (i, 0))
    w_spec = pl.BlockSpec((d, d), lambda i: (0, 0))
    return pl.pallas_call(
        _outproj_body,
        grid=(t // TOK_TILE,),
        in_specs=[tile, tile, tile, w_spec, w_spec],
        out_specs=tile,
        out_shape=jax.ShapeDtypeStruct((t, d), F32),
        compiler_params=_cparams(("arbitrary",)),
        name="out_proj",
    )(x2d, u, y, w[:CONV_CH], w[CONV_CH:])


def _xattn_body(x_ref, k_ref, v_ref, gx_ref, wq_ref, wo_ref, gm_ref, wr_hi_ref, wr_lo_ref, br_ref,
                x2_ref, h2_ref, route_ref):
    x = x_ref[...]
    h = _rms(x, gx_ref[...]).astype(BF16)
    q = (jnp.dot(h, wq_ref[...], preferred_element_type=F32) * (XA_HEAD_DIM ** -0.5)).astype(BF16)
    heads = []
    for i in range(XA_HEADS):
        cols = slice(i * XA_HEAD_DIM, (i + 1) * XA_HEAD_DIM)
        s = lax.dot_general(q[:, cols], k_ref[0, :, cols], (((1,), (1,)), ((), ())),
                            preferred_element_type=F32)
        p = jnp.exp(s - jnp.max(s, axis=-1, keepdims=True))
        p = p / jnp.sum(p, axis=-1, keepdims=True)
        heads.append(jnp.dot(p.astype(BF16), v_ref[0, :, cols], preferred_element_type=F32))
    o = jnp.concatenate(heads, axis=1).astype(BF16)
    x2 = x + jnp.dot(o, wo_ref[...], preferred_element_type=F32)
    x2_ref[...] = x2

    h2 = _rms(x2, gm_ref[...])
    h2_ref[...] = h2
    h_hi = h2.astype(BF16)
    h_lo = (h2 - h_hi.astype(F32)).astype(BF16)
    logits = (jnp.dot(h_hi, wr_hi_ref[...], preferred_element_type=F32)
              + jnp.dot(h_lo, wr_hi_ref[...], preferred_element_type=F32)
              + jnp.dot(h_hi, wr_lo_ref[...], preferred_element_type=F32)) + br_ref[...]
    lane = lax.broadcasted_iota(jnp.int32, logits.shape, 1)
    neg = -jnp.inf

    def first_argmax(v):
        m = jnp.max(v, axis=-1, keepdims=True)
        return m, jnp.min(jnp.where(v == m, lane, ROUTE_LANES), axis=-1, keepdims=True)

    gl = jnp.where(lane < N_GROUPS, logits, neg)
    g_max, g_sel = first_argmax(gl)
    p_top = 1.0 / jnp.sum(jnp.exp(gl - g_max), axis=-1, keepdims=True)
    e_lo = N_GROUPS + EXPERTS_PER_GROUP * g_sel
    el = jnp.where((lane >= e_lo) & (lane < e_lo + EXPERTS_PER_GROUP), logits, neg)
    m1, i1 = first_argmax(el)
    m2, i2 = first_argmax(jnp.where(lane == i1, neg, el))
    r = jnp.exp(m2 - m1)
    w1 = p_top / (1.0 + r)
    w2 = w1 * r
    route = jnp.where(lane == 0, (i1 - N_GROUPS).astype(F32),
            jnp.where(lane == 1, (i2 - N_GROUPS).astype(F32),
            jnp.where(lane == 2, w1, jnp.where(lane == 3, w2, 0.0))))
    route_ref[...] = route


def _xattn_route(x1, k, v, bsz, seqlen, g_xattn, w_q, w_o, g_moe, w_rg, b_rg, w_re, b_re):
    t, d = x1.shape
    nt = seqlen // TOK_TILE
    w_r = jnp.pad(jnp.concatenate([w_rg, w_re], axis=1), ((0, 0), (0, ROUTE_LANES - N_GROUPS - N_EXPERTS)))
    b_r = jnp.pad(jnp.concatenate([b_rg, b_re]), (0, ROUTE_LANES - N_GROUPS - N_EXPERTS)).reshape(1, ROUTE_LANES)
    wr_hi = w_r.astype(BF16)
    wr_lo = (w_r - wr_hi.astype(F32)).astype(BF16)
    tile = pl.BlockSpec((TOK_TILE, d), lambda b, j: (b * nt + j, 0))
    kv_spec = pl.BlockSpec((1, MEM_LEN, d), lambda b, j: (b, 0, 0))
    const = lambda shape: pl.BlockSpec(shape, lambda b, j: (0, 0))
    return pl.pallas_call(
        _xattn_body,
        grid=(bsz, nt),
        in_specs=[tile, kv_spec, kv_spec, const((1, d)), const((d, d)), const((d, d)), const((1, d)),
                  const((d, ROUTE_LANES)), const((d, ROUTE_LANES)), const((1, ROUTE_LANES))],
        out_specs=[tile, tile, pl.BlockSpec((TOK_TILE, ROUTE_LANES), lambda b, j: (b * nt + j, 0))],
        out_shape=[jax.ShapeDtypeStruct((t, d), F32), jax.ShapeDtypeStruct((t, d), F32),
                   jax.ShapeDtypeStruct((t, ROUTE_LANES), F32)],
        compiler_params=_cparams(("arbitrary", "arbitrary")),
        name="xattn_route",
    )(x1, k, v, g_xattn.reshape(1, d), w_q.astype(BF16), w_o.astype(BF16), g_moe.reshape(1, d),
      wr_hi, wr_lo, b_r)


def _row_copy(src_ref, src_row, dst_ref, dst_row, sem):
    return pltpu.make_async_copy(src_ref.at[pl.ds(src_row, 1), :], dst_ref.at[pl.ds(dst_row, 1), :], sem)


def _dispatch_body(dest_ref, fill_ref, h_ref, xs_ref, zero_ref, sem, zsem, *, n_slots):
    def issue(i, c):
        _row_copy(h_ref, i // TOP_K, xs_ref, dest_ref[0, 0, i], sem).start()
        return c

    lax.fori_loop(0, TOP_K * TOK_TILE, issue, 0)

    @pl.when(pl.program_id(0) == 0)
    def _():
        zero_ref[...] = jnp.zeros(zero_ref.shape, F32)

        def zfill(lo, hi):
            def zissue(s, c):
                _row_copy(zero_ref, 0, xs_ref, s, zsem).start()
                return c

            def zwait(s, c):
                _row_copy(zero_ref, 0, xs_ref, s, zsem).wait()
                return c

            lax.fori_loop(lo, hi, zissue, 0)
            lax.fori_loop(lo, hi, zwait, 0)

        def per_expert(e, c):
            zfill(fill_ref[0, e], fill_ref[1, e])
            return c

        lax.fori_loop(0, N_EXPERTS, per_expert, 0)
        zfill(fill_ref[2, 0], n_slots)

    def drain(i, c):
        _row_copy(h_ref, i // TOP_K, xs_ref, dest_ref[0, 0, i], sem).wait()
        return c

    lax.fori_loop(0, TOP_K * TOK_TILE, drain, 0)


def _dispatch(h2, dest, fill, n_slots):
    t, d = h2.shape
    nt = t // TOK_TILE
    return pl.pallas_call(
        functools.partial(_dispatch_body, n_slots=n_slots),
        grid=(nt,),
        in_specs=[
            pl.BlockSpec((1, 1, TOP_K * TOK_TILE), lambda i: (i, 0, 0), memory_space=pltpu.SMEM),
            pl.BlockSpec(memory_space=pltpu.SMEM),
            pl.BlockSpec((TOK_TILE, d), lambda i: (i, 0)),
        ],
        out_specs=pl.BlockSpec(memory_space=pl.ANY),
        out_shape=jax.ShapeDtypeStruct((n_slots, d), F32),
        scratch_shapes=[pltpu.VMEM((SUBLANES, d), F32), pltpu.SemaphoreType.DMA(()),
                        pltpu.SemaphoreType.DMA(())],
        compiler_params=_cparams(("arbitrary",)),
        name="dispatch",
    )(dest.reshape(nt, 1, TOP_K * TOK_TILE), fill, h2)


def _experts_body(be_ref, x_ref, wg_ref, wu_ref, wd_ref, y_ref):
    del be_ref
    x = x_ref[...].astype(BF16)
    g = jnp.dot(x, wg_ref[0], preferred_element_type=F32)
    u = jnp.dot(x, wu_ref[0], preferred_element_type=F32)
    a = (g * _sigmoid(g) * u).astype(BF16)
    y_ref[...] = jnp.dot(a, wd_ref[0], preferred_element_type=F32)


def _experts(xs, block_e, w_gate, w_up, w_down):
    n_slots, d = xs.shape
    n_blocks = n_slots // MOE_BLOCK
    tile = pl.BlockSpec((MOE_BLOCK, d), lambda b, be: (b, 0))
    return pl.pallas_call(
        _experts_body,
        grid_spec=pltpu.PrefetchScalarGridSpec(
            num_scalar_prefetch=1,
            grid=(n_blocks,),
            in_specs=[
                tile,
                pl.BlockSpec((1, d, D_EXPERT), lambda b, be: (be[b], 0, 0)),
                pl.BlockSpec((1, d, D_EXPERT), lambda b, be: (be[b], 0, 0)),
                pl.BlockSpec((1, D_EXPERT, d), lambda b, be: (be[b], 0, 0)),
            ],
            out_specs=tile,
        ),
        out_shape=jax.ShapeDtypeStruct((n_slots, d), F32),
        compiler_params=_cparams(("arbitrary",)),
        name="experts",
    )(block_e, xs, w_gate.astype(BF16), w_up.astype(BF16), w_down.astype(BF16))


def _combine_body(pos_ref, x_ref, route_ref, g_ref, ys_ref, o_ref, ybuf_ref, sem):
    def issue(i, c):
        _row_copy(ys_ref, pos_ref[0, 0, i], ybuf_ref, i, sem).start()
        return c

    def drain(i, c):
        _row_copy(ys_ref, pos_ref[0, 0, i], ybuf_ref, i, sem).wait()
        return c

    lax.fori_loop(0, TOP_K * TOK_TILE, issue, 0)
    lax.fori_loop(0, TOP_K * TOK_TILE, drain, 0)
    route = route_ref[...]
    x = (x_ref[...]
         + ybuf_ref[0:TOK_TILE, :] * route[:, 2:3]
         + ybuf_ref[TOK_TILE:TOP_K * TOK_TILE, :] * route[:, 3:4])
    o_ref[...] = _rms(x, g_ref[...])


def _combine(x2, route, pos_kmajor, ys, g_final):
    t, d = x2.shape
    nt = t // TOK_TILE
    tile = pl.BlockSpec((TOK_TILE, d), lambda i: (i, 0))
    return pl.pallas_call(
        _combine_body,
        grid=(nt,),
        in_specs=[
            pl.BlockSpec((1, 1, TOP_K * TOK_TILE), lambda i: (i, 0, 0), memory_space=pltpu.SMEM),
            tile,
            pl.BlockSpec((TOK_TILE, ROUTE_LANES), lambda i: (i, 0)),
            pl.BlockSpec((1, d), lambda i: (0, 0)),
            pl.BlockSpec(memory_space=pl.ANY),
        ],
        out_specs=tile,
        out_shape=jax.ShapeDtypeStruct((t, d), F32),
        scratch_shapes=[pltpu.VMEM((TOP_K * TOK_TILE, d), F32), pltpu.SemaphoreType.DMA(())],
        compiler_params=_cparams(("arbitrary",)),
        name="combine",
    )(pos_kmajor, x2, route, g_final.reshape(1, d), ys)


def _routing_tables(route, n_tok):
    e = route[:, :TOP_K].astype(jnp.int32)
    e_flat = e.reshape(-1)
    onehot = (e_flat[:, None] == jnp.arange(N_EXPERTS, dtype=jnp.int32)[None, :]).astype(jnp.int32)
    csum = jnp.cumsum(onehot, axis=0)
    rank = jnp.sum(onehot * csum, axis=1) - 1
    counts = csum[-1]
    padded = ((counts + MOE_BLOCK - 1) // MOE_BLOCK) * MOE_BLOCK
    pad_end = jnp.cumsum(padded)
    pad_start = pad_end - padded
    dest = (pad_start[e_flat] + rank).astype(jnp.int32)
    n_slots = n_tok * TOP_K + N_EXPERTS * MOE_BLOCK
    n_blocks = n_slots // MOE_BLOCK
    block_e = jnp.clip(jnp.searchsorted(pad_end, jnp.arange(n_blocks, dtype=jnp.int32) * MOE_BLOCK,
                                        side='right'), 0, N_EXPERTS - 1).astype(jnp.int32)
    fill = jnp.stack([pad_start + counts, pad_end,
                      jnp.broadcast_to(pad_end[-1], (N_EXPERTS,))]).astype(jnp.int32)
    return dest, block_e, fill, n_slots


def kernel(x, mem, g_mix, w_in, conv_w, conv_b, ln_g, ln_b, ssd_conv_w, ssd_conv_b, dt_bias, a_log, d_skip, ssd_norm_g, w_out, g_xattn, g_mem, w_q, w_k, w_v, w_o, g_moe, w_router_group, b_router_group, w_router_expert, b_router_expert, w_gate, w_up, w_down, g_final):
    bsz, seqlen, d = x.shape
    n_tok = bsz * seqlen
    xt = x.reshape(n_tok, d)
    assert g_mix.shape[0] == 1, "the combine kernel applies the final norm: single layer only"
    for l in range(1):
        main, dt_raw = _in_proj(xt, g_mix[l], w_in[l])
        u = _conv_module(main, bsz, seqlen, conv_w[l], conv_b[l], ln_g[l], ln_b[l])
        y = _ssd(main, dt_raw, bsz, seqlen, ssd_conv_w[l], ssd_conv_b[l], dt_bias[l], a_log[l],
                 d_skip[l], ssd_norm_g[l])
        x1 = _out_proj(xt, u, y, w_out[l])
        k, v = _kv_proj(mem, g_mem[l], w_k[l], w_v[l])
        x2, h2, route = _xattn_route(x1, k, v, bsz, seqlen, g_xattn[l], w_q[l], w_o[l], g_moe[l],
                                     w_router_group[l], b_router_group[l], w_router_expert[l],
                                     b_router_expert[l])
        dest, block_e, fill, n_slots = _routing_tables(route, n_tok)
        xs = _dispatch(h2, dest, fill, n_slots)
        ys = _experts(xs, block_e, w_gate[l], w_up[l], w_down[l])
        pos = dest.reshape(n_tok // TOK_TILE, TOK_TILE, TOP_K).transpose(0, 2, 1).reshape(
            n_tok // TOK_TILE, 1, TOP_K * TOK_TILE)
        xt = _combine(x2, route, pos, ys, g_final)
    return xt.reshape(bsz, seqlen, d)
```

```python
import functools

import jax
import jax.numpy as jnp
from jax import lax
from jax.experimental import pallas as pl
from jax.experimental.pallas import tpu as pltpu

F32 = jnp.float32
BF16 = jnp.bfloat16

D_MODEL = 1024
CONV_CH = 1024
CONV_K = 31
SSD_INNER = 1024
SSD_HEAD_DIM = 64
SSD_HEADS = 16
SSD_STATE = 128
SSD_GROUPS = 2
SSD_GROUP_W = SSD_INNER // SSD_GROUPS
SSD_CONV_K = 4
SSD_CHUNK = 128
SSD_BC = SSD_GROUPS * SSD_STATE
SSD_CONV_CH = SSD_INNER + 2 * SSD_BC
N_MAIN = 2 * CONV_CH + 2 * SSD_INNER + 2 * SSD_BC
XA_HEADS = 4
XA_HEAD_DIM = 256
MEM_LEN = 256
N_GROUPS = 4
EXPERTS_PER_GROUP = 8
N_EXPERTS = 32
TOP_K = 2
D_EXPERT = 512
MOE_BLOCK = 256
RMS_EPS = 1e-6
LN_EPS = 1e-5

LANES = 128
SUBLANES = 8
VMEM_LIMIT = 56 * 1024 * 1024

TOK_TILE = 512
CONV_TILE = 256
CONV_HALO = 32
CONV_ROWS = 64
SSD_TILE = 512
SSD_HALO = 8
ROUTE_LANES = 128
SORT_ROWS = TOP_K * TOK_TILE + N_EXPERTS * SUBLANES


def _cparams(sem):
    return pltpu.CompilerParams(dimension_semantics=sem, vmem_limit_bytes=VMEM_LIMIT)


def _rms(x, g):
    return x * lax.rsqrt(jnp.mean(x * x, axis=-1, keepdims=True) + RMS_EPS) * g


def _sigmoid(x):
    return 1.0 / (1.0 + jnp.exp(-x))


def _kv_body(m_ref, g_ref, wk_ref, wv_ref, k_ref, v_ref):
    m = _rms(m_ref[0], g_ref[...]).astype(BF16)
    k_ref[0] = jnp.dot(m, wk_ref[...], preferred_element_type=F32).astype(BF16)
    v_ref[0] = jnp.dot(m, wv_ref[...], preferred_element_type=F32).astype(BF16)


def _kv_proj(mem, g_mem, w_k, w_v):
    b, s, d = mem.shape
    w_spec = pl.BlockSpec((d, d), lambda i: (0, 0))
    kv_spec = pl.BlockSpec((1, s, d), lambda i: (i, 0, 0))
    return pl.pallas_call(
        _kv_body,
        grid=(b,),
        in_specs=[kv_spec, pl.BlockSpec((1, d), lambda i: (0, 0)), w_spec, w_spec],
        out_specs=[kv_spec, kv_spec],
        out_shape=[jax.ShapeDtypeStruct((b, s, d), BF16)] * 2,
        compiler_params=_cparams(("arbitrary",)),
        name="kv_proj",
    )(mem, g_mem.reshape(1, d), w_k.astype(BF16), w_v.astype(BF16))


def _inproj_body(x_ref, g_ref, w_ref, wdt_ref, main_ref, dt_ref, *, n_chunk):
    h = _rms(x_ref[...], g_ref[...]).astype(BF16)
    for j in range(N_MAIN // n_chunk):
        sl = slice(j * n_chunk, (j + 1) * n_chunk)
        main_ref[:, sl] = jnp.dot(h, w_ref[:, sl], preferred_element_type=F32).astype(BF16)
    dt_ref[...] = jnp.dot(h, wdt_ref[...], preferred_element_type=F32)


def _in_proj(x2d, g_mix, w_in):
    t, d = x2d.shape
    w_main = w_in[:, :N_MAIN].astype(BF16)
    w_dt = jnp.pad(w_in[:, N_MAIN:], ((0, 0), (0, LANES - SSD_HEADS))).astype(BF16)
    return pl.pallas_call(
        functools.partial(_inproj_body, n_chunk=512),
        grid=(t // TOK_TILE,),
        in_specs=[
            pl.BlockSpec((TOK_TILE, d), lambda i: (i, 0)),
            pl.BlockSpec((1, d), lambda i: (0, 0)),
            pl.BlockSpec((d, N_MAIN), lambda i: (0, 0)),
            pl.BlockSpec((d, LANES), lambda i: (0, 0)),
        ],
        out_specs=[
            pl.BlockSpec((TOK_TILE, N_MAIN), lambda i: (i, 0)),
            pl.BlockSpec((TOK_TILE, LANES), lambda i: (i, 0)),
        ],
        out_shape=[jax.ShapeDtypeStruct((t, N_MAIN), BF16), jax.ShapeDtypeStruct((t, LANES), F32)],
        compiler_params=_cparams(("arbitrary",)),
        name="in_proj",
    )(x2d, g_mix.reshape(1, d), w_main, w_dt)


def _conv_body(val_ref, gate_ref, w_ref, b_ref, lg_ref, lb_ref, o_ref, ubuf_ref, acc_ref):
    @pl.when(pl.program_id(1) == 0)
    def _():
        ubuf_ref[0:CONV_HALO, :] = jnp.zeros((CONV_HALO, CONV_CH), F32)

    val = val_ref[...].astype(F32)
    gate = gate_ref[...].astype(F32)
    ubuf_ref[CONV_HALO:CONV_HALO + CONV_TILE, :] = val * _sigmoid(gate)

    first = CONV_HALO - (CONV_K - 1)
    for cb in range(CONV_CH // LANES):
        cols = slice(cb * LANES, (cb + 1) * LANES)
        for rc in range(CONV_TILE // CONV_ROWS):
            acc = jnp.zeros((CONV_ROWS, LANES), F32)
            for k in range(CONV_K):
                start = rc * CONV_ROWS + first + k
                acc = acc + w_ref[k:k + 1, cols] * ubuf_ref[start:start + CONV_ROWS, cols]
            acc_ref[rc * CONV_ROWS:(rc + 1) * CONV_ROWS, cols] = acc

    ubuf_ref[0:CONV_HALO, :] = ubuf_ref[CONV_TILE:CONV_TILE + CONV_HALO, :]

    u = acc_ref[...] + b_ref[...]
    mu = jnp.mean(u, axis=-1, keepdims=True)
    uc = u - mu
    var = jnp.mean(uc * uc, axis=-1, keepdims=True)
    y = uc * lax.rsqrt(var + LN_EPS) * lg_ref[...] + lb_ref[...]
    o_ref[...] = (y * _sigmoid(y)).astype(BF16)


def _conv_module(main, bsz, seqlen, conv_w, conv_b, ln_g, ln_b):
    nt = seqlen // CONV_TILE
    row = lambda v: v.reshape(1, CONV_CH)
    vec_spec = pl.BlockSpec((1, CONV_CH), lambda b, j: (0, 0))
    return pl.pallas_call(
        _conv_body,
        grid=(bsz, nt),
        in_specs=[
            pl.BlockSpec((CONV_TILE, CONV_CH), lambda b, j: (b * nt + j, 0)),
            pl.BlockSpec((CONV_TILE, CONV_CH), lambda b, j: (b * nt + j, 1)),
            pl.BlockSpec((CONV_K + 1, CONV_CH), lambda b, j: (0, 0)),
            vec_spec, vec_spec, vec_spec,
        ],
        out_specs=pl.BlockSpec((CONV_TILE, CONV_CH), lambda b, j: (b * nt + j, 0)),
        out_shape=jax.ShapeDtypeStruct((bsz * seqlen, CONV_CH), BF16),
        scratch_shapes=[
            pltpu.VMEM((CONV_HALO + CONV_TILE, CONV_CH), F32),
            pltpu.VMEM((CONV_TILE, CONV_CH), F32),
        ],
        compiler_params=_cparams(("arbitrary", "arbitrary")),
        name="conv_module",
    )(main, main, jnp.pad(conv_w, ((0, 1), (0, 0))), row(conv_b), row(ln_g), row(ln_b))


def _ssd_body(xbc_ref, z_ref, dt_ref, cw_ref, cb_ref, dtb_ref, alog_ref, dsk_ref, ng_ref,
              expand_ref, y_ref, xbuf_ref, act_ref, dts_ref, state_ref):
    @pl.when(pl.program_id(1) == 0)
    def _():
        xbuf_ref[0:SSD_HALO, :] = jnp.zeros((SSD_HALO, SSD_CONV_CH), F32)
        state_ref[...] = jnp.zeros(state_ref.shape, F32)

    xbuf_ref[SSD_HALO:SSD_HALO + SSD_TILE, :] = xbc_ref[...].astype(F32)
    first = SSD_HALO - (SSD_CONV_K - 1)
    conv = cb_ref[...] + cw_ref[0:1, :] * xbuf_ref[first:first + SSD_TILE, :]
    for k in range(1, SSD_CONV_K):
        conv = conv + cw_ref[k:k + 1, :] * xbuf_ref[first + k:first + k + SSD_TILE, :]
    act_ref[...] = conv * _sigmoid(conv)
    xbuf_ref[0:SSD_HALO, :] = xbuf_ref[SSD_TILE:SSD_TILE + SSD_HALO, :]

    dt_in = dt_ref[...] + dtb_ref[...]
    dts_ref[...] = jnp.maximum(dt_in, 0.0) + jnp.log1p(jnp.exp(-jnp.abs(dt_in)))

    a_neg = -jnp.exp(alog_ref[...])
    q = SSD_CHUNK
    row_i = lax.broadcasted_iota(jnp.int32, (q, q), 0)
    col_i = lax.broadcasted_iota(jnp.int32, (q, q), 1)
    causal = row_i >= col_i
    tril = causal.astype(F32)
    lane_i = lax.broadcasted_iota(jnp.int32, (q, LANES), 1)
    low_half = lane_i < SSD_HEAD_DIM
    expand = expand_ref[...]

    def chunk(c, carry):
        r0 = pl.multiple_of(c * q, q)
        rows = pl.ds(r0, q)
        dtc = dts_ref[rows, :]
        a_cs = jnp.dot(tril, dtc * a_neg, preferred_element_type=F32,
                       precision=lax.Precision.HIGHEST)
        a_cs_t = a_cs.T
        dt_t = dtc.T
        a_end = a_cs[q - 1:q, :]
        e_exp = jnp.dot(jnp.exp(a_cs).astype(BF16), expand, preferred_element_type=F32)
        w_exp = jnp.dot((jnp.exp(a_end - a_cs) * dtc).astype(BF16), expand,
                        preferred_element_type=F32)
        dec_row = e_exp[q - 1:q, :]
        xc = act_ref[rows, 0:SSD_INNER]
        xw = (xc * w_exp).astype(BF16)
        y_parts = []
        for g in range(SSD_GROUPS):
            b_f = act_ref[rows, SSD_INNER + g * SSD_STATE:SSD_INNER + (g + 1) * SSD_STATE]
            c_f = act_ref[rows, SSD_INNER + SSD_BC + g * SSD_STATE:
                          SSD_INNER + SSD_BC + (g + 1) * SSD_STATE]
            b_g = b_f.astype(BF16)
            c_g = c_f.astype(BF16)
            cb = lax.dot_general(c_g, b_g, (((1,), (1,)), ((), ())),
                                 preferred_element_type=F32)
            gcols = slice(g * SSD_GROUP_W, (g + 1) * SSD_GROUP_W)
            st = state_ref[g]
            y_off = jnp.dot(c_g, st.astype(BF16), preferred_element_type=F32)
            state_ref[g] = st * dec_row[:, gcols] + jnp.dot(
                b_f.T.astype(BF16), xw[:, gcols], preferred_element_type=F32)
            for pair in range(SSD_GROUP_W // LANES):
                ms = []
                for hh in range(2):
                    h = g * (SSD_HEADS // SSD_GROUPS) + 2 * pair + hh
                    seg = a_cs[:, h:h + 1] - a_cs_t[h:h + 1, :]
                    dec = jnp.exp(jnp.where(causal, seg, -jnp.inf))
                    ms.append((cb * dec * dt_t[h:h + 1, :]).astype(BF16))
                lhs = jnp.concatenate(ms, axis=1)
                xp = xc[:, g * SSD_GROUP_W + pair * LANES:g * SSD_GROUP_W + (pair + 1) * LANES]
                rhs = jnp.concatenate([jnp.where(low_half, xp, 0.0),
                                       jnp.where(low_half, 0.0, xp)], axis=0).astype(BF16)
                y_diag = jnp.dot(lhs, rhs, preferred_element_type=F32)
                lo = pair * LANES
                y_parts.append(y_diag + y_off[:, lo:lo + LANES]
                               * e_exp[:, g * SSD_GROUP_W + lo:g * SSD_GROUP_W + lo + LANES])
        y = jnp.concatenate(y_parts, axis=1) + xc * dsk_ref[...]
        z = z_ref[rows, :].astype(F32)
        y = y * (z * _sigmoid(z))
        outs = []
        for g in range(SSD_GROUPS):
            yg = y[:, g * SSD_GROUP_W:(g + 1) * SSD_GROUP_W]
            outs.append(yg * lax.rsqrt(jnp.mean(yg * yg, axis=-1, keepdims=True) + RMS_EPS))
        y_ref[rows, :] = (jnp.concatenate(outs, axis=1) * ng_ref[...]).astype(BF16)
        return carry

    lax.fori_loop(0, SSD_TILE // q, chunk, 0)


def _ssd(main, dt_raw, bsz, seqlen, ssd_conv_w, ssd_conv_b, dt_bias, a_log, d_skip, ssd_norm_g):
    nt = seqlen // SSD_TILE
    pad_h = lambda v: jnp.pad(v, (0, LANES - SSD_HEADS)).reshape(1, LANES)
    expand = (jnp.arange(LANES)[:, None] == (jnp.arange(SSD_INNER) // SSD_HEAD_DIM)[None, :]).astype(BF16)
    const = lambda shape: pl.BlockSpec(shape, lambda b, j: (0, 0))
    xbc_col = (2 * CONV_CH + SSD_INNER) // SSD_CONV_CH
    z_col = (2 * CONV_CH) // SSD_INNER
    return pl.pallas_call(
        _ssd_body,
        grid=(bsz, nt),
        in_specs=[
            pl.BlockSpec((SSD_TILE, SSD_CONV_CH), lambda b, j: (b * nt + j, xbc_col)),
            pl.BlockSpec((SSD_TILE, SSD_INNER), lambda b, j: (b * nt + j, z_col)),
            pl.BlockSpec((SSD_TILE, LANES), lambda b, j: (b * nt + j, 0)),
            const((SSD_CONV_K, SSD_CONV_CH)),
            const((1, SSD_CONV_CH)),
            const((1, LANES)), const((1, LANES)),
            const((1, SSD_INNER)), const((1, SSD_INNER)),
            const((LANES, SSD_INNER)),
        ],
        out_specs=pl.BlockSpec((SSD_TILE, SSD_INNER), lambda b, j: (b * nt + j, 0)),
        out_shape=jax.ShapeDtypeStruct((bsz * seqlen, SSD_INNER), BF16),
        scratch_shapes=[
            pltpu.VMEM((SSD_HALO + SSD_TILE, SSD_CONV_CH), F32),
            pltpu.VMEM((SSD_TILE, SSD_CONV_CH), F32),
            pltpu.VMEM((SSD_TILE, LANES), F32),
            pltpu.VMEM((SSD_GROUPS, SSD_STATE, SSD_GROUP_W), F32),
        ],
        compiler_params=_cparams(("arbitrary", "arbitrary")),
        name="ssd",
    )(main, main, dt_raw, ssd_conv_w, ssd_conv_b.reshape(1, SSD_CONV_CH), pad_h(dt_bias),
      pad_h(a_log), jnp.repeat(d_skip, SSD_HEAD_DIM).reshape(1, SSD_INNER),
      ssd_norm_g.reshape(1, SSD_INNER), expand)


def _outproj_body(x_ref, u_ref, y_ref, wu_ref, wy_ref, o_ref):
    o_ref[...] = (x_ref[...]
                  + jnp.dot(u_ref[...], wu_ref[...], preferred_element_type=F32)
                  + jnp.dot(y_ref[...], wy_ref[...], preferred_element_type=F32))


def _out_proj(x2d, u, y, w_out):
    t, d = x2d.shape
    w = w_out.astype(BF16)
    tile = pl.BlockSpec((TOK_TILE, d), lambda i: (i, 0))
    w_spec = pl.BlockSpec((d, d), lambda i: (0, 0))
    return pl.pallas_call(
        _outproj_body,
        grid=(t // TOK_TILE,),
        in_specs=[tile, tile, tile, w_spec, w_spec],
        out_specs=tile,
        out_shape=jax.ShapeDtypeStruct((t, d), F32),
        compiler_params=_cparams(("arbitrary",)),
        name="out_proj",
    )(x2d, u, y, w[:CONV_CH], w[CONV_CH:])


def _xattn_body(x_ref, k_ref, v_ref, gx_ref, wq_ref, wo_ref, gm_ref, wr_hi_ref, wr_lo_ref, br_ref,
                x2_ref, h2_ref, route_ref, stats_ref):
    x = x_ref[...]
    h = _rms(x, gx_ref[...]).astype(BF16)
    q = (jnp.dot(h, wq_ref[...], preferred_element_type=F32) * (XA_HEAD_DIM ** -0.5)).astype(BF16)
    heads = []
    for i in range(XA_HEADS):
        cols = slice(i * XA_HEAD_DIM, (i + 1) * XA_HEAD_DIM)
        s = lax.dot_general(q[:, cols], k_ref[0, :, cols], (((1,), (1,)), ((), ())),
                            preferred_element_type=F32)
        p = jnp.exp(s - jnp.max(s, axis=-1, keepdims=True))
        p = p / jnp.sum(p, axis=-1, keepdims=True)
        heads.append(jnp.dot(p.astype(BF16), v_ref[0, :, cols], preferred_element_type=F32))
    o = jnp.concatenate(heads, axis=1).astype(BF16)
    x2 = x + jnp.dot(o, wo_ref[...], preferred_element_type=F32)
    x2_ref[...] = x2

    h2 = _rms(x2, gm_ref[...])
    h2_ref[...] = h2
    h_hi = h2.astype(BF16)
    h_lo = (h2 - h_hi.astype(F32)).astype(BF16)
    logits = (jnp.dot(h_hi, wr_hi_ref[...], preferred_element_type=F32)
              + jnp.dot(h_lo, wr_hi_ref[...], preferred_element_type=F32)
              + jnp.dot(h_hi, wr_lo_ref[...], preferred_element_type=F32)) + br_ref[...]
    lane = lax.broadcasted_iota(jnp.int32, logits.shape, 1)
    neg = -jnp.inf

    def first_argmax(v):
        m = jnp.max(v, axis=-1, keepdims=True)
        return m, jnp.min(jnp.where(v == m, lane, ROUTE_LANES), axis=-1, keepdims=True)

    gl = jnp.where(lane < N_GROUPS, logits, neg)
    g_max, g_sel = first_argmax(gl)
    p_top = 1.0 / jnp.sum(jnp.exp(gl - g_max), axis=-1, keepdims=True)
    e_lo = N_GROUPS + EXPERTS_PER_GROUP * g_sel
    el = jnp.where((lane >= e_lo) & (lane < e_lo + EXPERTS_PER_GROUP), logits, neg)
    m1, i1 = first_argmax(el)
    m2, i2 = first_argmax(jnp.where(lane == i1, neg, el))
    r = jnp.exp(m2 - m1)
    w1 = p_top / (1.0 + r)
    w2 = w1 * r
    e1 = i1 - N_GROUPS
    e2 = i2 - N_GROUPS
    oh1 = (lane == e1).astype(BF16)
    oh2 = (lane == e2).astype(BF16)
    n_t = logits.shape[0]
    before = (lax.broadcasted_iota(jnp.int32, (n_t, n_t), 0)
              > lax.broadcasted_iota(jnp.int32, (n_t, n_t), 1)).astype(BF16)
    c1 = jnp.dot(before, oh1, preferred_element_type=F32)
    c2 = jnp.dot(before, oh2, preferred_element_type=F32)
    tot1 = jnp.sum(oh1.astype(F32), axis=0, keepdims=True)
    cnt = tot1 + jnp.sum(oh2.astype(F32), axis=0, keepdims=True)
    cnt = jnp.floor((cnt + (SUBLANES - 1)) * (1.0 / SUBLANES)) * SUBLANES
    lanes_before = (lax.broadcasted_iota(jnp.int32, (ROUTE_LANES, ROUTE_LANES), 0)
                    < lax.broadcasted_iota(jnp.int32, (ROUTE_LANES, ROUTE_LANES), 1)).astype(F32)
    loff = jnp.dot(jnp.broadcast_to(cnt, (SUBLANES, ROUTE_LANES)), lanes_before,
                   preferred_element_type=F32, precision=lax.Precision.HIGHEST)[0:1, :]
    lp1 = jnp.sum(oh1.astype(F32) * (loff + c1), axis=-1, keepdims=True)
    lp2 = jnp.sum(oh2.astype(F32) * (loff + tot1 + c2), axis=-1, keepdims=True)
    route = jnp.where(lane == 0, e1.astype(F32),
            jnp.where(lane == 1, e2.astype(F32),
            jnp.where(lane == 2, w1,
            jnp.where(lane == 3, w2,
            jnp.where(lane == 4, lp1, jnp.where(lane == 5, lp2, 0.0))))))
    route_ref[...] = route
    row = lax.broadcasted_iota(jnp.int32, (SUBLANES, ROUTE_LANES), 0)
    stats_ref[0] = jnp.where(row == 0, cnt, jnp.where(row == 1, loff, 0.0))


def _xattn_route(x1, k, v, bsz, seqlen, g_xattn, w_q, w_o, g_moe, w_rg, b_rg, w_re, b_re):
    t, d = x1.shape
    nt = seqlen // TOK_TILE
    w_r = jnp.pad(jnp.concatenate([w_rg, w_re], axis=1), ((0, 0), (0, ROUTE_LANES - N_GROUPS - N_EXPERTS)))
    b_r = jnp.pad(jnp.concatenate([b_rg, b_re]), (0, ROUTE_LANES - N_GROUPS - N_EXPERTS)).reshape(1, ROUTE_LANES)
    wr_hi = w_r.astype(BF16)
    wr_lo = (w_r - wr_hi.astype(F32)).astype(BF16)
    tile = pl.BlockSpec((TOK_TILE, d), lambda b, j: (b * nt + j, 0))
    kv_spec = pl.BlockSpec((1, MEM_LEN, d), lambda b, j: (b, 0, 0))
    const = lambda shape: pl.BlockSpec(shape, lambda b, j: (0, 0))
    return pl.pallas_call(
        _xattn_body,
        grid=(bsz, nt),
        in_specs=[tile, kv_spec, kv_spec, const((1, d)), const((d, d)), const((d, d)), const((1, d)),
                  const((d, ROUTE_LANES)), const((d, ROUTE_LANES)), const((1, ROUTE_LANES))],
        out_specs=[tile, tile, pl.BlockSpec((TOK_TILE, ROUTE_LANES), lambda b, j: (b * nt + j, 0)),
                   pl.BlockSpec((1, SUBLANES, ROUTE_LANES), lambda b, j: (b * nt + j, 0, 0))],
        out_shape=[jax.ShapeDtypeStruct((t, d), F32), jax.ShapeDtypeStruct((t, d), F32),
                   jax.ShapeDtypeStruct((t, ROUTE_LANES), F32),
                   jax.ShapeDtypeStruct((t // TOK_TILE, SUBLANES, ROUTE_LANES), F32)],
        compiler_params=_cparams(("arbitrary", "arbitrary")),
        name="xattn_route",
    )(x1, k, v, g_xattn.reshape(1, d), w_q.astype(BF16), w_o.astype(BF16), g_moe.reshape(1, d),
      wr_hi, wr_lo, b_r)


def _rows_copy(src_ref, src0, dst_ref, dst0, n, sem):
    rows = lambda r0: pl.ds(pl.multiple_of(r0, SUBLANES), pl.multiple_of(n, SUBLANES))
    return pltpu.make_async_copy(src_ref.at[rows(src0), :], dst_ref.at[rows(dst0), :], sem)


def _dispatch_body(cnt_ref, loff_ref, gb_ref, tot_ref, fstart_ref, fcnt_ref, nu_ref, h_ref, route_ref,
                   xs_ref, buf_ref, zero_ref, sem, zsem):
    i = pl.program_id(0)
    rt = route_ref[...].T
    lp1 = rt[4:5, :].astype(jnp.int32)
    lp2 = rt[5:6, :].astype(jnp.int32)
    r_i = lax.broadcasted_iota(jnp.int32, (SORT_ROWS, TOK_TILE), 0)
    perm = jnp.where((r_i == lp1) | (r_i == lp2), 1.0, 0.0).astype(BF16)
    buf_ref[...] = jnp.dot(perm, h_ref[...].astype(BF16), preferred_element_type=F32)

    def per_expert(e, c):
        j = i * N_EXPERTS + e

        @pl.when(cnt_ref[j] > 0)
        def _():
            _rows_copy(buf_ref, loff_ref[j], xs_ref, gb_ref[j], cnt_ref[j], sem).start()
        return c

    lax.fori_loop(0, N_EXPERTS, per_expert, 0)

    @pl.when(i == 0)
    def _():
        zero_ref[...] = jnp.zeros(zero_ref.shape, F32)

        def fill(start):
            def body(e, c):
                @pl.when(fcnt_ref[e] > 0)
                def _():
                    copy = _rows_copy(zero_ref, 0, xs_ref, fstart_ref[e], fcnt_ref[e], zsem)
                    copy.start() if start else copy.wait()
                return c
            lax.fori_loop(0, N_EXPERTS, body, 0)

            def tail(blk, c):
                copy = _rows_copy(zero_ref, 0, xs_ref, blk * MOE_BLOCK, MOE_BLOCK, zsem)
                copy.start() if start else copy.wait()
                return c
            lax.fori_loop(nu_ref[0], xs_ref.shape[0] // MOE_BLOCK, tail, 0)

        fill(True)
        fill(False)

    _rows_copy(buf_ref, 0, xs_ref, 0, tot_ref[i], sem).wait()


def _dispatch(h2, route, tables, n_used, n_slots):
    t, d = h2.shape
    nt = t // TOK_TILE
    return pl.pallas_call(
        _dispatch_body,
        grid_spec=pltpu.PrefetchScalarGridSpec(
            num_scalar_prefetch=7,
            grid=(nt,),
            in_specs=[
                pl.BlockSpec((TOK_TILE, d), lambda i, *_: (i, 0)),
                pl.BlockSpec((TOK_TILE, ROUTE_LANES), lambda i, *_: (i, 0)),
            ],
            out_specs=pl.BlockSpec(memory_space=pl.ANY),
            scratch_shapes=[pltpu.VMEM((SORT_ROWS, d), F32), pltpu.VMEM((MOE_BLOCK, d), F32),
                            pltpu.SemaphoreType.DMA(()), pltpu.SemaphoreType.DMA(())],
        ),
        out_shape=jax.ShapeDtypeStruct((n_slots, d), F32),
        compiler_params=_cparams(("arbitrary",)),
        name="dispatch",
    )(tables["cnt"], tables["loff"], tables["gb"], tables["tot"], tables["fill_start"],
      tables["fill_cnt"], n_used, h2, route)


def _experts_body(be_ref, nu_ref, x_ref, wg_ref, wu_ref, wd_ref, y_ref, wg_bf, wu_bf, wd_bf):
    b = pl.program_id(0)
    used = b < nu_ref[0]

    @pl.when(used & ((b == 0) | (be_ref[b] != be_ref[jnp.maximum(b - 1, 0)])))
    def _():
        wg_bf[...] = wg_ref[0].astype(BF16)
        wu_bf[...] = wu_ref[0].astype(BF16)
        wd_bf[...] = wd_ref[0].astype(BF16)

    @pl.when(used)
    def _():
        x = x_ref[...].astype(BF16)
        g = jnp.dot(x, wg_bf[...], preferred_element_type=F32)
        u = jnp.dot(x, wu_bf[...], preferred_element_type=F32)
        a = (g * _sigmoid(g) * u).astype(BF16)
        y_ref[...] = jnp.dot(a, wd_bf[...], preferred_element_type=F32)

    @pl.when(jnp.logical_not(used))
    def _():
        y_ref[...] = jnp.zeros(y_ref.shape, F32)


def _experts(xs, block_e, n_used, w_gate, w_up, w_down):
    n_slots, d = xs.shape
    n_blocks = n_slots // MOE_BLOCK
    last = lambda b, nu: jnp.minimum(b, nu[0] - 1)
    tile = pl.BlockSpec((MOE_BLOCK, d), lambda b, be, nu: (last(b, nu), 0))
    return pl.pallas_call(
        _experts_body,
        grid_spec=pltpu.PrefetchScalarGridSpec(
            num_scalar_prefetch=2,
            grid=(n_blocks,),
            in_specs=[
                tile,
                pl.BlockSpec((1, d, D_EXPERT), lambda b, be, nu: (be[last(b, nu)], 0, 0)),
                pl.BlockSpec((1, d, D_EXPERT), lambda b, be, nu: (be[last(b, nu)], 0, 0)),
                pl.BlockSpec((1, D_EXPERT, d), lambda b, be, nu: (be[last(b, nu)], 0, 0)),
            ],
            out_specs=pl.BlockSpec((MOE_BLOCK, d), lambda b, be, nu: (b, 0)),
            scratch_shapes=[pltpu.VMEM((d, D_EXPERT), BF16), pltpu.VMEM((d, D_EXPERT), BF16),
                            pltpu.VMEM((D_EXPERT, d), BF16)],
        ),
        out_shape=jax.ShapeDtypeStruct((n_slots, d), F32),
        compiler_params=_cparams(("arbitrary",)),
        name="experts",
    )(block_e, n_used, xs, w_gate, w_up, w_down)


def _combine_body(cnt_ref, loff_ref, gb_ref, tot_ref, x_ref, route_ref, g_ref, ys_ref, o_ref, ybuf_ref, sem):
    i = pl.program_id(0)

    def per_expert(e, c):
        j = i * N_EXPERTS + e

        @pl.when(cnt_ref[j] > 0)
        def _():
            _rows_copy(ys_ref, gb_ref[j], ybuf_ref, loff_ref[j], cnt_ref[j], sem).start()
        return c

    lax.fori_loop(0, N_EXPERTS, per_expert, 0)
    route = route_ref[...]
    c_i = lax.broadcasted_iota(jnp.int32, (TOK_TILE, SORT_ROWS), 1)
    lp1 = route[:, 4:5].astype(jnp.int32)
    lp2 = route[:, 5:6].astype(jnp.int32)
    pw = jnp.where(c_i == lp1, route[:, 2:3], jnp.where(c_i == lp2, route[:, 3:4], 0.0)).astype(BF16)
    tot = tot_ref[i]
    _rows_copy(ys_ref, 0, ybuf_ref, 0, tot, sem).wait()
    r_i = lax.broadcasted_iota(jnp.int32, (SORT_ROWS, 1), 0)
    y = jnp.where(r_i < tot, ybuf_ref[...], 0.0).astype(BF16)
    moe = jnp.dot(pw, y, preferred_element_type=F32)
    o_ref[...] = _rms(x_ref[...] + moe, g_ref[...])


def _combine(x2, route, tables, ys, g_final):
    t, d = x2.shape
    nt = t // TOK_TILE
    tile = pl.BlockSpec((TOK_TILE, d), lambda i, *_: (i, 0))
    return pl.pallas_call(
        _combine_body,
        grid_spec=pltpu.PrefetchScalarGridSpec(
            num_scalar_prefetch=4,
            grid=(nt,),
            in_specs=[
                tile,
                pl.BlockSpec((TOK_TILE, ROUTE_LANES), lambda i, *_: (i, 0)),
                pl.BlockSpec((1, d), lambda i, *_: (0, 0)),
                pl.BlockSpec(memory_space=pl.ANY),
            ],
            out_specs=tile,
            scratch_shapes=[pltpu.VMEM((SORT_ROWS, d), F32), pltpu.SemaphoreType.DMA(())],
        ),
        out_shape=jax.ShapeDtypeStruct((t, d), F32),
        compiler_params=_cparams(("arbitrary",)),
        name="combine",
    )(tables["cnt"], tables["loff"], tables["gb"], tables["tot"], x2, route, g_final.reshape(1, d), ys)


def _routing_tables(stats, n_tok):
    cnt = stats[:, 0, :N_EXPERTS].astype(jnp.int32)
    loff = stats[:, 1, :N_EXPERTS].astype(jnp.int32)
    n_tiles = cnt.shape[0]
    counts = jnp.sum(cnt, axis=0)
    padded = ((counts + MOE_BLOCK - 1) // MOE_BLOCK) * MOE_BLOCK
    pad_end = jnp.cumsum(padded)
    pad_start = pad_end - padded
    gb = pad_start[None, :] + jnp.cumsum(cnt, axis=0) - cnt
    max_rows = (n_tok * TOP_K + n_tiles * N_EXPERTS * (SUBLANES - 1)
                + N_EXPERTS * (MOE_BLOCK - SUBLANES))
    n_blocks = -(-max_rows // MOE_BLOCK)
    n_slots = n_blocks * MOE_BLOCK
    block_e = jnp.clip(jnp.searchsorted(pad_end, jnp.arange(n_blocks, dtype=jnp.int32) * MOE_BLOCK,
                                        side='right'), 0, N_EXPERTS - 1).astype(jnp.int32)
    tables = dict(cnt=cnt.reshape(-1), loff=loff.reshape(-1), gb=gb.reshape(-1).astype(jnp.int32),
                  tot=jnp.sum(cnt, axis=1).astype(jnp.int32),
                  fill_start=(pad_start + counts).astype(jnp.int32),
                  fill_cnt=(padded - counts).astype(jnp.int32))
    n_used = (pad_end[-1:] // MOE_BLOCK).astype(jnp.int32)
    return tables, block_e, n_used, n_slots


def kernel(x, mem, g_mix, w_in, conv_w, conv_b, ln_g, ln_b, ssd_conv_w, ssd_conv_b, dt_bias, a_log, d_skip, ssd_norm_g, w_out, g_xattn, g_mem, w_q, w_k, w_v, w_o, g_moe, w_router_group, b_router_group, w_router_expert, b_router_expert, w_gate, w_up, w_down, g_final):
    bsz, seqlen, d = x.shape
    n_tok = bsz * seqlen
    xt = x.reshape(n_tok, d)
    assert g_mix.shape[0] == 1, "the combine kernel applies the final norm: single layer only"
    for l in range(1):
        main, dt_raw = _in_proj(xt, g_mix[l], w_in[l])
        u = _conv_module(main, bsz, seqlen, conv_w[l], conv_b[l], ln_g[l], ln_b[l])
        y = _ssd(main, dt_raw, bsz, seqlen, ssd_conv_w[l], ssd_conv_b[l], dt_bias[l], a_log[l],
                 d_skip[l], ssd_norm_g[l])
        x1 = _out_proj(xt, u, y, w_out[l])
        k, v = _kv_proj(mem, g_mem[l], w_k[l], w_v[l])
        x2, h2, route, stats = _xattn_route(x1, k, v, bsz, seqlen, g_xattn[l], w_q[l], w_o[l], g_moe[l],
                                            w_router_group[l], b_router_group[l], w_router_expert[l],
                                            b_router_expert[l])
        tables, block_e, n_used, n_slots = _routing_tables(stats, n_tok)
        xs = _dispatch(h2, route, tables, n_used, n_slots)
        ys = _experts(xs, block_e, n_used, w_gate[l], w_up[l], w_down[l])
        xt = _combine(x2, route, tables, ys, g_final)
    return xt.reshape(bsz, seqlen, d)
```

```python
import functools

import jax
import jax.numpy as jnp
from jax import lax
from jax.experimental import pallas as pl
from jax.experimental.pallas import tpu as pltpu

F32 = jnp.float32
BF16 = jnp.bfloat16

D_MODEL = 1024
CONV_CH = 1024
CONV_K = 31
SSD_INNER = 1024
SSD_HEAD_DIM = 64
SSD_HEADS = 16
SSD_STATE = 128
SSD_GROUPS = 2
SSD_GROUP_W = SSD_INNER // SSD_GROUPS
SSD_CONV_K = 4
SSD_CHUNK = 128
SSD_BC = SSD_GROUPS * SSD_STATE
SSD_CONV_CH = SSD_INNER + 2 * SSD_BC
N_MAIN = 2 * CONV_CH + 2 * SSD_INNER + 2 * SSD_BC
XA_HEADS = 4
XA_HEAD_DIM = 256
MEM_LEN = 256
N_GROUPS = 4
EXPERTS_PER_GROUP = 8
N_EXPERTS = 32
TOP_K = 2
D_EXPERT = 512
MOE_BLOCK = 256
RMS_EPS = 1e-6
LN_EPS = 1e-5

LANES = 128
SUBLANES = 8
VMEM_LIMIT = 56 * 1024 * 1024

TOK_TILE = 512
CONV_TILE = 256
CONV_HALO = 32
CONV_ROWS = 128
SSD_TILE = 512
SSD_HALO = 8
ROUTE_LANES = 128
SORT_ROWS = TOP_K * TOK_TILE + N_EXPERTS * SUBLANES


def _cparams(sem):
    return pltpu.CompilerParams(dimension_semantics=sem, vmem_limit_bytes=VMEM_LIMIT)


def _rms(x, g):
    return x * lax.rsqrt(jnp.mean(x * x, axis=-1, keepdims=True) + RMS_EPS) * g


def _sigmoid(x):
    return 1.0 / (1.0 + jnp.exp(-x))


def _kv_body(m_ref, g_ref, wk_ref, wv_ref, k_ref, v_ref):
    m = _rms(m_ref[0], g_ref[...]).astype(BF16)
    k_ref[0] = jnp.dot(m, wk_ref[...], preferred_element_type=F32).astype(BF16)
    v_ref[0] = jnp.dot(m, wv_ref[...], preferred_element_type=F32).astype(BF16)


def _kv_proj(mem, g_mem, w_k, w_v):
    b, s, d = mem.shape
    w_spec = pl.BlockSpec((d, d), lambda i: (0, 0))
    kv_spec = pl.BlockSpec((1, s, d), lambda i: (i, 0, 0))
    return pl.pallas_call(
        _kv_body,
        grid=(b,),
        in_specs=[kv_spec, pl.BlockSpec((1, d), lambda i: (0, 0)), w_spec, w_spec],
        out_specs=[kv_spec, kv_spec],
        out_shape=[jax.ShapeDtypeStruct((b, s, d), BF16)] * 2,
        compiler_params=_cparams(("arbitrary",)),
        name="kv_proj",
    )(mem, g_mem.reshape(1, d), w_k.astype(BF16), w_v.astype(BF16))


def _inproj_body(x_ref, g_ref, w_ref, wdt_ref, main_ref, dt_ref, *, n_chunk):
    h = _rms(x_ref[...], g_ref[...]).astype(BF16)
    for j in range(N_MAIN // n_chunk):
        sl = slice(j * n_chunk, (j + 1) * n_chunk)
        main_ref[:, sl] = jnp.dot(h, w_ref[:, sl], preferred_element_type=F32).astype(BF16)
    dt_ref[...] = jnp.dot(h, wdt_ref[...], preferred_element_type=F32)


def _in_proj(x2d, g_mix, w_in):
    t, d = x2d.shape
    w_main = w_in[:, :N_MAIN].astype(BF16)
    w_dt = jnp.pad(w_in[:, N_MAIN:], ((0, 0), (0, LANES - SSD_HEADS))).astype(BF16)
    return pl.pallas_call(
        functools.partial(_inproj_body, n_chunk=512),
        grid=(t // TOK_TILE,),
        in_specs=[
            pl.BlockSpec((TOK_TILE, d), lambda i: (i, 0)),
            pl.BlockSpec((1, d), lambda i: (0, 0)),
            pl.BlockSpec((d, N_MAIN), lambda i: (0, 0)),
            pl.BlockSpec((d, LANES), lambda i: (0, 0)),
        ],
        out_specs=[
            pl.BlockSpec((TOK_TILE, N_MAIN), lambda i: (i, 0)),
            pl.BlockSpec((TOK_TILE, LANES), lambda i: (i, 0)),
        ],
        out_shape=[jax.ShapeDtypeStruct((t, N_MAIN), BF16), jax.ShapeDtypeStruct((t, LANES), F32)],
        compiler_params=_cparams(("arbitrary",)),
        name="in_proj",
    )(x2d, g_mix.reshape(1, d), w_main, w_dt)


def _conv_body(val_ref, gate_ref, w_ref, b_ref, lg_ref, lb_ref, o_ref, ubuf_ref, acc_ref):
    @pl.when(pl.program_id(1) == 0)
    def _():
        ubuf_ref[0:CONV_HALO, :] = jnp.zeros((CONV_HALO, CONV_CH), F32)

    val = val_ref[...].astype(F32)
    gate = gate_ref[...].astype(F32)
    ubuf_ref[CONV_HALO:CONV_HALO + CONV_TILE, :] = val * _sigmoid(gate)

    first = CONV_HALO - (CONV_K - 1)
    for cb in range(CONV_CH // LANES):
        cols = slice(cb * LANES, (cb + 1) * LANES)
        for rc in range(CONV_TILE // CONV_ROWS):
            r0 = rc * CONV_ROWS
            acc = None
            for res in range(SUBLANES):
                part = None
                for k in range(CONV_K):
                    if (first + k) % SUBLANES == res:
                        term = w_ref[k:k + 1, cols] * ubuf_ref[r0 + first + k:r0 + first + k + CONV_ROWS, cols]
                        part = term if part is None else part + term
                acc = part if acc is None else acc + part
            acc_ref[r0:r0 + CONV_ROWS, cols] = acc

    ubuf_ref[0:CONV_HALO, :] = ubuf_ref[CONV_TILE:CONV_TILE + CONV_HALO, :]

    u = acc_ref[...] + b_ref[...]
    mu = jnp.mean(u, axis=-1, keepdims=True)
    uc = u - mu
    var = jnp.mean(uc * uc, axis=-1, keepdims=True)
    y = uc * lax.rsqrt(var + LN_EPS) * lg_ref[...] + lb_ref[...]
    o_ref[...] = (y * _sigmoid(y)).astype(BF16)


def _conv_module(main, bsz, seqlen, conv_w, conv_b, ln_g, ln_b):
    nt = seqlen // CONV_TILE
    row = lambda v: v.reshape(1, CONV_CH)
    vec_spec = pl.BlockSpec((1, CONV_CH), lambda b, j: (0, 0))
    return pl.pallas_call(
        _conv_body,
        grid=(bsz, nt),
        in_specs=[
            pl.BlockSpec((CONV_TILE, CONV_CH), lambda b, j: (b * nt + j, 0)),
            pl.BlockSpec((CONV_TILE, CONV_CH), lambda b, j: (b * nt + j, 1)),
            pl.BlockSpec((CONV_K + 1, CONV_CH), lambda b, j: (0, 0)),
            vec_spec, vec_spec, vec_spec,
        ],
        out_specs=pl.BlockSpec((CONV_TILE, CONV_CH), lambda b, j: (b * nt + j, 0)),
        out_shape=jax.ShapeDtypeStruct((bsz * seqlen, CONV_CH), BF16),
        scratch_shapes=[
            pltpu.VMEM((CONV_HALO + CONV_TILE, CONV_CH), F32),
            pltpu.VMEM((CONV_TILE, CONV_CH), F32),
        ],
        compiler_params=_cparams(("arbitrary", "arbitrary")),
        name="conv_module",
    )(main, main, jnp.pad(conv_w, ((0, 1), (0, 0))), row(conv_b), row(ln_g), row(ln_b))


def _ssd_body(xbc_ref, z_ref, dt_ref, cw_ref, cb_ref, dtb_ref, alog_ref, dsk_ref, ng_ref,
              expand_ref, y_ref, xbuf_ref, act_ref, dts_ref, state_ref):
    @pl.when(pl.program_id(1) == 0)
    def _():
        xbuf_ref[0:SSD_HALO, :] = jnp.zeros((SSD_HALO, SSD_CONV_CH), F32)
        state_ref[...] = jnp.zeros(state_ref.shape, F32)

    xbuf_ref[SSD_HALO:SSD_HALO + SSD_TILE, :] = xbc_ref[...].astype(F32)
    first = SSD_HALO - (SSD_CONV_K - 1)
    conv = cb_ref[...] + cw_ref[0:1, :] * xbuf_ref[first:first + SSD_TILE, :]
    for k in range(1, SSD_CONV_K):
        conv = conv + cw_ref[k:k + 1, :] * xbuf_ref[first + k:first + k + SSD_TILE, :]
    act_ref[...] = conv * _sigmoid(conv)
    xbuf_ref[0:SSD_HALO, :] = xbuf_ref[SSD_TILE:SSD_TILE + SSD_HALO, :]

    dt_in = dt_ref[...] + dtb_ref[...]
    dts_ref[...] = jnp.maximum(dt_in, 0.0) + jnp.log1p(jnp.exp(-jnp.abs(dt_in)))

    a_neg = -jnp.exp(alog_ref[...])
    q = SSD_CHUNK
    row_i = lax.broadcasted_iota(jnp.int32, (q, q), 0)
    col_i = lax.broadcasted_iota(jnp.int32, (q, q), 1)
    causal = row_i >= col_i
    tril = causal.astype(F32)
    lane_i = lax.broadcasted_iota(jnp.int32, (q, LANES), 1)
    low_half = lane_i < SSD_HEAD_DIM
    expand = expand_ref[...]

    def chunk(c, carry):
        r0 = pl.multiple_of(c * q, q)
        rows = pl.ds(r0, q)
        dtc = dts_ref[rows, :]
        a_cs = jnp.dot(tril, dtc * a_neg, preferred_element_type=F32,
                       precision=lax.Precision.HIGHEST)
        a_cs_t = a_cs.T
        dt_t = dtc.T
        a_end = a_cs[q - 1:q, :]
        e_exp = jnp.dot(jnp.exp(a_cs).astype(BF16), expand, preferred_element_type=F32)
        w_exp = jnp.dot((jnp.exp(a_end - a_cs) * dtc).astype(BF16), expand,
                        preferred_element_type=F32)
        dec_row = e_exp[q - 1:q, :]
        xc = act_ref[rows, 0:SSD_INNER]
        xw = (xc * w_exp).astype(BF16)
        y_parts = []
        for g in range(SSD_GROUPS):
            b_f = act_ref[rows, SSD_INNER + g * SSD_STATE:SSD_INNER + (g + 1) * SSD_STATE]
            c_f = act_ref[rows, SSD_INNER + SSD_BC + g * SSD_STATE:
                          SSD_INNER + SSD_BC + (g + 1) * SSD_STATE]
            b_g = b_f.astype(BF16)
            c_g = c_f.astype(BF16)
            cb = lax.dot_general(c_g, b_g, (((1,), (1,)), ((), ())),
                                 preferred_element_type=F32)
            gcols = slice(g * SSD_GROUP_W, (g + 1) * SSD_GROUP_W)
            st = state_ref[g]
            y_off = jnp.dot(c_g, st.astype(BF16), preferred_element_type=F32)
            state_ref[g] = st * dec_row[:, gcols] + jnp.dot(
                b_f.T.astype(BF16), xw[:, gcols], preferred_element_type=F32)
            for pair in range(SSD_GROUP_W // LANES):
                ms = []
                for hh in range(2):
                    h = g * (SSD_HEADS // SSD_GROUPS) + 2 * pair + hh
                    seg = a_cs[:, h:h + 1] - a_cs_t[h:h + 1, :]
                    dec = jnp.exp(jnp.where(causal, seg, -jnp.inf))
                    ms.append((cb * dec * dt_t[h:h + 1, :]).astype(BF16))
                lhs = jnp.concatenate(ms, axis=1)
                xp = xc[:, g * SSD_GROUP_W + pair * LANES:g * SSD_GROUP_W + (pair + 1) * LANES]
                rhs = jnp.concatenate([jnp.where(low_half, xp, 0.0),
                                       jnp.where(low_half, 0.0, xp)], axis=0).astype(BF16)
                y_diag = jnp.dot(lhs, rhs, preferred_element_type=F32)
                lo = pair * LANES
                y_parts.append(y_diag + y_off[:, lo:lo + LANES]
                               * e_exp[:, g * SSD_GROUP_W + lo:g * SSD_GROUP_W + lo + LANES])
        y = jnp.concatenate(y_parts, axis=1) + xc * dsk_ref[...]
        z = z_ref[rows, :].astype(F32)
        y = y * (z * _sigmoid(z))
        outs = []
        for g in range(SSD_GROUPS):
            yg = y[:, g * SSD_GROUP_W:(g + 1) * SSD_GROUP_W]
            outs.append(yg * lax.rsqrt(jnp.mean(yg * yg, axis=-1, keepdims=True) + RMS_EPS))
        y_ref[rows, :] = (jnp.concatenate(outs, axis=1) * ng_ref[...]).astype(BF16)
        return carry

    lax.fori_loop(0, SSD_TILE // q, chunk, 0)


def _ssd(main, dt_raw, bsz, seqlen, ssd_conv_w, ssd_conv_b, dt_bias, a_log, d_skip, ssd_norm_g):
    nt = seqlen // SSD_TILE
    pad_h = lambda v: jnp.pad(v, (0, LANES - SSD_HEADS)).reshape(1, LANES)
    expand = (jnp.arange(LANES)[:, None] == (jnp.arange(SSD_INNER) // SSD_HEAD_DIM)[None, :]).astype(BF16)
    const = lambda shape: pl.BlockSpec(shape, lambda b, j: (0, 0))
    xbc_col = (2 * CONV_CH + SSD_INNER) // SSD_CONV_CH
    z_col = (2 * CONV_CH) // SSD_INNER
    return pl.pallas_call(
        _ssd_body,
        grid=(bsz, nt),
        in_specs=[
            pl.BlockSpec((SSD_TILE, SSD_CONV_CH), lambda b, j: (b * nt + j, xbc_col)),
            pl.BlockSpec((SSD_TILE, SSD_INNER), lambda b, j: (b * nt + j, z_col)),
            pl.BlockSpec((SSD_TILE, LANES), lambda b, j: (b * nt + j, 0)),
            const((SSD_CONV_K, SSD_CONV_CH)),
            const((1, SSD_CONV_CH)),
            const((1, LANES)), const((1, LANES)),
            const((1, SSD_INNER)), const((1, SSD_INNER)),
            const((LANES, SSD_INNER)),
        ],
        out_specs=pl.BlockSpec((SSD_TILE, SSD_INNER), lambda b, j: (b * nt + j, 0)),
        out_shape=jax.ShapeDtypeStruct((bsz * seqlen, SSD_INNER), BF16),
        scratch_shapes=[
            pltpu.VMEM((SSD_HALO + SSD_TILE, SSD_CONV_CH), F32),
            pltpu.VMEM((SSD_TILE, SSD_CONV_CH), F32),
            pltpu.VMEM((SSD_TILE, LANES), F32),
            pltpu.VMEM((SSD_GROUPS, SSD_STATE, SSD_GROUP_W), F32),
        ],
        compiler_params=_cparams(("arbitrary", "arbitrary")),
        name="ssd",
    )(main, main, dt_raw, ssd_conv_w, ssd_conv_b.reshape(1, SSD_CONV_CH), pad_h(dt_bias),
      pad_h(a_log), jnp.repeat(d_skip, SSD_HEAD_DIM).reshape(1, SSD_INNER),
      ssd_norm_g.reshape(1, SSD_INNER), expand)


def _outproj_body(x_ref, u_ref, y_ref, wu_ref, wy_ref, o_ref):
    o_ref[...] = (x_ref[...]
                  + jnp.dot(u_ref[...], wu_ref[...], preferred_element_type=F32)
                  + jnp.dot(y_ref[...], wy_ref[...], preferred_element_type=F32))


def _out_proj(x2d, u, y, w_out):
    t, d = x2d.shape
    w = w_out.astype(BF16)
    tile = pl.BlockSpec((TOK_TILE, d), lambda i: (i, 0))
    w_spec = pl.BlockSpec((d, d), lambda i: (0, 0))
    return pl.pallas_call(
        _outproj_body,
        grid=(t // TOK_TILE,),
        in_specs=[tile, tile, tile, w_spec, w_spec],
        out_specs=tile,
        out_shape=jax.ShapeDtypeStruct((t, d), F32),
        compiler_params=_cparams(("arbitrary",)),
        name="out_proj",
    )(x2d, u, y, w[:CONV_CH], w[CONV_CH:])


def _xattn_body(x_ref, k_ref, v_ref, gx_ref, wq_ref, wo_ref, gm_ref, wr_hi_ref, wr_lo_ref, br_ref,
                x2_ref, h2_ref, route_ref, stats_ref):
    x = x_ref[...]
    h = _rms(x, gx_ref[...]).astype(BF16)
    q = (jnp.dot(h, wq_ref[...], preferred_element_type=F32) * (XA_HEAD_DIM ** -0.5)).astype(BF16)
    heads = []
    for i in range(XA_HEADS):
        cols = slice(i * XA_HEAD_DIM, (i + 1) * XA_HEAD_DIM)
        s = lax.dot_general(q[:, cols], k_ref[0, :, cols], (((1,), (1,)), ((), ())),
                            preferred_element_type=F32)
        p = jnp.exp(s - jnp.max(s, axis=-1, keepdims=True))
        p = p / jnp.sum(p, axis=-1, keepdims=True)
        heads.append(jnp.dot(p.astype(BF16), v_ref[0, :, cols], preferred_element_type=F32))
    o = jnp.concatenate(heads, axis=1).astype(BF16)
    x2 = x + jnp.dot(o, wo_ref[...], preferred_element_type=F32)
    x2_ref[...] = x2

    h2 = _rms(x2, gm_ref[...])
    h2_ref[...] = h2
    h_hi = h2.astype(BF16)
    h_lo = (h2 - h_hi.astype(F32)).astype(BF16)
    logits = (jnp.dot(h_hi, wr_hi_ref[...], preferred_element_type=F32)
              + jnp.dot(h_lo, wr_hi_ref[...], preferred_element_type=F32)
              + jnp.dot(h_hi, wr_lo_ref[...], preferred_element_type=F32)) + br_ref[...]
    lane = lax.broadcasted_iota(jnp.int32, logits.shape, 1)
    neg = -jnp.inf

    def first_argmax(v):
        m = jnp.max(v, axis=-1, keepdims=True)
        return m, jnp.min(jnp.where(v == m, lane, ROUTE_LANES), axis=-1, keepdims=True)

    gl = jnp.where(lane < N_GROUPS, logits, neg)
    g_max, g_sel = first_argmax(gl)
    p_top = 1.0 / jnp.sum(jnp.exp(gl - g_max), axis=-1, keepdims=True)
    e_lo = N_GROUPS + EXPERTS_PER_GROUP * g_sel
    el = jnp.where((lane >= e_lo) & (lane < e_lo + EXPERTS_PER_GROUP), logits, neg)
    m1, i1 = first_argmax(el)
    m2, i2 = first_argmax(jnp.where(lane == i1, neg, el))
    r = jnp.exp(m2 - m1)
    w1 = p_top / (1.0 + r)
    w2 = w1 * r
    e1 = i1 - N_GROUPS
    e2 = i2 - N_GROUPS
    oh1 = (lane == e1).astype(BF16)
    oh2 = (lane == e2).astype(BF16)
    n_t = logits.shape[0]
    before = (lax.broadcasted_iota(jnp.int32, (n_t, n_t), 0)
              > lax.broadcasted_iota(jnp.int32, (n_t, n_t), 1)).astype(BF16)
    c1 = jnp.dot(before, oh1, preferred_element_type=F32)
    c2 = jnp.dot(before, oh2, preferred_element_type=F32)
    tot1 = jnp.sum(oh1.astype(F32), axis=0, keepdims=True)
    cnt = tot1 + jnp.sum(oh2.astype(F32), axis=0, keepdims=True)
    cnt = jnp.floor((cnt + (SUBLANES - 1)) * (1.0 / SUBLANES)) * SUBLANES
    lanes_before = (lax.broadcasted_iota(jnp.int32, (ROUTE_LANES, ROUTE_LANES), 0)
                    < lax.broadcasted_iota(jnp.int32, (ROUTE_LANES, ROUTE_LANES), 1)).astype(F32)
    loff = jnp.dot(jnp.broadcast_to(cnt, (SUBLANES, ROUTE_LANES)), lanes_before,
                   preferred_element_type=F32, precision=lax.Precision.HIGHEST)[0:1, :]
    lp1 = jnp.sum(oh1.astype(F32) * (loff + c1), axis=-1, keepdims=True)
    lp2 = jnp.sum(oh2.astype(F32) * (loff + tot1 + c2), axis=-1, keepdims=True)
    route = jnp.where(lane == 0, e1.astype(F32),
            jnp.where(lane == 1, e2.astype(F32),
            jnp.where(lane == 2, w1,
            jnp.where(lane == 3, w2,
            jnp.where(lane == 4, lp1, jnp.where(lane == 5, lp2, 0.0))))))
    route_ref[...] = route
    row = lax.broadcasted_iota(jnp.int32, (SUBLANES, ROUTE_LANES), 0)
    stats_ref[0] = jnp.where(row == 0, cnt, jnp.where(row == 1, loff, 0.0))


def _xattn_route(x1, k, v, bsz, seqlen, g_xattn, w_q, w_o, g_moe, w_rg, b_rg, w_re, b_re):
    t, d = x1.shape
    nt = seqlen // TOK_TILE
    w_r = jnp.pad(jnp.concatenate([w_rg, w_re], axis=1), ((0, 0), (0, ROUTE_LANES - N_GROUPS - N_EXPERTS)))
    b_r = jnp.pad(jnp.concatenate([b_rg, b_re]), (0, ROUTE_LANES - N_GROUPS - N_EXPERTS)).reshape(1, ROUTE_LANES)
    wr_hi = w_r.astype(BF16)
    wr_lo = (w_r - wr_hi.astype(F32)).astype(BF16)
    tile = pl.BlockSpec((TOK_TILE, d), lambda b, j: (b * nt + j, 0))
    kv_spec = pl.BlockSpec((1, MEM_LEN, d), lambda b, j: (b, 0, 0))
    const = lambda shape: pl.BlockSpec(shape, lambda b, j: (0, 0))
    return pl.pallas_call(
        _xattn_body,
        grid=(bsz, nt),
        in_specs=[tile, kv_spec, kv_spec, const((1, d)), const((d, d)), const((d, d)), const((1, d)),
                  const((d, ROUTE_LANES)), const((d, ROUTE_LANES)), const((1, ROUTE_LANES))],
        out_specs=[tile, tile, pl.BlockSpec((TOK_TILE, ROUTE_LANES), lambda b, j: (b * nt + j, 0)),
                   pl.BlockSpec((1, SUBLANES, ROUTE_LANES), lambda b, j: (b * nt + j, 0, 0))],
        out_shape=[jax.ShapeDtypeStruct((t, d), F32), jax.ShapeDtypeStruct((t, d), F32),
                   jax.ShapeDtypeStruct((t, ROUTE_LANES), F32),
                   jax.ShapeDtypeStruct((t // TOK_TILE, SUBLANES, ROUTE_LANES), F32)],
        compiler_params=_cparams(("arbitrary", "arbitrary")),
        name="xattn_route",
    )(x1, k, v, g_xattn.reshape(1, d), w_q.astype(BF16), w_o.astype(BF16), g_moe.reshape(1, d),
      wr_hi, wr_lo, b_r)


def _rows_copy(src_ref, src0, dst_ref, dst0, n, sem):
    rows = lambda r0: pl.ds(pl.multiple_of(r0, SUBLANES), pl.multiple_of(n, SUBLANES))
    return pltpu.make_async_copy(src_ref.at[rows(src0), :], dst_ref.at[rows(dst0), :], sem)


def _dispatch_body(cnt_ref, loff_ref, gb_ref, tot_ref, fstart_ref, fcnt_ref, nu_ref, h_ref, route_ref,
                   xs_ref, buf_ref, zero_ref, sem, zsem):
    i = pl.program_id(0)
    rt = route_ref[...].T
    lp1 = rt[4:5, :].astype(jnp.int32)
    lp2 = rt[5:6, :].astype(jnp.int32)
    r_i = lax.broadcasted_iota(jnp.int32, (SORT_ROWS, TOK_TILE), 0)
    perm = jnp.where((r_i == lp1) | (r_i == lp2), 1.0, 0.0).astype(BF16)
    buf_ref[...] = jnp.dot(perm, h_ref[...].astype(BF16), preferred_element_type=F32)

    def per_expert(e, c):
        j = i * N_EXPERTS + e

        @pl.when(cnt_ref[j] > 0)
        def _():
            _rows_copy(buf_ref, loff_ref[j], xs_ref, gb_ref[j], cnt_ref[j], sem).start()
        return c

    lax.fori_loop(0, N_EXPERTS, per_expert, 0)

    @pl.when(i == 0)
    def _():
        zero_ref[...] = jnp.zeros(zero_ref.shape, F32)

        def fill(start):
            def body(e, c):
                @pl.when(fcnt_ref[e] > 0)
                def _():
                    copy = _rows_copy(zero_ref, 0, xs_ref, fstart_ref[e], fcnt_ref[e], zsem)
                    copy.start() if start else copy.wait()
                return c
            lax.fori_loop(0, N_EXPERTS, body, 0)

            def tail(blk, c):
                copy = _rows_copy(zero_ref, 0, xs_ref, blk * MOE_BLOCK, MOE_BLOCK, zsem)
                copy.start() if start else copy.wait()
                return c
            lax.fori_loop(nu_ref[0], xs_ref.shape[0] // MOE_BLOCK, tail, 0)

        fill(True)
        fill(False)

    _rows_copy(buf_ref, 0, xs_ref, 0, tot_ref[i], sem).wait()


def _dispatch(h2, route, tables, n_used, n_slots):
    t, d = h2.shape
    nt = t // TOK_TILE
    return pl.pallas_call(
        _dispatch_body,
        grid_spec=pltpu.PrefetchScalarGridSpec(
            num_scalar_prefetch=7,
            grid=(nt,),
            in_specs=[
                pl.BlockSpec((TOK_TILE, d), lambda i, *_: (i, 0)),
                pl.BlockSpec((TOK_TILE, ROUTE_LANES), lambda i, *_: (i, 0)),
            ],
            out_specs=pl.BlockSpec(memory_space=pl.ANY),
            scratch_shapes=[pltpu.VMEM((SORT_ROWS, d), F32), pltpu.VMEM((MOE_BLOCK, d), F32),
                            pltpu.SemaphoreType.DMA(()), pltpu.SemaphoreType.DMA(())],
        ),
        out_shape=jax.ShapeDtypeStruct((n_slots, d), F32),
        compiler_params=_cparams(("arbitrary",)),
        name="dispatch",
    )(tables["cnt"], tables["loff"], tables["gb"], tables["tot"], tables["fill_start"],
      tables["fill_cnt"], n_used, h2, route)


def _experts_body(be_ref, nu_ref, x_ref, wg_ref, wu_ref, wd_ref, y_ref, wg_bf, wu_bf, wd_bf):
    b = pl.program_id(0)
    used = b < nu_ref[0]

    @pl.when(used & ((b == 0) | (be_ref[b] != be_ref[jnp.maximum(b - 1, 0)])))
    def _():
        wg_bf[...] = wg_ref[0].astype(BF16)
        wu_bf[...] = wu_ref[0].astype(BF16)
        wd_bf[...] = wd_ref[0].astype(BF16)

    @pl.when(used)
    def _():
        x = x_ref[...].astype(BF16)
        g = jnp.dot(x, wg_bf[...], preferred_element_type=F32)
        u = jnp.dot(x, wu_bf[...], preferred_element_type=F32)
        a = (g * _sigmoid(g) * u).astype(BF16)
        y_ref[...] = jnp.dot(a, wd_bf[...], preferred_element_type=F32)

    @pl.when(jnp.logical_not(used))
    def _():
        y_ref[...] = jnp.zeros(y_ref.shape, F32)


def _experts(xs, block_e, n_used, w_gate, w_up, w_down):
    n_slots, d = xs.shape
    n_blocks = n_slots // MOE_BLOCK
    last = lambda b, nu: jnp.maximum(jnp.minimum(b, nu[0] - 1), 0)
    tile = pl.BlockSpec((MOE_BLOCK, d), lambda b, be, nu: (last(b, nu), 0))
    return pl.pallas_call(
        _experts_body,
        grid_spec=pltpu.PrefetchScalarGridSpec(
            num_scalar_prefetch=2,
            grid=(n_blocks,),
            in_specs=[
                tile,
                pl.BlockSpec((1, d, D_EXPERT), lambda b, be, nu: (be[last(b, nu)], 0, 0)),
                pl.BlockSpec((1, d, D_EXPERT), lambda b, be, nu: (be[last(b, nu)], 0, 0)),
                pl.BlockSpec((1, D_EXPERT, d), lambda b, be, nu: (be[last(b, nu)], 0, 0)),
            ],
            out_specs=pl.BlockSpec((MOE_BLOCK, d), lambda b, be, nu: (b, 0)),
            scratch_shapes=[pltpu.VMEM((d, D_EXPERT), BF16), pltpu.VMEM((d, D_EXPERT), BF16),
                            pltpu.VMEM((D_EXPERT, d), BF16)],
        ),
        out_shape=jax.ShapeDtypeStruct((n_slots, d), F32),
        compiler_params=_cparams(("arbitrary",)),
        name="experts",
    )(block_e, n_used, xs, w_gate, w_up, w_down)


def _combine_body(cnt_ref, loff_ref, gb_ref, tot_ref, x_ref, route_ref, g_ref, ys_ref, o_ref, ybuf_ref, sem):
    i = pl.program_id(0)

    def per_expert(e, c):
        j = i * N_EXPERTS + e

        @pl.when(cnt_ref[j] > 0)
        def _():
            _rows_copy(ys_ref, gb_ref[j], ybuf_ref, loff_ref[j], cnt_ref[j], sem).start()
        return c

    lax.fori_loop(0, N_EXPERTS, per_expert, 0)
    route = route_ref[...]
    c_i = lax.broadcasted_iota(jnp.int32, (TOK_TILE, SORT_ROWS), 1)
    lp1 = route[:, 4:5].astype(jnp.int32)
    lp2 = route[:, 5:6].astype(jnp.int32)
    pw = jnp.where(c_i == lp1, route[:, 2:3], jnp.where(c_i == lp2, route[:, 3:4], 0.0)).astype(BF16)
    tot = tot_ref[i]
    _rows_copy(ys_ref, 0, ybuf_ref, 0, tot, sem).wait()
    r_i = lax.broadcasted_iota(jnp.int32, (SORT_ROWS, 1), 0)
    y = jnp.where(r_i < tot, ybuf_ref[...], 0.0).astype(BF16)
    moe = jnp.dot(pw, y, preferred_element_type=F32)
    o_ref[...] = _rms(x_ref[...] + moe, g_ref[...])


def _combine(x2, route, tables, ys, g_final):
    t, d = x2.shape
    nt = t // TOK_TILE
    tile = pl.BlockSpec((TOK_TILE, d), lambda i, *_: (i, 0))
    return pl.pallas_call(
        _combine_body,
        grid_spec=pltpu.PrefetchScalarGridSpec(
            num_scalar_prefetch=4,
            grid=(nt,),
            in_specs=[
                tile,
                pl.BlockSpec((TOK_TILE, ROUTE_LANES), lambda i, *_: (i, 0)),
                pl.BlockSpec((1, d), lambda i, *_: (0, 0)),
                pl.BlockSpec(memory_space=pl.ANY),
            ],
            out_specs=tile,
            scratch_shapes=[pltpu.VMEM((SORT_ROWS, d), F32), pltpu.SemaphoreType.DMA(())],
        ),
        out_shape=jax.ShapeDtypeStruct((t, d), F32),
        compiler_params=_cparams(("arbitrary",)),
        name="combine",
    )(tables["cnt"], tables["loff"], tables["gb"], tables["tot"], x2, route, g_final.reshape(1, d), ys)


def _routing_tables(stats, n_tok):
    cnt = stats[:, 0, :N_EXPERTS].astype(jnp.int32)
    loff = stats[:, 1, :N_EXPERTS].astype(jnp.int32)
    n_tiles = cnt.shape[0]
    counts = jnp.sum(cnt, axis=0)
    padded = ((counts + MOE_BLOCK - 1) // MOE_BLOCK) * MOE_BLOCK
    pad_end = jnp.cumsum(padded)
    pad_start = pad_end - padded
    gb = pad_start[None, :] + jnp.cumsum(cnt, axis=0) - cnt
    max_rows = (n_tok * TOP_K + n_tiles * N_EXPERTS * (SUBLANES - 1)
                + N_EXPERTS * (MOE_BLOCK - SUBLANES))
    n_blocks = -(-max_rows // MOE_BLOCK)
    n_slots = n_blocks * MOE_BLOCK
    block_start = jnp.arange(n_blocks, dtype=jnp.int32) * MOE_BLOCK
    block_e = jnp.minimum(jnp.sum((pad_end[None, :] <= block_start[:, None]).astype(jnp.int32), axis=1),
                          N_EXPERTS - 1)
    tables = dict(cnt=cnt.reshape(-1), loff=loff.reshape(-1), gb=gb.reshape(-1).astype(jnp.int32),
                  tot=jnp.sum(cnt, axis=1).astype(jnp.int32),
                  fill_start=(pad_start + counts).astype(jnp.int32),
                  fill_cnt=(padded - counts).astype(jnp.int32))
    n_used = (pad_end[-1:] // MOE_BLOCK).astype(jnp.int32)
    return tables, block_e, n_used, n_slots


def kernel(x, mem, g_mix, w_in, conv_w, conv_b, ln_g, ln_b, ssd_conv_w, ssd_conv_b, dt_bias, a_log, d_skip, ssd_norm_g, w_out, g_xattn, g_mem, w_q, w_k, w_v, w_o, g_moe, w_router_group, b_router_group, w_router_expert, b_router_expert, w_gate, w_up, w_down, g_final):
    bsz, seqlen, d = x.shape
    n_tok = bsz * seqlen
    xt = x.reshape(n_tok, d)
    assert g_mix.shape[0] == 1, "the combine kernel applies the final norm: single layer only"
    for l in range(1):
        main, dt_raw = _in_proj(xt, g_mix[l], w_in[l])
        u = _conv_module(main, bsz, seqlen, conv_w[l], conv_b[l], ln_g[l], ln_b[l])
        y = _ssd(main, dt_raw, bsz, seqlen, ssd_conv_w[l], ssd_conv_b[l], dt_bias[l], a_log[l],
                 d_skip[l], ssd_norm_g[l])
        x1 = _out_proj(xt, u, y, w_out[l])
        k, v = _kv_proj(mem, g_mem[l], w_k[l], w_v[l])
        x2, h2, route, stats = _xattn_route(x1, k, v, bsz, seqlen, g_xattn[l], w_q[l], w_o[l], g_moe[l],
                                            w_router_group[l], b_router_group[l], w_router_expert[l],
                                            b_router_expert[l])
        tables, block_e, n_used, n_slots = _routing_tables(stats, n_tok)
        xs = _dispatch(h2, route, tables, n_used, n_slots)
        ys = _experts(xs, block_e, n_used, w_gate[l], w_up[l], w_down[l])
        xt = _combine(x2, route, tables, ys, g_final)
    return xt.reshape(bsz, seqlen, d)
```

```python
import functools

import jax
import jax.numpy as jnp
from jax import lax
from jax.experimental import pallas as pl
from jax.experimental.pallas import tpu as pltpu

F32 = jnp.float32
BF16 = jnp.bfloat16

D_MODEL = 1024
CONV_CH = 1024
CONV_K = 31
SSD_INNER = 1024
SSD_HEAD_DIM = 64
SSD_HEADS = 16
SSD_STATE = 128
SSD_GROUPS = 2
SSD_GROUP_W = SSD_INNER // SSD_GROUPS
SSD_CONV_K = 4
SSD_CHUNK = 128
SSD_BC = SSD_GROUPS * SSD_STATE
SSD_CONV_CH = SSD_INNER + 2 * SSD_BC
N_MAIN = 2 * CONV_CH + 2 * SSD_INNER + 2 * SSD_BC
XA_HEADS = 4
XA_HEAD_DIM = 256
MEM_LEN = 256
N_GROUPS = 4
EXPERTS_PER_GROUP = 8
N_EXPERTS = 32
TOP_K = 2
D_EXPERT = 512
MOE_BLOCK = 512
RMS_EPS = 1e-6
LN_EPS = 1e-5

LANES = 128
SUBLANES = 8
VMEM_LIMIT = 56 * 1024 * 1024

TOK_TILE = 512
CONV_TILE = 512
MM_CHUNK = 512
CONV_HALO = 32
CONV_ROWS = 128
SSD_TILE = 512
SSD_HALO = 8
ROUTE_LANES = 128
SORT_ROWS = TOP_K * TOK_TILE + N_EXPERTS * SUBLANES


def _cparams(sem):
    return pltpu.CompilerParams(dimension_semantics=sem, vmem_limit_bytes=VMEM_LIMIT)


def _rms(x, g):
    return x * lax.rsqrt(jnp.mean(x * x, axis=-1, keepdims=True) + RMS_EPS) * g


def _sigmoid(x):
    return 1.0 / (1.0 + jnp.exp(-x))


def _kv_body(m_ref, g_ref, wk_ref, wv_ref, k_ref, v_ref):
    m = _rms(m_ref[0], g_ref[...]).astype(BF16)
    k_ref[0] = jnp.dot(m, wk_ref[...], preferred_element_type=F32).astype(BF16)
    v_ref[0] = jnp.dot(m, wv_ref[...], preferred_element_type=F32).astype(BF16)


def _kv_proj(mem, g_mem, w_k, w_v):
    b, s, d = mem.shape
    w_spec = pl.BlockSpec((d, d), lambda i: (0, 0))
    kv_spec = pl.BlockSpec((1, s, d), lambda i: (i, 0, 0))
    return pl.pallas_call(
        _kv_body,
        grid=(b,),
        in_specs=[kv_spec, pl.BlockSpec((1, d), lambda i: (0, 0)), w_spec, w_spec],
        out_specs=[kv_spec, kv_spec],
        out_shape=[jax.ShapeDtypeStruct((b, s, d), BF16)] * 2,
        compiler_params=_cparams(("arbitrary",)),
        name="kv_proj",
    )(mem, g_mem.reshape(1, d), w_k.astype(BF16), w_v.astype(BF16))


def _mix_in_body(x_ref, g_ref, w_ref, wdt_ref, cw_ref, cb_ref, lg_ref, lb_ref,
                 u_ref, xbc_ref, z_ref, dt_ref, ubuf_ref, acc_ref):
    @pl.when(pl.program_id(1) == 0)
    def _():
        ubuf_ref[0:CONV_HALO, :] = jnp.zeros((CONV_HALO, CONV_CH), F32)

    h = _rms(x_ref[...], g_ref[...]).astype(BF16)
    proj = lambda lo: jnp.dot(h, w_ref[:, lo:lo + MM_CHUNK], preferred_element_type=F32)
    for c in range(CONV_CH // MM_CHUNK):
        cols = slice(c * MM_CHUNK, (c + 1) * MM_CHUNK)
        ubuf_ref[CONV_HALO:CONV_HALO + CONV_TILE, cols] = (
            proj(c * MM_CHUNK) * _sigmoid(proj(CONV_CH + c * MM_CHUNK)))
    z0 = 2 * CONV_CH
    x0 = z0 + SSD_INNER

    def put(ref, c, lo):
        def run():
            ref[:, c * MM_CHUNK:(c + 1) * MM_CHUNK] = proj(lo + c * MM_CHUNK).astype(BF16)
        return run

    def put_dt():
        dt_ref[...] = jnp.dot(h, wdt_ref[...], preferred_element_type=F32)

    jobs =([put(z_ref, c, z0) for c in range(SSD_INNER // MM_CHUNK)]
            + [put(xbc_ref, c, x0) for c in range(SSD_CONV_CH // MM_CHUNK)] + [put_dt])
    n_rc = CONV_TILE // CONV_ROWS

    first = CONV_HALO - (CONV_K - 1)
    for rc in range(n_rc):
        for job in jobs[rc::n_rc]:
            job()
        r0 = rc * CONV_ROWS
        rows = slice(r0, r0 + CONV_ROWS)
        for cb in range(CONV_CH // LANES):
            cols = slice(cb * LANES, (cb + 1) * LANES)
            acc = None
            for res in range(SUBLANES):
                part = None
                for k in range(CONV_K):
                    if (first + k) % SUBLANES == res:
                        term = cw_ref[k:k + 1, cols] * ubuf_ref[r0 + first + k:r0 + first + k + CONV_ROWS, cols]
                        part = term if part is None else part + term
                acc = part if acc is None else acc + part
            acc_ref[rows, cols] = acc
        u = acc_ref[rows, :] + cb_ref[...]
        mu = jnp.mean(u, axis=-1, keepdims=True)
        uc = u - mu
        var = jnp.mean(uc * uc, axis=-1, keepdims=True)
        y = uc * lax.rsqrt(var + LN_EPS) * lg_ref[...] + lb_ref[...]
        u_ref[rows, :] = (y * _sigmoid(y)).astype(BF16)

    ubuf_ref[0:CONV_HALO, :] = ubuf_ref[CONV_TILE:CONV_TILE + CONV_HALO, :]


def _mix_in(x2d, bsz, seqlen, g_mix, w_in, conv_w, conv_b, ln_g, ln_b):
    t, d = x2d.shape
    nt = seqlen // CONV_TILE
    w_main = w_in[:, :N_MAIN].astype(BF16)
    w_dt = jnp.pad(w_in[:, N_MAIN:], ((0, 0), (0, LANES - SSD_HEADS))).astype(BF16)
    row = lambda v: v.reshape(1, -1)
    const = lambda shape: pl.BlockSpec(shape, lambda b, j: (0, 0))
    tile = lambda width: pl.BlockSpec((CONV_TILE, width), lambda b, j: (b * nt + j, 0))
    return pl.pallas_call(
        _mix_in_body,
        grid=(bsz, nt),
        in_specs=[
            tile(d), const((1, d)), const((d, N_MAIN)), const((d, LANES)),
            const((CONV_K + 1, CONV_CH)), const((1, CONV_CH)), const((1, CONV_CH)), const((1, CONV_CH)),
        ],
        out_specs=[tile(CONV_CH), tile(SSD_CONV_CH), tile(SSD_INNER), tile(LANES)],
        out_shape=[jax.ShapeDtypeStruct((t, CONV_CH), BF16), jax.ShapeDtypeStruct((t, SSD_CONV_CH), BF16),
                   jax.ShapeDtypeStruct((t, SSD_INNER), BF16), jax.ShapeDtypeStruct((t, LANES), F32)],
        scratch_shapes=[
            pltpu.VMEM((CONV_HALO + CONV_TILE, CONV_CH), F32),
            pltpu.VMEM((CONV_TILE, CONV_CH), F32),
        ],
        compiler_params=_cparams(("arbitrary", "arbitrary")),
        name="mix_in",
    )(x2d, row(g_mix), w_main, w_dt, jnp.pad(conv_w, ((0, 1), (0, 0))), row(conv_b), row(ln_g), row(ln_b))


def _ssd_body(xbc_ref, z_ref, dt_ref, cw_ref, cb_ref, dtb_ref, alog_ref, dsk_ref, ng_ref,
              expand_ref, y_ref, xbuf_ref, act_ref, dts_ref, state_ref):
    @pl.when(pl.program_id(1) == 0)
    def _():
        xbuf_ref[0:SSD_HALO, :] = jnp.zeros((SSD_HALO, SSD_CONV_CH), F32)
        state_ref[...] = jnp.zeros(state_ref.shape, F32)

    xbuf_ref[SSD_HALO:SSD_HALO + SSD_TILE, :] = xbc_ref[...].astype(F32)
    first = SSD_HALO - (SSD_CONV_K - 1)
    conv = cb_ref[...] + cw_ref[0:1, :] * xbuf_ref[first:first + SSD_TILE, :]
    for k in range(1, SSD_CONV_K):
        conv = conv + cw_ref[k:k + 1, :] * xbuf_ref[first + k:first + k + SSD_TILE, :]
    act_ref[...] = conv * _sigmoid(conv)
    xbuf_ref[0:SSD_HALO, :] = xbuf_ref[SSD_TILE:SSD_TILE + SSD_HALO, :]

    dt_in = dt_ref[...] + dtb_ref[...]
    dts_ref[...] = jnp.maximum(dt_in, 0.0) + jnp.log1p(jnp.exp(-jnp.abs(dt_in)))

    a_neg = -jnp.exp(alog_ref[...])
    q = SSD_CHUNK
    row_i = lax.broadcasted_iota(jnp.int32, (q, q), 0)
    col_i = lax.broadcasted_iota(jnp.int32, (q, q), 1)
    causal = row_i >= col_i
    tril = causal.astype(F32)
    lane_i = lax.broadcasted_iota(jnp.int32, (q, LANES), 1)
    low_half = lane_i < SSD_HEAD_DIM
    expand = expand_ref[...]

    def chunk(c, carry):
        r0 = pl.multiple_of(c * q, q)
        rows = pl.ds(r0, q)
        dtc = dts_ref[rows, :]
        a_cs = jnp.dot(tril, dtc * a_neg, preferred_element_type=F32,
                       precision=lax.Precision.HIGHEST)
        a_cs_t = a_cs.T
        dt_t = dtc.T
        a_end = a_cs[q - 1:q, :]
        e_exp = jnp.dot(jnp.exp(a_cs).astype(BF16), expand, preferred_element_type=F32)
        w_exp = jnp.dot((jnp.exp(a_end - a_cs) * dtc).astype(BF16), expand,
                        preferred_element_type=F32)
        dec_row = e_exp[q - 1:q, :]
        xc = act_ref[rows, 0:SSD_INNER]
        xw = (xc * w_exp).astype(BF16)
        y_parts = []
        for g in range(SSD_GROUPS):
            b_f = act_ref[rows, SSD_INNER + g * SSD_STATE:SSD_INNER + (g + 1) * SSD_STATE]
            c_f = act_ref[rows, SSD_INNER + SSD_BC + g * SSD_STATE:
                          SSD_INNER + SSD_BC + (g + 1) * SSD_STATE]
            b_g = b_f.astype(BF16)
            c_g = c_f.astype(BF16)
            cb = lax.dot_general(c_g, b_g, (((1,), (1,)), ((), ())),
                                 preferred_element_type=F32)
            gcols = slice(g * SSD_GROUP_W, (g + 1) * SSD_GROUP_W)
            st = state_ref[g]
            y_off = jnp.dot(c_g, st.astype(BF16), preferred_element_type=F32)
            state_ref[g] = st * dec_row[:, gcols] + jnp.dot(
                b_f.T.astype(BF16), xw[:, gcols], preferred_element_type=F32)
            for pair in range(SSD_GROUP_W // LANES):
                ms = []
                for hh in range(2):
                    h = g * (SSD_HEADS // SSD_GROUPS) + 2 * pair + hh
                    seg = a_cs[:, h:h + 1] - a_cs_t[h:h + 1, :]
                    dec = jnp.exp(jnp.where(causal, seg, -jnp.inf))
                    ms.append((cb * dec * dt_t[h:h + 1, :]).astype(BF16))
                lhs = jnp.concatenate(ms, axis=1)
                xp = xc[:, g * SSD_GROUP_W + pair * LANES:g * SSD_GROUP_W + (pair + 1) * LANES]
                rhs = jnp.concatenate([jnp.where(low_half, xp, 0.0),
                                       jnp.where(low_half, 0.0, xp)], axis=0).astype(BF16)
                y_diag = jnp.dot(lhs, rhs, preferred_element_type=F32)
                lo = pair * LANES
                y_parts.append(y_diag + y_off[:, lo:lo + LANES]
                               * e_exp[:, g * SSD_GROUP_W + lo:g * SSD_GROUP_W + lo + LANES])
        y = jnp.concatenate(y_parts, axis=1) + xc * dsk_ref[...]
        z = z_ref[rows, :].astype(F32)
        y = y * (z * _sigmoid(z))
        outs = []
        for g in range(SSD_GROUPS):
            yg = y[:, g * SSD_GROUP_W:(g + 1) * SSD_GROUP_W]
            outs.append(yg * lax.rsqrt(jnp.mean(yg * yg, axis=-1, keepdims=True) + RMS_EPS))
        y_ref[rows, :] = (jnp.concatenate(outs, axis=1) * ng_ref[...]).astype(BF16)
        return carry

    lax.fori_loop(0, SSD_TILE // q, chunk, 0)


def _ssd(xbc, z, dt_raw, bsz, seqlen, ssd_conv_w, ssd_conv_b, dt_bias, a_log, d_skip, ssd_norm_g):
    nt = seqlen // SSD_TILE
    pad_h = lambda v: jnp.pad(v, (0, LANES - SSD_HEADS)).reshape(1, LANES)
    expand = (jnp.arange(LANES)[:, None] == (jnp.arange(SSD_INNER) // SSD_HEAD_DIM)[None, :]).astype(BF16)
    const = lambda shape: pl.BlockSpec(shape, lambda b, j: (0, 0))
    return pl.pallas_call(
        _ssd_body,
        grid=(bsz, nt),
        in_specs=[
            pl.BlockSpec((SSD_TILE, SSD_CONV_CH), lambda b, j: (b * nt + j, 0)),
            pl.BlockSpec((SSD_TILE, SSD_INNER), lambda b, j: (b * nt + j, 0)),
            pl.BlockSpec((SSD_TILE, LANES), lambda b, j: (b * nt + j, 0)),
            const((SSD_CONV_K, SSD_CONV_CH)),
            const((1, SSD_CONV_CH)),
            const((1, LANES)), const((1, LANES)),
            const((1, SSD_INNER)), const((1, SSD_INNER)),
            const((LANES, SSD_INNER)),
        ],
        out_specs=pl.BlockSpec((SSD_TILE, SSD_INNER), lambda b, j: (b * nt + j, 0)),
        out_shape=jax.ShapeDtypeStruct((bsz * seqlen, SSD_INNER), BF16),
        scratch_shapes=[
            pltpu.VMEM((SSD_HALO + SSD_TILE, SSD_CONV_CH), F32),
            pltpu.VMEM((SSD_TILE, SSD_CONV_CH), F32),
            pltpu.VMEM((SSD_TILE, LANES), F32),
            pltpu.VMEM((SSD_GROUPS, SSD_STATE, SSD_GROUP_W), F32),
        ],
        compiler_params=_cparams(("arbitrary", "arbitrary")),
        name="ssd",
    )(xbc, z, dt_raw, ssd_conv_w, ssd_conv_b.reshape(1, SSD_CONV_CH), pad_h(dt_bias),
      pad_h(a_log), jnp.repeat(d_skip, SSD_HEAD_DIM).reshape(1, SSD_INNER),
      ssd_norm_g.reshape(1, SSD_INNER), expand)


def _outproj_body(x_ref, u_ref, y_ref, wu_ref, wy_ref, o_ref):
    o_ref[...] = (x_ref[...]
                  + jnp.dot(u_ref[...], wu_ref[...], preferred_element_type=F32)
                  + jnp.dot(y_ref[...], wy_ref[...], preferred_element_type=F32))


def _out_proj(x2d, u, y, w_out):
    t, d = x2d.shape
    w = w_out.astype(BF16)
    tile = pl.BlockSpec((TOK_TILE, d), lambda i: (i, 0))
    w_spec = pl.BlockSpec((d, d), lambda i: (0, 0))
    return pl.pallas_call(
        _outproj_body,
        grid=(t // TOK_TILE,),
        in_specs=[tile, tile, tile, w_spec, w_spec],
        out_specs=tile,
        out_shape=jax.ShapeDtypeStruct((t, d), F32),
        compiler_params=_cparams(("arbitrary",)),
        name="out_proj",
    )(x2d, u, y, w[:CONV_CH], w[CONV_CH:])


def _xattn_body(x_ref, k_ref, v_ref, gx_ref, wq_ref, wo_ref, gm_ref, wr_hi_ref, wr_lo_ref, br_ref,
                x2_ref, h2_ref, route_ref, stats_ref):
    x = x_ref[...]
    h = _rms(x, gx_ref[...]).astype(BF16)
    q = (jnp.dot(h, wq_ref[...], preferred_element_type=F32) * (XA_HEAD_DIM ** -0.5)).astype(BF16)
    heads = []
    for i in range(XA_HEADS):
        cols = slice(i * XA_HEAD_DIM, (i + 1) * XA_HEAD_DIM)
        s = lax.dot_general(q[:, cols], k_ref[0, :, cols], (((1,), (1,)), ((), ())),
                            preferred_element_type=F32)
        p = jnp.exp(s - jnp.max(s, axis=-1, keepdims=True))
        p = p / jnp.sum(p, axis=-1, keepdims=True)
        heads.append(jnp.dot(p.astype(BF16), v_ref[0, :, cols], preferred_element_type=F32))
    o = jnp.concatenate(heads, axis=1).astype(BF16)
    x2 = x + jnp.dot(o, wo_ref[...], preferred_element_type=F32)
    x2_ref[...] = x2

    h2 = _rms(x2, gm_ref[...])
    h2_ref[...] = h2.astype(BF16)
    h_hi = h2.astype(BF16)
    h_lo = (h2 - h_hi.astype(F32)).astype(BF16)
    logits = (jnp.dot(h_hi, wr_hi_ref[...], preferred_element_type=F32)
              + jnp.dot(h_lo, wr_hi_ref[...], preferred_element_type=F32)
              + jnp.dot(h_hi, wr_lo_ref[...], preferred_element_type=F32)) + br_ref[...]
    lane = lax.broadcasted_iota(jnp.int32, logits.shape, 1)
    neg = -jnp.inf

    def first_argmax(v):
        m = jnp.max(v, axis=-1, keepdims=True)
        return m, jnp.min(jnp.where(v == m, lane, ROUTE_LANES), axis=-1, keepdims=True)

    gl = jnp.where(lane < N_GROUPS, logits, neg)
    g_max, g_sel = first_argmax(gl)
    p_top = 1.0 / jnp.sum(jnp.exp(gl - g_max), axis=-1, keepdims=True)
    e_lo = N_GROUPS + EXPERTS_PER_GROUP * g_sel
    el = jnp.where((lane >= e_lo) & (lane < e_lo + EXPERTS_PER_GROUP), logits, neg)
    m1, i1 = first_argmax(el)
    m2, i2 = first_argmax(jnp.where(lane == i1, neg, el))
    r = jnp.exp(m2 - m1)
    w1 = p_top / (1.0 + r)
    w2 = w1 * r
    e1 = i1 - N_GROUPS
    e2 = i2 - N_GROUPS
    oh1 = (lane == e1).astype(BF16)
    oh2 = (lane == e2).astype(BF16)
    n_t = logits.shape[0]
    before = (lax.broadcasted_iota(jnp.int32, (n_t, n_t), 0)
              > lax.broadcasted_iota(jnp.int32, (n_t, n_t), 1)).astype(BF16)
    c1 = jnp.dot(before, oh1, preferred_element_type=F32)
    c2 = jnp.dot(before, oh2, preferred_element_type=F32)
    tot1 = jnp.sum(oh1.astype(F32), axis=0, keepdims=True)
    cnt = tot1 + jnp.sum(oh2.astype(F32), axis=0, keepdims=True)
    cnt = jnp.floor((cnt + (SUBLANES - 1)) * (1.0 / SUBLANES)) * SUBLANES
    lanes_before = (lax.broadcasted_iota(jnp.int32, (ROUTE_LANES, ROUTE_LANES), 0)
                    < lax.broadcasted_iota(jnp.int32, (ROUTE_LANES, ROUTE_LANES), 1)).astype(F32)
    loff = jnp.dot(jnp.broadcast_to(cnt, (SUBLANES, ROUTE_LANES)), lanes_before,
                   preferred_element_type=F32, precision=lax.Precision.HIGHEST)[0:1, :]
    lp1 = jnp.sum(oh1.astype(F32) * (loff + c1), axis=-1, keepdims=True)
    lp2 = jnp.sum(oh2.astype(F32) * (loff + tot1 + c2), axis=-1, keepdims=True)
    route = jnp.where(lane == 0, e1.astype(F32),
            jnp.where(lane == 1, e2.astype(F32),
            jnp.where(lane == 2, w1,
            jnp.where(lane == 3, w2,
            jnp.where(lane == 4, lp1, jnp.where(lane == 5, lp2, 0.0))))))
    route_ref[...] = route
    row = lax.broadcasted_iota(jnp.int32, (SUBLANES, ROUTE_LANES), 0)
    stats_ref[0] = jnp.where(row == 0, cnt, jnp.where(row == 1, loff, 0.0))


def _xattn_route(x1, k, v, bsz, seqlen, g_xattn, w_q, w_o, g_moe, w_rg, b_rg, w_re, b_re):
    t, d = x1.shape
    nt = seqlen // TOK_TILE
    w_r = jnp.pad(jnp.concatenate([w_rg, w_re], axis=1), ((0, 0), (0, ROUTE_LANES - N_GROUPS - N_EXPERTS)))
    b_r = jnp.pad(jnp.concatenate([b_rg, b_re]), (0, ROUTE_LANES - N_GROUPS - N_EXPERTS)).reshape(1, ROUTE_LANES)
    wr_hi = w_r.astype(BF16)
    wr_lo = (w_r - wr_hi.astype(F32)).astype(BF16)
    tile = pl.BlockSpec((TOK_TILE, d), lambda b, j: (b * nt + j, 0))
    kv_spec = pl.BlockSpec((1, MEM_LEN, d), lambda b, j: (b, 0, 0))
    const = lambda shape: pl.BlockSpec(shape, lambda b, j: (0, 0))
    return pl.pallas_call(
        _xattn_body,
        grid=(bsz, nt),
        in_specs=[tile, kv_spec, kv_spec, const((1, d)), const((d, d)), const((d, d)), const((1, d)),
                  const((d, ROUTE_LANES)), const((d, ROUTE_LANES)), const((1, ROUTE_LANES))],
        out_specs=[tile, tile, pl.BlockSpec((TOK_TILE, ROUTE_LANES), lambda b, j: (b * nt + j, 0)),
                   pl.BlockSpec((1, SUBLANES, ROUTE_LANES), lambda b, j: (b * nt + j, 0, 0))],
        out_shape=[jax.ShapeDtypeStruct((t, d), F32), jax.ShapeDtypeStruct((t, d), BF16),
                   jax.ShapeDtypeStruct((t, ROUTE_LANES), F32),
                   jax.ShapeDtypeStruct((t // TOK_TILE, SUBLANES, ROUTE_LANES), F32)],
        compiler_params=_cparams(("arbitrary", "arbitrary")),
        name="xattn_route",
    )(x1, k, v, g_xattn.reshape(1, d), w_q.astype(BF16), w_o.astype(BF16), g_moe.reshape(1, d),
      wr_hi, wr_lo, b_r)


def _rows_copy(src_ref, src0, dst_ref, dst0, n, sem):
    rows = lambda r0: pl.ds(pl.multiple_of(r0, SUBLANES), pl.multiple_of(n, SUBLANES))
    return pltpu.make_async_copy(src_ref.at[rows(src0), :], dst_ref.at[rows(dst0), :], sem)


def _pack_pairs(v):
    half = v.shape[1] // 2
    lo = lax.bitcast_convert_type(v[:, :half], jnp.uint32)
    hi = lax.bitcast_convert_type(v[:, half:], jnp.uint32)
    return (lo >> 16) | (hi & jnp.uint32(0xFFFF0000))


def _unpack_pairs(w):
    lo = lax.bitcast_convert_type(w << 16, F32)
    hi = lax.bitcast_convert_type(w & jnp.uint32(0xFFFF0000), F32)
    return jnp.concatenate([lo, hi], axis=1).astype(BF16)


def _dispatch_body(cnt_ref, loff_ref, gb_ref, tot_ref, fstart_ref, fcnt_ref, nu_ref, h_ref, route_ref,
                   xs_ref, buf_ref, zero_ref, sem, zsem):
    i = pl.program_id(0)
    nt = pl.num_programs(0)
    slot = i % 2
    buf = buf_ref.at[slot]

    def drain(tile, s):
        _rows_copy(buf_ref.at[s], 0, xs_ref, 0, tot_ref[tile], sem.at[s]).wait()

    @pl.when(i >= 2)
    def _():
        drain(i - 2, slot)

    rt = route_ref[...].T
    lp1 = rt[4:5, :].astype(jnp.int32)
    lp2 = rt[5:6, :].astype(jnp.int32)
    r_i = lax.broadcasted_iota(jnp.int32, (SORT_ROWS, TOK_TILE), 0)
    perm = jnp.where((r_i == lp1) | (r_i == lp2), 1.0, 0.0).astype(BF16)
    buf[...] = _pack_pairs(jnp.dot(perm, h_ref[...], preferred_element_type=F32))

    def per_expert(e, c):
        j = i * N_EXPERTS + e

        @pl.when(cnt_ref[j] > 0)
        def _():
            _rows_copy(buf, loff_ref[j], xs_ref, gb_ref[j], cnt_ref[j], sem.at[slot]).start()
        return c

    lax.fori_loop(0, N_EXPERTS, per_expert, 0)

    @pl.when(i == 0)
    def _():
        zero_ref[...] = jnp.zeros(zero_ref.shape, zero_ref.dtype)

        def fill(start):
            def body(e, c):
                @pl.when(fcnt_ref[e] > 0)
                def _():
                    copy = _rows_copy(zero_ref, 0, xs_ref, fstart_ref[e], fcnt_ref[e], zsem)
                    copy.start() if start else copy.wait()
                return c
            lax.fori_loop(0, N_EXPERTS, body, 0)

            def tail(blk, c):
                copy = _rows_copy(zero_ref, 0, xs_ref, blk * MOE_BLOCK, MOE_BLOCK, zsem)
                copy.start() if start else copy.wait()
                return c
            lax.fori_loop(nu_ref[0], xs_ref.shape[0] // MOE_BLOCK, tail, 0)

        fill(True)
        fill(False)

    @pl.when(i == nt - 1)
    def _():
        drain(i, slot)

        @pl.when(i >= 1)
        def _():
            drain(i - 1, 1 - slot)


def _dispatch(h2, route, tables, n_used, n_slots):
    t, d = h2.shape
    nt = t // TOK_TILE
    return pl.pallas_call(
        _dispatch_body,
        grid_spec=pltpu.PrefetchScalarGridSpec(
            num_scalar_prefetch=7,
            grid=(nt,),
            in_specs=[
                pl.BlockSpec((TOK_TILE, d), lambda i, *_: (i, 0)),
                pl.BlockSpec((TOK_TILE, ROUTE_LANES), lambda i, *_: (i, 0)),
            ],
            out_specs=pl.BlockSpec(memory_space=pl.ANY),
            scratch_shapes=[pltpu.VMEM((2, SORT_ROWS, d // 2), jnp.uint32),
                            pltpu.VMEM((MOE_BLOCK, d // 2), jnp.uint32),
                            pltpu.SemaphoreType.DMA((2,)), pltpu.SemaphoreType.DMA(())],
        ),
        out_shape=jax.ShapeDtypeStruct((n_slots, d // 2), jnp.uint32),
        compiler_params=_cparams(("arbitrary",)),
        name="dispatch",
    )(tables["cnt"], tables["loff"], tables["gb"], tables["tot"], tables["fill_start"],
      tables["fill_cnt"], n_used, h2, route)


def _experts_body(be_ref, nu_ref, x_ref, wg_ref, wu_ref, wd_ref, y_ref, wg_bf, wu_bf, wd_bf):
    b = pl.program_id(0)
    used = b < nu_ref[0]

    @pl.when(used & ((b == 0) | (be_ref[b] != be_ref[jnp.maximum(b - 1, 0)])))
    def _():
        wg_bf[...] = wg_ref[0].astype(BF16)
        wu_bf[...] = wu_ref[0].astype(BF16)
        wd_bf[...] = wd_ref[0].astype(BF16)

    @pl.when(used)
    def _():
        x = _unpack_pairs(x_ref[...])
        g = jnp.dot(x, wg_bf[...], preferred_element_type=F32)
        u = jnp.dot(x, wu_bf[...], preferred_element_type=F32)
        a = (g * _sigmoid(g) * u).astype(BF16)
        y = jnp.dot(a, wd_bf[...], preferred_element_type=F32)
        y_ref[...] = _pack_pairs(y.astype(BF16).astype(F32))

    @pl.when(jnp.logical_not(used))
    def _():
        y_ref[...] = jnp.zeros(y_ref.shape, y_ref.dtype)


def _experts(xs, block_e, n_used, w_gate, w_up, w_down):
    n_slots, dp = xs.shape
    d = w_gate.shape[1]
    n_blocks = n_slots // MOE_BLOCK
    last = lambda b, nu: jnp.maximum(jnp.minimum(b, nu[0] - 1), 0)
    tile = pl.BlockSpec((MOE_BLOCK, dp), lambda b, be, nu: (last(b, nu), 0))
    return pl.pallas_call(
        _experts_body,
        grid_spec=pltpu.PrefetchScalarGridSpec(
            num_scalar_prefetch=2,
            grid=(n_blocks,),
            in_specs=[
                tile,
                pl.BlockSpec((1, d, D_EXPERT), lambda b, be, nu: (be[last(b, nu)], 0, 0)),
                pl.BlockSpec((1, d, D_EXPERT), lambda b, be, nu: (be[last(b, nu)], 0, 0)),
                pl.BlockSpec((1, D_EXPERT, d), lambda b, be, nu: (be[last(b, nu)], 0, 0)),
            ],
            out_specs=pl.BlockSpec((MOE_BLOCK, dp), lambda b, be, nu: (b, 0)),
            scratch_shapes=[pltpu.VMEM((d, D_EXPERT), BF16), pltpu.VMEM((d, D_EXPERT), BF16),
                            pltpu.VMEM((D_EXPERT, d), BF16)],
        ),
        out_shape=jax.ShapeDtypeStruct((n_slots, dp), jnp.uint32),
        compiler_params=_cparams(("arbitrary",)),
        name="experts",
    )(block_e, n_used, xs, w_gate, w_up, w_down)


def _combine_body(cnt_ref, loff_ref, gb_ref, tot_ref, x_ref, route_ref, g_ref, ys_ref, o_ref, ybuf_ref, sem):
    i = pl.program_id(0)
    nt = pl.num_programs(0)
    slot = i % 2

    def gather(tile, s):
        def per_expert(e, c):
            j = tile * N_EXPERTS + e

            @pl.when(cnt_ref[j] > 0)
            def _():
                _rows_copy(ys_ref, gb_ref[j], ybuf_ref.at[s], loff_ref[j], cnt_ref[j], sem.at[s]).start()
            return c

        lax.fori_loop(0, N_EXPERTS, per_expert, 0)

    @pl.when(i == 0)
    def _():
        gather(0, 0)

    @pl.when(i + 1 < nt)
    def _():
        gather(i + 1, 1 - slot)

    route = route_ref[...]
    c_i = lax.broadcasted_iota(jnp.int32, (TOK_TILE, SORT_ROWS), 1)
    lp1 = route[:, 4:5].astype(jnp.int32)
    lp2 = route[:, 5:6].astype(jnp.int32)
    pw = jnp.where(c_i == lp1, route[:, 2:3], jnp.where(c_i == lp2, route[:, 3:4], 0.0)).astype(BF16)
    tot = tot_ref[i]
    _rows_copy(ys_ref, 0, ybuf_ref.at[slot], 0, tot, sem.at[slot]).wait()
    r_i = lax.broadcasted_iota(jnp.int32, (SORT_ROWS, 1), 0)
    y = _unpack_pairs(jnp.where(r_i < tot, ybuf_ref[slot], jnp.uint32(0)))
    moe = jnp.dot(pw, y, preferred_element_type=F32)
    o_ref[...] = _rms(x_ref[...] + moe, g_ref[...])


def _combine(x2, route, tables, ys, g_final):
    t, d = x2.shape
    nt = t // TOK_TILE
    tile = pl.BlockSpec((TOK_TILE, d), lambda i, *_: (i, 0))
    return pl.pallas_call(
        _combine_body,
        grid_spec=pltpu.PrefetchScalarGridSpec(
            num_scalar_prefetch=4,
            grid=(nt,),
            in_specs=[
                tile,
                pl.BlockSpec((TOK_TILE, ROUTE_LANES), lambda i, *_: (i, 0)),
                pl.BlockSpec((1, d), lambda i, *_: (0, 0)),
                pl.BlockSpec(memory_space=pl.ANY),
            ],
            out_specs=tile,
            scratch_shapes=[pltpu.VMEM((2, SORT_ROWS, d // 2), jnp.uint32),
                            pltpu.SemaphoreType.DMA((2,))],
        ),
        out_shape=jax.ShapeDtypeStruct((t, d), F32),
        compiler_params=_cparams(("arbitrary",)),
        name="combine",
    )(tables["cnt"], tables["loff"], tables["gb"], tables["tot"], x2, route, g_final.reshape(1, d), ys)


def _routing_tables(stats, n_tok):
    cnt = stats[:, 0, :N_EXPERTS].astype(jnp.int32)
    loff = stats[:, 1, :N_EXPERTS].astype(jnp.int32)
    n_tiles = cnt.shape[0]
    counts = jnp.sum(cnt, axis=0)
    padded = ((counts + MOE_BLOCK - 1) // MOE_BLOCK) * MOE_BLOCK
    pad_end = jnp.cumsum(padded)
    pad_start = pad_end - padded
    gb = pad_start[None, :] + jnp.cumsum(cnt, axis=0) - cnt
    max_rows = (n_tok * TOP_K + n_tiles * N_EXPERTS * (SUBLANES - 1)
                + N_EXPERTS * (MOE_BLOCK - SUBLANES))
    n_blocks = -(-max_rows // MOE_BLOCK)
    n_slots = n_blocks * MOE_BLOCK
    block_start = jnp.arange(n_blocks, dtype=jnp.int32) * MOE_BLOCK
    block_e = jnp.minimum(jnp.sum((pad_end[None, :] <= block_start[:, None]).astype(jnp.int32), axis=1),
                          N_EXPERTS - 1)
    tables = dict(cnt=cnt.reshape(-1), loff=loff.reshape(-1), gb=gb.reshape(-1).astype(jnp.int32),
                  tot=jnp.sum(cnt, axis=1).astype(jnp.int32),
                  fill_start=(pad_start + counts).astype(jnp.int32),
                  fill_cnt=(padded - counts).astype(jnp.int32))
    n_used = (pad_end[-1:] // MOE_BLOCK).astype(jnp.int32)
    return tables, block_e, n_used, n_slots


def kernel(x, mem, g_mix, w_in, conv_w, conv_b, ln_g, ln_b, ssd_conv_w, ssd_conv_b, dt_bias, a_log, d_skip, ssd_norm_g, w_out, g_xattn, g_mem, w_q, w_k, w_v, w_o, g_moe, w_router_group, b_router_group, w_router_expert, b_router_expert, w_gate, w_up, w_down, g_final):
    bsz, seqlen, d = x.shape
    n_tok = bsz * seqlen
    xt = x.reshape(n_tok, d)
    assert g_mix.shape[0] == 1, "the combine kernel applies the final norm: single layer only"
    for l in range(1):
        u, xbc, z, dt_raw = _mix_in(xt, bsz, seqlen, g_mix[l], w_in[l], conv_w[l], conv_b[l],
                                    ln_g[l], ln_b[l])
        y = _ssd(xbc, z, dt_raw, bsz, seqlen, ssd_conv_w[l], ssd_conv_b[l], dt_bias[l], a_log[l],
                 d_skip[l], ssd_norm_g[l])
        x1 = _out_proj(xt, u, y, w_out[l])
        k, v = _kv_proj(mem, g_mem[l], w_k[l], w_v[l])
        x2, h2, route, stats = _xattn_route(x1, k, v, bsz, seqlen, g_xattn[l], w_q[l], w_o[l], g_moe[l],
                                            w_router_group[l], b_router_group[l], w_router_expert[l],
                                            b_router_expert[l])
        tables, block_e, n_used, n_slots = _routing_tables(stats, n_tok)
        xs = _dispatch(h2, route, tables, n_used, n_slots)
        ys = _experts(xs, block_e, n_used, w_gate[l], w_up[l], w_down[l])
        xt = _combine(x2, route, tables, ys, g_final)
    return xt.reshape(bsz, seqlen, d)
```

```python
import functools

import jax
import jax.numpy as jnp
from jax import lax
from jax.experimental import pallas as pl
from jax.experimental.pallas import tpu as pltpu

F32 = jnp.float32
BF16 = jnp.bfloat16

D_MODEL = 1024
CONV_CH = 1024
CONV_K = 31
SSD_INNER = 1024
SSD_HEAD_DIM = 64
SSD_HEADS = 16
SSD_STATE = 128
SSD_GROUPS = 2
SSD_GROUP_W = SSD_INNER // SSD_GROUPS
SSD_CONV_K = 4
SSD_CHUNK = 128
SSD_BC = SSD_GROUPS * SSD_STATE
SSD_CONV_CH = SSD_INNER + 2 * SSD_BC
N_MAIN = 2 * CONV_CH + 2 * SSD_INNER + 2 * SSD_BC
XA_HEADS = 4
XA_HEAD_DIM = 256
MEM_LEN = 256
N_GROUPS = 4
EXPERTS_PER_GROUP = 8
N_EXPERTS = 32
TOP_K = 2
D_EXPERT = 512
MOE_BLOCK = 512
RMS_EPS = 1e-6
LN_EPS = 1e-5

LANES = 128
SUBLANES = 8
VMEM_LIMIT = 56 * 1024 * 1024

TOK_TILE = 512
CONV_TILE = 512
MM_CHUNK = 512
CONV_HALO = 32
CONV_ROWS = 128
SSD_TILE = 512
XA_TILES = 2
SSD_UNROLL = 2
SSD_HALO = 8
ROUTE_LANES = 128
SORT_ROWS = TOP_K * TOK_TILE + N_EXPERTS * SUBLANES


def _cparams(sem):
    return pltpu.CompilerParams(dimension_semantics=sem, vmem_limit_bytes=VMEM_LIMIT)


def _rms(x, g):
    return x * lax.rsqrt(jnp.mean(x * x, axis=-1, keepdims=True) + RMS_EPS) * g


def _sigmoid(x):
    return 1.0 / (1.0 + jnp.exp(-x))


def _kv_body(m_ref, g_ref, wk_ref, wv_ref, k_ref, v_ref):
    m = _rms(m_ref[0], g_ref[...]).astype(BF16)
    k_ref[0] = jnp.dot(m, wk_ref[...], preferred_element_type=F32).astype(BF16)
    v_ref[0] = jnp.dot(m, wv_ref[...], preferred_element_type=F32).astype(BF16)


def _kv_proj(mem, g_mem, w_k, w_v):
    b, s, d = mem.shape
    w_spec = pl.BlockSpec((d, d), lambda i: (0, 0))
    kv_spec = pl.BlockSpec((1, s, d), lambda i: (i, 0, 0))
    return pl.pallas_call(
        _kv_body,
        grid=(b,),
        in_specs=[kv_spec, pl.BlockSpec((1, d), lambda i: (0, 0)), w_spec, w_spec],
        out_specs=[kv_spec, kv_spec],
        out_shape=[jax.ShapeDtypeStruct((b, s, d), BF16)] * 2,
        compiler_params=_cparams(("arbitrary",)),
        name="kv_proj",
    )(mem, g_mem.reshape(1, d), w_k.astype(BF16), w_v.astype(BF16))


def _mix_in_body(x_ref, g_ref, w_ref, wdt_ref, cw_ref, cb_ref, lg_ref, lb_ref,
                 u_ref, xbc_ref, z_ref, dt_ref, ubuf_ref, acc_ref):
    @pl.when(pl.program_id(1) == 0)
    def _():
        ubuf_ref[0:CONV_HALO, :] = jnp.zeros((CONV_HALO, CONV_CH), F32)

    h = _rms(x_ref[...], g_ref[...]).astype(BF16)
    proj = lambda lo: jnp.dot(h, w_ref[:, lo:lo + MM_CHUNK], preferred_element_type=F32)
    for c in range(CONV_CH // MM_CHUNK):
        cols = slice(c * MM_CHUNK, (c + 1) * MM_CHUNK)
        ubuf_ref[CONV_HALO:CONV_HALO + CONV_TILE, cols] = (
            proj(c * MM_CHUNK) * _sigmoid(proj(CONV_CH + c * MM_CHUNK)))
    z0 = 2 * CONV_CH
    for c in range(SSD_INNER // MM_CHUNK):
        z_ref[:, c * MM_CHUNK:(c + 1) * MM_CHUNK] = proj(z0 + c * MM_CHUNK).astype(BF16)
    x0 = z0 + SSD_INNER
    for c in range(SSD_CONV_CH // MM_CHUNK):
        xbc_ref[:, c * MM_CHUNK:(c + 1) * MM_CHUNK] = proj(x0 + c * MM_CHUNK).astype(BF16)
    dt_ref[...] = jnp.dot(h, wdt_ref[...], preferred_element_type=F32)

    first = CONV_HALO - (CONV_K - 1)
    for rc in range(CONV_TILE // CONV_ROWS):
        r0 = rc * CONV_ROWS
        rows = slice(r0, r0 + CONV_ROWS)
        for cb in range(CONV_CH // LANES):
            cols = slice(cb * LANES, (cb + 1) * LANES)
            acc = None
            for res in range(SUBLANES):
                part = None
                for k in range(CONV_K):
                    if (first + k) % SUBLANES == res:
                        term = cw_ref[k:k + 1, cols] * ubuf_ref[r0 + first + k:r0 + first + k + CONV_ROWS, cols]
                        part = term if part is None else part + term
                acc = part if acc is None else acc + part
            acc_ref[rows, cols] = acc
        u = acc_ref[rows, :] + cb_ref[...]
        mu = jnp.mean(u, axis=-1, keepdims=True)
        uc = u - mu
        var = jnp.mean(uc * uc, axis=-1, keepdims=True)
        y = uc * lax.rsqrt(var + LN_EPS) * lg_ref[...] + lb_ref[...]
        u_ref[rows, :] = (y * _sigmoid(y)).astype(BF16)

    ubuf_ref[0:CONV_HALO, :] = ubuf_ref[CONV_TILE:CONV_TILE + CONV_HALO, :]


def _mix_in(x2d, bsz, seqlen, g_mix, w_in, conv_w, conv_b, ln_g, ln_b):
    t, d = x2d.shape
    nt = seqlen // CONV_TILE
    w_main = w_in[:, :N_MAIN].astype(BF16)
    w_dt = jnp.pad(w_in[:, N_MAIN:], ((0, 0), (0, LANES - SSD_HEADS))).astype(BF16)
    row = lambda v: v.reshape(1, -1)
    const = lambda shape: pl.BlockSpec(shape, lambda b, j: (0, 0))
    tile = lambda width: pl.BlockSpec((CONV_TILE, width), lambda b, j: (b * nt + j, 0))
    return pl.pallas_call(
        _mix_in_body,
        grid=(bsz, nt),
        in_specs=[
            tile(d), const((1, d)), const((d, N_MAIN)), const((d, LANES)),
            const((CONV_K + 1, CONV_CH)), const((1, CONV_CH)), const((1, CONV_CH)), const((1, CONV_CH)),
        ],
        out_specs=[tile(CONV_CH), tile(SSD_CONV_CH), tile(SSD_INNER), tile(LANES)],
        out_shape=[jax.ShapeDtypeStruct((t, CONV_CH), BF16), jax.ShapeDtypeStruct((t, SSD_CONV_CH), BF16),
                   jax.ShapeDtypeStruct((t, SSD_INNER), BF16), jax.ShapeDtypeStruct((t, LANES), F32)],
        scratch_shapes=[
            pltpu.VMEM((CONV_HALO + CONV_TILE, CONV_CH), F32),
            pltpu.VMEM((CONV_TILE, CONV_CH), F32),
        ],
        compiler_params=_cparams(("arbitrary", "arbitrary")),
        name="mix_in",
    )(x2d, row(g_mix), w_main, w_dt, jnp.pad(conv_w, ((0, 1), (0, 0))), row(conv_b), row(ln_g), row(ln_b))


def _ssd_body(xbc_ref, z_ref, dt_ref, cw_ref, cb_ref, dtb_ref, alog_ref, dsk_ref, ng_ref,
              expand_ref, y_ref, xbuf_ref, act_ref, dts_ref, state_ref):
    @pl.when(pl.program_id(1) == 0)
    def _():
        xbuf_ref[0:SSD_HALO, :] = jnp.zeros((SSD_HALO, SSD_CONV_CH), F32)
        state_ref[...] = jnp.zeros(state_ref.shape, F32)

    xbuf_ref[SSD_HALO:SSD_HALO + SSD_TILE, :] = xbc_ref[...].astype(F32)
    first = SSD_HALO - (SSD_CONV_K - 1)
    conv = cb_ref[...] + cw_ref[0:1, :] * xbuf_ref[first:first + SSD_TILE, :]
    for k in range(1, SSD_CONV_K):
        conv = conv + cw_ref[k:k + 1, :] * xbuf_ref[first + k:first + k + SSD_TILE, :]
    act_ref[...] = conv * _sigmoid(conv)
    xbuf_ref[0:SSD_HALO, :] = xbuf_ref[SSD_TILE:SSD_TILE + SSD_HALO, :]

    dt_in = dt_ref[...] + dtb_ref[...]
    dts_ref[...] = jnp.maximum(dt_in, 0.0) + jnp.log1p(jnp.exp(-jnp.abs(dt_in)))

    a_neg = -jnp.exp(alog_ref[...])
    q = SSD_CHUNK
    row_i = lax.broadcasted_iota(jnp.int32, (q, q), 0)
    col_i = lax.broadcasted_iota(jnp.int32, (q, q), 1)
    causal = row_i >= col_i
    tril = causal.astype(F32)
    lane_i = lax.broadcasted_iota(jnp.int32, (q, LANES), 1)
    low_half = lane_i < SSD_HEAD_DIM
    expand = expand_ref[...]

    def chunk(c, carry):
        r0 = pl.multiple_of(c * q, q)
        rows = pl.ds(r0, q)
        dtc = dts_ref[rows, :]
        a_cs = jnp.dot(tril, dtc * a_neg, preferred_element_type=F32,
                       precision=lax.Precision.HIGHEST)
        a_cs_t = a_cs.T
        dt_t = dtc.T
        a_end = a_cs[q - 1:q, :]
        e_exp = jnp.dot(jnp.exp(a_cs).astype(BF16), expand, preferred_element_type=F32)
        w_exp = jnp.dot((jnp.exp(a_end - a_cs) * dtc).astype(BF16), expand,
                        preferred_element_type=F32)
        dec_row = e_exp[q - 1:q, :]
        xc = act_ref[rows, 0:SSD_INNER]
        xw = (xc * w_exp).astype(BF16)
        y_parts = []
        for g in range(SSD_GROUPS):
            b_f = act_ref[rows, SSD_INNER + g * SSD_STATE:SSD_INNER + (g + 1) * SSD_STATE]
            c_f = act_ref[rows, SSD_INNER + SSD_BC + g * SSD_STATE:
                          SSD_INNER + SSD_BC + (g + 1) * SSD_STATE]
            b_g = b_f.astype(BF16)
            c_g = c_f.astype(BF16)
            cb = lax.dot_general(c_g, b_g, (((1,), (1,)), ((), ())),
                                 preferred_element_type=F32)
            gcols = slice(g * SSD_GROUP_W, (g + 1) * SSD_GROUP_W)
            st = state_ref[g]
            y_off = jnp.dot(c_g, st.astype(BF16), preferred_element_type=F32)
            state_ref[g] = st * dec_row[:, gcols] + jnp.dot(
                b_f.T.astype(BF16), xw[:, gcols], preferred_element_type=F32)
            for pair in range(SSD_GROUP_W // LANES):
                ms = []
                for hh in range(2):
                    h = g * (SSD_HEADS // SSD_GROUPS) + 2 * pair + hh
                    seg = a_cs[:, h:h + 1] - a_cs_t[h:h + 1, :]
                    dec = jnp.exp(jnp.where(causal, seg, -jnp.inf))
                    ms.append((cb * dec * dt_t[h:h + 1, :]).astype(BF16))
                lhs = jnp.concatenate(ms, axis=1)
                xp = xc[:, g * SSD_GROUP_W + pair * LANES:g * SSD_GROUP_W + (pair + 1) * LANES]
                rhs = jnp.concatenate([jnp.where(low_half, xp, 0.0),
                                       jnp.where(low_half, 0.0, xp)], axis=0).astype(BF16)
                y_diag = jnp.dot(lhs, rhs, preferred_element_type=F32)
                lo = pair * LANES
                y_parts.append(y_diag + y_off[:, lo:lo + LANES]
                               * e_exp[:, g * SSD_GROUP_W + lo:g * SSD_GROUP_W + lo + LANES])
        y = jnp.concatenate(y_parts, axis=1) + xc * dsk_ref[...]
        z = z_ref[rows, :].astype(F32)
        y = y * (z * _sigmoid(z))
        outs = []
        for g in range(SSD_GROUPS):
            yg = y[:, g * SSD_GROUP_W:(g + 1) * SSD_GROUP_W]
            outs.append(yg * lax.rsqrt(jnp.mean(yg * yg, axis=-1, keepdims=True) + RMS_EPS))
        y_ref[rows, :] = (jnp.concatenate(outs, axis=1) * ng_ref[...]).astype(BF16)
        return carry

    lax.fori_loop(0, SSD_TILE // q, chunk, 0, unroll=SSD_UNROLL)


def _ssd(xbc, z, dt_raw, bsz, seqlen, ssd_conv_w, ssd_conv_b, dt_bias, a_log, d_skip, ssd_norm_g):
    nt = seqlen // SSD_TILE
    pad_h = lambda v: jnp.pad(v, (0, LANES - SSD_HEADS)).reshape(1, LANES)
    expand = (jnp.arange(LANES)[:, None] == (jnp.arange(SSD_INNER) // SSD_HEAD_DIM)[None, :]).astype(BF16)
    const = lambda shape: pl.BlockSpec(shape, lambda b, j: (0, 0))
    return pl.pallas_call(
        _ssd_body,
        grid=(bsz, nt),
        in_specs=[
            pl.BlockSpec((SSD_TILE, SSD_CONV_CH), lambda b, j: (b * nt + j, 0)),
            pl.BlockSpec((SSD_TILE, SSD_INNER), lambda b, j: (b * nt + j, 0)),
            pl.BlockSpec((SSD_TILE, LANES), lambda b, j: (b * nt + j, 0)),
            const((SSD_CONV_K, SSD_CONV_CH)),
            const((1, SSD_CONV_CH)),
            const((1, LANES)), const((1, LANES)),
            const((1, SSD_INNER)), const((1, SSD_INNER)),
            const((LANES, SSD_INNER)),
        ],
        out_specs=pl.BlockSpec((SSD_TILE, SSD_INNER), lambda b, j: (b * nt + j, 0)),
        out_shape=jax.ShapeDtypeStruct((bsz * seqlen, SSD_INNER), BF16),
        scratch_shapes=[
            pltpu.VMEM((SSD_HALO + SSD_TILE, SSD_CONV_CH), F32),
            pltpu.VMEM((SSD_TILE, SSD_CONV_CH), F32),
            pltpu.VMEM((SSD_TILE, LANES), F32),
            pltpu.VMEM((SSD_GROUPS, SSD_STATE, SSD_GROUP_W), F32),
        ],
        compiler_params=_cparams(("arbitrary", "arbitrary")),
        name="ssd",
    )(xbc, z, dt_raw, ssd_conv_w, ssd_conv_b.reshape(1, SSD_CONV_CH), pad_h(dt_bias),
      pad_h(a_log), jnp.repeat(d_skip, SSD_HEAD_DIM).reshape(1, SSD_INNER),
      ssd_norm_g.reshape(1, SSD_INNER), expand)


def _outproj_body(x_ref, u_ref, y_ref, wu_ref, wy_ref, o_ref):
    o_ref[...] = (x_ref[...]
                  + jnp.dot(u_ref[...], wu_ref[...], preferred_element_type=F32)
                  + jnp.dot(y_ref[...], wy_ref[...], preferred_element_type=F32))


def _out_proj(x2d, u, y, w_out):
    t, d = x2d.shape
    w = w_out.astype(BF16)
    tile = pl.BlockSpec((TOK_TILE, d), lambda i: (i, 0))
    w_spec = pl.BlockSpec((d, d), lambda i: (0, 0))
    return pl.pallas_call(
        _outproj_body,
        grid=(t // TOK_TILE,),
        in_specs=[tile, tile, tile, w_spec, w_spec],
        out_specs=tile,
        out_shape=jax.ShapeDtypeStruct((t, d), F32),
        compiler_params=_cparams(("arbitrary",)),
        name="out_proj",
    )(x2d, u, y, w[:CONV_CH], w[CONV_CH:])


def _xattn_body(x_ref, k_ref, v_ref, gx_ref, wq_ref, wo_ref, gm_ref, wr_hi_ref, wr_lo_ref, br_ref,
                x2_ref, h2_ref, route_ref, stats_ref):
    for s in range(XA_TILES):
        rows = slice(s * TOK_TILE, (s + 1) * TOK_TILE)
        x2, h2, route, stats = _xattn_tile(x_ref[rows, :], k_ref, v_ref, gx_ref, wq_ref, wo_ref, gm_ref,
                                           wr_hi_ref, wr_lo_ref, br_ref)
        x2_ref[rows, :] = x2
        h2_ref[rows, :] = h2
        route_ref[rows, :] = route
        stats_ref[s] = stats


def _xattn_tile(x, k_ref, v_ref, gx_ref, wq_ref, wo_ref, gm_ref, wr_hi_ref, wr_lo_ref, br_ref):
    h = _rms(x, gx_ref[...]).astype(BF16)
    q = (jnp.dot(h, wq_ref[...], preferred_element_type=F32) * (XA_HEAD_DIM ** -0.5)).astype(BF16)
    heads = []
    for i in range(XA_HEADS):
        cols = slice(i * XA_HEAD_DIM, (i + 1) * XA_HEAD_DIM)
        s = jnp.dot(q[:, cols], k_ref[0, cols, :], preferred_element_type=F32)
        p = jnp.exp(s - jnp.max(s, axis=-1, keepdims=True))
        p = p / jnp.sum(p, axis=-1, keepdims=True)
        heads.append(jnp.dot(p.astype(BF16), v_ref[0, :, cols], preferred_element_type=F32))
    o = jnp.concatenate(heads, axis=1).astype(BF16)
    x2 = x + jnp.dot(o, wo_ref[...], preferred_element_type=F32)

    h2 = _rms(x2, gm_ref[...])
    h_hi = h2.astype(BF16)
    h_lo = (h2 - h_hi.astype(F32)).astype(BF16)
    logits = (jnp.dot(h_hi, wr_hi_ref[...], preferred_element_type=F32)
              + jnp.dot(h_lo, wr_hi_ref[...], preferred_element_type=F32)
              + jnp.dot(h_hi, wr_lo_ref[...], preferred_element_type=F32)) + br_ref[...]
    lane = lax.broadcasted_iota(jnp.int32, logits.shape, 1)
    neg = -jnp.inf

    def first_argmax(v):
        m = jnp.max(v, axis=-1, keepdims=True)
        return m, jnp.min(jnp.where(v == m, lane, ROUTE_LANES), axis=-1, keepdims=True)

    gl = jnp.where(lane < N_GROUPS, logits, neg)
    g_max, g_sel = first_argmax(gl)
    p_top = 1.0 / jnp.sum(jnp.exp(gl - g_max), axis=-1, keepdims=True)
    e_lo = N_GROUPS + EXPERTS_PER_GROUP * g_sel
    el = jnp.where((lane >= e_lo) & (lane < e_lo + EXPERTS_PER_GROUP), logits, neg)
    m1, i1 = first_argmax(el)
    m2, i2 = first_argmax(jnp.where(lane == i1, neg, el))
    r = jnp.exp(m2 - m1)
    w1 = p_top / (1.0 + r)
    w2 = w1 * r
    e1 = i1 - N_GROUPS
    e2 = i2 - N_GROUPS
    oh1 = (lane == e1).astype(BF16)
    oh2 = (lane == e2).astype(BF16)
    n_t = logits.shape[0]
    before = (lax.broadcasted_iota(jnp.int32, (n_t, n_t), 0)
              > lax.broadcasted_iota(jnp.int32, (n_t, n_t), 1)).astype(BF16)
    c1 = jnp.dot(before, oh1, preferred_element_type=F32)
    c2 = jnp.dot(before, oh2, preferred_element_type=F32)
    tot1 = jnp.sum(oh1.astype(F32), axis=0, keepdims=True)
    cnt = tot1 + jnp.sum(oh2.astype(F32), axis=0, keepdims=True)
    cnt = jnp.floor((cnt + (SUBLANES - 1)) * (1.0 / SUBLANES)) * SUBLANES
    lanes_before = (lax.broadcasted_iota(jnp.int32, (ROUTE_LANES, ROUTE_LANES), 0)
                    < lax.broadcasted_iota(jnp.int32, (ROUTE_LANES, ROUTE_LANES), 1)).astype(F32)
    loff = jnp.dot(jnp.broadcast_to(cnt, (SUBLANES, ROUTE_LANES)), lanes_before,
                   preferred_element_type=F32, precision=lax.Precision.HIGHEST)[0:1, :]
    lp1 = jnp.sum(oh1.astype(F32) * (loff + c1), axis=-1, keepdims=True)
    lp2 = jnp.sum(oh2.astype(F32) * (loff + tot1 + c2), axis=-1, keepdims=True)
    route = jnp.where(lane == 0, e1.astype(F32),
            jnp.where(lane == 1, e2.astype(F32),
            jnp.where(lane == 2, w1,
            jnp.where(lane == 3, w2,
            jnp.where(lane == 4, lp1, jnp.where(lane == 5, lp2, 0.0))))))
    row = lax.broadcasted_iota(jnp.int32, (SUBLANES, ROUTE_LANES), 0)
    return x2, h_hi, route, jnp.where(row == 0, cnt, jnp.where(row == 1, loff, 0.0))


def _xattn_route(x1, k, v, bsz, seqlen, g_xattn, w_q, w_o, g_moe, w_rg, b_rg, w_re, b_re):
    t, d = x1.shape
    w_r = jnp.pad(jnp.concatenate([w_rg, w_re], axis=1), ((0, 0), (0, ROUTE_LANES - N_GROUPS - N_EXPERTS)))
    b_r = jnp.pad(jnp.concatenate([b_rg, b_re]), (0, ROUTE_LANES - N_GROUPS - N_EXPERTS)).reshape(1, ROUTE_LANES)
    wr_hi = w_r.astype(BF16)
    wr_lo = (w_r - wr_hi.astype(F32)).astype(BF16)
    step_rows = XA_TILES * TOK_TILE
    nt = seqlen // step_rows
    tile = pl.BlockSpec((step_rows, d), lambda b, j: (b * nt + j, 0))
    kv_spec = pl.BlockSpec((1, MEM_LEN, d), lambda b, j: (b, 0, 0))
    const = lambda shape: pl.BlockSpec(shape, lambda b, j: (0, 0))
    return pl.pallas_call(
        _xattn_body,
        grid=(bsz, nt),
        in_specs=[tile, pl.BlockSpec((1, d, MEM_LEN), lambda b, j: (b, 0, 0)), kv_spec, const((1, d)),
                  const((d, d)), const((d, d)), const((1, d)),
                  const((d, ROUTE_LANES)), const((d, ROUTE_LANES)), const((1, ROUTE_LANES))],
        out_specs=[tile, tile, pl.BlockSpec((step_rows, ROUTE_LANES), lambda b, j: (b * nt + j, 0)),
                   pl.BlockSpec((XA_TILES, SUBLANES, ROUTE_LANES), lambda b, j: (b * nt + j, 0, 0))],
        out_shape=[jax.ShapeDtypeStruct((t, d), F32), jax.ShapeDtypeStruct((t, d), BF16),
                   jax.ShapeDtypeStruct((t, ROUTE_LANES), F32),
                   jax.ShapeDtypeStruct((t // TOK_TILE, SUBLANES, ROUTE_LANES), F32)],
        compiler_params=_cparams(("arbitrary", "arbitrary")),
        name="xattn_route",
    )(x1, jnp.swapaxes(k, 1, 2), v, g_xattn.reshape(1, d), w_q.astype(BF16), w_o.astype(BF16), g_moe.reshape(1, d),
      wr_hi, wr_lo, b_r)


def _rows_copy(src_ref, src0, dst_ref, dst0, n, sem):
    rows = lambda r0: pl.ds(pl.multiple_of(r0, SUBLANES), pl.multiple_of(n, SUBLANES))
    return pltpu.make_async_copy(src_ref.at[rows(src0), :], dst_ref.at[rows(dst0), :], sem)


def _pack_pairs(v):
    half = v.shape[1] // 2
    lo = lax.bitcast_convert_type(v[:, :half], jnp.uint32)
    hi = lax.bitcast_convert_type(v[:, half:], jnp.uint32)
    return (lo >> 16) | (hi & jnp.uint32(0xFFFF0000))


def _unpack_pairs(w):
    lo = lax.bitcast_convert_type(w << 16, F32)
    hi = lax.bitcast_convert_type(w & jnp.uint32(0xFFFF0000), F32)
    return jnp.concatenate([lo, hi], axis=1).astype(BF16)


def _dispatch_body(cnt_ref, loff_ref, gb_ref, tot_ref, fstart_ref, fcnt_ref, nu_ref, h_ref, route_ref,
                   xs_ref, buf_ref, zero_ref, sem, zsem):
    i = pl.program_id(0)
    nt = pl.num_programs(0)
    slot = i % 2
    buf = buf_ref.at[slot]

    def drain(tile, s):
        _rows_copy(buf_ref.at[s], 0, xs_ref, 0, tot_ref[tile], sem.at[s]).wait()

    @pl.when(i >= 2)
    def _():
        drain(i - 2, slot)

    rt = route_ref[...].T
    lp1 = rt[4:5, :].astype(jnp.int32)
    lp2 = rt[5:6, :].astype(jnp.int32)
    r_i = lax.broadcasted_iota(jnp.int32, (SORT_ROWS, TOK_TILE), 0)
    perm = jnp.where((r_i == lp1) | (r_i == lp2), 1.0, 0.0).astype(BF16)
    buf[...] = _pack_pairs(jnp.dot(perm, h_ref[...], preferred_element_type=F32))

    def per_expert(e, c):
        j = i * N_EXPERTS + e

        @pl.when(cnt_ref[j] > 0)
        def _():
            _rows_copy(buf, loff_ref[j], xs_ref, gb_ref[j], cnt_ref[j], sem.at[slot]).start()
        return c

    lax.fori_loop(0, N_EXPERTS, per_expert, 0)

    @pl.when(i == 0)
    def _():
        zero_ref[...] = jnp.zeros(zero_ref.shape, zero_ref.dtype)

        def fill(start):
            def body(e, c):
                @pl.when(fcnt_ref[e] > 0)
                def _():
                    copy = _rows_copy(zero_ref, 0, xs_ref, fstart_ref[e], fcnt_ref[e], zsem)
                    copy.start() if start else copy.wait()
                return c
            lax.fori_loop(0, N_EXPERTS, body, 0)

            def tail(blk, c):
                copy = _rows_copy(zero_ref, 0, xs_ref, blk * MOE_BLOCK, MOE_BLOCK, zsem)
                copy.start() if start else copy.wait()
                return c
            lax.fori_loop(nu_ref[0], xs_ref.shape[0] // MOE_BLOCK, tail, 0)

        fill(True)
        fill(False)

    @pl.when(i == nt - 1)
    def _():
        drain(i, slot)

        @pl.when(i >= 1)
        def _():
            drain(i - 1, 1 - slot)


def _dispatch(h2, route, tables, n_used, n_slots):
    t, d = h2.shape
    nt = t // TOK_TILE
    return pl.pallas_call(
        _dispatch_body,
        grid_spec=pltpu.PrefetchScalarGridSpec(
            num_scalar_prefetch=7,
            grid=(nt,),
            in_specs=[
                pl.BlockSpec((TOK_TILE, d), lambda i, *_: (i, 0)),
                pl.BlockSpec((TOK_TILE, ROUTE_LANES), lambda i, *_: (i, 0)),
            ],
            out_specs=pl.BlockSpec(memory_space=pl.ANY),
            scratch_shapes=[pltpu.VMEM((2, SORT_ROWS, d // 2), jnp.uint32),
                            pltpu.VMEM((MOE_BLOCK, d // 2), jnp.uint32),
                            pltpu.SemaphoreType.DMA((2,)), pltpu.SemaphoreType.DMA(())],
        ),
        out_shape=jax.ShapeDtypeStruct((n_slots, d // 2), jnp.uint32),
        compiler_params=_cparams(("arbitrary",)),
        name="dispatch",
    )(tables["cnt"], tables["loff"], tables["gb"], tables["tot"], tables["fill_start"],
      tables["fill_cnt"], n_used, h2, route)


def _experts_body(be_ref, nu_ref, x_ref, wg_ref, wu_ref, wd_ref, y_ref, wg_bf, wu_bf, wd_bf):
    b = pl.program_id(0)
    used = b < nu_ref[0]

    @pl.when(used & ((b == 0) | (be_ref[b] != be_ref[jnp.maximum(b - 1, 0)])))
    def _():
        wg_bf[...] = wg_ref[0].astype(BF16)
        wu_bf[...] = wu_ref[0].astype(BF16)
        wd_bf[...] = wd_ref[0].astype(BF16)

    @pl.when(used)
    def _():
        x = _unpack_pairs(x_ref[...])
        g = jnp.dot(x, wg_bf[...], preferred_element_type=F32)
        u = jnp.dot(x, wu_bf[...], preferred_element_type=F32)
        a = (g * _sigmoid(g) * u).astype(BF16)
        y = jnp.dot(a, wd_bf[...], preferred_element_type=F32)
        y_ref[...] = _pack_pairs(y.astype(BF16).astype(F32))

    @pl.when(jnp.logical_not(used))
    def _():
        y_ref[...] = jnp.zeros(y_ref.shape, y_ref.dtype)


def _experts(xs, block_e, n_used, w_gate, w_up, w_down):
    n_slots, dp = xs.shape
    d = w_gate.shape[1]
    n_blocks = n_slots // MOE_BLOCK
    last = lambda b, nu: jnp.maximum(jnp.minimum(b, nu[0] - 1), 0)
    tile = pl.BlockSpec((MOE_BLOCK, dp), lambda b, be, nu: (last(b, nu), 0))
    return pl.pallas_call(
        _experts_body,
        grid_spec=pltpu.PrefetchScalarGridSpec(
            num_scalar_prefetch=2,
            grid=(n_blocks,),
            in_specs=[
                tile,
                pl.BlockSpec((1, d, D_EXPERT), lambda b, be, nu: (be[last(b, nu)], 0, 0)),
                pl.BlockSpec((1, d, D_EXPERT), lambda b, be, nu: (be[last(b, nu)], 0, 0)),
                pl.BlockSpec((1, D_EXPERT, d), lambda b, be, nu: (be[last(b, nu)], 0, 0)),
            ],
            out_specs=pl.BlockSpec((MOE_BLOCK, dp), lambda b, be, nu: (b, 0)),
            scratch_shapes=[pltpu.VMEM((d, D_EXPERT), BF16), pltpu.VMEM((d, D_EXPERT), BF16),
                            pltpu.VMEM((D_EXPERT, d), BF16)],
        ),
        out_shape=jax.ShapeDtypeStruct((n_slots, dp), jnp.uint32),
        compiler_params=_cparams(("arbitrary",)),
        name="experts",
    )(block_e, n_used, xs, w_gate, w_up, w_down)


def _combine_body(cnt_ref, loff_ref, gb_ref, tot_ref, x_ref, route_ref, g_ref, ys_ref, o_ref, ybuf_ref, sem):
    i = pl.program_id(0)
    nt = pl.num_programs(0)
    slot = i % 2

    def gather(tile, s):
        def per_expert(e, c):
            j = tile * N_EXPERTS + e

            @pl.when(cnt_ref[j] > 0)
            def _():
                _rows_copy(ys_ref, gb_ref[j], ybuf_ref.at[s], loff_ref[j], cnt_ref[j], sem.at[s]).start()
            return c

        lax.fori_loop(0, N_EXPERTS, per_expert, 0)

    @pl.when(i == 0)
    def _():
        gather(0, 0)

    @pl.when(i + 1 < nt)
    def _():
        gather(i + 1, 1 - slot)

    route = route_ref[...]
    c_i = lax.broadcasted_iota(jnp.int32, (TOK_TILE, SORT_ROWS), 1)
    lp1 = route[:, 4:5].astype(jnp.int32)
    lp2 = route[:, 5:6].astype(jnp.int32)
    pw = jnp.where(c_i == lp1, route[:, 2:3], jnp.where(c_i == lp2, route[:, 3:4], 0.0)).astype(BF16)
    tot = tot_ref[i]
    _rows_copy(ys_ref, 0, ybuf_ref.at[slot], 0, tot, sem.at[slot]).wait()
    r_i = lax.broadcasted_iota(jnp.int32, (SORT_ROWS, 1), 0)
    y = _unpack_pairs(jnp.where(r_i < tot, ybuf_ref[slot], jnp.uint32(0)))
    moe = jnp.dot(pw, y, preferred_element_type=F32)
    o_ref[...] = _rms(x_ref[...] + moe, g_ref[...])


def _combine(x2, route, tables, ys, g_final):
    t, d = x2.shape
    nt = t // TOK_TILE
    tile = pl.BlockSpec((TOK_TILE, d), lambda i, *_: (i, 0))
    return pl.pallas_call(
        _combine_body,
        grid_spec=pltpu.PrefetchScalarGridSpec(
            num_scalar_prefetch=4,
            grid=(nt,),
            in_specs=[
                tile,
                pl.BlockSpec((TOK_TILE, ROUTE_LANES), lambda i, *_: (i, 0)),
                pl.BlockSpec((1, d), lambda i, *_: (0, 0)),
                pl.BlockSpec(memory_space=pl.ANY),
            ],
            out_specs=tile,
            scratch_shapes=[pltpu.VMEM((2, SORT_ROWS, d // 2), jnp.uint32),
                            pltpu.SemaphoreType.DMA((2,))],
        ),
        out_shape=jax.ShapeDtypeStruct((t, d), F32),
        compiler_params=_cparams(("arbitrary",)),
        name="combine",
    )(tables["cnt"], tables["loff"], tables["gb"], tables["tot"], x2, route, g_final.reshape(1, d), ys)


def _routing_tables(stats, n_tok):
    cnt = stats[:, 0, :N_EXPERTS].astype(jnp.int32)
    loff = stats[:, 1, :N_EXPERTS].astype(jnp.int32)
    n_tiles = cnt.shape[0]
    counts = jnp.sum(cnt, axis=0)
    padded = ((counts + MOE_BLOCK - 1) // MOE_BLOCK) * MOE_BLOCK
    pad_end = jnp.cumsum(padded)
    pad_start = pad_end - padded
    gb = pad_start[None, :] + jnp.cumsum(cnt, axis=0) - cnt
    max_rows = (n_tok * TOP_K + n_tiles * N_EXPERTS * (SUBLANES - 1)
                + N_EXPERTS * (MOE_BLOCK - SUBLANES))
    n_blocks = -(-max_rows // MOE_BLOCK)
    n_slots = n_blocks * MOE_BLOCK
    block_start = jnp.arange(n_blocks, dtype=jnp.int32) * MOE_BLOCK
    block_e = jnp.minimum(jnp.sum((pad_end[None, :] <= block_start[:, None]).astype(jnp.int32), axis=1),
                          N_EXPERTS - 1)
    tables = dict(cnt=cnt.reshape(-1), loff=loff.reshape(-1), gb=gb.reshape(-1).astype(jnp.int32),
                  tot=jnp.sum(cnt, axis=1).astype(jnp.int32),
                  fill_start=(pad_start + counts).astype(jnp.int32),
                  fill_cnt=(padded - counts).astype(jnp.int32))
    n_used = (pad_end[-1:] // MOE_BLOCK).astype(jnp.int32)
    return tables, block_e, n_used, n_slots


def kernel(x, mem, g_mix, w_in, conv_w, conv_b, ln_g, ln_b, ssd_conv_w, ssd_conv_b, dt_bias, a_log, d_skip, ssd_norm_g, w_out, g_xattn, g_mem, w_q, w_k, w_v, w_o, g_moe, w_router_group, b_router_group, w_router_expert, b_router_expert, w_gate, w_up, w_down, g_final):
    bsz, seqlen, d = x.shape
    n_tok = bsz * seqlen
    xt = x.reshape(n_tok, d)
    assert g_mix.shape[0] == 1, "the combine kernel applies the final norm: single layer only"
    for l in range(1):
        u, xbc, z, dt_raw = _mix_in(xt, bsz, seqlen, g_mix[l], w_in[l], conv_w[l], conv_b[l],
                                    ln_g[l], ln_b[l])
        y = _ssd(xbc, z, dt_raw, bsz, seqlen, ssd_conv_w[l], ssd_conv_b[l], dt_bias[l], a_log[l],
                 d_skip[l], ssd_norm_g[l])
        x1 = _out_proj(xt, u, y, w_out[l])
        k, v = _kv_proj(mem, g_mem[l], w_k[l], w_v[l])
        x2, h2, route, stats = _xattn_route(x1, k, v, bsz, seqlen, g_xattn[l], w_q[l], w_o[l], g_moe[l],
                                            w_router_group[l], b_router_group[l], w_router_expert[l],
                                            b_router_expert[l])
        tables, block_e, n_used, n_slots = _routing_tables(stats, n_tok)
        xs = _dispatch(h2, route, tables, n_used, n_slots)
        ys = _experts(xs, block_e, n_used, w_gate[l], w_up[l], w_down[l])
        xt = _combine(x2, route, tables, ys, g_final)
    return xt.reshape(bsz, seqlen, d)
```

```python
import functools

import jax
import jax.numpy as jnp
from jax import lax
from jax.experimental import pallas as pl
from jax.experimental.pallas import tpu as pltpu

F32 = jnp.float32
BF16 = jnp.bfloat16

D_MODEL = 1024
CONV_CH = 1024
CONV_K = 31
SSD_INNER = 1024
SSD_HEAD_DIM = 64
SSD_HEADS = 16
SSD_STATE = 128
SSD_GROUPS = 2
SSD_GROUP_W = SSD_INNER // SSD_GROUPS
SSD_CONV_K = 4
SSD_CHUNK = 128
SSD_BC = SSD_GROUPS * SSD_STATE
SSD_CONV_CH = SSD_INNER + 2 * SSD_BC
N_MAIN = 2 * CONV_CH + 2 * SSD_INNER + 2 * SSD_BC
XA_HEADS = 4
XA_HEAD_DIM = 256
MEM_LEN = 256
N_GROUPS = 4
EXPERTS_PER_GROUP = 8
N_EXPERTS = 32
TOP_K = 2
D_EXPERT = 512
MOE_BLOCK = 512
RMS_EPS = 1e-6
LN_EPS = 1e-5

LANES = 128
SUBLANES = 8
VMEM_LIMIT = 56 * 1024 * 1024

TOK_TILE = 512
CONV_TILE = 512
MM_CHUNK = 512
CONV_HALO = 32
CONV_ROWS = 128
SSD_TILE = 512
XA_TILES = 2
SSD_UNROLL = 4
SSD_HALO = 8
ROUTE_LANES = 128
SORT_ROWS = TOP_K * TOK_TILE + N_EXPERTS * SUBLANES


def _cparams(sem):
    return pltpu.CompilerParams(dimension_semantics=sem, vmem_limit_bytes=VMEM_LIMIT)


def _rms(x, g):
    return x * lax.rsqrt(jnp.mean(x * x, axis=-1, keepdims=True) + RMS_EPS) * g


def _sigmoid(x):
    return 1.0 / (1.0 + jnp.exp(-x))


def _kv_body(m_ref, g_ref, wk_ref, wv_ref, k_ref, v_ref):
    m = _rms(m_ref[0], g_ref[...]).astype(BF16)
    k_ref[0] = jnp.dot(m, wk_ref[...], preferred_element_type=F32).astype(BF16)
    v_ref[0] = jnp.dot(m, wv_ref[...], preferred_element_type=F32).astype(BF16)


def _kv_proj(mem, g_mem, w_k, w_v):
    b, s, d = mem.shape
    w_spec = pl.BlockSpec((d, d), lambda i: (0, 0))
    kv_spec = pl.BlockSpec((1, s, d), lambda i: (i, 0, 0))
    return pl.pallas_call(
        _kv_body,
        grid=(b,),
        in_specs=[kv_spec, pl.BlockSpec((1, d), lambda i: (0, 0)), w_spec, w_spec],
        out_specs=[kv_spec, kv_spec],
        out_shape=[jax.ShapeDtypeStruct((b, s, d), BF16)] * 2,
        compiler_params=_cparams(("arbitrary",)),
        name="kv_proj",
    )(mem, g_mem.reshape(1, d), w_k.astype(BF16), w_v.astype(BF16))


def _mix_in_body(x_ref, g_ref, w_ref, wdt_ref, cw_ref, cb_ref, lg_ref, lb_ref,
                 u_ref, xbc_ref, z_ref, dt_ref, ubuf_ref, acc_ref):
    @pl.when(pl.program_id(1) == 0)
    def _():
        ubuf_ref[0:CONV_HALO, :] = jnp.zeros((CONV_HALO, CONV_CH), F32)

    h = _rms(x_ref[...], g_ref[...]).astype(BF16)
    proj = lambda lo: jnp.dot(h, w_ref[:, lo:lo + MM_CHUNK], preferred_element_type=F32)
    for c in range(CONV_CH // MM_CHUNK):
        cols = slice(c * MM_CHUNK, (c + 1) * MM_CHUNK)
        ubuf_ref[CONV_HALO:CONV_HALO + CONV_TILE, cols] = (
            proj(c * MM_CHUNK) * _sigmoid(proj(CONV_CH + c * MM_CHUNK)))
    z0 = 2 * CONV_CH
    for c in range(SSD_INNER // MM_CHUNK):
        z_ref[:, c * MM_CHUNK:(c + 1) * MM_CHUNK] = proj(z0 + c * MM_CHUNK).astype(BF16)
    x0 = z0 + SSD_INNER
    for c in range(SSD_CONV_CH // MM_CHUNK):
        xbc_ref[:, c * MM_CHUNK:(c + 1) * MM_CHUNK] = proj(x0 + c * MM_CHUNK).astype(BF16)
    dt_ref[...] = jnp.dot(h, wdt_ref[...], preferred_element_type=F32)

    first = CONV_HALO - (CONV_K - 1)
    for rc in range(CONV_TILE // CONV_ROWS):
        r0 = rc * CONV_ROWS
        rows = slice(r0, r0 + CONV_ROWS)
        for cb in range(CONV_CH // LANES):
            cols = slice(cb * LANES, (cb + 1) * LANES)
            acc = None
            for res in range(SUBLANES):
                part = None
                for k in range(CONV_K):
                    if (first + k) % SUBLANES == res:
                        term = cw_ref[k:k + 1, cols] * ubuf_ref[r0 + first + k:r0 + first + k + CONV_ROWS, cols]
                        part = term if part is None else part + term
                acc = part if acc is None else acc + part
            acc_ref[rows, cols] = acc
        u = acc_ref[rows, :] + cb_ref[...]
        mu = jnp.mean(u, axis=-1, keepdims=True)
        uc = u - mu
        var = jnp.mean(uc * uc, axis=-1, keepdims=True)
        y = uc * lax.rsqrt(var + LN_EPS) * lg_ref[...] + lb_ref[...]
        u_ref[rows, :] = (y * _sigmoid(y)).astype(BF16)

    ubuf_ref[0:CONV_HALO, :] = ubuf_ref[CONV_TILE:CONV_TILE + CONV_HALO, :]


def _mix_in(x2d, bsz, seqlen, g_mix, w_in, conv_w, conv_b, ln_g, ln_b):
    t, d = x2d.shape
    nt = seqlen // CONV_TILE
    w_main = w_in[:, :N_MAIN].astype(BF16)
    w_dt = jnp.pad(w_in[:, N_MAIN:], ((0, 0), (0, LANES - SSD_HEADS))).astype(BF16)
    row = lambda v: v.reshape(1, -1)
    const = lambda shape: pl.BlockSpec(shape, lambda b, j: (0, 0))
    tile = lambda width: pl.BlockSpec((CONV_TILE, width), lambda b, j: (b * nt + j, 0))
    return pl.pallas_call(
        _mix_in_body,
        grid=(bsz, nt),
        in_specs=[
            tile(d), const((1, d)), const((d, N_MAIN)), const((d, LANES)),
            const((CONV_K + 1, CONV_CH)), const((1, CONV_CH)), const((1, CONV_CH)), const((1, CONV_CH)),
        ],
        out_specs=[tile(CONV_CH), tile(SSD_CONV_CH), tile(SSD_INNER), tile(LANES)],
        out_shape=[jax.ShapeDtypeStruct((t, CONV_CH), BF16), jax.ShapeDtypeStruct((t, SSD_CONV_CH), BF16),
                   jax.ShapeDtypeStruct((t, SSD_INNER), BF16), jax.ShapeDtypeStruct((t, LANES), F32)],
        scratch_shapes=[
            pltpu.VMEM((CONV_HALO + CONV_TILE, CONV_CH), F32),
            pltpu.VMEM((CONV_TILE, CONV_CH), F32),
        ],
        compiler_params=_cparams(("arbitrary", "arbitrary")),
        name="mix_in",
    )(x2d, row(g_mix), w_main, w_dt, jnp.pad(conv_w, ((0, 1), (0, 0))), row(conv_b), row(ln_g), row(ln_b))


def _ssd_body(xbc_ref, z_ref, dt_ref, cw_ref, cb_ref, dtb_ref, alog_ref, dsk_ref, ng_ref,
              expand_ref, y_ref, xbuf_ref, act_ref, dts_ref, state_ref):
    @pl.when(pl.program_id(1) == 0)
    def _():
        xbuf_ref[0:SSD_HALO, :] = jnp.zeros((SSD_HALO, SSD_CONV_CH), F32)
        state_ref[...] = jnp.zeros(state_ref.shape, F32)

    xbuf_ref[SSD_HALO:SSD_HALO + SSD_TILE, :] = xbc_ref[...].astype(F32)
    full = xbuf_ref[...]
    conv = cb_ref[...] + cw_ref[SSD_CONV_K - 1:SSD_CONV_K, :] * full[SSD_HALO:, :]
    for back in range(1, SSD_CONV_K):
        past = pltpu.roll(full, back, 0)[SSD_HALO:, :]
        conv = conv + cw_ref[SSD_CONV_K - 1 - back:SSD_CONV_K - back, :] * past
    act_ref[...] = conv * _sigmoid(conv)
    xbuf_ref[0:SSD_HALO, :] = xbuf_ref[SSD_TILE:SSD_TILE + SSD_HALO, :]

    dt_in = dt_ref[...] + dtb_ref[...]
    dts_ref[...] = jnp.maximum(dt_in, 0.0) + jnp.log1p(jnp.exp(-jnp.abs(dt_in)))

    a_neg = -jnp.exp(alog_ref[...])
    q = SSD_CHUNK
    row_i = lax.broadcasted_iota(jnp.int32, (q, q), 0)
    col_i = lax.broadcasted_iota(jnp.int32, (q, q), 1)
    causal = row_i >= col_i
    tril = causal.astype(F32)
    lane_i = lax.broadcasted_iota(jnp.int32, (q, LANES), 1)
    low_half = lane_i < SSD_HEAD_DIM
    expand = expand_ref[...]

    def chunk(c, carry):
        r0 = pl.multiple_of(c * q, q)
        rows = pl.ds(r0, q)
        dtc = dts_ref[rows, :]
        a_cs = jnp.dot(tril, dtc * a_neg, preferred_element_type=F32,
                       precision=lax.Precision.HIGHEST)
        a_cs_t = a_cs.T
        dt_t = dtc.T
        a_end = a_cs[q - 1:q, :]
        e_exp = jnp.dot(jnp.exp(a_cs).astype(BF16), expand, preferred_element_type=F32)
        w_exp = jnp.dot((jnp.exp(a_end - a_cs) * dtc).astype(BF16), expand,
                        preferred_element_type=F32)
        dec_row = e_exp[q - 1:q, :]
        xc = act_ref[rows, 0:SSD_INNER]
        xw = (xc * w_exp).astype(BF16)
        y_parts = []
        for g in range(SSD_GROUPS):
            b_f = act_ref[rows, SSD_INNER + g * SSD_STATE:SSD_INNER + (g + 1) * SSD_STATE]
            c_f = act_ref[rows, SSD_INNER + SSD_BC + g * SSD_STATE:
                          SSD_INNER + SSD_BC + (g + 1) * SSD_STATE]
            b_g = b_f.astype(BF16)
            c_g = c_f.astype(BF16)
            cb = lax.dot_general(c_g, b_g, (((1,), (1,)), ((), ())),
                                 preferred_element_type=F32)
            gcols = slice(g * SSD_GROUP_W, (g + 1) * SSD_GROUP_W)
            st = state_ref[g]
            y_off = jnp.dot(c_g, st.astype(BF16), preferred_element_type=F32)
            state_ref[g] = st * dec_row[:, gcols] + jnp.dot(
                b_f.T.astype(BF16), xw[:, gcols], preferred_element_type=F32)
            for pair in range(SSD_GROUP_W // LANES):
                ms = []
                for hh in range(2):
                    h = g * (SSD_HEADS // SSD_GROUPS) + 2 * pair + hh
                    seg = a_cs[:, h:h + 1] - a_cs_t[h:h + 1, :]
                    dec = jnp.exp(jnp.where(causal, seg, -jnp.inf))
                    ms.append((cb * dec * dt_t[h:h + 1, :]).astype(BF16))
                lhs = jnp.concatenate(ms, axis=1)
                xp = xc[:, g * SSD_GROUP_W + pair * LANES:g * SSD_GROUP_W + (pair + 1) * LANES]
                rhs = jnp.concatenate([jnp.where(low_half, xp, 0.0),
                                       jnp.where(low_half, 0.0, xp)], axis=0).astype(BF16)
                y_diag = jnp.dot(lhs, rhs, preferred_element_type=F32)
                lo = pair * LANES
                y_parts.append(y_diag + y_off[:, lo:lo + LANES]
                               * e_exp[:, g * SSD_GROUP_W + lo:g * SSD_GROUP_W + lo + LANES])
        y = jnp.concatenate(y_parts, axis=1) + xc * dsk_ref[...]
        z = z_ref[rows, :].astype(F32)
        y = y * (z * _sigmoid(z))
        outs = []
        for g in range(SSD_GROUPS):
            yg = y[:, g * SSD_GROUP_W:(g + 1) * SSD_GROUP_W]
            outs.append(yg * lax.rsqrt(jnp.mean(yg * yg, axis=-1, keepdims=True) + RMS_EPS))
        y_ref[rows, :] = (jnp.concatenate(outs, axis=1) * ng_ref[...]).astype(BF16)
        return carry

    lax.fori_loop(0, SSD_TILE // q, chunk, 0, unroll=SSD_UNROLL)


def _ssd(xbc, z, dt_raw, bsz, seqlen, ssd_conv_w, ssd_conv_b, dt_bias, a_log, d_skip, ssd_norm_g):
    nt = seqlen // SSD_TILE
    pad_h = lambda v: jnp.pad(v, (0, LANES - SSD_HEADS)).reshape(1, LANES)
    expand = (jnp.arange(LANES)[:, None] == (jnp.arange(SSD_INNER) // SSD_HEAD_DIM)[None, :]).astype(BF16)
    const = lambda shape: pl.BlockSpec(shape, lambda b, j: (0, 0))
    return pl.pallas_call(
        _ssd_body,
        grid=(bsz, nt),
        in_specs=[
            pl.BlockSpec((SSD_TILE, SSD_CONV_CH), lambda b, j: (b * nt + j, 0)),
            pl.BlockSpec((SSD_TILE, SSD_INNER), lambda b, j: (b * nt + j, 0)),
            pl.BlockSpec((SSD_TILE, LANES), lambda b, j: (b * nt + j, 0)),
            const((SSD_CONV_K, SSD_CONV_CH)),
            const((1, SSD_CONV_CH)),
            const((1, LANES)), const((1, LANES)),
            const((1, SSD_INNER)), const((1, SSD_INNER)),
            const((LANES, SSD_INNER)),
        ],
        out_specs=pl.BlockSpec((SSD_TILE, SSD_INNER), lambda b, j: (b * nt + j, 0)),
        out_shape=jax.ShapeDtypeStruct((bsz * seqlen, SSD_INNER), BF16),
        scratch_shapes=[
            pltpu.VMEM((SSD_HALO + SSD_TILE, SSD_CONV_CH), F32),
            pltpu.VMEM((SSD_TILE, SSD_CONV_CH), F32),
            pltpu.VMEM((SSD_TILE, LANES), F32),
            pltpu.VMEM((SSD_GROUPS, SSD_STATE, SSD_GROUP_W), F32),
        ],
        compiler_params=_cparams(("arbitrary", "arbitrary")),
        name="ssd",
    )(xbc, z, dt_raw, ssd_conv_w, ssd_conv_b.reshape(1, SSD_CONV_CH), pad_h(dt_bias),
      pad_h(a_log), jnp.repeat(d_skip, SSD_HEAD_DIM).reshape(1, SSD_INNER),
      ssd_norm_g.reshape(1, SSD_INNER), expand)


def _outproj_body(x_ref, u_ref, y_ref, wu_ref, wy_ref, o_ref):
    o_ref[...] = (x_ref[...]
                  + jnp.dot(u_ref[...], wu_ref[...], preferred_element_type=F32)
                  + jnp.dot(y_ref[...], wy_ref[...], preferred_element_type=F32))


def _out_proj(x2d, u, y, w_out):
    t, d = x2d.shape
    w = w_out.astype(BF16)
    tile = pl.BlockSpec((TOK_TILE, d), lambda i: (i, 0))
    w_spec = pl.BlockSpec((d, d), lambda i: (0, 0))
    return pl.pallas_call(
        _outproj_body,
        grid=(t // TOK_TILE,),
        in_specs=[tile, tile, tile, w_spec, w_spec],
        out_specs=tile,
        out_shape=jax.ShapeDtypeStruct((t, d), F32),
        compiler_params=_cparams(("arbitrary",)),
        name="out_proj",
    )(x2d, u, y, w[:CONV_CH], w[CONV_CH:])


def _xattn_body(x_ref, k_ref, v_ref, gx_ref, wq_ref, wo_ref, gm_ref, wr_hi_ref, wr_lo_ref, br_ref,
                x2_ref, h2_ref, route_ref, stats_ref):
    for s in range(XA_TILES):
        rows = slice(s * TOK_TILE, (s + 1) * TOK_TILE)
        x2, h2, route, stats = _xattn_tile(x_ref[rows, :], k_ref, v_ref, gx_ref, wq_ref, wo_ref, gm_ref,
                                           wr_hi_ref, wr_lo_ref, br_ref)
        x2_ref[rows, :] = x2
        h2_ref[rows, :] = h2
        route_ref[rows, :] = route
        stats_ref[s] = stats


def _xattn_tile(x, k_ref, v_ref, gx_ref, wq_ref, wo_ref, gm_ref, wr_hi_ref, wr_lo_ref, br_ref):
    h = _rms(x, gx_ref[...]).astype(BF16)
    q = (jnp.dot(h, wq_ref[...], preferred_element_type=F32) * (XA_HEAD_DIM ** -0.5)).astype(BF16)
    heads = []
    for i in range(XA_HEADS):
        cols = slice(i * XA_HEAD_DIM, (i + 1) * XA_HEAD_DIM)
        s = jnp.dot(q[:, cols], k_ref[0, cols, :], preferred_element_type=F32)
        p = jnp.exp(s - jnp.max(s, axis=-1, keepdims=True))
        p = p / jnp.sum(p, axis=-1, keepdims=True)
        heads.append(jnp.dot(p.astype(BF16), v_ref[0, :, cols], preferred_element_type=F32))
    o = jnp.concatenate(heads, axis=1).astype(BF16)
    x2 = x + jnp.dot(o, wo_ref[...], preferred_element_type=F32)

    h2 = _rms(x2, gm_ref[...])
    h_hi = h2.astype(BF16)
    h_lo = (h2 - h_hi.astype(F32)).astype(BF16)
    logits = (jnp.dot(h_hi, wr_hi_ref[...], preferred_element_type=F32)
              + jnp.dot(h_lo, wr_hi_ref[...], preferred_element_type=F32)
              + jnp.dot(h_hi, wr_lo_ref[...], preferred_element_type=F32)) + br_ref[...]
    lane = lax.broadcasted_iota(jnp.int32, logits.shape, 1)
    neg = -jnp.inf

    def first_argmax(v):
        m = jnp.max(v, axis=-1, keepdims=True)
        return m, jnp.min(jnp.where(v == m, lane, ROUTE_LANES), axis=-1, keepdims=True)

    gl = jnp.where(lane < N_GROUPS, logits, neg)
    g_max, g_sel = first_argmax(gl)
    p_top = 1.0 / jnp.sum(jnp.exp(gl - g_max), axis=-1, keepdims=True)
    e_lo = N_GROUPS + EXPERTS_PER_GROUP * g_sel
    el = jnp.where((lane >= e_lo) & (lane < e_lo + EXPERTS_PER_GROUP), logits, neg)
    m1, i1 = first_argmax(el)
    m2, i2 = first_argmax(jnp.where(lane == i1, neg, el))
    r = jnp.exp(m2 - m1)
    w1 = p_top / (1.0 + r)
    w2 = w1 * r
    e1 = i1 - N_GROUPS
    e2 = i2 - N_GROUPS
    oh1 = (lane == e1).astype(BF16)
    oh2 = (lane == e2).astype(BF16)
    n_t = logits.shape[0]
    before = (lax.broadcasted_iota(jnp.int32, (n_t, n_t), 0)
              > lax.broadcasted_iota(jnp.int32, (n_t, n_t), 1)).astype(BF16)
    c1 = jnp.dot(before, oh1, preferred_element_type=F32)
    c2 = jnp.dot(before, oh2, preferred_element_type=F32)
    tot1 = jnp.sum(oh1.astype(F32), axis=0, keepdims=True)
    cnt = tot1 + jnp.sum(oh2.astype(F32), axis=0, keepdims=True)
    cnt = jnp.floor((cnt + (SUBLANES - 1)) * (1.0 / SUBLANES)) * SUBLANES
    lanes_before = (lax.broadcasted_iota(jnp.int32, (ROUTE_LANES, ROUTE_LANES), 0)
                    < lax.broadcasted_iota(jnp.int32, (ROUTE_LANES, ROUTE_LANES), 1)).astype(F32)
    loff = jnp.dot(jnp.broadcast_to(cnt, (SUBLANES, ROUTE_LANES)), lanes_before,
                   preferred_element_type=F32, precision=lax.Precision.HIGHEST)[0:1, :]
    lp1 = jnp.sum(oh1.astype(F32) * (loff + c1), axis=-1, keepdims=True)
    lp2 = jnp.sum(oh2.astype(F32) * (loff + tot1 + c2), axis=-1, keepdims=True)
    route = jnp.where(lane == 0, e1.astype(F32),
            jnp.where(lane == 1, e2.astype(F32),
            jnp.where(lane == 2, w1,
            jnp.where(lane == 3, w2,
            jnp.where(lane == 4, lp1, jnp.where(lane == 5, lp2, 0.0))))))
    row = lax.broadcasted_iota(jnp.int32, (SUBLANES, ROUTE_LANES), 0)
    return x2, h_hi, route, jnp.where(row == 0, cnt, jnp.where(row == 1, loff, 0.0))


def _xattn_route(x1, k, v, bsz, seqlen, g_xattn, w_q, w_o, g_moe, w_rg, b_rg, w_re, b_re):
    t, d = x1.shape
    w_r = jnp.pad(jnp.concatenate([w_rg, w_re], axis=1), ((0, 0), (0, ROUTE_LANES - N_GROUPS - N_EXPERTS)))
    b_r = jnp.pad(jnp.concatenate([b_rg, b_re]), (0, ROUTE_LANES - N_GROUPS - N_EXPERTS)).reshape(1, ROUTE_LANES)
    wr_hi = w_r.astype(BF16)
    wr_lo = (w_r - wr_hi.astype(F32)).astype(BF16)
    step_rows = XA_TILES * TOK_TILE
    nt = seqlen // step_rows
    tile = pl.BlockSpec((step_rows, d), lambda b, j: (b * nt + j, 0))
    kv_spec = pl.BlockSpec((1, MEM_LEN, d), lambda b, j: (b, 0, 0))
    const = lambda shape: pl.BlockSpec(shape, lambda b, j: (0, 0))
    return pl.pallas_call(
        _xattn_body,
        grid=(bsz, nt),
        in_specs=[tile, pl.BlockSpec((1, d, MEM_LEN), lambda b, j: (b, 0, 0)), kv_spec, const((1, d)),
                  const((d, d)), const((d, d)), const((1, d)),
                  const((d, ROUTE_LANES)), const((d, ROUTE_LANES)), const((1, ROUTE_LANES))],
        out_specs=[tile, tile, pl.BlockSpec((step_rows, ROUTE_LANES), lambda b, j: (b * nt + j, 0)),
                   pl.BlockSpec((XA_TILES, SUBLANES, ROUTE_LANES), lambda b, j: (b * nt + j, 0, 0))],
        out_shape=[jax.ShapeDtypeStruct((t, d), F32), jax.ShapeDtypeStruct((t, d), BF16),
                   jax.ShapeDtypeStruct((t, ROUTE_LANES), F32),
                   jax.ShapeDtypeStruct((t // TOK_TILE, SUBLANES, ROUTE_LANES), F32)],
        compiler_params=_cparams(("arbitrary", "arbitrary")),
        name="xattn_route",
    )(x1, jnp.swapaxes(k, 1, 2), v, g_xattn.reshape(1, d), w_q.astype(BF16), w_o.astype(BF16), g_moe.reshape(1, d),
      wr_hi, wr_lo, b_r)


def _rows_copy(src_ref, src0, dst_ref, dst0, n, sem):
    rows = lambda r0: pl.ds(pl.multiple_of(r0, SUBLANES), pl.multiple_of(n, SUBLANES))
    return pltpu.make_async_copy(src_ref.at[rows(src0), :], dst_ref.at[rows(dst0), :], sem)


def _pack_pairs(v):
    half = v.shape[1] // 2
    lo = lax.bitcast_convert_type(v[:, :half], jnp.uint32)
    hi = lax.bitcast_convert_type(v[:, half:], jnp.uint32)
    return (lo >> 16) | (hi & jnp.uint32(0xFFFF0000))


def _unpack_pairs(w):
    lo = lax.bitcast_convert_type(w << 16, F32)
    hi = lax.bitcast_convert_type(w & jnp.uint32(0xFFFF0000), F32)
    return jnp.concatenate([lo, hi], axis=1).astype(BF16)


def _dispatch_body(cnt_ref, loff_ref, gb_ref, tot_ref, fstart_ref, fcnt_ref, nu_ref, h_ref, route_ref,
                   xs_ref, buf_ref, zero_ref, sem, zsem):
    i = pl.program_id(0)
    nt = pl.num_programs(0)
    slot = i % 2
    buf = buf_ref.at[slot]

    def drain(tile, s):
        _rows_copy(buf_ref.at[s], 0, xs_ref, 0, tot_ref[tile], sem.at[s]).wait()

    @pl.when(i >= 2)
    def _():
        drain(i - 2, slot)

    rt = route_ref[...].T
    lp1 = rt[4:5, :].astype(jnp.int32)
    lp2 = rt[5:6, :].astype(jnp.int32)
    r_i = lax.broadcasted_iota(jnp.int32, (SORT_ROWS, TOK_TILE), 0)
    perm = jnp.where((r_i == lp1) | (r_i == lp2), 1.0, 0.0).astype(BF16)
    buf[...] = _pack_pairs(jnp.dot(perm, h_ref[...], preferred_element_type=F32))

    def per_expert(e, c):
        j = i * N_EXPERTS + e

        @pl.when(cnt_ref[j] > 0)
        def _():
            _rows_copy(buf, loff_ref[j], xs_ref, gb_ref[j], cnt_ref[j], sem.at[slot]).start()
        return c

    lax.fori_loop(0, N_EXPERTS, per_expert, 0)

    @pl.when(i == 0)
    def _():
        zero_ref[...] = jnp.zeros(zero_ref.shape, zero_ref.dtype)

        def fill(start):
            def body(e, c):
                @pl.when(fcnt_ref[e] > 0)
                def _():
                    copy = _rows_copy(zero_ref, 0, xs_ref, fstart_ref[e], fcnt_ref[e], zsem)
                    copy.start() if start else copy.wait()
                return c
            lax.fori_loop(0, N_EXPERTS, body, 0)

            def tail(blk, c):
                copy = _rows_copy(zero_ref, 0, xs_ref, blk * MOE_BLOCK, MOE_BLOCK, zsem)
                copy.start() if start else copy.wait()
                return c
            lax.fori_loop(nu_ref[0], xs_ref.shape[0] // MOE_BLOCK, tail, 0)

        fill(True)
        fill(False)

    @pl.when(i == nt - 1)
    def _():
        drain(i, slot)

        @pl.when(i >= 1)
        def _():
            drain(i - 1, 1 - slot)


def _dispatch(h2, route, tables, n_used, n_slots):
    t, d = h2.shape
    nt = t // TOK_TILE
    return pl.pallas_call(
        _dispatch_body,
        grid_spec=pltpu.PrefetchScalarGridSpec(
            num_scalar_prefetch=7,
            grid=(nt,),
            in_specs=[
                pl.BlockSpec((TOK_TILE, d), lambda i, *_: (i, 0)),
                pl.BlockSpec((TOK_TILE, ROUTE_LANES), lambda i, *_: (i, 0)),
            ],
            out_specs=pl.BlockSpec(memory_space=pl.ANY),
            scratch_shapes=[pltpu.VMEM((2, SORT_ROWS, d // 2), jnp.uint32),
                            pltpu.VMEM((MOE_BLOCK, d // 2), jnp.uint32),
                            pltpu.SemaphoreType.DMA((2,)), pltpu.SemaphoreType.DMA(())],
        ),
        out_shape=jax.ShapeDtypeStruct((n_slots, d // 2), jnp.uint32),
        compiler_params=_cparams(("arbitrary",)),
        name="dispatch",
    )(tables["cnt"], tables["loff"], tables["gb"], tables["tot"], tables["fill_start"],
      tables["fill_cnt"], n_used, h2, route)


def _experts_body(be_ref, nu_ref, x_ref, wg_ref, wu_ref, wd_ref, y_ref, wg_bf, wu_bf, wd_bf):
    b = pl.program_id(0)
    used = b < nu_ref[0]

    @pl.when(used & ((b == 0) | (be_ref[b] != be_ref[jnp.maximum(b - 1, 0)])))
    def _():
        wg_bf[...] = wg_ref[0].astype(BF16)
        wu_bf[...] = wu_ref[0].astype(BF16)
        wd_bf[...] = wd_ref[0].astype(BF16)

    @pl.when(used)
    def _():
        x = _unpack_pairs(x_ref[...])
        g = jnp.dot(x, wg_bf[...], preferred_element_type=F32)
        u = jnp.dot(x, wu_bf[...], preferred_element_type=F32)
        a = (g * _sigmoid(g) * u).astype(BF16)
        y = jnp.dot(a, wd_bf[...], preferred_element_type=F32)
        y_ref[...] = _pack_pairs(y.astype(BF16).astype(F32))

    @pl.when(jnp.logical_not(used))
    def _():
        y_ref[...] = jnp.zeros(y_ref.shape, y_ref.dtype)


def _experts(xs, block_e, n_used, w_gate, w_up, w_down):
    n_slots, dp = xs.shape
    d = w_gate.shape[1]
    n_blocks = n_slots // MOE_BLOCK
    last = lambda b, nu: jnp.maximum(jnp.minimum(b, nu[0] - 1), 0)
    tile = pl.BlockSpec((MOE_BLOCK, dp), lambda b, be, nu: (last(b, nu), 0))
    return pl.pallas_call(
        _experts_body,
        grid_spec=pltpu.PrefetchScalarGridSpec(
            num_scalar_prefetch=2,
            grid=(n_blocks,),
            in_specs=[
                tile,
                pl.BlockSpec((1, d, D_EXPERT), lambda b, be, nu: (be[last(b, nu)], 0, 0)),
                pl.BlockSpec((1, d, D_EXPERT), lambda b, be, nu: (be[last(b, nu)], 0, 0)),
                pl.BlockSpec((1, D_EXPERT, d), lambda b, be, nu: (be[last(b, nu)], 0, 0)),
            ],
            out_specs=pl.BlockSpec((MOE_BLOCK, dp), lambda b, be, nu: (b, 0)),
            scratch_shapes=[pltpu.VMEM((d, D_EXPERT), BF16), pltpu.VMEM((d, D_EXPERT), BF16),
                            pltpu.VMEM((D_EXPERT, d), BF16)],
        ),
        out_shape=jax.ShapeDtypeStruct((n_slots, dp), jnp.uint32),
        compiler_params=_cparams(("arbitrary",)),
        name="experts",
    )(block_e, n_used, xs, w_gate, w_up, w_down)


def _combine_body(cnt_ref, loff_ref, gb_ref, tot_ref, x_ref, route_ref, g_ref, ys_ref, o_ref, ybuf_ref, sem):
    i = pl.program_id(0)
    nt = pl.num_programs(0)
    slot = i % 2

    def gather(tile, s):
        def per_expert(e, c):
            j = tile * N_EXPERTS + e

            @pl.when(cnt_ref[j] > 0)
            def _():
                _rows_copy(ys_ref, gb_ref[j], ybuf_ref.at[s], loff_ref[j], cnt_ref[j], sem.at[s]).start()
            return c

        lax.fori_loop(0, N_EXPERTS, per_expert, 0)

    @pl.when(i == 0)
    def _():
        ybuf_ref[...] = jnp.zeros(ybuf_ref.shape, ybuf_ref.dtype)
        gather(0, 0)

    @pl.when(i + 1 < nt)
    def _():
        gather(i + 1, 1 - slot)

    route = route_ref[...]
    c_i = lax.broadcasted_iota(jnp.int32, (TOK_TILE, SORT_ROWS), 1)
    lp1 = route[:, 4:5].astype(jnp.int32)
    lp2 = route[:, 5:6].astype(jnp.int32)
    pw = jnp.where(c_i == lp1, route[:, 2:3], jnp.where(c_i == lp2, route[:, 3:4], 0.0)).astype(BF16)
    tot = tot_ref[i]
    _rows_copy(ys_ref, 0, ybuf_ref.at[slot], 0, tot, sem.at[slot]).wait()
    r_i = lax.broadcasted_iota(jnp.int32, (SORT_ROWS, 1), 0)
    y = _unpack_pairs(jnp.where(r_i < tot, ybuf_ref[slot], jnp.uint32(0)))
    moe = jnp.dot(pw, y, preferred_element_type=F32)
    o_ref[...] = _rms(x_ref[...] + moe, g_ref[...])


def _combine(x2, route, tables, ys, g_final):
    t, d = x2.shape
    nt = t // TOK_TILE
    tile = pl.BlockSpec((TOK_TILE, d), lambda i, *_: (i, 0))
    return pl.pallas_call(
        _combine_body,
        grid_spec=pltpu.PrefetchScalarGridSpec(
            num_scalar_prefetch=4,
            grid=(nt,),
            in_specs=[
                tile,
                pl.BlockSpec((TOK_TILE, ROUTE_LANES), lambda i, *_: (i, 0)),
                pl.BlockSpec((1, d), lambda i, *_: (0, 0)),
                pl.BlockSpec(memory_space=pl.ANY),
            ],
            out_specs=tile,
            scratch_shapes=[pltpu.VMEM((2, SORT_ROWS, d // 2), jnp.uint32),
                            pltpu.SemaphoreType.DMA((2,))],
        ),
        out_shape=jax.ShapeDtypeStruct((t, d), F32),
        compiler_params=_cparams(("arbitrary",)),
        name="combine",
    )(tables["cnt"], tables["loff"], tables["gb"], tables["tot"], x2, route, g_final.reshape(1, d), ys)


def _routing_tables(stats, n_tok):
    cnt = stats[:, 0, :N_EXPERTS].astype(jnp.int32)
    loff = stats[:, 1, :N_EXPERTS].astype(jnp.int32)
    n_tiles = cnt.shape[0]
    counts = jnp.sum(cnt, axis=0)
    padded = ((counts + MOE_BLOCK - 1) // MOE_BLOCK) * MOE_BLOCK
    pad_end = jnp.cumsum(padded)
    pad_start = pad_end - padded
    gb = pad_start[None, :] + jnp.cumsum(cnt, axis=0) - cnt
    max_rows = (n_tok * TOP_K + n_tiles * N_EXPERTS * (SUBLANES - 1)
                + N_EXPERTS * (MOE_BLOCK - SUBLANES))
    n_blocks = -(-max_rows // MOE_BLOCK)
    n_slots = n_blocks * MOE_BLOCK
    block_start = jnp.arange(n_blocks, dtype=jnp.int32) * MOE_BLOCK
    block_e = jnp.minimum(jnp.sum((pad_end[None, :] <= block_start[:, None]).astype(jnp.int32), axis=1),
                          N_EXPERTS - 1)
    tables = dict(cnt=cnt.reshape(-1), loff=loff.reshape(-1), gb=gb.reshape(-1).astype(jnp.int32),
                  tot=jnp.sum(cnt, axis=1).astype(jnp.int32),
                  fill_start=(pad_start + counts).astype(jnp.int32),
                  fill_cnt=(padded - counts).astype(jnp.int32))
    n_used = (pad_end[-1:] // MOE_BLOCK).astype(jnp.int32)
    return tables, block_e, n_used, n_slots


def kernel(x, mem, g_mix, w_in, conv_w, conv_b, ln_g, ln_b, ssd_conv_w, ssd_conv_b, dt_bias, a_log, d_skip, ssd_norm_g, w_out, g_xattn, g_mem, w_q, w_k, w_v, w_o, g_moe, w_router_group, b_router_group, w_router_expert, b_router_expert, w_gate, w_up, w_down, g_final):
    bsz, seqlen, d = x.shape
    n_tok = bsz * seqlen
    xt = x.reshape(n_tok, d)
    assert g_mix.shape[0] == 1, "the combine kernel applies the final norm: single layer only"
    for l in range(1):
        u, xbc, z, dt_raw = _mix_in(xt, bsz, seqlen, g_mix[l], w_in[l], conv_w[l], conv_b[l],
                                    ln_g[l], ln_b[l])
        y = _ssd(xbc, z, dt_raw, bsz, seqlen, ssd_conv_w[l], ssd_conv_b[l], dt_bias[l], a_log[l],
                 d_skip[l], ssd_norm_g[l])
        x1 = _out_proj(xt, u, y, w_out[l])
        k, v = _kv_proj(mem, g_mem[l], w_k[l], w_v[l])
        x2, h2, route, stats = _xattn_route(x1, k, v, bsz, seqlen, g_xattn[l], w_q[l], w_o[l], g_moe[l],
                                            w_router_group[l], b_router_group[l], w_router_expert[l],
                                            b_router_expert[l])
        tables, block_e, n_used, n_slots = _routing_tables(stats, n_tok)
        xs = _dispatch(h2, route, tables, n_used, n_slots)
        ys = _experts(xs, block_e, n_used, w_gate[l], w_up[l], w_down[l])
        xt = _combine(x2, route, tables, ys, g_final)
    return xt.reshape(bsz, seqlen, d)
```

```python
import functools

import jax
import jax.numpy as jnp
from jax import lax
from jax.experimental import pallas as pl
from jax.experimental.pallas import tpu as pltpu

F32 = jnp.float32
BF16 = jnp.bfloat16

D_MODEL = 1024
CONV_CH = 1024
CONV_K = 31
SSD_INNER = 1024
SSD_HEAD_DIM = 64
SSD_HEADS = 16
SSD_STATE = 128
SSD_GROUPS = 2
SSD_GROUP_W = SSD_INNER // SSD_GROUPS
SSD_CONV_K = 4
SSD_CHUNK = 128
SSD_BC = SSD_GROUPS * SSD_STATE
SSD_CONV_CH = SSD_INNER + 2 * SSD_BC
N_MAIN = 2 * CONV_CH + 2 * SSD_INNER + 2 * SSD_BC
XA_HEADS = 4
XA_HEAD_DIM = 256
MEM_LEN = 256
N_GROUPS = 4
EXPERTS_PER_GROUP = 8
N_EXPERTS = 32
TOP_K = 2
D_EXPERT = 512
MOE_BLOCK = 512
RMS_EPS = 1e-6
LN_EPS = 1e-5

LANES = 128
SUBLANES = 8
VMEM_LIMIT = 56 * 1024 * 1024

TOK_TILE = 512
CONV_TILE = 512
MM_CHUNK = 512
CONV_HALO = 32
CONV_ROWS = 128
SSD_TILE = 512
XA_TILES = 2
SSD_UNROLL = 4
SSD_HALO = 8
ROUTE_LANES = 128
SORT_ROWS = TOP_K * TOK_TILE + N_EXPERTS * SUBLANES


def _cparams(sem):
    return pltpu.CompilerParams(dimension_semantics=sem, vmem_limit_bytes=VMEM_LIMIT)


def _rms(x, g):
    return x * lax.rsqrt(jnp.mean(x * x, axis=-1, keepdims=True) + RMS_EPS) * g


def _sigmoid(x):
    return 1.0 / (1.0 + jnp.exp(-x))


def _kv_body(m_ref, g_ref, wk_ref, wv_ref, k_ref, v_ref):
    m = _rms(m_ref[0], g_ref[...]).astype(BF16)
    k_ref[0] = jnp.dot(m, wk_ref[...], preferred_element_type=F32).astype(BF16)
    v_ref[0] = jnp.dot(m, wv_ref[...], preferred_element_type=F32).astype(BF16)


def _kv_proj(mem, g_mem, w_k, w_v):
    b, s, d = mem.shape
    w_spec = pl.BlockSpec((d, d), lambda i: (0, 0))
    kv_spec = pl.BlockSpec((1, s, d), lambda i: (i, 0, 0))
    return pl.pallas_call(
        _kv_body,
        grid=(b,),
        in_specs=[kv_spec, pl.BlockSpec((1, d), lambda i: (0, 0)), w_spec, w_spec],
        out_specs=[kv_spec, kv_spec],
        out_shape=[jax.ShapeDtypeStruct((b, s, d), BF16)] * 2,
        compiler_params=_cparams(("arbitrary",)),
        name="kv_proj",
    )(mem, g_mem.reshape(1, d), w_k.astype(BF16), w_v.astype(BF16))


def _mix_in_body(x_ref, g_ref, w_ref, wdt_ref, cw_ref, cb_ref, lg_ref, lb_ref,
                 u_ref, xbc_ref, z_ref, dt_ref, ubuf_ref, acc_ref):
    @pl.when(pl.program_id(1) == 0)
    def _():
        ubuf_ref[0:CONV_HALO, :] = jnp.zeros((CONV_HALO, CONV_CH), F32)

    h = _rms(x_ref[...], g_ref[...]).astype(BF16)
    proj = lambda lo: jnp.dot(h, w_ref[:, lo:lo + MM_CHUNK], preferred_element_type=F32)
    for c in range(CONV_CH // MM_CHUNK):
        cols = slice(c * MM_CHUNK, (c + 1) * MM_CHUNK)
        ubuf_ref[CONV_HALO:CONV_HALO + CONV_TILE, cols] = (
            proj(c * MM_CHUNK) * _sigmoid(proj(CONV_CH + c * MM_CHUNK)))
    z0 = 2 * CONV_CH
    for c in range(SSD_INNER // MM_CHUNK):
        z_ref[:, c * MM_CHUNK:(c + 1) * MM_CHUNK] = proj(z0 + c * MM_CHUNK).astype(BF16)
    x0 = z0 + SSD_INNER
    for c in range(SSD_CONV_CH // MM_CHUNK):
        xbc_ref[:, c * MM_CHUNK:(c + 1) * MM_CHUNK] = proj(x0 + c * MM_CHUNK).astype(BF16)
    dt_ref[...] = jnp.dot(h, wdt_ref[...], preferred_element_type=F32)

    first = CONV_HALO - (CONV_K - 1)
    for rc in range(CONV_TILE // CONV_ROWS):
        r0 = rc * CONV_ROWS
        rows = slice(r0, r0 + CONV_ROWS)
        for cb in range(CONV_CH // LANES):
            cols = slice(cb * LANES, (cb + 1) * LANES)
            acc = None
            for res in range(SUBLANES):
                part = None
                for k in range(CONV_K):
                    if (first + k) % SUBLANES == res:
                        term = cw_ref[k:k + 1, cols] * ubuf_ref[r0 + first + k:r0 + first + k + CONV_ROWS, cols]
                        part = term if part is None else part + term
                acc = part if acc is None else acc + part
            acc_ref[rows, cols] = acc
        u = acc_ref[rows, :] + cb_ref[...]
        mu = jnp.mean(u, axis=-1, keepdims=True)
        uc = u - mu
        var = jnp.mean(uc * uc, axis=-1, keepdims=True)
        y = uc * lax.rsqrt(var + LN_EPS) * lg_ref[...] + lb_ref[...]
        u_ref[rows, :] = (y * _sigmoid(y)).astype(BF16)

    ubuf_ref[0:CONV_HALO, :] = ubuf_ref[CONV_TILE:CONV_TILE + CONV_HALO, :]


def _mix_in(x2d, bsz, seqlen, g_mix, w_in, conv_w, conv_b, ln_g, ln_b):
    t, d = x2d.shape
    nt = seqlen // CONV_TILE
    w_main = w_in[:, :N_MAIN].astype(BF16)
    w_dt = jnp.pad(w_in[:, N_MAIN:], ((0, 0), (0, LANES - SSD_HEADS))).astype(BF16)
    row = lambda v: v.reshape(1, -1)
    const = lambda shape: pl.BlockSpec(shape, lambda b, j: (0, 0))
    tile = lambda width: pl.BlockSpec((CONV_TILE, width), lambda b, j: (b * nt + j, 0))
    return pl.pallas_call(
        _mix_in_body,
        grid=(bsz, nt),
        in_specs=[
            tile(d), const((1, d)), const((d, N_MAIN)), const((d, LANES)),
            const((CONV_K + 1, CONV_CH)), const((1, CONV_CH)), const((1, CONV_CH)), const((1, CONV_CH)),
        ],
        out_specs=[tile(CONV_CH), tile(SSD_CONV_CH), tile(SSD_INNER), tile(LANES)],
        out_shape=[jax.ShapeDtypeStruct((t, CONV_CH), BF16), jax.ShapeDtypeStruct((t, SSD_CONV_CH), BF16),
                   jax.ShapeDtypeStruct((t, SSD_INNER), BF16), jax.ShapeDtypeStruct((t, LANES), F32)],
        scratch_shapes=[
            pltpu.VMEM((CONV_HALO + CONV_TILE, CONV_CH), F32),
            pltpu.VMEM((CONV_TILE, CONV_CH), F32),
        ],
        compiler_params=_cparams(("arbitrary", "arbitrary")),
        name="mix_in",
    )(x2d, row(g_mix), w_main, w_dt, jnp.pad(conv_w, ((0, 1), (0, 0))), row(conv_b), row(ln_g), row(ln_b))


def _ssd_body(xbc_ref, z_ref, dt_ref, cw_ref, cb_ref, dtb_ref, alog_ref, dsk_ref, ng_ref,
              expand_ref, y_ref, xbuf_ref, act_ref, dts_ref, state_ref):
    @pl.when(pl.program_id(1) == 0)
    def _():
        xbuf_ref[0:SSD_HALO, :] = jnp.zeros((SSD_HALO, SSD_CONV_CH), F32)
        state_ref[...] = jnp.zeros(state_ref.shape, F32)

    xbuf_ref[SSD_HALO:SSD_HALO + SSD_TILE, :] = xbc_ref[...].astype(F32)
    full = xbuf_ref[...]
    conv = cb_ref[...] + cw_ref[SSD_CONV_K - 1:SSD_CONV_K, :] * full[SSD_HALO:, :]
    for back in range(1, SSD_CONV_K):
        past = pltpu.roll(full, back, 0)[SSD_HALO:, :]
        conv = conv + cw_ref[SSD_CONV_K - 1 - back:SSD_CONV_K - back, :] * past
    act_ref[...] = conv * _sigmoid(conv)
    xbuf_ref[0:SSD_HALO, :] = xbuf_ref[SSD_TILE:SSD_TILE + SSD_HALO, :]

    dt_in = dt_ref[...] + dtb_ref[...]
    dts_ref[...] = jnp.maximum(dt_in, 0.0) + jnp.log1p(jnp.exp(-jnp.abs(dt_in)))

    a_neg = -jnp.exp(alog_ref[...])
    q = SSD_CHUNK
    row_i = lax.broadcasted_iota(jnp.int32, (q, q), 0)
    col_i = lax.broadcasted_iota(jnp.int32, (q, q), 1)
    causal = row_i >= col_i
    tril = causal.astype(F32)
    lane_i = lax.broadcasted_iota(jnp.int32, (q, LANES), 1)
    low_half = lane_i < SSD_HEAD_DIM
    expand = expand_ref[...]

    def chunk(c, carry):
        r0 = pl.multiple_of(c * q, q)
        rows = pl.ds(r0, q)
        dtc = dts_ref[rows, :]
        a_cs = jnp.dot(tril, dtc * a_neg, preferred_element_type=F32,
                       precision=lax.Precision.HIGHEST)
        a_cs_t = a_cs.T
        dt_t = dtc.T
        a_end = a_cs[q - 1:q, :]
        e_exp = jnp.dot(jnp.exp(a_cs).astype(BF16), expand, preferred_element_type=F32)
        w_exp = jnp.dot((jnp.exp(a_end - a_cs) * dtc).astype(BF16), expand,
                        preferred_element_type=F32)
        dec_row = e_exp[q - 1:q, :]
        xc = act_ref[rows, 0:SSD_INNER]
        xw = (xc * w_exp).astype(BF16)
        y_parts = []
        for g in range(SSD_GROUPS):
            b_f = act_ref[rows, SSD_INNER + g * SSD_STATE:SSD_INNER + (g + 1) * SSD_STATE]
            c_f = act_ref[rows, SSD_INNER + SSD_BC + g * SSD_STATE:
                          SSD_INNER + SSD_BC + (g + 1) * SSD_STATE]
            b_g = b_f.astype(BF16)
            c_g = c_f.astype(BF16)
            cb = lax.dot_general(c_g, b_g, (((1,), (1,)), ((), ())),
                                 preferred_element_type=F32)
            gcols = slice(g * SSD_GROUP_W, (g + 1) * SSD_GROUP_W)
            st = state_ref[g]
            y_off = jnp.dot(c_g, st.astype(BF16), preferred_element_type=F32)
            state_ref[g] = st * dec_row[:, gcols] + jnp.dot(
                b_f.T.astype(BF16), xw[:, gcols], preferred_element_type=F32)
            for pair in range(SSD_GROUP_W // LANES):
                ms = []
                for hh in range(2):
                    h = g * (SSD_HEADS // SSD_GROUPS) + 2 * pair + hh
                    seg = a_cs[:, h:h + 1] - a_cs_t[h:h + 1, :]
                    dec = jnp.exp(jnp.where(causal, seg, -jnp.inf))
                    ms.append((cb * dec * dt_t[h:h + 1, :]).astype(BF16))
                lhs = jnp.concatenate(ms, axis=1)
                xp = xc[:, g * SSD_GROUP_W + pair * LANES:g * SSD_GROUP_W + (pair + 1) * LANES]
                rhs = jnp.concatenate([jnp.where(low_half, xp, 0.0),
                                       jnp.where(low_half, 0.0, xp)], axis=0).astype(BF16)
                y_diag = jnp.dot(lhs, rhs, preferred_element_type=F32)
                lo = pair * LANES
                y_parts.append(y_diag + y_off[:, lo:lo + LANES]
                               * e_exp[:, g * SSD_GROUP_W + lo:g * SSD_GROUP_W + lo + LANES])
        y = jnp.concatenate(y_parts, axis=1) + xc * dsk_ref[...]
        z = z_ref[rows, :].astype(F32)
        y = y * (z * _sigmoid(z))
        outs = []
        for g in range(SSD_GROUPS):
            yg = y[:, g * SSD_GROUP_W:(g + 1) * SSD_GROUP_W]
            outs.append(yg * lax.rsqrt(jnp.mean(yg * yg, axis=-1, keepdims=True) + RMS_EPS))
        y_ref[rows, :] = (jnp.concatenate(outs, axis=1) * ng_ref[...]).astype(BF16)
        return carry

    lax.fori_loop(0, SSD_TILE // q, chunk, 0, unroll=SSD_UNROLL)


def _ssd(xbc, z, dt_raw, bsz, seqlen, ssd_conv_w, ssd_conv_b, dt_bias, a_log, d_skip, ssd_norm_g):
    nt = seqlen // SSD_TILE
    pad_h = lambda v: jnp.pad(v, (0, LANES - SSD_HEADS)).reshape(1, LANES)
    expand = (jnp.arange(LANES)[:, None] == (jnp.arange(SSD_INNER) // SSD_HEAD_DIM)[None, :]).astype(BF16)
    const = lambda shape: pl.BlockSpec(shape, lambda b, j: (0, 0))
    return pl.pallas_call(
        _ssd_body,
        grid=(bsz, nt),
        in_specs=[
            pl.BlockSpec((SSD_TILE, SSD_CONV_CH), lambda b, j: (b * nt + j, 0)),
            pl.BlockSpec((SSD_TILE, SSD_INNER), lambda b, j: (b * nt + j, 0)),
            pl.BlockSpec((SSD_TILE, LANES), lambda b, j: (b * nt + j, 0)),
            const((SSD_CONV_K, SSD_CONV_CH)),
            const((1, SSD_CONV_CH)),
            const((1, LANES)), const((1, LANES)),
            const((1, SSD_INNER)), const((1, SSD_INNER)),
            const((LANES, SSD_INNER)),
        ],
        out_specs=pl.BlockSpec((SSD_TILE, SSD_INNER), lambda b, j: (b * nt + j, 0)),
        out_shape=jax.ShapeDtypeStruct((bsz * seqlen, SSD_INNER), BF16),
        scratch_shapes=[
            pltpu.VMEM((SSD_HALO + SSD_TILE, SSD_CONV_CH), F32),
            pltpu.VMEM((SSD_TILE, SSD_CONV_CH), F32),
            pltpu.VMEM((SSD_TILE, LANES), F32),
            pltpu.VMEM((SSD_GROUPS, SSD_STATE, SSD_GROUP_W), F32),
        ],
        compiler_params=_cparams(("arbitrary", "arbitrary")),
        name="ssd",
    )(xbc, z, dt_raw, ssd_conv_w, ssd_conv_b.reshape(1, SSD_CONV_CH), pad_h(dt_bias),
      pad_h(a_log), jnp.repeat(d_skip, SSD_HEAD_DIM).reshape(1, SSD_INNER),
      ssd_norm_g.reshape(1, SSD_INNER), expand)


def _outproj_body(x_ref, u_ref, y_ref, wu_ref, wy_ref, o_ref):
    o_ref[...] = (x_ref[...]
                  + jnp.dot(u_ref[...], wu_ref[...], preferred_element_type=F32)
                  + jnp.dot(y_ref[...], wy_ref[...], preferred_element_type=F32))


def _out_proj(x2d, u, y, w_out):
    t, d = x2d.shape
    w = w_out.astype(BF16)
    tile = pl.BlockSpec((TOK_TILE, d), lambda i: (i, 0))
    w_spec = pl.BlockSpec((d, d), lambda i: (0, 0))
    return pl.pallas_call(
        _outproj_body,
        grid=(t // TOK_TILE,),
        in_specs=[tile, tile, tile, w_spec, w_spec],
        out_specs=tile,
        out_shape=jax.ShapeDtypeStruct((t, d), F32),
        compiler_params=_cparams(("arbitrary",)),
        name="out_proj",
    )(x2d, u, y, w[:CONV_CH], w[CONV_CH:])


def _xattn_body(x_ref, k_ref, v_ref, gx_ref, wq_ref, wo_ref, gm_ref, wr_hi_ref, wr_lo_ref, br_ref,
                x2_ref, h2_ref, route_ref, stats_ref):
    tiles = [_xattn_tile(x_ref[s * TOK_TILE:(s + 1) * TOK_TILE, :], k_ref, v_ref, gx_ref, wq_ref, wo_ref,
                         gm_ref, wr_hi_ref, wr_lo_ref, br_ref) for s in range(XA_TILES)]
    results = {}
    while len(results) < XA_TILES:
        for s, tile in enumerate(tiles):
            if s not in results:
                try:
                    next(tile)
                except StopIteration as done:
                    results[s] = done.value
    for s in range(XA_TILES):
        rows = slice(s * TOK_TILE, (s + 1) * TOK_TILE)
        x2_ref[rows, :], h2_ref[rows, :], route_ref[rows, :], stats_ref[s] = results[s]


def _xattn_tile(x, k_ref, v_ref, gx_ref, wq_ref, wo_ref, gm_ref, wr_hi_ref, wr_lo_ref, br_ref):
    h = _rms(x, gx_ref[...]).astype(BF16)
    q = (jnp.dot(h, wq_ref[...], preferred_element_type=F32) * (XA_HEAD_DIM ** -0.5)).astype(BF16)
    yield
    heads = []
    for i in range(XA_HEADS):
        cols = slice(i * XA_HEAD_DIM, (i + 1) * XA_HEAD_DIM)
        s = jnp.dot(q[:, cols], k_ref[0, cols, :], preferred_element_type=F32)
        p = jnp.exp(s - jnp.max(s, axis=-1, keepdims=True))
        p = p / jnp.sum(p, axis=-1, keepdims=True)
        heads.append(jnp.dot(p.astype(BF16), v_ref[0, :, cols], preferred_element_type=F32))
        yield
    o = jnp.concatenate(heads, axis=1).astype(BF16)
    x2 = x + jnp.dot(o, wo_ref[...], preferred_element_type=F32)
    yield

    h2 = _rms(x2, gm_ref[...])
    h_hi = h2.astype(BF16)
    h_lo = (h2 - h_hi.astype(F32)).astype(BF16)
    logits = (jnp.dot(h_hi, wr_hi_ref[...], preferred_element_type=F32)
              + jnp.dot(h_lo, wr_hi_ref[...], preferred_element_type=F32)
              + jnp.dot(h_hi, wr_lo_ref[...], preferred_element_type=F32)) + br_ref[...]
    yield
    lane = lax.broadcasted_iota(jnp.int32, logits.shape, 1)
    neg = -jnp.inf

    def first_argmax(v):
        m = jnp.max(v, axis=-1, keepdims=True)
        return m, jnp.min(jnp.where(v == m, lane, ROUTE_LANES), axis=-1, keepdims=True)

    gl = jnp.where(lane < N_GROUPS, logits, neg)
    g_max, g_sel = first_argmax(gl)
    p_top = 1.0 / jnp.sum(jnp.exp(gl - g_max), axis=-1, keepdims=True)
    e_lo = N_GROUPS + EXPERTS_PER_GROUP * g_sel
    el = jnp.where((lane >= e_lo) & (lane < e_lo + EXPERTS_PER_GROUP), logits, neg)
    m1, i1 = first_argmax(el)
    m2, i2 = first_argmax(jnp.where(lane == i1, neg, el))
    r = jnp.exp(m2 - m1)
    w1 = p_top / (1.0 + r)
    w2 = w1 * r
    yield
    e1 = i1 - N_GROUPS
    e2 = i2 - N_GROUPS
    oh1 = (lane == e1).astype(BF16)
    oh2 = (lane == e2).astype(BF16)
    n_t = logits.shape[0]
    before = (lax.broadcasted_iota(jnp.int32, (n_t, n_t), 0)
              > lax.broadcasted_iota(jnp.int32, (n_t, n_t), 1)).astype(BF16)
    c1 = jnp.dot(before, oh1, preferred_element_type=F32)
    c2 = jnp.dot(before, oh2, preferred_element_type=F32)
    tot1 = jnp.sum(oh1.astype(F32), axis=0, keepdims=True)
    cnt = tot1 + jnp.sum(oh2.astype(F32), axis=0, keepdims=True)
    cnt = jnp.floor((cnt + (SUBLANES - 1)) * (1.0 / SUBLANES)) * SUBLANES
    lanes_before = (lax.broadcasted_iota(jnp.int32, (ROUTE_LANES, ROUTE_LANES), 0)
                    < lax.broadcasted_iota(jnp.int32, (ROUTE_LANES, ROUTE_LANES), 1)).astype(F32)
    loff = jnp.dot(jnp.broadcast_to(cnt, (SUBLANES, ROUTE_LANES)), lanes_before,
                   preferred_element_type=F32, precision=lax.Precision.HIGHEST)[0:1, :]
    lp1 = jnp.sum(oh1.astype(F32) * (loff + c1), axis=-1, keepdims=True)
    lp2 = jnp.sum(oh2.astype(F32) * (loff + tot1 + c2), axis=-1, keepdims=True)
    route = jnp.where(lane == 0, e1.astype(F32),
            jnp.where(lane == 1, e2.astype(F32),
            jnp.where(lane == 2, w1,
            jnp.where(lane == 3, w2,
            jnp.where(lane == 4, lp1, jnp.where(lane == 5, lp2, 0.0))))))
    row = lax.broadcasted_iota(jnp.int32, (SUBLANES, ROUTE_LANES), 0)
    return x2, h_hi, route, jnp.where(row == 0, cnt, jnp.where(row == 1, loff, 0.0))


def _xattn_route(x1, k, v, bsz, seqlen, g_xattn, w_q, w_o, g_moe, w_rg, b_rg, w_re, b_re):
    t, d = x1.shape
    w_r = jnp.pad(jnp.concatenate([w_rg, w_re], axis=1), ((0, 0), (0, ROUTE_LANES - N_GROUPS - N_EXPERTS)))
    b_r = jnp.pad(jnp.concatenate([b_rg, b_re]), (0, ROUTE_LANES - N_GROUPS - N_EXPERTS)).reshape(1, ROUTE_LANES)
    wr_hi = w_r.astype(BF16)
    wr_lo = (w_r - wr_hi.astype(F32)).astype(BF16)
    step_rows = XA_TILES * TOK_TILE
    nt = seqlen // step_rows
    tile = pl.BlockSpec((step_rows, d), lambda b, j: (b * nt + j, 0))
    kv_spec = pl.BlockSpec((1, MEM_LEN, d), lambda b, j: (b, 0, 0))
    const = lambda shape: pl.BlockSpec(shape, lambda b, j: (0, 0))
    return pl.pallas_call(
        _xattn_body,
        grid=(bsz, nt),
        in_specs=[tile, pl.BlockSpec((1, d, MEM_LEN), lambda b, j: (b, 0, 0)), kv_spec, const((1, d)),
                  const((d, d)), const((d, d)), const((1, d)),
                  const((d, ROUTE_LANES)), const((d, ROUTE_LANES)), const((1, ROUTE_LANES))],
        out_specs=[tile, tile, pl.BlockSpec((step_rows, ROUTE_LANES), lambda b, j: (b * nt + j, 0)),
                   pl.BlockSpec((XA_TILES, SUBLANES, ROUTE_LANES), lambda b, j: (b * nt + j, 0, 0))],
        out_shape=[jax.ShapeDtypeStruct((t, d), F32), jax.ShapeDtypeStruct((t, d), BF16),
                   jax.ShapeDtypeStruct((t, ROUTE_LANES), F32),
                   jax.ShapeDtypeStruct((t // TOK_TILE, SUBLANES, ROUTE_LANES), F32)],
        compiler_params=_cparams(("arbitrary", "arbitrary")),
        name="xattn_route",
    )(x1, jnp.swapaxes(k, 1, 2), v, g_xattn.reshape(1, d), w_q.astype(BF16), w_o.astype(BF16), g_moe.reshape(1, d),
      wr_hi, wr_lo, b_r)


def _rows_copy(src_ref, src0, dst_ref, dst0, n, sem):
    rows = lambda r0: pl.ds(pl.multiple_of(r0, SUBLANES), pl.multiple_of(n, SUBLANES))
    return pltpu.make_async_copy(src_ref.at[rows(src0), :], dst_ref.at[rows(dst0), :], sem)


def _pack_pairs(v):
    half = v.shape[1] // 2
    lo = lax.bitcast_convert_type(v[:, :half], jnp.uint32)
    hi = lax.bitcast_convert_type(v[:, half:], jnp.uint32)
    return (lo >> 16) | (hi & jnp.uint32(0xFFFF0000))


def _unpack_pairs(w):
    lo = lax.bitcast_convert_type(w << 16, F32)
    hi = lax.bitcast_convert_type(w & jnp.uint32(0xFFFF0000), F32)
    return jnp.concatenate([lo, hi], axis=1).astype(BF16)


def _dispatch_body(cnt_ref, loff_ref, gb_ref, tot_ref, fstart_ref, fcnt_ref, nu_ref, h_ref, route_ref,
                   xs_ref, buf_ref, zero_ref, sem, zsem):
    i = pl.program_id(0)
    nt = pl.num_programs(0)
    slot = i % 2
    buf = buf_ref.at[slot]

    def drain(tile, s):
        _rows_copy(buf_ref.at[s], 0, xs_ref, 0, tot_ref[tile], sem.at[s]).wait()

    @pl.when(i >= 2)
    def _():
        drain(i - 2, slot)

    rt = route_ref[...].T
    lp1 = rt[4:5, :].astype(jnp.int32)
    lp2 = rt[5:6, :].astype(jnp.int32)
    r_i = lax.broadcasted_iota(jnp.int32, (SORT_ROWS, TOK_TILE), 0)
    perm = jnp.where((r_i == lp1) | (r_i == lp2), 1.0, 0.0).astype(BF16)
    buf[...] = _pack_pairs(jnp.dot(perm, h_ref[...], preferred_element_type=F32))

    def per_expert(e, c):
        j = i * N_EXPERTS + e

        @pl.when(cnt_ref[j] > 0)
        def _():
            _rows_copy(buf, loff_ref[j], xs_ref, gb_ref[j], cnt_ref[j], sem.at[slot]).start()
        return c

    lax.fori_loop(0, N_EXPERTS, per_expert, 0)

    @pl.when(i == 0)
    def _():
        zero_ref[...] = jnp.zeros(zero_ref.shape, zero_ref.dtype)

        def fill(start):
            def body(e, c):
                @pl.when(fcnt_ref[e] > 0)
                def _():
                    copy = _rows_copy(zero_ref, 0, xs_ref, fstart_ref[e], fcnt_ref[e], zsem)
                    copy.start() if start else copy.wait()
                return c
            lax.fori_loop(0, N_EXPERTS, body, 0)

            def tail(blk, c):
                copy = _rows_copy(zero_ref, 0, xs_ref, blk * MOE_BLOCK, MOE_BLOCK, zsem)
                copy.start() if start else copy.wait()
                return c
            lax.fori_loop(nu_ref[0], xs_ref.shape[0] // MOE_BLOCK, tail, 0)

        fill(True)
        fill(False)

    @pl.when(i == nt - 1)
    def _():
        drain(i, slot)

        @pl.when(i >= 1)
        def _():
            drain(i - 1, 1 - slot)


def _dispatch(h2, route, tables, n_used, n_slots):
    t, d = h2.shape
    nt = t // TOK_TILE
    return pl.pallas_call(
        _dispatch_body,
        grid_spec=pltpu.PrefetchScalarGridSpec(
            num_scalar_prefetch=7,
            grid=(nt,),
            in_specs=[
                pl.BlockSpec((TOK_TILE, d), lambda i, *_: (i, 0)),
                pl.BlockSpec((TOK_TILE, ROUTE_LANES), lambda i, *_: (i, 0)),
            ],
            out_specs=pl.BlockSpec(memory_space=pl.ANY),
            scratch_shapes=[pltpu.VMEM((2, SORT_ROWS, d // 2), jnp.uint32),
                            pltpu.VMEM((MOE_BLOCK, d // 2), jnp.uint32),
                            pltpu.SemaphoreType.DMA((2,)), pltpu.SemaphoreType.DMA(())],
        ),
        out_shape=jax.ShapeDtypeStruct((n_slots, d // 2), jnp.uint32),
        compiler_params=_cparams(("arbitrary",)),
        name="dispatch",
    )(tables["cnt"], tables["loff"], tables["gb"], tables["tot"], tables["fill_start"],
      tables["fill_cnt"], n_used, h2, route)


def _experts_body(be_ref, nu_ref, x_ref, wg_ref, wu_ref, wd_ref, y_ref, wg_bf, wu_bf, wd_bf):
    b = pl.program_id(0)
    used = b < nu_ref[0]

    @pl.when(used & ((b == 0) | (be_ref[b] != be_ref[jnp.maximum(b - 1, 0)])))
    def _():
        wg_bf[...] = wg_ref[0].astype(BF16)
        wu_bf[...] = wu_ref[0].astype(BF16)
        wd_bf[...] = wd_ref[0].astype(BF16)

    @pl.when(used)
    def _():
        x = _unpack_pairs(x_ref[...])
        g = jnp.dot(x, wg_bf[...], preferred_element_type=F32)
        u = jnp.dot(x, wu_bf[...], preferred_element_type=F32)
        a = (g * _sigmoid(g) * u).astype(BF16)
        y = jnp.dot(a, wd_bf[...], preferred_element_type=F32)
        y_ref[...] = _pack_pairs(y.astype(BF16).astype(F32))

    @pl.when(jnp.logical_not(used))
    def _():
        y_ref[...] = jnp.zeros(y_ref.shape, y_ref.dtype)


def _experts(xs, block_e, n_used, w_gate, w_up, w_down):
    n_slots, dp = xs.shape
    d = w_gate.shape[1]
    n_blocks = n_slots // MOE_BLOCK
    last = lambda b, nu: jnp.maximum(jnp.minimum(b, nu[0] - 1), 0)
    tile = pl.BlockSpec((MOE_BLOCK, dp), lambda b, be, nu: (last(b, nu), 0))
    return pl.pallas_call(
        _experts_body,
        grid_spec=pltpu.PrefetchScalarGridSpec(
            num_scalar_prefetch=2,
            grid=(n_blocks,),
            in_specs=[
                tile,
                pl.BlockSpec((1, d, D_EXPERT), lambda b, be, nu: (be[last(b, nu)], 0, 0)),
                pl.BlockSpec((1, d, D_EXPERT), lambda b, be, nu: (be[last(b, nu)], 0, 0)),
                pl.BlockSpec((1, D_EXPERT, d), lambda b, be, nu: (be[last(b, nu)], 0, 0)),
            ],
            out_specs=pl.BlockSpec((MOE_BLOCK, dp), lambda b, be, nu: (b, 0)),
            scratch_shapes=[pltpu.VMEM((d, D_EXPERT), BF16), pltpu.VMEM((d, D_EXPERT), BF16),
                            pltpu.VMEM((D_EXPERT, d), BF16)],
        ),
        out_shape=jax.ShapeDtypeStruct((n_slots, dp), jnp.uint32),
        compiler_params=_cparams(("arbitrary",)),
        name="experts",
    )(block_e, n_used, xs, w_gate, w_up, w_down)


def _combine_body(cnt_ref, loff_ref, gb_ref, tot_ref, x_ref, route_ref, g_ref, ys_ref, o_ref, ybuf_ref, sem):
    i = pl.program_id(0)
    nt = pl.num_programs(0)
    slot = i % 2

    def gather(tile, s):
        def per_expert(e, c):
            j = tile * N_EXPERTS + e

            @pl.when(cnt_ref[j] > 0)
            def _():
                _rows_copy(ys_ref, gb_ref[j], ybuf_ref.at[s], loff_ref[j], cnt_ref[j], sem.at[s]).start()
            return c

        lax.fori_loop(0, N_EXPERTS, per_expert, 0)

    @pl.when(i == 0)
    def _():
        ybuf_ref[...] = jnp.zeros(ybuf_ref.shape, ybuf_ref.dtype)
        gather(0, 0)

    @pl.when(i + 1 < nt)
    def _():
        gather(i + 1, 1 - slot)

    route = route_ref[...]
    c_i = lax.broadcasted_iota(jnp.int32, (TOK_TILE, SORT_ROWS), 1)
    lp1 = route[:, 4:5].astype(jnp.int32)
    lp2 = route[:, 5:6].astype(jnp.int32)
    pw = jnp.where(c_i == lp1, route[:, 2:3], jnp.where(c_i == lp2, route[:, 3:4], 0.0)).astype(BF16)
    tot = tot_ref[i]
    _rows_copy(ys_ref, 0, ybuf_ref.at[slot], 0, tot, sem.at[slot]).wait()
    r_i = lax.broadcasted_iota(jnp.int32, (SORT_ROWS, 1), 0)
    y = _unpack_pairs(jnp.where(r_i < tot, ybuf_ref[slot], jnp.uint32(0)))
    moe = jnp.dot(pw, y, preferred_element_type=F32)
    o_ref[...] = _rms(x_ref[...] + moe, g_ref[...])


def _combine(x2, route, tables, ys, g_final):
    t, d = x2.shape
    nt = t // TOK_TILE
    tile = pl.BlockSpec((TOK_TILE, d), lambda i, *_: (i, 0))
    return pl.pallas_call(
        _combine_body,
        grid_spec=pltpu.PrefetchScalarGridSpec(
            num_scalar_prefetch=4,
            grid=(nt,),
            in_specs=[
                tile,
                pl.BlockSpec((TOK_TILE, ROUTE_LANES), lambda i, *_: (i, 0)),
                pl.BlockSpec((1, d), lambda i, *_: (0, 0)),
                pl.BlockSpec(memory_space=pl.ANY),
            ],
            out_specs=tile,
            scratch_shapes=[pltpu.VMEM((2, SORT_ROWS, d // 2), jnp.uint32),
                            pltpu.SemaphoreType.DMA((2,))],
        ),
        out_shape=jax.ShapeDtypeStruct((t, d), F32),
        compiler_params=_cparams(("arbitrary",)),
        name="combine",
    )(tables["cnt"], tables["loff"], tables["gb"], tables["tot"], x2, route, g_final.reshape(1, d), ys)


def _routing_tables(stats, n_tok):
    cnt = stats[:, 0, :N_EXPERTS].astype(jnp.int32)
    loff = stats[:, 1, :N_EXPERTS].astype(jnp.int32)
    n_tiles = cnt.shape[0]
    counts = jnp.sum(cnt, axis=0)
    padded = ((counts + MOE_BLOCK - 1) // MOE_BLOCK) * MOE_BLOCK
    pad_end = jnp.cumsum(padded)
    pad_start = pad_end - padded
    gb = pad_start[None, :] + jnp.cumsum(cnt, axis=0) - cnt
    max_rows = (n_tok * TOP_K + n_tiles * N_EXPERTS * (SUBLANES - 1)
                + N_EXPERTS * (MOE_BLOCK - SUBLANES))
    n_blocks = -(-max_rows // MOE_BLOCK)
    n_slots = n_blocks * MOE_BLOCK
    block_start = jnp.arange(n_blocks, dtype=jnp.int32) * MOE_BLOCK
    block_e = jnp.minimum(jnp.sum((pad_end[None, :] <= block_start[:, None]).astype(jnp.int32), axis=1),
                          N_EXPERTS - 1)
    tables = dict(cnt=cnt.reshape(-1), loff=loff.reshape(-1), gb=gb.reshape(-1).astype(jnp.int32),
                  tot=jnp.sum(cnt, axis=1).astype(jnp.int32),
                  fill_start=(pad_start + counts).astype(jnp.int32),
                  fill_cnt=(padded - counts).astype(jnp.int32))
    n_used = (pad_end[-1:] // MOE_BLOCK).astype(jnp.int32)
    return tables, block_e, n_used, n_slots


def kernel(x, mem, g_mix, w_in, conv_w, conv_b, ln_g, ln_b, ssd_conv_w, ssd_conv_b, dt_bias, a_log, d_skip, ssd_norm_g, w_out, g_xattn, g_mem, w_q, w_k, w_v, w_o, g_moe, w_router_group, b_router_group, w_router_expert, b_router_expert, w_gate, w_up, w_down, g_final):
    bsz, seqlen, d = x.shape
    n_tok = bsz * seqlen
    xt = x.reshape(n_tok, d)
    assert g_mix.shape[0] == 1, "the combine kernel applies the final norm: single layer only"
    for l in range(1):
        u, xbc, z, dt_raw = _mix_in(xt, bsz, seqlen, g_mix[l], w_in[l], conv_w[l], conv_b[l],
                                    ln_g[l], ln_b[l])
        y = _ssd(xbc, z, dt_raw, bsz, seqlen, ssd_conv_w[l], ssd_conv_b[l], dt_bias[l], a_log[l],
                 d_skip[l], ssd_norm_g[l])
        x1 = _out_proj(xt, u, y, w_out[l])
        k, v = _kv_proj(mem, g_mem[l], w_k[l], w_v[l])
        x2, h2, route, stats = _xattn_route(x1, k, v, bsz, seqlen, g_xattn[l], w_q[l], w_o[l], g_moe[l],
                                            w_router_group[l], b_router_group[l], w_router_expert[l],
                                            b_router_expert[l])
        tables, block_e, n_used, n_slots = _routing_tables(stats, n_tok)
        xs = _dispatch(h2, route, tables, n_used, n_slots)
        ys = _experts(xs, block_e, n_used, w_gate[l], w_up[l], w_down[l])
        xt = _combine(x2, route, tables, ys, g_final)
    return xt.reshape(bsz, seqlen, d)
```

```python
import functools

import jax
import jax.numpy as jnp
from jax import lax
from jax.experimental import pallas as pl
from jax.experimental.pallas import tpu as pltpu

F32 = jnp.float32
BF16 = jnp.bfloat16

D_MODEL = 1024
CONV_CH = 1024
CONV_K = 31
SSD_INNER = 1024
SSD_HEAD_DIM = 64
SSD_HEADS = 16
SSD_STATE = 128
SSD_GROUPS = 2
SSD_GROUP_W = SSD_INNER // SSD_GROUPS
SSD_CONV_K = 4
SSD_CHUNK = 128
SSD_BC = SSD_GROUPS * SSD_STATE
SSD_CONV_CH = SSD_INNER + 2 * SSD_BC
N_MAIN = 2 * CONV_CH + 2 * SSD_INNER + 2 * SSD_BC
XA_HEADS = 4
XA_HEAD_DIM = 256
MEM_LEN = 256
N_GROUPS = 4
EXPERTS_PER_GROUP = 8
N_EXPERTS = 32
TOP_K = 2
D_EXPERT = 512
MOE_BLOCK = 512
RMS_EPS = 1e-6
LN_EPS = 1e-5

LANES = 128
SUBLANES = 8
VMEM_LIMIT = 56 * 1024 * 1024

TOK_TILE = 512
CONV_TILE = 512
MM_CHUNK = 512
CONV_HALO = 32
CONV_ROWS = 128
SSD_TILE = 512
XA_TILES = 2
SSD_UNROLL = 4
SSD_HALO = 8
ROUTE_LANES = 128
SORT_ROWS = TOP_K * TOK_TILE + N_EXPERTS * SUBLANES


def _cparams(sem):
    return pltpu.CompilerParams(dimension_semantics=sem, vmem_limit_bytes=VMEM_LIMIT)


def _rms(x, g):
    return x * lax.rsqrt(jnp.mean(x * x, axis=-1, keepdims=True) + RMS_EPS) * g


def _sigmoid(x):
    return 1.0 / (1.0 + jnp.exp(-x))


def _kv_body(m_ref, g_ref, wk_ref, wv_ref, k_ref, v_ref):
    m = _rms(m_ref[0], g_ref[...]).astype(BF16)
    k_ref[0] = jnp.dot(m, wk_ref[...], preferred_element_type=F32).astype(BF16)
    v_ref[0] = jnp.dot(m, wv_ref[...], preferred_element_type=F32).astype(BF16)


def _kv_proj(mem, g_mem, w_k, w_v):
    b, s, d = mem.shape
    w_spec = pl.BlockSpec((d, d), lambda i: (0, 0))
    kv_spec = pl.BlockSpec((1, s, d), lambda i: (i, 0, 0))
    return pl.pallas_call(
        _kv_body,
        grid=(b,),
        in_specs=[kv_spec, pl.BlockSpec((1, d), lambda i: (0, 0)), w_spec, w_spec],
        out_specs=[kv_spec, kv_spec],
        out_shape=[jax.ShapeDtypeStruct((b, s, d), BF16)] * 2,
        compiler_params=_cparams(("arbitrary",)),
        name="kv_proj",
    )(mem, g_mem.reshape(1, d), w_k.astype(BF16), w_v.astype(BF16))


def _mix_in_body(x_ref, g_ref, w_ref, wdt_ref, cw_ref, cb_ref, lg_ref, lb_ref,
                 u_ref, xbc_ref, z_ref, dt_ref, ubuf_ref, acc_ref):
    @pl.when(pl.program_id(1) == 0)
    def _():
        ubuf_ref[0:CONV_HALO, :] = jnp.zeros((CONV_HALO, CONV_CH), F32)

    h = _rms(x_ref[...], g_ref[...]).astype(BF16)
    proj = lambda lo: jnp.dot(h, w_ref[:, lo:lo + MM_CHUNK], preferred_element_type=F32)
    for c in range(CONV_CH // MM_CHUNK):
        cols = slice(c * MM_CHUNK, (c + 1) * MM_CHUNK)
        ubuf_ref[CONV_HALO:CONV_HALO + CONV_TILE, cols] = (
            proj(c * MM_CHUNK) * _sigmoid(proj(CONV_CH + c * MM_CHUNK)))
    z0 = 2 * CONV_CH
    for c in range(SSD_INNER // MM_CHUNK):
        z_ref[:, c * MM_CHUNK:(c + 1) * MM_CHUNK] = proj(z0 + c * MM_CHUNK).astype(BF16)
    x0 = z0 + SSD_INNER
    for c in range(SSD_CONV_CH // MM_CHUNK):
        xbc_ref[:, c * MM_CHUNK:(c + 1) * MM_CHUNK] = proj(x0 + c * MM_CHUNK).astype(BF16)
    dt_ref[...] = jnp.dot(h, wdt_ref[...], preferred_element_type=F32)

    first = CONV_HALO - (CONV_K - 1)
    for rc in range(CONV_TILE // CONV_ROWS):
        r0 = rc * CONV_ROWS
        rows = slice(r0, r0 + CONV_ROWS)
        for cb in range(CONV_CH // LANES):
            cols = slice(cb * LANES, (cb + 1) * LANES)
            acc = None
            for res in range(SUBLANES):
                part = None
                for k in range(CONV_K):
                    if (first + k) % SUBLANES == res:
                        term = cw_ref[k:k + 1, cols] * ubuf_ref[r0 + first + k:r0 + first + k + CONV_ROWS, cols]
                        part = term if part is None else part + term
                acc = part if acc is None else acc + part
            acc_ref[rows, cols] = acc
        u = acc_ref[rows, :] + cb_ref[...]
        mu = jnp.mean(u, axis=-1, keepdims=True)
        uc = u - mu
        var = jnp.mean(uc * uc, axis=-1, keepdims=True)
        y = uc * lax.rsqrt(var + LN_EPS) * lg_ref[...] + lb_ref[...]
        u_ref[rows, :] = (y * _sigmoid(y)).astype(BF16)

    ubuf_ref[0:CONV_HALO, :] = ubuf_ref[CONV_TILE:CONV_TILE + CONV_HALO, :]


def _mix_in(x2d, bsz, seqlen, g_mix, w_in, conv_w, conv_b, ln_g, ln_b):
    t, d = x2d.shape
    nt = seqlen // CONV_TILE
    w_main = w_in[:, :N_MAIN].astype(BF16)
    w_dt = jnp.pad(w_in[:, N_MAIN:], ((0, 0), (0, LANES - SSD_HEADS))).astype(BF16)
    row = lambda v: v.reshape(1, -1)
    const = lambda shape: pl.BlockSpec(shape, lambda b, j: (0, 0))
    tile = lambda width: pl.BlockSpec((CONV_TILE, width), lambda b, j: (b * nt + j, 0))
    return pl.pallas_call(
        _mix_in_body,
        grid=(bsz, nt),
        in_specs=[
            tile(d), const((1, d)), const((d, N_MAIN)), const((d, LANES)),
            const((CONV_K + 1, CONV_CH)), const((1, CONV_CH)), const((1, CONV_CH)), const((1, CONV_CH)),
        ],
        out_specs=[tile(CONV_CH), tile(SSD_CONV_CH), tile(SSD_INNER), tile(LANES)],
        out_shape=[jax.ShapeDtypeStruct((t, CONV_CH), BF16), jax.ShapeDtypeStruct((t, SSD_CONV_CH), BF16),
                   jax.ShapeDtypeStruct((t, SSD_INNER), BF16), jax.ShapeDtypeStruct((t, LANES), F32)],
        scratch_shapes=[
            pltpu.VMEM((CONV_HALO + CONV_TILE, CONV_CH), F32),
            pltpu.VMEM((CONV_TILE, CONV_CH), F32),
        ],
        compiler_params=_cparams(("arbitrary", "arbitrary")),
        name="mix_in",
    )(x2d, row(g_mix), w_main, w_dt, jnp.pad(conv_w, ((0, 1), (0, 0))), row(conv_b), row(ln_g), row(ln_b))


def _ssd_body(xbc_ref, z_ref, dt_ref, cw_ref, cb_ref, dtb_ref, alog_ref, dsk_ref, ng_ref,
              expand_ref, y_ref, xbuf_ref, act_ref, dts_ref, state_ref):
    @pl.when(pl.program_id(1) == 0)
    def _():
        xbuf_ref[0:SSD_HALO, :] = jnp.zeros((SSD_HALO, SSD_CONV_CH), F32)
        state_ref[...] = jnp.zeros(state_ref.shape, F32)

    xbuf_ref[SSD_HALO:SSD_HALO + SSD_TILE, :] = xbc_ref[...].astype(F32)
    full = xbuf_ref[...]
    conv = cb_ref[...] + cw_ref[SSD_CONV_K - 1:SSD_CONV_K, :] * full[SSD_HALO:, :]
    for back in range(1, SSD_CONV_K):
        past = pltpu.roll(full, back, 0)[SSD_HALO:, :]
        conv = conv + cw_ref[SSD_CONV_K - 1 - back:SSD_CONV_K - back, :] * past
    act_ref[...] = conv * _sigmoid(conv)
    xbuf_ref[0:SSD_HALO, :] = xbuf_ref[SSD_TILE:SSD_TILE + SSD_HALO, :]

    dt_in = dt_ref[...] + dtb_ref[...]
    dts_ref[...] = jnp.maximum(dt_in, 0.0) + jnp.log1p(jnp.exp(-jnp.abs(dt_in)))

    a_neg = -jnp.exp(alog_ref[...])
    q = SSD_CHUNK
    row_i = lax.broadcasted_iota(jnp.int32, (q, q), 0)
    col_i = lax.broadcasted_iota(jnp.int32, (q, q), 1)
    causal = row_i >= col_i
    tril = causal.astype(F32)
    lane_i = lax.broadcasted_iota(jnp.int32, (q, LANES), 1)
    low_half = lane_i < SSD_HEAD_DIM
    expand = expand_ref[...]

    def chunk(c, carry):
        r0 = pl.multiple_of(c * q, q)
        rows = pl.ds(r0, q)
        dtc = dts_ref[rows, :]
        a_cs = jnp.dot(tril, dtc * a_neg, preferred_element_type=F32,
                       precision=lax.Precision.HIGHEST)
        a_cs_t = a_cs.T
        dt_t = dtc.T
        a_end = a_cs[q - 1:q, :]
        e_exp = jnp.dot(jnp.exp(a_cs).astype(BF16), expand, preferred_element_type=F32)
        w_exp = jnp.dot((jnp.exp(a_end - a_cs) * dtc).astype(BF16), expand,
                        preferred_element_type=F32)
        dec_row = e_exp[q - 1:q, :]
        xc = act_ref[rows, 0:SSD_INNER]
        xw = (xc * w_exp).astype(BF16)
        y_parts = []
        for g in range(SSD_GROUPS):
            b_f = act_ref[rows, SSD_INNER + g * SSD_STATE:SSD_INNER + (g + 1) * SSD_STATE]
            c_f = act_ref[rows, SSD_INNER + SSD_BC + g * SSD_STATE:
                          SSD_INNER + SSD_BC + (g + 1) * SSD_STATE]
            b_g = b_f.astype(BF16)
            c_g = c_f.astype(BF16)
            cb = lax.dot_general(c_g, b_g, (((1,), (1,)), ((), ())),
                                 preferred_element_type=F32)
            gcols = slice(g * SSD_GROUP_W, (g + 1) * SSD_GROUP_W)
            st = state_ref[g]
            y_off = jnp.dot(c_g, st.astype(BF16), preferred_element_type=F32)
            state_ref[g] = st * dec_row[:, gcols] + jnp.dot(
                b_f.T.astype(BF16), xw[:, gcols], preferred_element_type=F32)
            for pair in range(SSD_GROUP_W // LANES):
                ms = []
                for hh in range(2):
                    h = g * (SSD_HEADS // SSD_GROUPS) + 2 * pair + hh
                    seg = a_cs[:, h:h + 1] - a_cs_t[h:h + 1, :]
                    dec = jnp.exp(jnp.where(causal, seg, -jnp.inf))
                    ms.append((cb * dec * dt_t[h:h + 1, :]).astype(BF16))
                lhs = jnp.concatenate(ms, axis=1)
                xp = xc[:, g * SSD_GROUP_W + pair * LANES:g * SSD_GROUP_W + (pair + 1) * LANES]
                rhs = jnp.concatenate([jnp.where(low_half, xp, 0.0),
                                       jnp.where(low_half, 0.0, xp)], axis=0).astype(BF16)
                y_diag = jnp.dot(lhs, rhs, preferred_element_type=F32)
                lo = pair * LANES
                y_parts.append(y_diag + y_off[:, lo:lo + LANES]
                               * e_exp[:, g * SSD_GROUP_W + lo:g * SSD_GROUP_W + lo + LANES])
        y = jnp.concatenate(y_parts, axis=1) + xc * dsk_ref[...]
        z = z_ref[rows, :].astype(F32)
        y = y * (z * _sigmoid(z))
        outs = []
        for g in range(SSD_GROUPS):
            yg = y[:, g * SSD_GROUP_W:(g + 1) * SSD_GROUP_W]
            outs.append(yg * lax.rsqrt(jnp.mean(yg * yg, axis=-1, keepdims=True) + RMS_EPS))
        y_ref[rows, :] = (jnp.concatenate(outs, axis=1) * ng_ref[...]).astype(BF16)
        return carry

    lax.fori_loop(0, SSD_TILE // q, chunk, 0, unroll=SSD_UNROLL)


def _ssd(xbc, z, dt_raw, bsz, seqlen, ssd_conv_w, ssd_conv_b, dt_bias, a_log, d_skip, ssd_norm_g):
    nt = seqlen // SSD_TILE
    pad_h = lambda v: jnp.pad(v, (0, LANES - SSD_HEADS)).reshape(1, LANES)
    expand = (jnp.arange(LANES)[:, None] == (jnp.arange(SSD_INNER) // SSD_HEAD_DIM)[None, :]).astype(BF16)
    const = lambda shape: pl.BlockSpec(shape, lambda b, j: (0, 0))
    return pl.pallas_call(
        _ssd_body,
        grid=(bsz, nt),
        in_specs=[
            pl.BlockSpec((SSD_TILE, SSD_CONV_CH), lambda b, j: (b * nt + j, 0)),
            pl.BlockSpec((SSD_TILE, SSD_INNER), lambda b, j: (b * nt + j, 0)),
            pl.BlockSpec((SSD_TILE, LANES), lambda b, j: (b * nt + j, 0)),
            const((SSD_CONV_K, SSD_CONV_CH)),
            const((1, SSD_CONV_CH)),
            const((1, LANES)), const((1, LANES)),
            const((1, SSD_INNER)), const((1, SSD_INNER)),
            const((LANES, SSD_INNER)),
        ],
        out_specs=pl.BlockSpec((SSD_TILE, SSD_INNER), lambda b, j: (b * nt + j, 0)),
        out_shape=jax.ShapeDtypeStruct((bsz * seqlen, SSD_INNER), BF16),
        scratch_shapes=[
            pltpu.VMEM((SSD_HALO + SSD_TILE, SSD_CONV_CH), F32),
            pltpu.VMEM((SSD_TILE, SSD_CONV_CH), F32),
            pltpu.VMEM((SSD_TILE, LANES), F32),
            pltpu.VMEM((SSD_GROUPS, SSD_STATE, SSD_GROUP_W), F32),
        ],
        compiler_params=_cparams(("arbitrary", "arbitrary")),
        name="ssd",
    )(xbc, z, dt_raw, ssd_conv_w, ssd_conv_b.reshape(1, SSD_CONV_CH), pad_h(dt_bias),
      pad_h(a_log), jnp.repeat(d_skip, SSD_HEAD_DIM).reshape(1, SSD_INNER),
      ssd_norm_g.reshape(1, SSD_INNER), expand)


def _outproj_body(x_ref, u_ref, y_ref, wu_ref, wy_ref, o_ref):
    o_ref[...] = (x_ref[...]
                  + jnp.dot(u_ref[...], wu_ref[...], preferred_element_type=F32)
                  + jnp.dot(y_ref[...], wy_ref[...], preferred_element_type=F32))


def _out_proj(x2d, u, y, w_out):
    t, d = x2d.shape
    w = w_out.astype(BF16)
    tile = pl.BlockSpec((TOK_TILE, d), lambda i: (i, 0))
    w_spec = pl.BlockSpec((d, d), lambda i: (0, 0))
    return pl.pallas_call(
        _outproj_body,
        grid=(t // TOK_TILE,),
        in_specs=[tile, tile, tile, w_spec, w_spec],
        out_specs=tile,
        out_shape=jax.ShapeDtypeStruct((t, d), F32),
        compiler_params=_cparams(("arbitrary",)),
        name="out_proj",
    )(x2d, u, y, w[:CONV_CH], w[CONV_CH:])


def _xattn_body(x_ref, k_ref, v_ref, gx_ref, wq_ref, wo_ref, gm_ref, wr_hi_ref, wr_lo_ref, br_ref,
                x2_ref, h2_ref, route_ref, stats_ref):
    tiles = [_xattn_tile(x_ref[s * TOK_TILE:(s + 1) * TOK_TILE, :], k_ref, v_ref, gx_ref, wq_ref, wo_ref,
                         gm_ref, wr_hi_ref, wr_lo_ref, br_ref) for s in range(XA_TILES)]
    results = {}
    while len(results) < XA_TILES:
        for s, tile in enumerate(tiles):
            if s not in results:
                try:
                    next(tile)
                except StopIteration as done:
                    results[s] = done.value
    for s in range(XA_TILES):
        rows = slice(s * TOK_TILE, (s + 1) * TOK_TILE)
        x2_ref[rows, :], h2_ref[rows, :], route_ref[rows, :], stats_ref[s] = results[s]


def _xattn_tile(x, k_ref, v_ref, gx_ref, wq_ref, wo_ref, gm_ref, wr_hi_ref, wr_lo_ref, br_ref):
    h = _rms(x, gx_ref[...]).astype(BF16)
    q = (jnp.dot(h, wq_ref[...], preferred_element_type=F32) * (XA_HEAD_DIM ** -0.5)).astype(BF16)
    yield
    heads = []
    for i in range(XA_HEADS):
        cols = slice(i * XA_HEAD_DIM, (i + 1) * XA_HEAD_DIM)
        s = jnp.dot(q[:, cols], k_ref[0, cols, :], preferred_element_type=F32)
        p = jnp.exp(s - jnp.max(s, axis=-1, keepdims=True))
        p = p / jnp.sum(p, axis=-1, keepdims=True)
        heads.append(jnp.dot(p.astype(BF16), v_ref[0, :, cols], preferred_element_type=F32))
        yield
    o = jnp.concatenate(heads, axis=1).astype(BF16)
    x2 = x + jnp.dot(o, wo_ref[...], preferred_element_type=F32)
    yield

    h2 = _rms(x2, gm_ref[...])
    h_hi = h2.astype(BF16)
    h_lo = (h2 - h_hi.astype(F32)).astype(BF16)
    logits = (jnp.dot(h_hi, wr_hi_ref[...], preferred_element_type=F32)
              + jnp.dot(h_lo, wr_hi_ref[...], preferred_element_type=F32)
              + jnp.dot(h_hi, wr_lo_ref[...], preferred_element_type=F32)) + br_ref[...]
    yield
    lane = lax.broadcasted_iota(jnp.int32, logits.shape, 1)
    neg = -jnp.inf

    def first_argmax(v):
        m = jnp.max(v, axis=-1, keepdims=True)
        return m, jnp.min(jnp.where(v == m, lane, ROUTE_LANES), axis=-1, keepdims=True)

    gl = jnp.where(lane < N_GROUPS, logits, neg)
    g_max, g_sel = first_argmax(gl)
    p_top = 1.0 / jnp.sum(jnp.exp(gl - g_max), axis=-1, keepdims=True)
    e_lo = N_GROUPS + EXPERTS_PER_GROUP * g_sel
    el = jnp.where((lane >= e_lo) & (lane < e_lo + EXPERTS_PER_GROUP), logits, neg)
    m1, i1 = first_argmax(el)
    m2, i2 = first_argmax(jnp.where(lane == i1, neg, el))
    r = jnp.exp(m2 - m1)
    w1 = p_top / (1.0 + r)
    w2 = w1 * r
    yield
    e1 = i1 - N_GROUPS
    e2 = i2 - N_GROUPS
    oh1 = (lane == e1).astype(BF16)
    oh2 = (lane == e2).astype(BF16)
    n_t = logits.shape[0]
    before = (lax.broadcasted_iota(jnp.int32, (n_t, n_t), 0)
              > lax.broadcasted_iota(jnp.int32, (n_t, n_t), 1)).astype(BF16)
    c1 = jnp.dot(before, oh1, preferred_element_type=F32)
    c2 = jnp.dot(before, oh2, preferred_element_type=F32)
    tot1 = jnp.sum(oh1.astype(F32), axis=0, keepdims=True)
    cnt = tot1 + jnp.sum(oh2.astype(F32), axis=0, keepdims=True)
    cnt = jnp.floor((cnt + (SUBLANES - 1)) * (1.0 / SUBLANES)) * SUBLANES
    lanes_before = (lax.broadcasted_iota(jnp.int32, (ROUTE_LANES, ROUTE_LANES), 0)
                    < lax.broadcasted_iota(jnp.int32, (ROUTE_LANES, ROUTE_LANES), 1)).astype(F32)
    loff = jnp.dot(jnp.broadcast_to(cnt, (SUBLANES, ROUTE_LANES)), lanes_before,
                   preferred_element_type=F32, precision=lax.Precision.HIGHEST)[0:1, :]
    lp1 = jnp.sum(oh1.astype(F32) * (loff + c1), axis=-1, keepdims=True)
    lp2 = jnp.sum(oh2.astype(F32) * (loff + tot1 + c2), axis=-1, keepdims=True)
    route = jnp.where(lane == 0, e1.astype(F32),
            jnp.where(lane == 1, e2.astype(F32),
            jnp.where(lane == 2, w1,
            jnp.where(lane == 3, w2,
            jnp.where(lane == 4, lp1, jnp.where(lane == 5, lp2, 0.0))))))
    row = lax.broadcasted_iota(jnp.int32, (SUBLANES, ROUTE_LANES), 0)
    return x2, h_hi, route, jnp.where(row == 0, cnt, jnp.where(row == 1, loff, 0.0))


def _xattn_route(x1, k, v, bsz, seqlen, g_xattn, w_q, w_o, g_moe, w_rg, b_rg, w_re, b_re):
    t, d = x1.shape
    w_r = jnp.pad(jnp.concatenate([w_rg, w_re], axis=1), ((0, 0), (0, ROUTE_LANES - N_GROUPS - N_EXPERTS)))
    b_r = jnp.pad(jnp.concatenate([b_rg, b_re]), (0, ROUTE_LANES - N_GROUPS - N_EXPERTS)).reshape(1, ROUTE_LANES)
    wr_hi = w_r.astype(BF16)
    wr_lo = (w_r - wr_hi.astype(F32)).astype(BF16)
    step_rows = XA_TILES * TOK_TILE
    nt = seqlen // step_rows
    tile = pl.BlockSpec((step_rows, d), lambda b, j: (b * nt + j, 0))
    kv_spec = pl.BlockSpec((1, MEM_LEN, d), lambda b, j: (b, 0, 0))
    const = lambda shape: pl.BlockSpec(shape, lambda b, j: (0, 0))
    return pl.pallas_call(
        _xattn_body,
        grid=(bsz, nt),
        in_specs=[tile, pl.BlockSpec((1, d, MEM_LEN), lambda b, j: (b, 0, 0)), kv_spec, const((1, d)),
                  const((d, d)), const((d, d)), const((1, d)),
                  const((d, ROUTE_LANES)), const((d, ROUTE_LANES)), const((1, ROUTE_LANES))],
        out_specs=[tile, tile, pl.BlockSpec((step_rows, ROUTE_LANES), lambda b, j: (b * nt + j, 0)),
                   pl.BlockSpec((XA_TILES, SUBLANES, ROUTE_LANES), lambda b, j: (b * nt + j, 0, 0))],
        out_shape=[jax.ShapeDtypeStruct((t, d), F32), jax.ShapeDtypeStruct((t, d), BF16),
                   jax.ShapeDtypeStruct((t, ROUTE_LANES), F32),
                   jax.ShapeDtypeStruct((t // TOK_TILE, SUBLANES, ROUTE_LANES), F32)],
        compiler_params=_cparams(("arbitrary", "arbitrary")),
        name="xattn_route",
    )(x1, jnp.swapaxes(k, 1, 2), v, g_xattn.reshape(1, d), w_q.astype(BF16), w_o.astype(BF16), g_moe.reshape(1, d),
      wr_hi, wr_lo, b_r)


def _rows_copy(src_ref, src0, dst_ref, dst0, n, sem):
    rows = lambda r0: pl.ds(pl.multiple_of(r0, SUBLANES), pl.multiple_of(n, SUBLANES))
    return pltpu.make_async_copy(src_ref.at[rows(src0), :], dst_ref.at[rows(dst0), :], sem)


def _pack_pairs(v):
    half = v.shape[1] // 2
    lo = lax.bitcast_convert_type(v[:, :half], jnp.uint32)
    hi = lax.bitcast_convert_type(v[:, half:], jnp.uint32)
    return (lo >> 16) | (hi & jnp.uint32(0xFFFF0000))


def _unpack_pairs(w):
    lo = lax.bitcast_convert_type(w << 16, F32)
    hi = lax.bitcast_convert_type(w & jnp.uint32(0xFFFF0000), F32)
    return jnp.concatenate([lo, hi], axis=1).astype(BF16)


def _dispatch_body(cnt_ref, loff_ref, gb_ref, tot_ref, fstart_ref, fcnt_ref, nu_ref, h_ref, route_ref,
                   xs_ref, buf_ref, zero_ref, sem, zsem):
    i = pl.program_id(0)
    nt = pl.num_programs(0)
    slot = i % 2
    buf = buf_ref.at[slot]

    def drain(tile, s):
        _rows_copy(buf_ref.at[s], 0, xs_ref, 0, tot_ref[tile], sem.at[s]).wait()

    @pl.when(i >= 2)
    def _():
        drain(i - 2, slot)

    rt = route_ref[...].T
    lp1 = rt[4:5, :].astype(jnp.int32)
    lp2 = rt[5:6, :].astype(jnp.int32)
    r_i = lax.broadcasted_iota(jnp.int32, (SORT_ROWS, TOK_TILE), 0)
    perm = jnp.where((r_i == lp1) | (r_i == lp2), 1.0, 0.0).astype(BF16)
    buf[...] = _pack_pairs(jnp.dot(perm, h_ref[...], preferred_element_type=F32))

    def per_expert(e, c):
        j = i * N_EXPERTS + e

        @pl.when(cnt_ref[j] > 0)
        def _():
            _rows_copy(buf, loff_ref[j], xs_ref, gb_ref[j], cnt_ref[j], sem.at[slot]).start()
        return c

    lax.fori_loop(0, N_EXPERTS, per_expert, 0)

    @pl.when(i == 0)
    def _():
        zero_ref[...] = jnp.zeros(zero_ref.shape, zero_ref.dtype)

        def fill(start):
            def body(e, c):
                @pl.when(fcnt_ref[e] > 0)
                def _():
                    copy = _rows_copy(zero_ref, 0, xs_ref, fstart_ref[e], fcnt_ref[e], zsem)
                    copy.start() if start else copy.wait()
                return c
            lax.fori_loop(0, N_EXPERTS, body, 0)

            def tail(blk, c):
                copy = _rows_copy(zero_ref, 0, xs_ref, blk * MOE_BLOCK, MOE_BLOCK, zsem)
                copy.start() if start else copy.wait()
                return c
            lax.fori_loop(nu_ref[0], xs_ref.shape[0] // MOE_BLOCK, tail, 0)

        fill(True)
        fill(False)

    @pl.when(i == nt - 1)
    def _():
        drain(i, slot)

        @pl.when(i >= 1)
        def _():
            drain(i - 1, 1 - slot)


def _dispatch(h2, route, tables, n_used, n_slots):
    t, d = h2.shape
    nt = t // TOK_TILE
    return pl.pallas_call(
        _dispatch_body,
        grid_spec=pltpu.PrefetchScalarGridSpec(
            num_scalar_prefetch=7,
            grid=(nt,),
            in_specs=[
                pl.BlockSpec((TOK_TILE, d), lambda i, *_: (i, 0)),
                pl.BlockSpec((TOK_TILE, ROUTE_LANES), lambda i, *_: (i, 0)),
            ],
            out_specs=pl.BlockSpec(memory_space=pl.ANY),
            scratch_shapes=[pltpu.VMEM((2, SORT_ROWS, d // 2), jnp.uint32),
                            pltpu.VMEM((MOE_BLOCK, d // 2), jnp.uint32),
                            pltpu.SemaphoreType.DMA((2,)), pltpu.SemaphoreType.DMA(())],
        ),
        out_shape=jax.ShapeDtypeStruct((n_slots, d // 2), jnp.uint32),
        compiler_params=_cparams(("arbitrary",)),
        name="dispatch",
    )(tables["cnt"], tables["loff"], tables["gb"], tables["tot"], tables["fill_start"],
      tables["fill_cnt"], n_used, h2, route)


def _experts_body(be_ref, nu_ref, first_ref, slot_ref, nxt_ref, x_ref, wg_hbm, wu_hbm, wd_hbm, y_ref,
                  wg_f, wu_f, wd_f, wg_bf, wu_bf, wd_bf, sem):
    b = pl.program_id(0)
    used = b < nu_ref[0]

    def weights(e, s):
        return [pltpu.make_async_copy(hbm.at[e], buf.at[s], sem.at[s])
                for hbm, buf in ((wg_hbm, wg_f), (wu_hbm, wu_f), (wd_hbm, wd_f))]

    @pl.when(used & (first_ref[b] == 1))
    def _():
        s = slot_ref[b]

        @pl.when(b == 0)
        def _():
            for copy in weights(be_ref[0], 0):
                copy.start()

        for copy in weights(be_ref[b], s):
            copy.wait()
        wg_bf[...] = wg_f[s].astype(BF16)
        wu_bf[...] = wu_f[s].astype(BF16)
        wd_bf[...] = wd_f[s].astype(BF16)

        @pl.when(nxt_ref[b] >= 0)
        def _():
            for copy in weights(nxt_ref[b], 1 - s):
                copy.start()

    @pl.when(used)
    def _():
        x = _unpack_pairs(x_ref[...])
        g = jnp.dot(x, wg_bf[...], preferred_element_type=F32)
        u = jnp.dot(x, wu_bf[...], preferred_element_type=F32)
        a = (g * _sigmoid(g) * u).astype(BF16)
        y = jnp.dot(a, wd_bf[...], preferred_element_type=F32)
        y_ref[...] = _pack_pairs(y.astype(BF16).astype(F32))

    @pl.when(jnp.logical_not(used))
    def _():
        y_ref[...] = jnp.zeros(y_ref.shape, y_ref.dtype)


def _experts(xs, etab, w_gate, w_up, w_down):
    n_slots, dp = xs.shape
    d = w_gate.shape[1]
    n_blocks = n_slots // MOE_BLOCK
    last = lambda b, nu: jnp.maximum(jnp.minimum(b, nu[0] - 1), 0)
    hbm = pl.BlockSpec(memory_space=pl.ANY)
    return pl.pallas_call(
        _experts_body,
        grid_spec=pltpu.PrefetchScalarGridSpec(
            num_scalar_prefetch=5,
            grid=(n_blocks,),
            in_specs=[pl.BlockSpec((MOE_BLOCK, dp), lambda b, be, nu, *_: (last(b, nu), 0)), hbm, hbm, hbm],
            out_specs=pl.BlockSpec((MOE_BLOCK, dp), lambda b, *_: (b, 0)),
            scratch_shapes=[pltpu.VMEM((2, d, D_EXPERT), F32), pltpu.VMEM((2, d, D_EXPERT), F32),
                            pltpu.VMEM((2, D_EXPERT, d), F32),
                            pltpu.VMEM((d, D_EXPERT), BF16), pltpu.VMEM((d, D_EXPERT), BF16),
                            pltpu.VMEM((D_EXPERT, d), BF16), pltpu.SemaphoreType.DMA((2,))],
        ),
        out_shape=jax.ShapeDtypeStruct((n_slots, dp), jnp.uint32),
        compiler_params=_cparams(("arbitrary",)),
        name="experts",
    )(etab["block_e"], etab["n_used"], etab["first"], etab["slot"], etab["nxt"], xs, w_gate, w_up, w_down)


def _combine_body(cnt_ref, loff_ref, gb_ref, tot_ref, x_ref, route_ref, g_ref, ys_ref, o_ref, ybuf_ref, sem):
    i = pl.program_id(0)
    nt = pl.num_programs(0)
    slot = i % 2

    def gather(tile, s):
        def per_expert(e, c):
            j = tile * N_EXPERTS + e

            @pl.when(cnt_ref[j] > 0)
            def _():
                _rows_copy(ys_ref, gb_ref[j], ybuf_ref.at[s], loff_ref[j], cnt_ref[j], sem.at[s]).start()
            return c

        lax.fori_loop(0, N_EXPERTS, per_expert, 0)

    @pl.when(i == 0)
    def _():
        ybuf_ref[...] = jnp.zeros(ybuf_ref.shape, ybuf_ref.dtype)
        gather(0, 0)

    @pl.when(i + 1 < nt)
    def _():
        gather(i + 1, 1 - slot)

    route = route_ref[...]
    c_i = lax.broadcasted_iota(jnp.int32, (TOK_TILE, SORT_ROWS), 1)
    lp1 = route[:, 4:5].astype(jnp.int32)
    lp2 = route[:, 5:6].astype(jnp.int32)
    pw = jnp.where(c_i == lp1, route[:, 2:3], jnp.where(c_i == lp2, route[:, 3:4], 0.0)).astype(BF16)
    tot = tot_ref[i]
    _rows_copy(ys_ref, 0, ybuf_ref.at[slot], 0, tot, sem.at[slot]).wait()
    r_i = lax.broadcasted_iota(jnp.int32, (SORT_ROWS, 1), 0)
    y = _unpack_pairs(jnp.where(r_i < tot, ybuf_ref[slot], jnp.uint32(0)))
    moe = jnp.dot(pw, y, preferred_element_type=F32)
    o_ref[...] = _rms(x_ref[...] + moe, g_ref[...])


def _combine(x2, route, tables, ys, g_final):
    t, d = x2.shape
    nt = t // TOK_TILE
    tile = pl.BlockSpec((TOK_TILE, d), lambda i, *_: (i, 0))
    return pl.pallas_call(
        _combine_body,
        grid_spec=pltpu.PrefetchScalarGridSpec(
            num_scalar_prefetch=4,
            grid=(nt,),
            in_specs=[
                tile,
                pl.BlockSpec((TOK_TILE, ROUTE_LANES), lambda i, *_: (i, 0)),
                pl.BlockSpec((1, d), lambda i, *_: (0, 0)),
                pl.BlockSpec(memory_space=pl.ANY),
            ],
            out_specs=tile,
            scratch_shapes=[pltpu.VMEM((2, SORT_ROWS, d // 2), jnp.uint32),
                            pltpu.SemaphoreType.DMA((2,))],
        ),
        out_shape=jax.ShapeDtypeStruct((t, d), F32),
        compiler_params=_cparams(("arbitrary",)),
        name="combine",
    )(tables["cnt"], tables["loff"], tables["gb"], tables["tot"], x2, route, g_final.reshape(1, d), ys)


def _routing_tables(stats, n_tok):
    cnt = stats[:, 0, :N_EXPERTS].astype(jnp.int32)
    loff = stats[:, 1, :N_EXPERTS].astype(jnp.int32)
    n_tiles = cnt.shape[0]
    counts = jnp.sum(cnt, axis=0)
    padded = ((counts + MOE_BLOCK - 1) // MOE_BLOCK) * MOE_BLOCK
    pad_end = jnp.cumsum(padded)
    pad_start = pad_end - padded
    gb = pad_start[None, :] + jnp.cumsum(cnt, axis=0) - cnt
    max_rows = (n_tok * TOP_K + n_tiles * N_EXPERTS * (SUBLANES - 1)
                + N_EXPERTS * (MOE_BLOCK - SUBLANES))
    n_blocks = -(-max_rows // MOE_BLOCK)
    n_slots = n_blocks * MOE_BLOCK
    block_start = jnp.arange(n_blocks, dtype=jnp.int32) * MOE_BLOCK
    block_e = jnp.minimum(jnp.sum((pad_end[None, :] <= block_start[:, None]).astype(jnp.int32), axis=1),
                          N_EXPERTS - 1)
    tables = dict(cnt=cnt.reshape(-1), loff=loff.reshape(-1), gb=gb.reshape(-1).astype(jnp.int32),
                  tot=jnp.sum(cnt, axis=1).astype(jnp.int32),
                  fill_start=(pad_start + counts).astype(jnp.int32),
                  fill_cnt=(padded - counts).astype(jnp.int32))
    n_used = (pad_end[-1:] // MOE_BLOCK).astype(jnp.int32)
    first = jnp.concatenate([jnp.ones((1,), jnp.int32), (block_e[1:] != block_e[:-1]).astype(jnp.int32)])
    seg_slot = (jnp.cumsum(first) - 1) % 2
    experts = jnp.arange(N_EXPERTS, dtype=jnp.int32)
    later = jnp.where((experts[None, :] > experts[:, None]) & (padded[None, :] > 0), experts[None, :], N_EXPERTS)
    next_e = jnp.min(later, axis=1)
    nxt = jnp.where(next_e < N_EXPERTS, next_e, -1)[block_e]
    etab = dict(block_e=block_e, n_used=n_used, first=first, slot=seg_slot.astype(jnp.int32),
                nxt=nxt.astype(jnp.int32))
    return tables, etab, n_slots


def kernel(x, mem, g_mix, w_in, conv_w, conv_b, ln_g, ln_b, ssd_conv_w, ssd_conv_b, dt_bias, a_log, d_skip, ssd_norm_g, w_out, g_xattn, g_mem, w_q, w_k, w_v, w_o, g_moe, w_router_group, b_router_group, w_router_expert, b_router_expert, w_gate, w_up, w_down, g_final):
    bsz, seqlen, d = x.shape
    n_tok = bsz * seqlen
    xt = x.reshape(n_tok, d)
    assert g_mix.shape[0] == 1, "the combine kernel applies the final norm: single layer only"
    for l in range(1):
        u, xbc, z, dt_raw = _mix_in(xt, bsz, seqlen, g_mix[l], w_in[l], conv_w[l], conv_b[l],
                                    ln_g[l], ln_b[l])
        y = _ssd(xbc, z, dt_raw, bsz, seqlen, ssd_conv_w[l], ssd_conv_b[l], dt_bias[l], a_log[l],
                 d_skip[l], ssd_norm_g[l])
        x1 = _out_proj(xt, u, y, w_out[l])
        k, v = _kv_proj(mem, g_mem[l], w_k[l], w_v[l])
        x2, h2, route, stats = _xattn_route(x1, k, v, bsz, seqlen, g_xattn[l], w_q[l], w_o[l], g_moe[l],
                                            w_router_group[l], b_router_group[l], w_router_expert[l],
                                            b_router_expert[l])
        tables, etab, n_slots = _routing_tables(stats, n_tok)
        xs = _dispatch(h2, route, tables, etab["n_used"], n_slots)
        ys = _experts(xs, etab, w_gate[l], w_up[l], w_down[l])
        xt = _combine(x2, route, tables, ys, g_final)
    return xt.reshape(bsz, seqlen, d)
```

```python
import functools

import jax
import jax.numpy as jnp
from jax import lax
from jax.experimental import pallas as pl
from jax.experimental.pallas import tpu as pltpu

F32 = jnp.float32
BF16 = jnp.bfloat16

D_MODEL = 1024
CONV_CH = 1024
CONV_K = 31
SSD_INNER = 1024
SSD_HEAD_DIM = 64
SSD_HEADS = 16
SSD_STATE = 128
SSD_GROUPS = 2
SSD_GROUP_W = SSD_INNER // SSD_GROUPS
SSD_CONV_K = 4
SSD_CHUNK = 128
SSD_BC = SSD_GROUPS * SSD_STATE
SSD_CONV_CH = SSD_INNER + 2 * SSD_BC
N_MAIN = 2 * CONV_CH + 2 * SSD_INNER + 2 * SSD_BC
XA_HEADS = 4
XA_HEAD_DIM = 256
MEM_LEN = 256
N_GROUPS = 4
EXPERTS_PER_GROUP = 8
N_EXPERTS = 32
TOP_K = 2
D_EXPERT = 512
MOE_BLOCK = 512
RMS_EPS = 1e-6
LN_EPS = 1e-5

LANES = 128
SUBLANES = 8
VMEM_LIMIT = 56 * 1024 * 1024

TOK_TILE = 512
CONV_TILE = 512
MM_CHUNK = 512
CONV_HALO = 32
CONV_ROWS = 128
SSD_TILE = 512
MOE_TILES = 2
XA_TILES = 2
SSD_UNROLL = 4
SSD_HALO = 8
ROUTE_LANES = 128
SORT_ROWS = TOP_K * TOK_TILE + N_EXPERTS * SUBLANES


def _cparams(sem):
    return pltpu.CompilerParams(dimension_semantics=sem, vmem_limit_bytes=VMEM_LIMIT)


def _rms(x, g):
    return x * lax.rsqrt(jnp.mean(x * x, axis=-1, keepdims=True) + RMS_EPS) * g


def _sigmoid(x):
    return 1.0 / (1.0 + jnp.exp(-x))


def _interleave(stages):
    results = {}
    while len(results) < len(stages):
        for k, item in enumerate(stages):
            if k not in results:
                try:
                    next(item)
                except StopIteration as done:
                    results[k] = done.value
    return [results[k] for k in range(len(stages))]


def _kv_body(m_ref, g_ref, wk_ref, wv_ref, k_ref, v_ref):
    m = _rms(m_ref[0], g_ref[...]).astype(BF16)
    k_ref[0] = jnp.dot(m, wk_ref[...], preferred_element_type=F32).astype(BF16)
    v_ref[0] = jnp.dot(m, wv_ref[...], preferred_element_type=F32).astype(BF16)


def _kv_proj(mem, g_mem, w_k, w_v):
    b, s, d = mem.shape
    w_spec = pl.BlockSpec((d, d), lambda i: (0, 0))
    kv_spec = pl.BlockSpec((1, s, d), lambda i: (i, 0, 0))
    return pl.pallas_call(
        _kv_body,
        grid=(b,),
        in_specs=[kv_spec, pl.BlockSpec((1, d), lambda i: (0, 0)), w_spec, w_spec],
        out_specs=[kv_spec, kv_spec],
        out_shape=[jax.ShapeDtypeStruct((b, s, d), BF16)] * 2,
        compiler_params=_cparams(("arbitrary",)),
        name="kv_proj",
    )(mem, g_mem.reshape(1, d), w_k.astype(BF16), w_v.astype(BF16))


def _mix_in_body(x_ref, g_ref, w_ref, wdt_ref, cw_ref, cb_ref, lg_ref, lb_ref,
                 u_ref, xbc_ref, z_ref, dt_ref, ubuf_ref, acc_ref):
    @pl.when(pl.program_id(1) == 0)
    def _():
        ubuf_ref[0:CONV_HALO, :] = jnp.zeros((CONV_HALO, CONV_CH), F32)

    h = _rms(x_ref[...], g_ref[...]).astype(BF16)
    proj = lambda lo: jnp.dot(h, w_ref[:, lo:lo + MM_CHUNK], preferred_element_type=F32)
    for c in range(CONV_CH // MM_CHUNK):
        cols = slice(c * MM_CHUNK, (c + 1) * MM_CHUNK)
        ubuf_ref[CONV_HALO:CONV_HALO + CONV_TILE, cols] = (
            proj(c * MM_CHUNK) * _sigmoid(proj(CONV_CH + c * MM_CHUNK)))
    z0 = 2 * CONV_CH
    for c in range(SSD_INNER // MM_CHUNK):
        z_ref[:, c * MM_CHUNK:(c + 1) * MM_CHUNK] = proj(z0 + c * MM_CHUNK).astype(BF16)
    x0 = z0 + SSD_INNER
    for c in range(SSD_CONV_CH // MM_CHUNK):
        xbc_ref[:, c * MM_CHUNK:(c + 1) * MM_CHUNK] = proj(x0 + c * MM_CHUNK).astype(BF16)
    dt_ref[...] = jnp.dot(h, wdt_ref[...], preferred_element_type=F32)

    first = CONV_HALO - (CONV_K - 1)
    for rc in range(CONV_TILE // CONV_ROWS):
        r0 = rc * CONV_ROWS
        rows = slice(r0, r0 + CONV_ROWS)
        for cb in range(CONV_CH // LANES):
            cols = slice(cb * LANES, (cb + 1) * LANES)
            acc = None
            for res in range(SUBLANES):
                part = None
                for k in range(CONV_K):
                    if (first + k) % SUBLANES == res:
                        term = cw_ref[k:k + 1, cols] * ubuf_ref[r0 + first + k:r0 + first + k + CONV_ROWS, cols]
                        part = term if part is None else part + term
                acc = part if acc is None else acc + part
            acc_ref[rows, cols] = acc
        u = acc_ref[rows, :] + cb_ref[...]
        mu = jnp.mean(u, axis=-1, keepdims=True)
        uc = u - mu
        var = jnp.mean(uc * uc, axis=-1, keepdims=True)
        y = uc * lax.rsqrt(var + LN_EPS) * lg_ref[...] + lb_ref[...]
        u_ref[rows, :] = (y * _sigmoid(y)).astype(BF16)

    ubuf_ref[0:CONV_HALO, :] = ubuf_ref[CONV_TILE:CONV_TILE + CONV_HALO, :]


def _mix_in(x2d, bsz, seqlen, g_mix, w_in, conv_w, conv_b, ln_g, ln_b):
    t, d = x2d.shape
    nt = seqlen // CONV_TILE
    w_main = w_in[:, :N_MAIN].astype(BF16)
    w_dt = jnp.pad(w_in[:, N_MAIN:], ((0, 0), (0, LANES - SSD_HEADS))).astype(BF16)
    row = lambda v: v.reshape(1, -1)
    const = lambda shape: pl.BlockSpec(shape, lambda b, j: (0, 0))
    tile = lambda width: pl.BlockSpec((CONV_TILE, width), lambda b, j: (b * nt + j, 0))
    return pl.pallas_call(
        _mix_in_body,
        grid=(bsz, nt),
        in_specs=[
            tile(d), const((1, d)), const((d, N_MAIN)), const((d, LANES)),
            const((CONV_K + 1, CONV_CH)), const((1, CONV_CH)), const((1, CONV_CH)), const((1, CONV_CH)),
        ],
        out_specs=[tile(CONV_CH), tile(SSD_CONV_CH), tile(SSD_INNER), tile(LANES)],
        out_shape=[jax.ShapeDtypeStruct((t, CONV_CH), BF16), jax.ShapeDtypeStruct((t, SSD_CONV_CH), BF16),
                   jax.ShapeDtypeStruct((t, SSD_INNER), BF16), jax.ShapeDtypeStruct((t, LANES), F32)],
        scratch_shapes=[
            pltpu.VMEM((CONV_HALO + CONV_TILE, CONV_CH), F32),
            pltpu.VMEM((CONV_TILE, CONV_CH), F32),
        ],
        compiler_params=_cparams(("arbitrary", "arbitrary")),
        name="mix_in",
    )(x2d, row(g_mix), w_main, w_dt, jnp.pad(conv_w, ((0, 1), (0, 0))), row(conv_b), row(ln_g), row(ln_b))


def _ssd_body(xbc_ref, z_ref, dt_ref, cw_ref, cb_ref, dtb_ref, alog_ref, dsk_ref, ng_ref,
              expand_ref, y_ref, xbuf_ref, act_ref, dts_ref, state_ref):
    @pl.when(pl.program_id(1) == 0)
    def _():
        xbuf_ref[0:SSD_HALO, :] = jnp.zeros((SSD_HALO, SSD_CONV_CH), F32)
        state_ref[...] = jnp.zeros(state_ref.shape, F32)

    xbuf_ref[SSD_HALO:SSD_HALO + SSD_TILE, :] = xbc_ref[...].astype(F32)
    full = xbuf_ref[...]
    conv = cb_ref[...] + cw_ref[SSD_CONV_K - 1:SSD_CONV_K, :] * full[SSD_HALO:, :]
    for back in range(1, SSD_CONV_K):
        past = pltpu.roll(full, back, 0)[SSD_HALO:, :]
        conv = conv + cw_ref[SSD_CONV_K - 1 - back:SSD_CONV_K - back, :] * past
    act_ref[...] = conv * _sigmoid(conv)
    xbuf_ref[0:SSD_HALO, :] = xbuf_ref[SSD_TILE:SSD_TILE + SSD_HALO, :]

    dt_in = dt_ref[...] + dtb_ref[...]
    dts_ref[...] = jnp.maximum(dt_in, 0.0) + jnp.log1p(jnp.exp(-jnp.abs(dt_in)))

    a_neg = -jnp.exp(alog_ref[...])
    q = SSD_CHUNK
    row_i = lax.broadcasted_iota(jnp.int32, (q, q), 0)
    col_i = lax.broadcasted_iota(jnp.int32, (q, q), 1)
    causal = row_i >= col_i
    tril = causal.astype(F32)
    lane_i = lax.broadcasted_iota(jnp.int32, (q, LANES), 1)
    low_half = lane_i < SSD_HEAD_DIM
    expand = expand_ref[...]

    def chunk(c, carry):
        r0 = pl.multiple_of(c * q, q)
        rows = pl.ds(r0, q)
        dtc = dts_ref[rows, :]
        a_cs = jnp.dot(tril, dtc * a_neg, preferred_element_type=F32,
                       precision=lax.Precision.HIGHEST)
        a_cs_t = a_cs.T
        dt_t = dtc.T
        a_end = a_cs[q - 1:q, :]
        e_exp = jnp.dot(jnp.exp(a_cs).astype(BF16), expand, preferred_element_type=F32)
        w_exp = jnp.dot((jnp.exp(a_end - a_cs) * dtc).astype(BF16), expand,
                        preferred_element_type=F32)
        dec_row = e_exp[q - 1:q, :]
        xc = act_ref[rows, 0:SSD_INNER]
        xw = (xc * w_exp).astype(BF16)
        y_parts = []
        for g in range(SSD_GROUPS):
            b_f = act_ref[rows, SSD_INNER + g * SSD_STATE:SSD_INNER + (g + 1) * SSD_STATE]
            c_f = act_ref[rows, SSD_INNER + SSD_BC + g * SSD_STATE:
                          SSD_INNER + SSD_BC + (g + 1) * SSD_STATE]
            b_g = b_f.astype(BF16)
            c_g = c_f.astype(BF16)
            cb = lax.dot_general(c_g, b_g, (((1,), (1,)), ((), ())),
                                 preferred_element_type=F32)
            gcols = slice(g * SSD_GROUP_W, (g + 1) * SSD_GROUP_W)
            st = state_ref[g]
            y_off = jnp.dot(c_g, st.astype(BF16), preferred_element_type=F32)
            state_ref[g] = st * dec_row[:, gcols] + jnp.dot(
                b_f.T.astype(BF16), xw[:, gcols], preferred_element_type=F32)
            for pair in range(SSD_GROUP_W // LANES):
                ms = []
                for hh in range(2):
                    h = g * (SSD_HEADS // SSD_GROUPS) + 2 * pair + hh
                    seg = a_cs[:, h:h + 1] - a_cs_t[h:h + 1, :]
                    dec = jnp.exp(jnp.where(causal, seg, -jnp.inf))
                    ms.append((cb * dec * dt_t[h:h + 1, :]).astype(BF16))
                lhs = jnp.concatenate(ms, axis=1)
                xp = xc[:, g * SSD_GROUP_W + pair * LANES:g * SSD_GROUP_W + (pair + 1) * LANES]
                rhs = jnp.concatenate([jnp.where(low_half, xp, 0.0),
                                       jnp.where(low_half, 0.0, xp)], axis=0).astype(BF16)
                y_diag = jnp.dot(lhs, rhs, preferred_element_type=F32)
                lo = pair * LANES
                y_parts.append(y_diag + y_off[:, lo:lo + LANES]
                               * e_exp[:, g * SSD_GROUP_W + lo:g * SSD_GROUP_W + lo + LANES])
        y = jnp.concatenate(y_parts, axis=1) + xc * dsk_ref[...]
        z = z_ref[rows, :].astype(F32)
        y = y * (z * _sigmoid(z))
        outs = []
        for g in range(SSD_GROUPS):
            yg = y[:, g * SSD_GROUP_W:(g + 1) * SSD_GROUP_W]
            outs.append(yg * lax.rsqrt(jnp.mean(yg * yg, axis=-1, keepdims=True) + RMS_EPS))
        y_ref[rows, :] = (jnp.concatenate(outs, axis=1) * ng_ref[...]).astype(BF16)
        return carry

    lax.fori_loop(0, SSD_TILE // q, chunk, 0, unroll=SSD_UNROLL)


def _ssd(xbc, z, dt_raw, bsz, seqlen, ssd_conv_w, ssd_conv_b, dt_bias, a_log, d_skip, ssd_norm_g):
    nt = seqlen // SSD_TILE
    pad_h = lambda v: jnp.pad(v, (0, LANES - SSD_HEADS)).reshape(1, LANES)
    expand = (jnp.arange(LANES)[:, None] == (jnp.arange(SSD_INNER) // SSD_HEAD_DIM)[None, :]).astype(BF16)
    const = lambda shape: pl.BlockSpec(shape, lambda b, j: (0, 0))
    return pl.pallas_call(
        _ssd_body,
        grid=(bsz, nt),
        in_specs=[
            pl.BlockSpec((SSD_TILE, SSD_CONV_CH), lambda b, j: (b * nt + j, 0)),
            pl.BlockSpec((SSD_TILE, SSD_INNER), lambda b, j: (b * nt + j, 0)),
            pl.BlockSpec((SSD_TILE, LANES), lambda b, j: (b * nt + j, 0)),
            const((SSD_CONV_K, SSD_CONV_CH)),
            const((1, SSD_CONV_CH)),
            const((1, LANES)), const((1, LANES)),
            const((1, SSD_INNER)), const((1, SSD_INNER)),
            const((LANES, SSD_INNER)),
        ],
        out_specs=pl.BlockSpec((SSD_TILE, SSD_INNER), lambda b, j: (b * nt + j, 0)),
        out_shape=jax.ShapeDtypeStruct((bsz * seqlen, SSD_INNER), BF16),
        scratch_shapes=[
            pltpu.VMEM((SSD_HALO + SSD_TILE, SSD_CONV_CH), F32),
            pltpu.VMEM((SSD_TILE, SSD_CONV_CH), F32),
            pltpu.VMEM((SSD_TILE, LANES), F32),
            pltpu.VMEM((SSD_GROUPS, SSD_STATE, SSD_GROUP_W), F32),
        ],
        compiler_params=_cparams(("arbitrary", "arbitrary")),
        name="ssd",
    )(xbc, z, dt_raw, ssd_conv_w, ssd_conv_b.reshape(1, SSD_CONV_CH), pad_h(dt_bias),
      pad_h(a_log), jnp.repeat(d_skip, SSD_HEAD_DIM).reshape(1, SSD_INNER),
      ssd_norm_g.reshape(1, SSD_INNER), expand)


def _outproj_body(x_ref, u_ref, y_ref, wu_ref, wy_ref, o_ref):
    o_ref[...] = (x_ref[...]
                  + jnp.dot(u_ref[...], wu_ref[...], preferred_element_type=F32)
                  + jnp.dot(y_ref[...], wy_ref[...], preferred_element_type=F32))


def _out_proj(x2d, u, y, w_out):
    t, d = x2d.shape
    w = w_out.astype(BF16)
    tile = pl.BlockSpec((TOK_TILE, d), lambda i: (i, 0))
    w_spec = pl.BlockSpec((d, d), lambda i: (0, 0))
    return pl.pallas_call(
        _outproj_body,
        grid=(t // TOK_TILE,),
        in_specs=[tile, tile, tile, w_spec, w_spec],
        out_specs=tile,
        out_shape=jax.ShapeDtypeStruct((t, d), F32),
        compiler_params=_cparams(("arbitrary",)),
        name="out_proj",
    )(x2d, u, y, w[:CONV_CH], w[CONV_CH:])


def _xattn_body(x_ref, k_ref, v_ref, gx_ref, wq_ref, wo_ref, gm_ref, wr_hi_ref, wr_lo_ref, br_ref,
                x2_ref, h2_ref, route_ref, stats_ref):
    results = _interleave([
        _xattn_tile(x_ref[s * TOK_TILE:(s + 1) * TOK_TILE, :], k_ref, v_ref, gx_ref, wq_ref, wo_ref,
                    gm_ref, wr_hi_ref, wr_lo_ref, br_ref) for s in range(XA_TILES)])
    for s in range(XA_TILES):
        rows = slice(s * TOK_TILE, (s + 1) * TOK_TILE)
        x2_ref[rows, :], h2_ref[rows, :], route_ref[rows, :], stats_ref[s] = results[s]


def _xattn_tile(x, k_ref, v_ref, gx_ref, wq_ref, wo_ref, gm_ref, wr_hi_ref, wr_lo_ref, br_ref):
    h = _rms(x, gx_ref[...]).astype(BF16)
    q = (jnp.dot(h, wq_ref[...], preferred_element_type=F32) * (XA_HEAD_DIM ** -0.5)).astype(BF16)
    yield
    heads = []
    for i in range(XA_HEADS):
        cols = slice(i * XA_HEAD_DIM, (i + 1) * XA_HEAD_DIM)
        s = jnp.dot(q[:, cols], k_ref[0, cols, :], preferred_element_type=F32)
        p = jnp.exp(s - jnp.max(s, axis=-1, keepdims=True))
        p = p / jnp.sum(p, axis=-1, keepdims=True)
        heads.append(jnp.dot(p.astype(BF16), v_ref[0, :, cols], preferred_element_type=F32))
        yield
    o = jnp.concatenate(heads, axis=1).astype(BF16)
    x2 = x + jnp.dot(o, wo_ref[...], preferred_element_type=F32)
    yield

    h2 = _rms(x2, gm_ref[...])
    h_hi = h2.astype(BF16)
    h_lo = (h2 - h_hi.astype(F32)).astype(BF16)
    logits = (jnp.dot(h_hi, wr_hi_ref[...], preferred_element_type=F32)
              + jnp.dot(h_lo, wr_hi_ref[...], preferred_element_type=F32)
              + jnp.dot(h_hi, wr_lo_ref[...], preferred_element_type=F32)) + br_ref[...]
    yield
    lane = lax.broadcasted_iota(jnp.int32, logits.shape, 1)
    neg = -jnp.inf

    def first_argmax(v):
        m = jnp.max(v, axis=-1, keepdims=True)
        return m, jnp.min(jnp.where(v == m, lane, ROUTE_LANES), axis=-1, keepdims=True)

    gl = jnp.where(lane < N_GROUPS, logits, neg)
    g_max, g_sel = first_argmax(gl)
    p_top = 1.0 / jnp.sum(jnp.exp(gl - g_max), axis=-1, keepdims=True)
    e_lo = N_GROUPS + EXPERTS_PER_GROUP * g_sel
    el = jnp.where((lane >= e_lo) & (lane < e_lo + EXPERTS_PER_GROUP), logits, neg)
    m1, i1 = first_argmax(el)
    m2, i2 = first_argmax(jnp.where(lane == i1, neg, el))
    r = jnp.exp(m2 - m1)
    w1 = p_top / (1.0 + r)
    w2 = w1 * r
    yield
    e1 = i1 - N_GROUPS
    e2 = i2 - N_GROUPS
    oh1 = (lane == e1).astype(BF16)
    oh2 = (lane == e2).astype(BF16)
    n_t = logits.shape[0]
    before = (lax.broadcasted_iota(jnp.int32, (n_t, n_t), 0)
              > lax.broadcasted_iota(jnp.int32, (n_t, n_t), 1)).astype(BF16)
    c1 = jnp.dot(before, oh1, preferred_element_type=F32)
    c2 = jnp.dot(before, oh2, preferred_element_type=F32)
    tot1 = jnp.sum(oh1.astype(F32), axis=0, keepdims=True)
    cnt = tot1 + jnp.sum(oh2.astype(F32), axis=0, keepdims=True)
    cnt = jnp.floor((cnt + (SUBLANES - 1)) * (1.0 / SUBLANES)) * SUBLANES
    lanes_before = (lax.broadcasted_iota(jnp.int32, (ROUTE_LANES, ROUTE_LANES), 0)
                    < lax.broadcasted_iota(jnp.int32, (ROUTE_LANES, ROUTE_LANES), 1)).astype(F32)
    loff = jnp.dot(jnp.broadcast_to(cnt, (SUBLANES, ROUTE_LANES)), lanes_before,
                   preferred_element_type=F32, precision=lax.Precision.HIGHEST)[0:1, :]
    lp1 = jnp.sum(oh1.astype(F32) * (loff + c1), axis=-1, keepdims=True)
    lp2 = jnp.sum(oh2.astype(F32) * (loff + tot1 + c2), axis=-1, keepdims=True)
    route = jnp.where(lane == 0, e1.astype(F32),
            jnp.where(lane == 1, e2.astype(F32),
            jnp.where(lane == 2, w1,
            jnp.where(lane == 3, w2,
            jnp.where(lane == 4, lp1, jnp.where(lane == 5, lp2, 0.0))))))
    row = lax.broadcasted_iota(jnp.int32, (SUBLANES, ROUTE_LANES), 0)
    return x2, h_hi, route, jnp.where(row == 0, cnt, jnp.where(row == 1, loff, 0.0))


def _xattn_route(x1, k, v, bsz, seqlen, g_xattn, w_q, w_o, g_moe, w_rg, b_rg, w_re, b_re):
    t, d = x1.shape
    w_r = jnp.pad(jnp.concatenate([w_rg, w_re], axis=1), ((0, 0), (0, ROUTE_LANES - N_GROUPS - N_EXPERTS)))
    b_r = jnp.pad(jnp.concatenate([b_rg, b_re]), (0, ROUTE_LANES - N_GROUPS - N_EXPERTS)).reshape(1, ROUTE_LANES)
    wr_hi = w_r.astype(BF16)
    wr_lo = (w_r - wr_hi.astype(F32)).astype(BF16)
    step_rows = XA_TILES * TOK_TILE
    nt = seqlen // step_rows
    tile = pl.BlockSpec((step_rows, d), lambda b, j: (b * nt + j, 0))
    kv_spec = pl.BlockSpec((1, MEM_LEN, d), lambda b, j: (b, 0, 0))
    const = lambda shape: pl.BlockSpec(shape, lambda b, j: (0, 0))
    return pl.pallas_call(
        _xattn_body,
        grid=(bsz, nt),
        in_specs=[tile, pl.BlockSpec((1, d, MEM_LEN), lambda b, j: (b, 0, 0)), kv_spec, const((1, d)),
                  const((d, d)), const((d, d)), const((1, d)),
                  const((d, ROUTE_LANES)), const((d, ROUTE_LANES)), const((1, ROUTE_LANES))],
        out_specs=[tile, tile, pl.BlockSpec((step_rows, ROUTE_LANES), lambda b, j: (b * nt + j, 0)),
                   pl.BlockSpec((XA_TILES, SUBLANES, ROUTE_LANES), lambda b, j: (b * nt + j, 0, 0))],
        out_shape=[jax.ShapeDtypeStruct((t, d), F32), jax.ShapeDtypeStruct((t, d), BF16),
                   jax.ShapeDtypeStruct((t, ROUTE_LANES), F32),
                   jax.ShapeDtypeStruct((t // TOK_TILE, SUBLANES, ROUTE_LANES), F32)],
        compiler_params=_cparams(("arbitrary", "arbitrary")),
        name="xattn_route",
    )(x1, jnp.swapaxes(k, 1, 2), v, g_xattn.reshape(1, d), w_q.astype(BF16), w_o.astype(BF16), g_moe.reshape(1, d),
      wr_hi, wr_lo, b_r)


def _rows_copy(src_ref, src0, dst_ref, dst0, n, sem):
    rows = lambda r0: pl.ds(pl.multiple_of(r0, SUBLANES), pl.multiple_of(n, SUBLANES))
    return pltpu.make_async_copy(src_ref.at[rows(src0), :], dst_ref.at[rows(dst0), :], sem)


def _pack_pairs(v):
    half = v.shape[1] // 2
    lo = lax.bitcast_convert_type(v[:, :half], jnp.uint32)
    hi = lax.bitcast_convert_type(v[:, half:], jnp.uint32)
    return (lo >> 16) | (hi & jnp.uint32(0xFFFF0000))


def _unpack_pairs(w):
    lo = lax.bitcast_convert_type(w << 16, F32)
    hi = lax.bitcast_convert_type(w & jnp.uint32(0xFFFF0000), F32)
    return jnp.concatenate([lo, hi], axis=1).astype(BF16)


def _dispatch_body(cnt_ref, loff_ref, gb_ref, tot_ref, fstart_ref, fcnt_ref, nu_ref, h_ref, route_ref,
                   xs_ref, buf_ref, zero_ref, sem, zsem):
    i = pl.program_id(0)
    nt = pl.num_programs(0)
    slot = i % 2
    buf = buf_ref.at[slot]

    def drain(step, s):
        n = sum(tot_ref[step * MOE_TILES + k] for k in range(MOE_TILES))
        _rows_copy(buf_ref.at[s], 0, xs_ref, 0, n, sem.at[s]).wait()

    @pl.when(i >= 2)
    def _():
        drain(i - 2, slot)

    def sort_tile(k):
        rows = slice(k * TOK_TILE, (k + 1) * TOK_TILE)
        rt = route_ref[rows, :].T
        lp1 = rt[4:5, :].astype(jnp.int32)
        lp2 = rt[5:6, :].astype(jnp.int32)
        r_i = lax.broadcasted_iota(jnp.int32, (SORT_ROWS, TOK_TILE), 0)
        perm = jnp.where((r_i == lp1) | (r_i == lp2), 1.0, 0.0).astype(BF16)
        yield
        buf[k * SORT_ROWS:(k + 1) * SORT_ROWS, :] = _pack_pairs(
            jnp.dot(perm, h_ref[rows, :], preferred_element_type=F32))

    _interleave([sort_tile(k) for k in range(MOE_TILES)])

    for k in range(MOE_TILES):
        def per_expert(e, c, k=k):
            j = (i * MOE_TILES + k) * N_EXPERTS + e

            @pl.when(cnt_ref[j] > 0)
            def _():
                _rows_copy(buf, k * SORT_ROWS + loff_ref[j], xs_ref, gb_ref[j], cnt_ref[j],
                           sem.at[slot]).start()
            return c

        lax.fori_loop(0, N_EXPERTS, per_expert, 0)

    @pl.when(i == 0)
    def _():
        zero_ref[...] = jnp.zeros(zero_ref.shape, zero_ref.dtype)

        def fill(start):
            def body(e, c):
                @pl.when(fcnt_ref[e] > 0)
                def _():
                    copy = _rows_copy(zero_ref, 0, xs_ref, fstart_ref[e], fcnt_ref[e], zsem)
                    copy.start() if start else copy.wait()
                return c
            lax.fori_loop(0, N_EXPERTS, body, 0)

            def tail(blk, c):
                copy = _rows_copy(zero_ref, 0, xs_ref, blk * MOE_BLOCK, MOE_BLOCK, zsem)
                copy.start() if start else copy.wait()
                return c
            lax.fori_loop(nu_ref[0], xs_ref.shape[0] // MOE_BLOCK, tail, 0)

        fill(True)
        fill(False)

    @pl.when(i == nt - 1)
    def _():
        drain(i, slot)

        @pl.when(i >= 1)
        def _():
            drain(i - 1, 1 - slot)


def _dispatch(h2, route, tables, n_used, n_slots):
    t, d = h2.shape
    step_rows = MOE_TILES * TOK_TILE
    return pl.pallas_call(
        _dispatch_body,
        grid_spec=pltpu.PrefetchScalarGridSpec(
            num_scalar_prefetch=7,
            grid=(t // step_rows,),
            in_specs=[
                pl.BlockSpec((step_rows, d), lambda i, *_: (i, 0)),
                pl.BlockSpec((step_rows, ROUTE_LANES), lambda i, *_: (i, 0)),
            ],
            out_specs=pl.BlockSpec(memory_space=pl.ANY),
            scratch_shapes=[pltpu.VMEM((2, MOE_TILES * SORT_ROWS, d // 2), jnp.uint32),
                            pltpu.VMEM((MOE_BLOCK, d // 2), jnp.uint32),
                            pltpu.SemaphoreType.DMA((2,)), pltpu.SemaphoreType.DMA(())],
        ),
        out_shape=jax.ShapeDtypeStruct((n_slots, d // 2), jnp.uint32),
        compiler_params=_cparams(("arbitrary",)),
        name="dispatch",
    )(tables["cnt"], tables["loff"], tables["gb"], tables["tot"], tables["fill_start"],
      tables["fill_cnt"], n_used, h2, route)


def _experts_body(be_ref, nu_ref, first_ref, slot_ref, nxt_ref, x_ref, wg_hbm, wu_hbm, wd_hbm, y_ref,
                  wg_f, wu_f, wd_f, wg_bf, wu_bf, wd_bf, sem):
    b = pl.program_id(0)
    used = b < nu_ref[0]

    def weights(e, s):
        return [pltpu.make_async_copy(hbm.at[e], buf.at[s], sem.at[s])
                for hbm, buf in ((wg_hbm, wg_f), (wu_hbm, wu_f), (wd_hbm, wd_f))]

    @pl.when(used & (first_ref[b] == 1))
    def _():
        s = slot_ref[b]

        @pl.when(b == 0)
        def _():
            for copy in weights(be_ref[0], 0):
                copy.start()

        for copy in weights(be_ref[b], s):
            copy.wait()
        wg_bf[...] = wg_f[s].astype(BF16)
        wu_bf[...] = wu_f[s].astype(BF16)
        wd_bf[...] = wd_f[s].astype(BF16)

        @pl.when(nxt_ref[b] >= 0)
        def _():
            for copy in weights(nxt_ref[b], 1 - s):
                copy.start()

    @pl.when(used)
    def _():
        x = _unpack_pairs(x_ref[...])
        g = jnp.dot(x, wg_bf[...], preferred_element_type=F32)
        u = jnp.dot(x, wu_bf[...], preferred_element_type=F32)
        a = (g * _sigmoid(g) * u).astype(BF16)
        y = jnp.dot(a, wd_bf[...], preferred_element_type=F32)
        y_ref[...] = _pack_pairs(y.astype(BF16).astype(F32))

    @pl.when(jnp.logical_not(used))
    def _():
        y_ref[...] = jnp.zeros(y_ref.shape, y_ref.dtype)


def _experts(xs, etab, w_gate, w_up, w_down):
    n_slots, dp = xs.shape
    d = w_gate.shape[1]
    n_blocks = n_slots // MOE_BLOCK
    last = lambda b, nu: jnp.maximum(jnp.minimum(b, nu[0] - 1), 0)
    hbm = pl.BlockSpec(memory_space=pl.ANY)
    return pl.pallas_call(
        _experts_body,
        grid_spec=pltpu.PrefetchScalarGridSpec(
            num_scalar_prefetch=5,
            grid=(n_blocks,),
            in_specs=[pl.BlockSpec((MOE_BLOCK, dp), lambda b, be, nu, *_: (last(b, nu), 0)), hbm, hbm, hbm],
            out_specs=pl.BlockSpec((MOE_BLOCK, dp), lambda b, *_: (b, 0)),
            scratch_shapes=[pltpu.VMEM((2, d, D_EXPERT), F32), pltpu.VMEM((2, d, D_EXPERT), F32),
                            pltpu.VMEM((2, D_EXPERT, d), F32),
                            pltpu.VMEM((d, D_EXPERT), BF16), pltpu.VMEM((d, D_EXPERT), BF16),
                            pltpu.VMEM((D_EXPERT, d), BF16), pltpu.SemaphoreType.DMA((2,))],
        ),
        out_shape=jax.ShapeDtypeStruct((n_slots, dp), jnp.uint32),
        compiler_params=_cparams(("arbitrary",)),
        name="experts",
    )(etab["block_e"], etab["n_used"], etab["first"], etab["slot"], etab["nxt"], xs, w_gate, w_up, w_down)


def _combine_body(cnt_ref, loff_ref, gb_ref, tot_ref, x_ref, route_ref, g_ref, ys_ref, o_ref, ybuf_ref, sem):
    i = pl.program_id(0)
    nt = pl.num_programs(0)
    slot = i % 2

    def gather(step, s):
        for k in range(MOE_TILES):
            def per_expert(e, c, k=k):
                j = (step * MOE_TILES + k) * N_EXPERTS + e

                @pl.when(cnt_ref[j] > 0)
                def _():
                    _rows_copy(ys_ref, gb_ref[j], ybuf_ref.at[s], k * SORT_ROWS + loff_ref[j], cnt_ref[j],
                               sem.at[s]).start()
                return c

            lax.fori_loop(0, N_EXPERTS, per_expert, 0)

    @pl.when(i == 0)
    def _():
        ybuf_ref[...] = jnp.zeros(ybuf_ref.shape, ybuf_ref.dtype)
        gather(0, 0)

    @pl.when(i + 1 < nt)
    def _():
        gather(i + 1, 1 - slot)

    def combine_tile(k):
        rows = slice(k * TOK_TILE, (k + 1) * TOK_TILE)
        route = route_ref[rows, :]
        c_i = lax.broadcasted_iota(jnp.int32, (TOK_TILE, SORT_ROWS), 1)
        lp1 = route[:, 4:5].astype(jnp.int32)
        lp2 = route[:, 5:6].astype(jnp.int32)
        pw = jnp.where(c_i == lp1, route[:, 2:3], jnp.where(c_i == lp2, route[:, 3:4], 0.0)).astype(BF16)
        yield
        r_i = lax.broadcasted_iota(jnp.int32, (SORT_ROWS, 1), 0)
        y = _unpack_pairs(jnp.where(r_i < tot_ref[i * MOE_TILES + k],
                                    ybuf_ref[slot, k * SORT_ROWS:(k + 1) * SORT_ROWS, :], jnp.uint32(0)))
        moe = jnp.dot(pw, y, preferred_element_type=F32)
        yield
        o_ref[rows, :] = _rms(x_ref[rows, :] + moe, g_ref[...])

    tiles = [combine_tile(k) for k in range(MOE_TILES)]
    for tile in tiles:
        next(tile)
    n = sum(tot_ref[i * MOE_TILES + k] for k in range(MOE_TILES))
    _rows_copy(ys_ref, 0, ybuf_ref.at[slot], 0, n, sem.at[slot]).wait()
    _interleave(tiles)


def _combine(x2, route, tables, ys, g_final):
    t, d = x2.shape
    step_rows = MOE_TILES * TOK_TILE
    tile = pl.BlockSpec((step_rows, d), lambda i, *_: (i, 0))
    return pl.pallas_call(
        _combine_body,
        grid_spec=pltpu.PrefetchScalarGridSpec(
            num_scalar_prefetch=4,
            grid=(t // step_rows,),
            in_specs=[
                tile,
                pl.BlockSpec((step_rows, ROUTE_LANES), lambda i, *_: (i, 0)),
                pl.BlockSpec((1, d), lambda i, *_: (0, 0)),
                pl.BlockSpec(memory_space=pl.ANY),
            ],
            out_specs=tile,
            scratch_shapes=[pltpu.VMEM((2, MOE_TILES * SORT_ROWS, d // 2), jnp.uint32),
                            pltpu.SemaphoreType.DMA((2,))],
        ),
        out_shape=jax.ShapeDtypeStruct((t, d), F32),
        compiler_params=_cparams(("arbitrary",)),
        name="combine",
    )(tables["cnt"], tables["loff"], tables["gb"], tables["tot"], x2, route, g_final.reshape(1, d), ys)


def _routing_tables(stats, n_tok):
    cnt = stats[:, 0, :N_EXPERTS].astype(jnp.int32)
    loff = stats[:, 1, :N_EXPERTS].astype(jnp.int32)
    n_tiles = cnt.shape[0]
    counts = jnp.sum(cnt, axis=0)
    padded = ((counts + MOE_BLOCK - 1) // MOE_BLOCK) * MOE_BLOCK
    pad_end = jnp.cumsum(padded)
    pad_start = pad_end - padded
    gb = pad_start[None, :] + jnp.cumsum(cnt, axis=0) - cnt
    max_rows = (n_tok * TOP_K + n_tiles * N_EXPERTS * (SUBLANES - 1)
                + N_EXPERTS * (MOE_BLOCK - SUBLANES))
    n_blocks = -(-max_rows // MOE_BLOCK)
    n_slots = n_blocks * MOE_BLOCK
    block_start = jnp.arange(n_blocks, dtype=jnp.int32) * MOE_BLOCK
    block_e = jnp.minimum(jnp.sum((pad_end[None, :] <= block_start[:, None]).astype(jnp.int32), axis=1),
                          N_EXPERTS - 1)
    tables = dict(cnt=cnt.reshape(-1), loff=loff.reshape(-1), gb=gb.reshape(-1).astype(jnp.int32),
                  tot=jnp.sum(cnt, axis=1).astype(jnp.int32),
                  fill_start=(pad_start + counts).astype(jnp.int32),
                  fill_cnt=(padded - counts).astype(jnp.int32))
    n_used = (pad_end[-1:] // MOE_BLOCK).astype(jnp.int32)
    first = jnp.concatenate([jnp.ones((1,), jnp.int32), (block_e[1:] != block_e[:-1]).astype(jnp.int32)])
    seg_slot = (jnp.cumsum(first) - 1) % 2
    experts = jnp.arange(N_EXPERTS, dtype=jnp.int32)
    later = jnp.where((experts[None, :] > experts[:, None]) & (padded[None, :] > 0), experts[None, :], N_EXPERTS)
    next_e = jnp.min(later, axis=1)
    nxt = jnp.where(next_e < N_EXPERTS, next_e, -1)[block_e]
    etab = dict(block_e=block_e, n_used=n_used, first=first, slot=seg_slot.astype(jnp.int32),
                nxt=nxt.astype(jnp.int32))
    return tables, etab, n_slots


def kernel(x, mem, g_mix, w_in, conv_w, conv_b, ln_g, ln_b, ssd_conv_w, ssd_conv_b, dt_bias, a_log, d_skip, ssd_norm_g, w_out, g_xattn, g_mem, w_q, w_k, w_v, w_o, g_moe, w_router_group, b_router_group, w_router_expert, b_router_expert, w_gate, w_up, w_down, g_final):
    bsz, seqlen, d = x.shape
    n_tok = bsz * seqlen
    xt = x.reshape(n_tok, d)
    assert g_mix.shape[0] == 1, "the combine kernel applies the final norm: single layer only"
    for l in range(1):
        u, xbc, z, dt_raw = _mix_in(xt, bsz, seqlen, g_mix[l], w_in[l], conv_w[l], conv_b[l],
                                    ln_g[l], ln_b[l])
        y = _ssd(xbc, z, dt_raw, bsz, seqlen, ssd_conv_w[l], ssd_conv_b[l], dt_bias[l], a_log[l],
                 d_skip[l], ssd_norm_g[l])
        x1 = _out_proj(xt, u, y, w_out[l])
        k, v = _kv_proj(mem, g_mem[l], w_k[l], w_v[l])
        x2, h2, route, stats = _xattn_route(x1, k, v, bsz, seqlen, g_xattn[l], w_q[l], w_o[l], g_moe[l],
                                            w_router_group[l], b_router_group[l], w_router_expert[l],
                                            b_router_expert[l])
        tables, etab, n_slots = _routing_tables(stats, n_tok)
        xs = _dispatch(h2, route, tables, etab["n_used"], n_slots)
        ys = _experts(xs, etab, w_gate[l], w_up[l], w_down[l])
        xt = _combine(x2, route, tables, ys, g_final)
    return xt.reshape(bsz, seqlen, d)
```

```python
import functools

import jax
import jax.numpy as jnp
from jax import lax
from jax.experimental import pallas as pl
from jax.experimental.pallas import tpu as pltpu

F32 = jnp.float32
BF16 = jnp.bfloat16

D_MODEL = 1024
CONV_CH = 1024
CONV_K = 31
SSD_INNER = 1024
SSD_HEAD_DIM = 64
SSD_HEADS = 16
SSD_STATE = 128
SSD_GROUPS = 2
SSD_GROUP_W = SSD_INNER // SSD_GROUPS
SSD_CONV_K = 4
SSD_CHUNK = 128
SSD_BC = SSD_GROUPS * SSD_STATE
SSD_CONV_CH = SSD_INNER + 2 * SSD_BC
N_MAIN = 2 * CONV_CH + 2 * SSD_INNER + 2 * SSD_BC
XA_HEADS = 4
XA_HEAD_DIM = 256
MEM_LEN = 256
N_GROUPS = 4
EXPERTS_PER_GROUP = 8
N_EXPERTS = 32
TOP_K = 2
D_EXPERT = 512
MOE_BLOCK = 512
RMS_EPS = 1e-6
LN_EPS = 1e-5

LANES = 128
SUBLANES = 8
VMEM_LIMIT = 56 * 1024 * 1024

TOK_TILE = 512
CONV_TILE = 512
MM_CHUNK = 512
CONV_HALO = 32
CONV_ROWS = 128
SSD_TILE = 512
MOE_TILES = 2
XA_TILES = 2
SSD_UNROLL = 4
SSD_HALO = 8
ROUTE_LANES = 128
SORT_ROWS = TOP_K * TOK_TILE + N_EXPERTS * SUBLANES


def _cparams(sem):
    return pltpu.CompilerParams(dimension_semantics=sem, vmem_limit_bytes=VMEM_LIMIT)


def _rms(x, g):
    return x * lax.rsqrt(jnp.mean(x * x, axis=-1, keepdims=True) + RMS_EPS) * g


def _sigmoid(x):
    return 1.0 / (1.0 + jnp.exp(-x))


def _interleave(stages):
    results = {}
    while len(results) < len(stages):
        for k, item in enumerate(stages):
            if k not in results:
                try:
                    next(item)
                except StopIteration as done:
                    results[k] = done.value
    return [results[k] for k in range(len(stages))]


def _kv_body(m_ref, g_ref, wk_ref, wv_ref, k_ref, v_ref):
    m = _rms(m_ref[0], g_ref[...]).astype(BF16)
    k_ref[0] = jnp.dot(m, wk_ref[...], preferred_element_type=F32).astype(BF16)
    v_ref[0] = jnp.dot(m, wv_ref[...], preferred_element_type=F32).astype(BF16)


def _kv_proj(mem, g_mem, w_k, w_v):
    b, s, d = mem.shape
    w_spec = pl.BlockSpec((d, d), lambda i: (0, 0))
    kv_spec = pl.BlockSpec((1, s, d), lambda i: (i, 0, 0))
    return pl.pallas_call(
        _kv_body,
        grid=(b,),
        in_specs=[kv_spec, pl.BlockSpec((1, d), lambda i: (0, 0)), w_spec, w_spec],
        out_specs=[kv_spec, kv_spec],
        out_shape=[jax.ShapeDtypeStruct((b, s, d), BF16)] * 2,
        compiler_params=_cparams(("arbitrary",)),
        name="kv_proj",
    )(mem, g_mem.reshape(1, d), w_k.astype(BF16), w_v.astype(BF16))


def _mix_in_body(x_ref, g_ref, w_ref, wdt_ref, cw_ref, cb_ref, lg_ref, lb_ref,
                 u_ref, xbc_ref, z_ref, dt_ref, ubuf_ref, acc_ref):
    @pl.when(pl.program_id(1) == 0)
    def _():
        ubuf_ref[0:CONV_HALO, :] = jnp.zeros((CONV_HALO, CONV_CH), F32)

    h = _rms(x_ref[...], g_ref[...]).astype(BF16)
    proj = lambda lo: jnp.dot(h, w_ref[:, lo:lo + MM_CHUNK], preferred_element_type=F32)
    for c in range(CONV_CH // MM_CHUNK):
        cols = slice(c * MM_CHUNK, (c + 1) * MM_CHUNK)
        ubuf_ref[CONV_HALO:CONV_HALO + CONV_TILE, cols] = (
            proj(c * MM_CHUNK) * _sigmoid(proj(CONV_CH + c * MM_CHUNK)))
    z0 = 2 * CONV_CH
    for c in range(SSD_INNER // MM_CHUNK):
        z_ref[:, c * MM_CHUNK:(c + 1) * MM_CHUNK] = proj(z0 + c * MM_CHUNK).astype(BF16)
    x0 = z0 + SSD_INNER
    for c in range(SSD_CONV_CH // MM_CHUNK):
        xbc_ref[:, c * MM_CHUNK:(c + 1) * MM_CHUNK] = proj(x0 + c * MM_CHUNK).astype(BF16)
    dt_ref[...] = jnp.dot(h, wdt_ref[...], preferred_element_type=F32)

    first = CONV_HALO - (CONV_K - 1)
    for rc in range(CONV_TILE // CONV_ROWS):
        r0 = rc * CONV_ROWS
        rows = slice(r0, r0 + CONV_ROWS)
        for cb in range(CONV_CH // LANES):
            cols = slice(cb * LANES, (cb + 1) * LANES)
            acc = None
            for res in range(SUBLANES):
                part = None
                for k in range(CONV_K):
                    if (first + k) % SUBLANES == res:
                        term = cw_ref[k:k + 1, cols] * ubuf_ref[r0 + first + k:r0 + first + k + CONV_ROWS, cols]
                        part = term if part is None else part + term
                acc = part if acc is None else acc + part
            acc_ref[rows, cols] = acc
        u = acc_ref[rows, :] + cb_ref[...]
        mu = jnp.mean(u, axis=-1, keepdims=True)
        uc = u - mu
        var = jnp.mean(uc * uc, axis=-1, keepdims=True)
        y = uc * lax.rsqrt(var + LN_EPS) * lg_ref[...] + lb_ref[...]
        u_ref[rows, :] = (y * _sigmoid(y)).astype(BF16)

    ubuf_ref[0:CONV_HALO, :] = ubuf_ref[CONV_TILE:CONV_TILE + CONV_HALO, :]


def _mix_in(x2d, bsz, seqlen, g_mix, w_in, conv_w, conv_b, ln_g, ln_b):
    t, d = x2d.shape
    nt = seqlen // CONV_TILE
    w_main = w_in[:, :N_MAIN].astype(BF16)
    w_dt = jnp.pad(w_in[:, N_MAIN:], ((0, 0), (0, LANES - SSD_HEADS))).astype(BF16)
    row = lambda v: v.reshape(1, -1)
    const = lambda shape: pl.BlockSpec(shape, lambda b, j: (0, 0))
    tile = lambda width: pl.BlockSpec((CONV_TILE, width), lambda b, j: (b * nt + j, 0))
    return pl.pallas_call(
        _mix_in_body,
        grid=(bsz, nt),
        in_specs=[
            tile(d), const((1, d)), const((d, N_MAIN)), const((d, LANES)),
            const((CONV_K + 1, CONV_CH)), const((1, CONV_CH)), const((1, CONV_CH)), const((1, CONV_CH)),
        ],
        out_specs=[tile(CONV_CH), tile(SSD_CONV_CH), tile(SSD_INNER), tile(LANES)],
        out_shape=[jax.ShapeDtypeStruct((t, CONV_CH), BF16), jax.ShapeDtypeStruct((t, SSD_CONV_CH), BF16),
                   jax.ShapeDtypeStruct((t, SSD_INNER), BF16), jax.ShapeDtypeStruct((t, LANES), F32)],
        scratch_shapes=[
            pltpu.VMEM((CONV_HALO + CONV_TILE, CONV_CH), F32),
            pltpu.VMEM((CONV_TILE, CONV_CH), F32),
        ],
        compiler_params=_cparams(("arbitrary", "arbitrary")),
        name="mix_in",
    )(x2d, row(g_mix), w_main, w_dt, jnp.pad(conv_w, ((0, 1), (0, 0))), row(conv_b), row(ln_g), row(ln_b))


def _ssd_body(xbc_ref, z_ref, dt_ref, cw_ref, cb_ref, dtb_ref, alog_ref, dsk_ref, ng_ref,
              expand_ref, y_ref, xbuf_ref, act_ref, dts_ref, state_ref):
    @pl.when(pl.program_id(1) == 0)
    def _():
        xbuf_ref[0:SSD_HALO, :] = jnp.zeros((SSD_HALO, SSD_CONV_CH), F32)
        state_ref[...] = jnp.zeros(state_ref.shape, F32)

    xbuf_ref[SSD_HALO:SSD_HALO + SSD_TILE, :] = xbc_ref[...].astype(F32)
    full = xbuf_ref[...]
    conv = cb_ref[...] + cw_ref[SSD_CONV_K - 1:SSD_CONV_K, :] * full[SSD_HALO:, :]
    for back in range(1, SSD_CONV_K):
        past = pltpu.roll(full, back, 0)[SSD_HALO:, :]
        conv = conv + cw_ref[SSD_CONV_K - 1 - back:SSD_CONV_K - back, :] * past
    act_ref[...] = conv * _sigmoid(conv)
    xbuf_ref[0:SSD_HALO, :] = xbuf_ref[SSD_TILE:SSD_TILE + SSD_HALO, :]

    dt_in = dt_ref[...] + dtb_ref[...]
    dts_ref[...] = jnp.maximum(dt_in, 0.0) + jnp.log1p(jnp.exp(-jnp.abs(dt_in)))

    a_neg = -jnp.exp(alog_ref[...])
    q = SSD_CHUNK
    row_i = lax.broadcasted_iota(jnp.int32, (q, q), 0)
    col_i = lax.broadcasted_iota(jnp.int32, (q, q), 1)
    causal = row_i >= col_i
    tril = causal.astype(F32)
    lane_i = lax.broadcasted_iota(jnp.int32, (q, LANES), 1)
    low_half = lane_i < SSD_HEAD_DIM
    expand = expand_ref[...]

    def chunk(c, carry):
        r0 = pl.multiple_of(c * q, q)
        rows = pl.ds(r0, q)
        dtc = dts_ref[rows, :]
        a_cs = jnp.dot(tril, dtc * a_neg, preferred_element_type=F32,
                       precision=lax.Precision.HIGHEST)
        a_cs_t = a_cs.T
        dt_t = dtc.T
        a_end = a_cs[q - 1:q, :]
        e_exp = jnp.dot(jnp.exp(a_cs).astype(BF16), expand, preferred_element_type=F32)
        w_exp = jnp.dot((jnp.exp(a_end - a_cs) * dtc).astype(BF16), expand,
                        preferred_element_type=F32)
        dec_row = e_exp[q - 1:q, :]
        xc = act_ref[rows, 0:SSD_INNER]
        xw = (xc * w_exp).astype(BF16)
        y_parts = []
        for g in range(SSD_GROUPS):
            b_f = act_ref[rows, SSD_INNER + g * SSD_STATE:SSD_INNER + (g + 1) * SSD_STATE]
            c_f = act_ref[rows, SSD_INNER + SSD_BC + g * SSD_STATE:
                          SSD_INNER + SSD_BC + (g + 1) * SSD_STATE]
            b_g = b_f.astype(BF16)
            c_g = c_f.astype(BF16)
            cb = lax.dot_general(c_g, b_g, (((1,), (1,)), ((), ())),
                                 preferred_element_type=F32)
            gcols = slice(g * SSD_GROUP_W, (g + 1) * SSD_GROUP_W)
            st = state_ref[g]
            y_off = jnp.dot(c_g, st.astype(BF16), preferred_element_type=F32)
            state_ref[g] = st * dec_row[:, gcols] + jnp.dot(
                b_f.T.astype(BF16), xw[:, gcols], preferred_element_type=F32)
            for pair in range(SSD_GROUP_W // LANES):
                ms = []
                for hh in range(2):
                    h = g * (SSD_HEADS // SSD_GROUPS) + 2 * pair + hh
                    seg = a_cs[:, h:h + 1] - a_cs_t[h:h + 1, :]
                    dec = jnp.exp(jnp.where(causal, seg, -jnp.inf))
                    ms.append((cb * dec * dt_t[h:h + 1, :]).astype(BF16))
                lhs = jnp.concatenate(ms, axis=1)
                xp = xc[:, g * SSD_GROUP_W + pair * LANES:g * SSD_GROUP_W + (pair + 1) * LANES]
                rhs = jnp.concatenate([jnp.where(low_half, xp, 0.0),
                                       jnp.where(low_half, 0.0, xp)], axis=0).astype(BF16)
                y_diag = jnp.dot(lhs, rhs, preferred_element_type=F32)
                lo = pair * LANES
                y_parts.append(y_diag + y_off[:, lo:lo + LANES]
                               * e_exp[:, g * SSD_GROUP_W + lo:g * SSD_GROUP_W + lo + LANES])
        y = jnp.concatenate(y_parts, axis=1) + xc * dsk_ref[...]
        z = z_ref[rows, :].astype(F32)
        y = y * (z * _sigmoid(z))
        outs = []
        for g in range(SSD_GROUPS):
            yg = y[:, g * SSD_GROUP_W:(g + 1) * SSD_GROUP_W]
            outs.append(yg * lax.rsqrt(jnp.mean(yg * yg, axis=-1, keepdims=True) + RMS_EPS))
        y_ref[rows, :] = (jnp.concatenate(outs, axis=1) * ng_ref[...]).astype(BF16)
        return carry

    lax.fori_loop(0, SSD_TILE // q, chunk, 0, unroll=SSD_UNROLL)


def _ssd(xbc, z, dt_raw, bsz, seqlen, ssd_conv_w, ssd_conv_b, dt_bias, a_log, d_skip, ssd_norm_g):
    nt = seqlen // SSD_TILE
    pad_h = lambda v: jnp.pad(v, (0, LANES - SSD_HEADS)).reshape(1, LANES)
    expand = (jnp.arange(LANES)[:, None] == (jnp.arange(SSD_INNER) // SSD_HEAD_DIM)[None, :]).astype(BF16)
    const = lambda shape: pl.BlockSpec(shape, lambda b, j: (0, 0))
    return pl.pallas_call(
        _ssd_body,
        grid=(bsz, nt),
        in_specs=[
            pl.BlockSpec((SSD_TILE, SSD_CONV_CH), lambda b, j: (b * nt + j, 0)),
            pl.BlockSpec((SSD_TILE, SSD_INNER), lambda b, j: (b * nt + j, 0)),
            pl.BlockSpec((SSD_TILE, LANES), lambda b, j: (b * nt + j, 0)),
            const((SSD_CONV_K, SSD_CONV_CH)),
            const((1, SSD_CONV_CH)),
            const((1, LANES)), const((1, LANES)),
            const((1, SSD_INNER)), const((1, SSD_INNER)),
            const((LANES, SSD_INNER)),
        ],
        out_specs=pl.BlockSpec((SSD_TILE, SSD_INNER), lambda b, j: (b * nt + j, 0)),
        out_shape=jax.ShapeDtypeStruct((bsz * seqlen, SSD_INNER), BF16),
        scratch_shapes=[
            pltpu.VMEM((SSD_HALO + SSD_TILE, SSD_CONV_CH), F32),
            pltpu.VMEM((SSD_TILE, SSD_CONV_CH), F32),
            pltpu.VMEM((SSD_TILE, LANES), F32),
            pltpu.VMEM((SSD_GROUPS, SSD_STATE, SSD_GROUP_W), F32),
        ],
        compiler_params=_cparams(("arbitrary", "arbitrary")),
        name="ssd",
    )(xbc, z, dt_raw, ssd_conv_w, ssd_conv_b.reshape(1, SSD_CONV_CH), pad_h(dt_bias),
      pad_h(a_log), jnp.repeat(d_skip, SSD_HEAD_DIM).reshape(1, SSD_INNER),
      ssd_norm_g.reshape(1, SSD_INNER), expand)


def _xattn_body(x_ref, u_ref, y_ref, wu_ref, wy_ref, k_ref, v_ref, gx_ref, wq_ref, wo_ref, gm_ref,
                wr_hi_ref, wr_lo_ref, br_ref, x2_ref, h2_ref, route_ref, stats_ref):
    def tile(s):
        rows = slice(s * TOK_TILE, (s + 1) * TOK_TILE)
        return _xattn_tile(x_ref[rows, :], u_ref[rows, :], y_ref[rows, :], wu_ref, wy_ref, k_ref, v_ref,
                           gx_ref, wq_ref, wo_ref, gm_ref, wr_hi_ref, wr_lo_ref, br_ref)

    results = _interleave([tile(s) for s in range(XA_TILES)])
    for s in range(XA_TILES):
        rows = slice(s * TOK_TILE, (s + 1) * TOK_TILE)
        x2_ref[rows, :], h2_ref[rows, :], route_ref[rows, :], stats_ref[s] = results[s]


def _xattn_tile(x, u, y, wu_ref, wy_ref, k_ref, v_ref, gx_ref, wq_ref, wo_ref, gm_ref, wr_hi_ref,
                wr_lo_ref, br_ref):
    x = (x + jnp.dot(u, wu_ref[...], preferred_element_type=F32)
         + jnp.dot(y, wy_ref[...], preferred_element_type=F32))
    yield
    h = _rms(x, gx_ref[...]).astype(BF16)
    q = (jnp.dot(h, wq_ref[...], preferred_element_type=F32) * (XA_HEAD_DIM ** -0.5)).astype(BF16)
    yield
    heads = []
    for i in range(XA_HEADS):
        cols = slice(i * XA_HEAD_DIM, (i + 1) * XA_HEAD_DIM)
        s = jnp.dot(q[:, cols], k_ref[0, cols, :], preferred_element_type=F32)
        p = jnp.exp(s - jnp.max(s, axis=-1, keepdims=True))
        p = p / jnp.sum(p, axis=-1, keepdims=True)
        heads.append(jnp.dot(p.astype(BF16), v_ref[0, :, cols], preferred_element_type=F32))
        yield
    o = jnp.concatenate(heads, axis=1).astype(BF16)
    x2 = x + jnp.dot(o, wo_ref[...], preferred_element_type=F32)
    yield

    h2 = _rms(x2, gm_ref[...])
    h_hi = h2.astype(BF16)
    h_lo = (h2 - h_hi.astype(F32)).astype(BF16)
    logits = (jnp.dot(h_hi, wr_hi_ref[...], preferred_element_type=F32)
              + jnp.dot(h_lo, wr_hi_ref[...], preferred_element_type=F32)
              + jnp.dot(h_hi, wr_lo_ref[...], preferred_element_type=F32)) + br_ref[...]
    yield
    lane = lax.broadcasted_iota(jnp.int32, logits.shape, 1)
    neg = -jnp.inf

    def first_argmax(v):
        m = jnp.max(v, axis=-1, keepdims=True)
        return m, jnp.min(jnp.where(v == m, lane, ROUTE_LANES), axis=-1, keepdims=True)

    gl = jnp.where(lane < N_GROUPS, logits, neg)
    g_max, g_sel = first_argmax(gl)
    p_top = 1.0 / jnp.sum(jnp.exp(gl - g_max), axis=-1, keepdims=True)
    e_lo = N_GROUPS + EXPERTS_PER_GROUP * g_sel
    el = jnp.where((lane >= e_lo) & (lane < e_lo + EXPERTS_PER_GROUP), logits, neg)
    m1, i1 = first_argmax(el)
    m2, i2 = first_argmax(jnp.where(lane == i1, neg, el))
    r = jnp.exp(m2 - m1)
    w1 = p_top / (1.0 + r)
    w2 = w1 * r
    yield
    e1 = i1 - N_GROUPS
    e2 = i2 - N_GROUPS
    oh1 = (lane == e1).astype(BF16)
    oh2 = (lane == e2).astype(BF16)
    n_t = logits.shape[0]
    before = (lax.broadcasted_iota(jnp.int32, (n_t, n_t), 0)
              > lax.broadcasted_iota(jnp.int32, (n_t, n_t), 1)).astype(BF16)
    c1 = jnp.dot(before, oh1, preferred_element_type=F32)
    c2 = jnp.dot(before, oh2, preferred_element_type=F32)
    tot1 = jnp.sum(oh1.astype(F32), axis=0, keepdims=True)
    cnt = tot1 + jnp.sum(oh2.astype(F32), axis=0, keepdims=True)
    cnt = jnp.floor((cnt + (SUBLANES - 1)) * (1.0 / SUBLANES)) * SUBLANES
    lanes_before = (lax.broadcasted_iota(jnp.int32, (ROUTE_LANES, ROUTE_LANES), 0)
                    < lax.broadcasted_iota(jnp.int32, (ROUTE_LANES, ROUTE_LANES), 1)).astype(F32)
    loff = jnp.dot(jnp.broadcast_to(cnt, (SUBLANES, ROUTE_LANES)), lanes_before,
                   preferred_element_type=F32, precision=lax.Precision.HIGHEST)[0:1, :]
    lp1 = jnp.sum(oh1.astype(F32) * (loff + c1), axis=-1, keepdims=True)
    lp2 = jnp.sum(oh2.astype(F32) * (loff + tot1 + c2), axis=-1, keepdims=True)
    route = jnp.where(lane == 0, e1.astype(F32),
            jnp.where(lane == 1, e2.astype(F32),
            jnp.where(lane == 2, w1,
            jnp.where(lane == 3, w2,
            jnp.where(lane == 4, lp1, jnp.where(lane == 5, lp2, 0.0))))))
    row = lax.broadcasted_iota(jnp.int32, (SUBLANES, ROUTE_LANES), 0)
    return x2, h_hi, route, jnp.where(row == 0, cnt, jnp.where(row == 1, loff, 0.0))


def _xattn_route(x, u, y, w_out, k, v, bsz, seqlen, g_xattn, w_q, w_o, g_moe, w_rg, b_rg, w_re, b_re):
    t, d = x.shape
    w_mix = w_out.astype(BF16)
    w_r = jnp.pad(jnp.concatenate([w_rg, w_re], axis=1), ((0, 0), (0, ROUTE_LANES - N_GROUPS - N_EXPERTS)))
    b_r = jnp.pad(jnp.concatenate([b_rg, b_re]), (0, ROUTE_LANES - N_GROUPS - N_EXPERTS)).reshape(1, ROUTE_LANES)
    wr_hi = w_r.astype(BF16)
    wr_lo = (w_r - wr_hi.astype(F32)).astype(BF16)
    step_rows = XA_TILES * TOK_TILE
    nt = seqlen // step_rows
    tile = pl.BlockSpec((step_rows, d), lambda b, j: (b * nt + j, 0))
    kv_spec = pl.BlockSpec((1, MEM_LEN, d), lambda b, j: (b, 0, 0))
    const = lambda shape: pl.BlockSpec(shape, lambda b, j: (0, 0))
    return pl.pallas_call(
        _xattn_body,
        grid=(bsz, nt),
        in_specs=[tile, tile, tile, const((d, d)), const((d, d)),
                  pl.BlockSpec((1, d, MEM_LEN), lambda b, j: (b, 0, 0)), kv_spec, const((1, d)),
                  const((d, d)), const((d, d)), const((1, d)),
                  const((d, ROUTE_LANES)), const((d, ROUTE_LANES)), const((1, ROUTE_LANES))],
        out_specs=[tile, tile, pl.BlockSpec((step_rows, ROUTE_LANES), lambda b, j: (b * nt + j, 0)),
                   pl.BlockSpec((XA_TILES, SUBLANES, ROUTE_LANES), lambda b, j: (b * nt + j, 0, 0))],
        out_shape=[jax.ShapeDtypeStruct((t, d), F32), jax.ShapeDtypeStruct((t, d), BF16),
                   jax.ShapeDtypeStruct((t, ROUTE_LANES), F32),
                   jax.ShapeDtypeStruct((t // TOK_TILE, SUBLANES, ROUTE_LANES), F32)],
        compiler_params=_cparams(("arbitrary", "arbitrary")),
        name="xattn_route",
    )(x, u, y, w_mix[:CONV_CH], w_mix[CONV_CH:], jnp.swapaxes(k, 1, 2), v, g_xattn.reshape(1, d), w_q.astype(BF16), w_o.astype(BF16), g_moe.reshape(1, d),
      wr_hi, wr_lo, b_r)


def _rows_copy(src_ref, src0, dst_ref, dst0, n, sem):
    rows = lambda r0: pl.ds(pl.multiple_of(r0, SUBLANES), pl.multiple_of(n, SUBLANES))
    return pltpu.make_async_copy(src_ref.at[rows(src0), :], dst_ref.at[rows(dst0), :], sem)


def _pack_pairs(v):
    half = v.shape[1] // 2
    lo = lax.bitcast_convert_type(v[:, :half], jnp.uint32)
    hi = lax.bitcast_convert_type(v[:, half:], jnp.uint32)
    return (lo >> 16) | (hi & jnp.uint32(0xFFFF0000))


def _unpack_pairs(w):
    lo = lax.bitcast_convert_type(w << 16, F32)
    hi = lax.bitcast_convert_type(w & jnp.uint32(0xFFFF0000), F32)
    return jnp.concatenate([lo, hi], axis=1).astype(BF16)


def _dispatch_body(cnt_ref, loff_ref, gb_ref, tot_ref, fstart_ref, fcnt_ref, nu_ref, h_ref, route_ref,
                   xs_ref, buf_ref, zero_ref, sem, zsem):
    i = pl.program_id(0)
    nt = pl.num_programs(0)
    slot = i % 2
    buf = buf_ref.at[slot]

    def drain(step, s):
        n = sum(tot_ref[step * MOE_TILES + k] for k in range(MOE_TILES))
        _rows_copy(buf_ref.at[s], 0, xs_ref, 0, n, sem.at[s]).wait()

    @pl.when(i >= 2)
    def _():
        drain(i - 2, slot)

    def sort_tile(k):
        rows = slice(k * TOK_TILE, (k + 1) * TOK_TILE)
        rt = route_ref[rows, :].T
        lp1 = rt[4:5, :].astype(jnp.int32)
        lp2 = rt[5:6, :].astype(jnp.int32)
        r_i = lax.broadcasted_iota(jnp.int32, (SORT_ROWS, TOK_TILE), 0)
        perm = jnp.where((r_i == lp1) | (r_i == lp2), 1.0, 0.0).astype(BF16)
        yield
        buf[k * SORT_ROWS:(k + 1) * SORT_ROWS, :] = _pack_pairs(
            jnp.dot(perm, h_ref[rows, :], preferred_element_type=F32))

    _interleave([sort_tile(k) for k in range(MOE_TILES)])

    for k in range(MOE_TILES):
        def per_expert(e, c, k=k):
            j = (i * MOE_TILES + k) * N_EXPERTS + e

            @pl.when(cnt_ref[j] > 0)
            def _():
                _rows_copy(buf, k * SORT_ROWS + loff_ref[j], xs_ref, gb_ref[j], cnt_ref[j],
                           sem.at[slot]).start()
            return c

        lax.fori_loop(0, N_EXPERTS, per_expert, 0)

    @pl.when(i == 0)
    def _():
        zero_ref[...] = jnp.zeros(zero_ref.shape, zero_ref.dtype)

        def fill(start):
            def body(e, c):
                @pl.when(fcnt_ref[e] > 0)
                def _():
                    copy = _rows_copy(zero_ref, 0, xs_ref, fstart_ref[e], fcnt_ref[e], zsem)
                    copy.start() if start else copy.wait()
                return c
            lax.fori_loop(0, N_EXPERTS, body, 0)

            def tail(blk, c):
                copy = _rows_copy(zero_ref, 0, xs_ref, blk * MOE_BLOCK, MOE_BLOCK, zsem)
                copy.start() if start else copy.wait()
                return c
            lax.fori_loop(nu_ref[0], xs_ref.shape[0] // MOE_BLOCK, tail, 0)

        fill(True)
        fill(False)

    @pl.when(i == nt - 1)
    def _():
        drain(i, slot)

        @pl.when(i >= 1)
        def _():
            drain(i - 1, 1 - slot)


def _dispatch(h2, route, tables, n_used, n_slots):
    t, d = h2.shape
    step_rows = MOE_TILES * TOK_TILE
    return pl.pallas_call(
        _dispatch_body,
        grid_spec=pltpu.PrefetchScalarGridSpec(
            num_scalar_prefetch=7,
            grid=(t // step_rows,),
            in_specs=[
                pl.BlockSpec((step_rows, d), lambda i, *_: (i, 0)),
                pl.BlockSpec((step_rows, ROUTE_LANES), lambda i, *_: (i, 0)),
            ],
            out_specs=pl.BlockSpec(memory_space=pl.ANY),
            scratch_shapes=[pltpu.VMEM((2, MOE_TILES * SORT_ROWS, d // 2), jnp.uint32),
                            pltpu.VMEM((MOE_BLOCK, d // 2), jnp.uint32),
                            pltpu.SemaphoreType.DMA((2,)), pltpu.SemaphoreType.DMA(())],
        ),
        out_shape=jax.ShapeDtypeStruct((n_slots, d // 2), jnp.uint32),
        compiler_params=_cparams(("arbitrary",)),
        name="dispatch",
    )(tables["cnt"], tables["loff"], tables["gb"], tables["tot"], tables["fill_start"],
      tables["fill_cnt"], n_used, h2, route)


def _experts_body(be_ref, nu_ref, first_ref, slot_ref, nxt_ref, x_ref, wg_hbm, wu_hbm, wd_hbm, y_ref,
                  wg_f, wu_f, wd_f, wg_bf, wu_bf, wd_bf, sem):
    b = pl.program_id(0)
    used = b < nu_ref[0]

    def weights(e, s):
        return [pltpu.make_async_copy(hbm.at[e], buf.at[s], sem.at[s])
                for hbm, buf in ((wg_hbm, wg_f), (wu_hbm, wu_f), (wd_hbm, wd_f))]

    @pl.when(used & (first_ref[b] == 1))
    def _():
        s = slot_ref[b]

        @pl.when(b == 0)
        def _():
            for copy in weights(be_ref[0], 0):
                copy.start()

        for copy in weights(be_ref[b], s):
            copy.wait()
        wg_bf[...] = wg_f[s].astype(BF16)
        wu_bf[...] = wu_f[s].astype(BF16)
        wd_bf[...] = wd_f[s].astype(BF16)

        @pl.when(nxt_ref[b] >= 0)
        def _():
            for copy in weights(nxt_ref[b], 1 - s):
                copy.start()

    @pl.when(used)
    def _():
        x = _unpack_pairs(x_ref[...])
        g = jnp.dot(x, wg_bf[...], preferred_element_type=F32)
        u = jnp.dot(x, wu_bf[...], preferred_element_type=F32)
        a = (g * _sigmoid(g) * u).astype(BF16)
        y = jnp.dot(a, wd_bf[...], preferred_element_type=F32)
        y_ref[...] = _pack_pairs(y.astype(BF16).astype(F32))

    @pl.when(jnp.logical_not(used))
    def _():
        y_ref[...] = jnp.zeros(y_ref.shape, y_ref.dtype)


def _experts(xs, etab, w_gate, w_up, w_down):
    n_slots, dp = xs.shape
    d = w_gate.shape[1]
    n_blocks = n_slots // MOE_BLOCK
    last = lambda b, nu: jnp.maximum(jnp.minimum(b, nu[0] - 1), 0)
    hbm = pl.BlockSpec(memory_space=pl.ANY)
    return pl.pallas_call(
        _experts_body,
        grid_spec=pltpu.PrefetchScalarGridSpec(
            num_scalar_prefetch=5,
            grid=(n_blocks,),
            in_specs=[pl.BlockSpec((MOE_BLOCK, dp), lambda b, be, nu, *_: (last(b, nu), 0)), hbm, hbm, hbm],
            out_specs=pl.BlockSpec((MOE_BLOCK, dp), lambda b, *_: (b, 0)),
            scratch_shapes=[pltpu.VMEM((2, d, D_EXPERT), F32), pltpu.VMEM((2, d, D_EXPERT), F32),
                            pltpu.VMEM((2, D_EXPERT, d), F32),
                            pltpu.VMEM((d, D_EXPERT), BF16), pltpu.VMEM((d, D_EXPERT), BF16),
                            pltpu.VMEM((D_EXPERT, d), BF16), pltpu.SemaphoreType.DMA((2,))],
        ),
        out_shape=jax.ShapeDtypeStruct((n_slots, dp), jnp.uint32),
        compiler_params=_cparams(("arbitrary",)),
        name="experts",
    )(etab["block_e"], etab["n_used"], etab["first"], etab["slot"], etab["nxt"], xs, w_gate, w_up, w_down)


def _combine_body(cnt_ref, loff_ref, gb_ref, tot_ref, x_ref, route_ref, g_ref, ys_ref, o_ref, ybuf_ref, sem):
    i = pl.program_id(0)
    nt = pl.num_programs(0)
    slot = i % 2

    def gather(step, s):
        for k in range(MOE_TILES):
            def per_expert(e, c, k=k):
                j = (step * MOE_TILES + k) * N_EXPERTS + e

                @pl.when(cnt_ref[j] > 0)
                def _():
                    _rows_copy(ys_ref, gb_ref[j], ybuf_ref.at[s], k * SORT_ROWS + loff_ref[j], cnt_ref[j],
                               sem.at[s]).start()
                return c

            lax.fori_loop(0, N_EXPERTS, per_expert, 0)

    @pl.when(i == 0)
    def _():
        ybuf_ref[...] = jnp.zeros(ybuf_ref.shape, ybuf_ref.dtype)
        gather(0, 0)

    @pl.when(i + 1 < nt)
    def _():
        gather(i + 1, 1 - slot)

    def combine_tile(k):
        rows = slice(k * TOK_TILE, (k + 1) * TOK_TILE)
        route = route_ref[rows, :]
        c_i = lax.broadcasted_iota(jnp.int32, (TOK_TILE, SORT_ROWS), 1)
        lp1 = route[:, 4:5].astype(jnp.int32)
        lp2 = route[:, 5:6].astype(jnp.int32)
        pw = jnp.where(c_i == lp1, route[:, 2:3], jnp.where(c_i == lp2, route[:, 3:4], 0.0)).astype(BF16)
        yield
        r_i = lax.broadcasted_iota(jnp.int32, (SORT_ROWS, 1), 0)
        y = _unpack_pairs(jnp.where(r_i < tot_ref[i * MOE_TILES + k],
                                    ybuf_ref[slot, k * SORT_ROWS:(k + 1) * SORT_ROWS, :], jnp.uint32(0)))
        moe = jnp.dot(pw, y, preferred_element_type=F32)
        yield
        o_ref[rows, :] = _rms(x_ref[rows, :] + moe, g_ref[...])

    tiles = [combine_tile(k) for k in range(MOE_TILES)]
    for tile in tiles:
        next(tile)
    n = sum(tot_ref[i * MOE_TILES + k] for k in range(MOE_TILES))
    _rows_copy(ys_ref, 0, ybuf_ref.at[slot], 0, n, sem.at[slot]).wait()
    _interleave(tiles)


def _combine(x2, route, tables, ys, g_final):
    t, d = x2.shape
    step_rows = MOE_TILES * TOK_TILE
    tile = pl.BlockSpec((step_rows, d), lambda i, *_: (i, 0))
    return pl.pallas_call(
        _combine_body,
        grid_spec=pltpu.PrefetchScalarGridSpec(
            num_scalar_prefetch=4,
            grid=(t // step_rows,),
            in_specs=[
                tile,
                pl.BlockSpec((step_rows, ROUTE_LANES), lambda i, *_: (i, 0)),
                pl.BlockSpec((1, d), lambda i, *_: (0, 0)),
                pl.BlockSpec(memory_space=pl.ANY),
            ],
            out_specs=tile,
            scratch_shapes=[pltpu.VMEM((2, MOE_TILES * SORT_ROWS, d // 2), jnp.uint32),
                            pltpu.SemaphoreType.DMA((2,))],
        ),
        out_shape=jax.ShapeDtypeStruct((t, d), F32),
        compiler_params=_cparams(("arbitrary",)),
        name="combine",
    )(tables["cnt"], tables["loff"], tables["gb"], tables["tot"], x2, route, g_final.reshape(1, d), ys)


def _routing_tables(stats, n_tok):
    cnt = stats[:, 0, :N_EXPERTS].astype(jnp.int32)
    loff = stats[:, 1, :N_EXPERTS].astype(jnp.int32)
    n_tiles = cnt.shape[0]
    counts = jnp.sum(cnt, axis=0)
    padded = ((counts + MOE_BLOCK - 1) // MOE_BLOCK) * MOE_BLOCK
    pad_end = jnp.cumsum(padded)
    pad_start = pad_end - padded
    gb = pad_start[None, :] + jnp.cumsum(cnt, axis=0) - cnt
    max_rows = (n_tok * TOP_K + n_tiles * N_EXPERTS * (SUBLANES - 1)
                + N_EXPERTS * (MOE_BLOCK - SUBLANES))
    n_blocks = -(-max_rows // MOE_BLOCK)
    n_slots = n_blocks * MOE_BLOCK
    block_start = jnp.arange(n_blocks, dtype=jnp.int32) * MOE_BLOCK
    block_e = jnp.minimum(jnp.sum((pad_end[None, :] <= block_start[:, None]).astype(jnp.int32), axis=1),
                          N_EXPERTS - 1)
    tables = dict(cnt=cnt.reshape(-1), loff=loff.reshape(-1), gb=gb.reshape(-1).astype(jnp.int32),
                  tot=jnp.sum(cnt, axis=1).astype(jnp.int32),
                  fill_start=(pad_start + counts).astype(jnp.int32),
                  fill_cnt=(padded - counts).astype(jnp.int32))
    n_used = (pad_end[-1:] // MOE_BLOCK).astype(jnp.int32)
    first = jnp.concatenate([jnp.ones((1,), jnp.int32), (block_e[1:] != block_e[:-1]).astype(jnp.int32)])
    seg_slot = (jnp.cumsum(first) - 1) % 2
    experts = jnp.arange(N_EXPERTS, dtype=jnp.int32)
    later = jnp.where((experts[None, :] > experts[:, None]) & (padded[None, :] > 0), experts[None, :], N_EXPERTS)
    next_e = jnp.min(later, axis=1)
    nxt = jnp.where(next_e < N_EXPERTS, next_e, -1)[block_e]
    etab = dict(block_e=block_e, n_used=n_used, first=first, slot=seg_slot.astype(jnp.int32),
                nxt=nxt.astype(jnp.int32))
    return tables, etab, n_slots


def kernel(x, mem, g_mix, w_in, conv_w, conv_b, ln_g, ln_b, ssd_conv_w, ssd_conv_b, dt_bias, a_log, d_skip, ssd_norm_g, w_out, g_xattn, g_mem, w_q, w_k, w_v, w_o, g_moe, w_router_group, b_router_group, w_router_expert, b_router_expert, w_gate, w_up, w_down, g_final):
    bsz, seqlen, d = x.shape
    n_tok = bsz * seqlen
    xt = x.reshape(n_tok, d)
    assert g_mix.shape[0] == 1, "the combine kernel applies the final norm: single layer only"
    for l in range(1):
        u, xbc, z, dt_raw = _mix_in(xt, bsz, seqlen, g_mix[l], w_in[l], conv_w[l], conv_b[l],
                                    ln_g[l], ln_b[l])
        y = _ssd(xbc, z, dt_raw, bsz, seqlen, ssd_conv_w[l], ssd_conv_b[l], dt_bias[l], a_log[l],
                 d_skip[l], ssd_norm_g[l])
        k, v = _kv_proj(mem, g_mem[l], w_k[l], w_v[l])
        x2, h2, route, stats = _xattn_route(xt, u, y, w_out[l], k, v, bsz, seqlen, g_xattn[l], w_q[l],
                                            w_o[l], g_moe[l], w_router_group[l], b_router_group[l],
                                            w_router_expert[l], b_router_expert[l])
        tables, etab, n_slots = _routing_tables(stats, n_tok)
        xs = _dispatch(h2, route, tables, etab["n_used"], n_slots)
        ys = _experts(xs, etab, w_gate[l], w_up[l], w_down[l])
        xt = _combine(x2, route, tables, ys, g_final)
    return xt.reshape(bsz, seqlen, d)
```

```python
import functools

import jax
import jax.numpy as jnp
from jax import lax
from jax.experimental import pallas as pl
from jax.experimental.pallas import tpu as pltpu

F32 = jnp.float32
BF16 = jnp.bfloat16

D_MODEL = 1024
CONV_CH = 1024
CONV_K = 31
SSD_INNER = 1024
SSD_HEAD_DIM = 64
SSD_HEADS = 16
SSD_STATE = 128
SSD_GROUPS = 2
SSD_GROUP_W = SSD_INNER // SSD_GROUPS
SSD_CONV_K = 4
SSD_CHUNK = 128
SSD_BC = SSD_GROUPS * SSD_STATE
SSD_CONV_CH = SSD_INNER + 2 * SSD_BC
N_MAIN = 2 * CONV_CH + 2 * SSD_INNER + 2 * SSD_BC
XA_HEADS = 4
XA_HEAD_DIM = 256
MEM_LEN = 256
N_GROUPS = 4
EXPERTS_PER_GROUP = 8
N_EXPERTS = 32
TOP_K = 2
D_EXPERT = 512
MOE_BLOCK = 512
RMS_EPS = 1e-6
LN_EPS = 1e-5
LOG2_E = 1.4426950408889634

LANES = 128
SUBLANES = 8
VMEM_LIMIT = 56 * 1024 * 1024

TOK_TILE = 512
CONV_TILE = 512
MM_CHUNK = 512
CONV_HALO = 32
CONV_ROWS = 128
SSD_TILE = 512
MOE_TILES = 2
XA_TILES = 2
SSD_UNROLL = 4
SSD_HALO = 8
ROUTE_LANES = 128
SORT_ROWS = TOP_K * TOK_TILE + N_EXPERTS * SUBLANES


def _cparams(sem):
    return pltpu.CompilerParams(dimension_semantics=sem, vmem_limit_bytes=VMEM_LIMIT)


def _rms(x, g):
    return x * lax.rsqrt(jnp.mean(x * x, axis=-1, keepdims=True) + RMS_EPS) * g


def _sigmoid(x):
    return 1.0 / (1.0 + jnp.exp2(x * (-LOG2_E)))


def _interleave(stages):
    results = {}
    while len(results) < len(stages):
        for k, item in enumerate(stages):
            if k not in results:
                try:
                    next(item)
                except StopIteration as done:
                    results[k] = done.value
    return [results[k] for k in range(len(stages))]


def _kv_body(m_ref, g_ref, wk_ref, wv_ref, k_ref, v_ref):
    m = _rms(m_ref[0], g_ref[...]).astype(BF16)
    k_ref[0] = jnp.dot(m, wk_ref[...], preferred_element_type=F32).astype(BF16)
    v_ref[0] = jnp.dot(m, wv_ref[...], preferred_element_type=F32).astype(BF16)


def _kv_proj(mem, g_mem, w_k, w_v):
    b, s, d = mem.shape
    w_spec = pl.BlockSpec((d, d), lambda i: (0, 0))
    kv_spec = pl.BlockSpec((1, s, d), lambda i: (i, 0, 0))
    return pl.pallas_call(
        _kv_body,
        grid=(b,),
        in_specs=[kv_spec, pl.BlockSpec((1, d), lambda i: (0, 0)), w_spec, w_spec],
        out_specs=[kv_spec, kv_spec],
        out_shape=[jax.ShapeDtypeStruct((b, s, d), BF16)] * 2,
        compiler_params=_cparams(("arbitrary",)),
        name="kv_proj",
    )(mem, g_mem.reshape(1, d), w_k.astype(BF16), w_v.astype(BF16))


def _mix_in_body(x_ref, g_ref, w_ref, wdt_ref, cw_ref, cb_ref, lg_ref, lb_ref,
                 u_ref, xbc_ref, z_ref, dt_ref, ubuf_ref, acc_ref):
    @pl.when(pl.program_id(1) == 0)
    def _():
        ubuf_ref[0:CONV_HALO, :] = jnp.zeros((CONV_HALO, CONV_CH), F32)

    h = _rms(x_ref[...], g_ref[...]).astype(BF16)
    proj = lambda lo: jnp.dot(h, w_ref[:, lo:lo + MM_CHUNK], preferred_element_type=F32)
    for c in range(CONV_CH // MM_CHUNK):
        cols = slice(c * MM_CHUNK, (c + 1) * MM_CHUNK)
        ubuf_ref[CONV_HALO:CONV_HALO + CONV_TILE, cols] = (
            proj(c * MM_CHUNK) * _sigmoid(proj(CONV_CH + c * MM_CHUNK)))
    z0 = 2 * CONV_CH
    for c in range(SSD_INNER // MM_CHUNK):
        z_ref[:, c * MM_CHUNK:(c + 1) * MM_CHUNK] = proj(z0 + c * MM_CHUNK).astype(BF16)
    x0 = z0 + SSD_INNER
    for c in range(SSD_CONV_CH // MM_CHUNK):
        xbc_ref[:, c * MM_CHUNK:(c + 1) * MM_CHUNK] = proj(x0 + c * MM_CHUNK).astype(BF16)
    dt_ref[...] = jnp.dot(h, wdt_ref[...], preferred_element_type=F32)

    first = CONV_HALO - (CONV_K - 1)
    for rc in range(CONV_TILE // CONV_ROWS):
        r0 = rc * CONV_ROWS
        rows = slice(r0, r0 + CONV_ROWS)
        for cb in range(CONV_CH // LANES):
            cols = slice(cb * LANES, (cb + 1) * LANES)
            acc = None
            for res in range(SUBLANES):
                part = None
                for k in range(CONV_K):
                    if (first + k) % SUBLANES == res:
                        term = cw_ref[k:k + 1, cols] * ubuf_ref[r0 + first + k:r0 + first + k + CONV_ROWS, cols]
                        part = term if part is None else part + term
                acc = part if acc is None else acc + part
            acc_ref[rows, cols] = acc
        u = acc_ref[rows, :] + cb_ref[...]
        mu = jnp.mean(u, axis=-1, keepdims=True)
        uc = u - mu
        var = jnp.mean(uc * uc, axis=-1, keepdims=True)
        y = uc * lax.rsqrt(var + LN_EPS) * lg_ref[...] + lb_ref[...]
        u_ref[rows, :] = (y * _sigmoid(y)).astype(BF16)

    ubuf_ref[0:CONV_HALO, :] = ubuf_ref[CONV_TILE:CONV_TILE + CONV_HALO, :]


def _mix_in(x2d, bsz, seqlen, g_mix, w_in, conv_w, conv_b, ln_g, ln_b):
    t, d = x2d.shape
    nt = seqlen // CONV_TILE
    w_main = w_in.astype(BF16)
    w_dt = jnp.pad(w_in[:, N_MAIN:], ((0, 0), (0, LANES - SSD_HEADS))).astype(BF16)
    row = lambda v: v.reshape(1, -1)
    const = lambda shape: pl.BlockSpec(shape, lambda b, j: (0, 0))
    tile = lambda width: pl.BlockSpec((CONV_TILE, width), lambda b, j: (b * nt + j, 0))
    return pl.pallas_call(
        _mix_in_body,
        grid=(bsz, nt),
        in_specs=[
            tile(d), const((1, d)), const(w_in.shape), const((d, LANES)),
            const((CONV_K + 1, CONV_CH)), const((1, CONV_CH)), const((1, CONV_CH)), const((1, CONV_CH)),
        ],
        out_specs=[tile(CONV_CH), tile(SSD_CONV_CH), tile(SSD_INNER), tile(LANES)],
        out_shape=[jax.ShapeDtypeStruct((t, CONV_CH), BF16), jax.ShapeDtypeStruct((t, SSD_CONV_CH), BF16),
                   jax.ShapeDtypeStruct((t, SSD_INNER), BF16), jax.ShapeDtypeStruct((t, LANES), F32)],
        scratch_shapes=[
            pltpu.VMEM((CONV_HALO + CONV_TILE, CONV_CH), F32),
            pltpu.VMEM((CONV_TILE, CONV_CH), F32),
        ],
        compiler_params=_cparams(("arbitrary", "arbitrary")),
        name="mix_in",
    )(x2d, row(g_mix), w_main, w_dt, jnp.pad(conv_w, ((0, 1), (0, 0))), row(conv_b), row(ln_g), row(ln_b))


def _ssd_body(xbc_ref, z_ref, dt_ref, cw_ref, cb_ref, dtb_ref, alog_ref, dsk_ref, ng_ref,
              expand_ref, y_ref, xbuf_ref, act_ref, dts_ref, state_ref):
    @pl.when(pl.program_id(1) == 0)
    def _():
        xbuf_ref[0:SSD_HALO, :] = jnp.zeros((SSD_HALO, SSD_CONV_CH), F32)
        state_ref[...] = jnp.zeros(state_ref.shape, F32)

    xbuf_ref[SSD_HALO:SSD_HALO + SSD_TILE, :] = xbc_ref[...].astype(F32)
    full = xbuf_ref[...]
    conv = cb_ref[...] + cw_ref[SSD_CONV_K - 1:SSD_CONV_K, :] * full[SSD_HALO:, :]
    for back in range(1, SSD_CONV_K):
        past = pltpu.roll(full, back, 0)[SSD_HALO:, :]
        conv = conv + cw_ref[SSD_CONV_K - 1 - back:SSD_CONV_K - back, :] * past
    act_ref[...] = conv * _sigmoid(conv)
    xbuf_ref[0:SSD_HALO, :] = xbuf_ref[SSD_TILE:SSD_TILE + SSD_HALO, :]

    dt_in = dt_ref[...] + dtb_ref[...]
    dts_ref[...] = jnp.maximum(dt_in, 0.0) + jnp.log1p(jnp.exp(-jnp.abs(dt_in)))

    a_neg = -jnp.exp(alog_ref[...]) * LOG2_E
    q = SSD_CHUNK
    row_i = lax.broadcasted_iota(jnp.int32, (q, q), 0)
    col_i = lax.broadcasted_iota(jnp.int32, (q, q), 1)
    causal = row_i >= col_i
    tril = causal.astype(F32)
    lane_i = lax.broadcasted_iota(jnp.int32, (q, LANES), 1)
    low_half = lane_i < SSD_HEAD_DIM
    expand = expand_ref[...]

    def chunk(c, carry):
        r0 = pl.multiple_of(c * q, q)
        rows = pl.ds(r0, q)
        dtc = dts_ref[rows, :]
        a_cs = jnp.dot(tril, dtc * a_neg, preferred_element_type=F32,
                       precision=lax.Precision.HIGHEST)
        a_cs_t = a_cs.T
        dt_t = dtc.T
        a_end = a_cs[q - 1:q, :]
        e_exp = jnp.dot(jnp.exp2(a_cs).astype(BF16), expand, preferred_element_type=F32)
        w_exp = jnp.dot((jnp.exp2(a_end - a_cs) * dtc).astype(BF16), expand,
                        preferred_element_type=F32)
        dec_row = e_exp[q - 1:q, :]
        xc = act_ref[rows, 0:SSD_INNER]
        xw = (xc * w_exp).astype(BF16)
        y_parts = []
        for g in range(SSD_GROUPS):
            b_f = act_ref[rows, SSD_INNER + g * SSD_STATE:SSD_INNER + (g + 1) * SSD_STATE]
            c_f = act_ref[rows, SSD_INNER + SSD_BC + g * SSD_STATE:
                          SSD_INNER + SSD_BC + (g + 1) * SSD_STATE]
            b_g = b_f.astype(BF16)
            c_g = c_f.astype(BF16)
            cb = lax.dot_general(c_g, b_g, (((1,), (1,)), ((), ())),
                                 preferred_element_type=F32)
            gcols = slice(g * SSD_GROUP_W, (g + 1) * SSD_GROUP_W)
            st = state_ref[g]
            y_off = jnp.dot(c_g, st.astype(BF16), preferred_element_type=F32)
            state_ref[g] = st * dec_row[:, gcols] + jnp.dot(
                b_f.T.astype(BF16), xw[:, gcols], preferred_element_type=F32)
            for pair in range(SSD_GROUP_W // LANES):
                ms = []
                for hh in range(2):
                    h = g * (SSD_HEADS // SSD_GROUPS) + 2 * pair + hh
                    seg = a_cs[:, h:h + 1] - a_cs_t[h:h + 1, :]
                    dec = jnp.exp2(jnp.where(causal, seg, -jnp.inf))
                    ms.append((cb * dec * dt_t[h:h + 1, :]).astype(BF16))
                lhs = jnp.concatenate(ms, axis=1)
                xp = xc[:, g * SSD_GROUP_W + pair * LANES:g * SSD_GROUP_W + (pair + 1) * LANES]
                rhs = jnp.concatenate([jnp.where(low_half, xp, 0.0),
                                       jnp.where(low_half, 0.0, xp)], axis=0).astype(BF16)
                y_diag = jnp.dot(lhs, rhs, preferred_element_type=F32)
                lo = pair * LANES
                y_parts.append(y_diag + y_off[:, lo:lo + LANES]
                               * e_exp[:, g * SSD_GROUP_W + lo:g * SSD_GROUP_W + lo + LANES])
        y = jnp.concatenate(y_parts, axis=1) + xc * dsk_ref[...]
        z = z_ref[rows, :].astype(F32)
        y = y * (z * _sigmoid(z))
        outs = []
        for g in range(SSD_GROUPS):
            yg = y[:, g * SSD_GROUP_W:(g + 1) * SSD_GROUP_W]
            outs.append(yg * lax.rsqrt(jnp.mean(yg * yg, axis=-1, keepdims=True) + RMS_EPS))
        y_ref[rows, :] = (jnp.concatenate(outs, axis=1) * ng_ref[...]).astype(BF16)
        return carry

    lax.fori_loop(0, SSD_TILE // q, chunk, 0, unroll=SSD_UNROLL)


def _ssd(xbc, z, dt_raw, bsz, seqlen, ssd_conv_w, ssd_conv_b, dt_bias, a_log, d_skip, ssd_norm_g):
    nt = seqlen // SSD_TILE
    pad_h = lambda v: jnp.pad(v, (0, LANES - SSD_HEADS)).reshape(1, LANES)
    expand = (jnp.arange(LANES)[:, None] == (jnp.arange(SSD_INNER) // SSD_HEAD_DIM)[None, :]).astype(BF16)
    const = lambda shape: pl.BlockSpec(shape, lambda b, j: (0, 0))
    return pl.pallas_call(
        _ssd_body,
        grid=(bsz, nt),
        in_specs=[
            pl.BlockSpec((SSD_TILE, SSD_CONV_CH), lambda b, j: (b * nt + j, 0)),
            pl.BlockSpec((SSD_TILE, SSD_INNER), lambda b, j: (b * nt + j, 0)),
            pl.BlockSpec((SSD_TILE, LANES), lambda b, j: (b * nt + j, 0)),
            const((SSD_CONV_K, SSD_CONV_CH)),
            const((1, SSD_CONV_CH)),
            const((1, LANES)), const((1, LANES)),
            const((1, SSD_INNER)), const((1, SSD_INNER)),
            const((LANES, SSD_INNER)),
        ],
        out_specs=pl.BlockSpec((SSD_TILE, SSD_INNER), lambda b, j: (b * nt + j, 0)),
        out_shape=jax.ShapeDtypeStruct((bsz * seqlen, SSD_INNER), BF16),
        scratch_shapes=[
            pltpu.VMEM((SSD_HALO + SSD_TILE, SSD_CONV_CH), F32),
            pltpu.VMEM((SSD_TILE, SSD_CONV_CH), F32),
            pltpu.VMEM((SSD_TILE, LANES), F32),
            pltpu.VMEM((SSD_GROUPS, SSD_STATE, SSD_GROUP_W), F32),
        ],
        compiler_params=_cparams(("arbitrary", "arbitrary")),
        name="ssd",
    )(xbc, z, dt_raw, ssd_conv_w, ssd_conv_b.reshape(1, SSD_CONV_CH), pad_h(dt_bias),
      pad_h(a_log), jnp.repeat(d_skip, SSD_HEAD_DIM).reshape(1, SSD_INNER),
      ssd_norm_g.reshape(1, SSD_INNER), expand)


def _xattn_body(x_ref, u_ref, y_ref, wu_ref, wy_ref, k_ref, v_ref, gx_ref, wq_ref, wo_ref, gm_ref,
                wr_hi_ref, wr_lo_ref, br_ref, x2_ref, h2_ref, route_ref, stats_ref):
    def tile(s):
        rows = slice(s * TOK_TILE, (s + 1) * TOK_TILE)
        return _xattn_tile(x_ref[rows, :], u_ref[rows, :], y_ref[rows, :], wu_ref, wy_ref, k_ref, v_ref,
                           gx_ref, wq_ref, wo_ref, gm_ref, wr_hi_ref, wr_lo_ref, br_ref)

    results = _interleave([tile(s) for s in range(XA_TILES)])
    for s in range(XA_TILES):
        rows = slice(s * TOK_TILE, (s + 1) * TOK_TILE)
        x2_ref[rows, :], h2_ref[rows, :], route_ref[rows, :], stats_ref[s] = results[s]


def _xattn_tile(x, u, y, wu_ref, wy_ref, k_ref, v_ref, gx_ref, wq_ref, wo_ref, gm_ref, wr_hi_ref,
                wr_lo_ref, br_ref):
    x = (x + jnp.dot(u, wu_ref[...], preferred_element_type=F32)
         + jnp.dot(y, wy_ref[...], preferred_element_type=F32))
    yield
    h = _rms(x, gx_ref[...]).astype(BF16)
    q = (jnp.dot(h, wq_ref[...], preferred_element_type=F32) * (XA_HEAD_DIM ** -0.5)).astype(BF16)
    yield
    heads = []
    for i in range(XA_HEADS):
        cols = slice(i * XA_HEAD_DIM, (i + 1) * XA_HEAD_DIM)
        s = jnp.dot(q[:, cols], k_ref[0, cols, :], preferred_element_type=F32)
        p = jnp.exp(s - jnp.max(s, axis=-1, keepdims=True))
        p = p / jnp.sum(p, axis=-1, keepdims=True)
        heads.append(jnp.dot(p.astype(BF16), v_ref[0, :, cols], preferred_element_type=F32))
        yield
    o = jnp.concatenate(heads, axis=1).astype(BF16)
    x2 = x + jnp.dot(o, wo_ref[...], preferred_element_type=F32)
    yield

    h2 = _rms(x2, gm_ref[...])
    h_hi = h2.astype(BF16)
    h_lo = (h2 - h_hi.astype(F32)).astype(BF16)
    logits = (jnp.dot(h_hi, wr_hi_ref[...], preferred_element_type=F32)
              + jnp.dot(h_lo, wr_hi_ref[...], preferred_element_type=F32)
              + jnp.dot(h_hi, wr_lo_ref[...], preferred_element_type=F32)) + br_ref[...]
    yield
    lane = lax.broadcasted_iota(jnp.int32, logits.shape, 1)
    neg = -jnp.inf

    def first_argmax(v):
        m = jnp.max(v, axis=-1, keepdims=True)
        return m, jnp.min(jnp.where(v == m, lane, ROUTE_LANES), axis=-1, keepdims=True)

    gl = jnp.where(lane < N_GROUPS, logits, neg)
    g_max, g_sel = first_argmax(gl)
    p_top = 1.0 / jnp.sum(jnp.exp(gl - g_max), axis=-1, keepdims=True)
    e_lo = N_GROUPS + EXPERTS_PER_GROUP * g_sel
    el = jnp.where((lane >= e_lo) & (lane < e_lo + EXPERTS_PER_GROUP), logits, neg)
    m1, i1 = first_argmax(el)
    m2, i2 = first_argmax(jnp.where(lane == i1, neg, el))
    r = jnp.exp(m2 - m1)
    w1 = p_top / (1.0 + r)
    w2 = w1 * r
    yield
    e1 = i1 - N_GROUPS
    e2 = i2 - N_GROUPS
    oh1 = (lane == e1).astype(BF16)
    oh2 = (lane == e2).astype(BF16)
    n_t = logits.shape[0]
    before = (lax.broadcasted_iota(jnp.int32, (n_t, n_t), 0)
              > lax.broadcasted_iota(jnp.int32, (n_t, n_t), 1)).astype(BF16)
    c1 = jnp.dot(before, oh1, preferred_element_type=F32)
    c2 = jnp.dot(before, oh2, preferred_element_type=F32)
    tot1 = jnp.sum(oh1.astype(F32), axis=0, keepdims=True)
    cnt = tot1 + jnp.sum(oh2.astype(F32), axis=0, keepdims=True)
    cnt = jnp.floor((cnt + (SUBLANES - 1)) * (1.0 / SUBLANES)) * SUBLANES
    lanes_before = (lax.broadcasted_iota(jnp.int32, (ROUTE_LANES, ROUTE_LANES), 0)
                    < lax.broadcasted_iota(jnp.int32, (ROUTE_LANES, ROUTE_LANES), 1)).astype(F32)
    loff = jnp.dot(jnp.broadcast_to(cnt, (SUBLANES, ROUTE_LANES)), lanes_before,
                   preferred_element_type=F32, precision=lax.Precision.HIGHEST)[0:1, :]
    lp1 = jnp.sum(oh1.astype(F32) * (loff + c1), axis=-1, keepdims=True)
    lp2 = jnp.sum(oh2.astype(F32) * (loff + tot1 + c2), axis=-1, keepdims=True)
    route = jnp.where(lane == 0, e1.astype(F32),
            jnp.where(lane == 1, e2.astype(F32),
            jnp.where(lane == 2, w1,
            jnp.where(lane == 3, w2,
            jnp.where(lane == 4, lp1, jnp.where(lane == 5, lp2, 0.0))))))
    row = lax.broadcasted_iota(jnp.int32, (SUBLANES, ROUTE_LANES), 0)
    return x2, h_hi, route, jnp.where(row == 0, cnt, jnp.where(row == 1, loff, 0.0))


def _xattn_route(x, u, y, w_out, k, v, bsz, seqlen, g_xattn, w_q, w_o, g_moe, w_rg, b_rg, w_re, b_re):
    t, d = x.shape
    w_mix = w_out.astype(BF16)
    w_r = jnp.pad(jnp.concatenate([w_rg, w_re], axis=1), ((0, 0), (0, ROUTE_LANES - N_GROUPS - N_EXPERTS)))
    b_r = jnp.pad(jnp.concatenate([b_rg, b_re]), (0, ROUTE_LANES - N_GROUPS - N_EXPERTS)).reshape(1, ROUTE_LANES)
    wr_hi = w_r.astype(BF16)
    wr_lo = (w_r - wr_hi.astype(F32)).astype(BF16)
    step_rows = XA_TILES * TOK_TILE
    nt = seqlen // step_rows
    tile = pl.BlockSpec((step_rows, d), lambda b, j: (b * nt + j, 0))
    kv_spec = pl.BlockSpec((1, MEM_LEN, d), lambda b, j: (b, 0, 0))
    const = lambda shape: pl.BlockSpec(shape, lambda b, j: (0, 0))
    return pl.pallas_call(
        _xattn_body,
        grid=(bsz, nt),
        in_specs=[tile, tile, tile, const((d, d)), const((d, d)),
                  pl.BlockSpec((1, d, MEM_LEN), lambda b, j: (b, 0, 0)), kv_spec, const((1, d)),
                  const((d, d)), const((d, d)), const((1, d)),
                  const((d, ROUTE_LANES)), const((d, ROUTE_LANES)), const((1, ROUTE_LANES))],
        out_specs=[tile, tile, pl.BlockSpec((step_rows, ROUTE_LANES), lambda b, j: (b * nt + j, 0)),
                   pl.BlockSpec((XA_TILES, SUBLANES, ROUTE_LANES), lambda b, j: (b * nt + j, 0, 0))],
        out_shape=[jax.ShapeDtypeStruct((t, d), F32), jax.ShapeDtypeStruct((t, d), BF16),
                   jax.ShapeDtypeStruct((t, ROUTE_LANES), F32),
                   jax.ShapeDtypeStruct((t // TOK_TILE, SUBLANES, ROUTE_LANES), F32)],
        compiler_params=_cparams(("arbitrary", "arbitrary")),
        name="xattn_route",
    )(x, u, y, w_mix[:CONV_CH], w_mix[CONV_CH:], jnp.swapaxes(k, 1, 2), v, g_xattn.reshape(1, d), w_q.astype(BF16), w_o.astype(BF16), g_moe.reshape(1, d),
      wr_hi, wr_lo, b_r)


def _rows_copy(src_ref, src0, dst_ref, dst0, n, sem):
    rows = lambda r0: pl.ds(pl.multiple_of(r0, SUBLANES), pl.multiple_of(n, SUBLANES))
    return pltpu.make_async_copy(src_ref.at[rows(src0), :], dst_ref.at[rows(dst0), :], sem)


def _pack_pairs(v):
    half = v.shape[1] // 2
    lo = lax.bitcast_convert_type(v[:, :half], jnp.uint32)
    hi = lax.bitcast_convert_type(v[:, half:], jnp.uint32)
    return (lo >> 16) | (hi & jnp.uint32(0xFFFF0000))


def _unpack_pairs(w):
    lo = lax.bitcast_convert_type(w << 16, F32)
    hi = lax.bitcast_convert_type(w & jnp.uint32(0xFFFF0000), F32)
    return jnp.concatenate([lo, hi], axis=1).astype(BF16)


def _dispatch_body(cnt_ref, loff_ref, gb_ref, tot_ref, fstart_ref, fcnt_ref, nu_ref, h_ref, route_ref,
                   xs_ref, buf_ref, zero_ref, sem, zsem):
    i = pl.program_id(0)
    nt = pl.num_programs(0)
    slot = i % 2
    buf = buf_ref.at[slot]

    def drain(step, s):
        n = sum(tot_ref[step * MOE_TILES + k] for k in range(MOE_TILES))
        _rows_copy(buf_ref.at[s], 0, xs_ref, 0, n, sem.at[s]).wait()

    @pl.when(i >= 2)
    def _():
        drain(i - 2, slot)

    def sort_tile(k):
        rows = slice(k * TOK_TILE, (k + 1) * TOK_TILE)
        rt = route_ref[rows, :].T
        lp1 = rt[4:5, :].astype(jnp.int32)
        lp2 = rt[5:6, :].astype(jnp.int32)
        r_i = lax.broadcasted_iota(jnp.int32, (SORT_ROWS, TOK_TILE), 0)
        perm = jnp.where((r_i == lp1) | (r_i == lp2), 1.0, 0.0).astype(BF16)
        yield
        buf[k * SORT_ROWS:(k + 1) * SORT_ROWS, :] = _pack_pairs(
            jnp.dot(perm, h_ref[rows, :], preferred_element_type=F32))

    _interleave([sort_tile(k) for k in range(MOE_TILES)])

    for k in range(MOE_TILES):
        def per_expert(e, c, k=k):
            j = (i * MOE_TILES + k) * N_EXPERTS + e

            @pl.when(cnt_ref[j] > 0)
            def _():
                _rows_copy(buf, k * SORT_ROWS + loff_ref[j], xs_ref, gb_ref[j], cnt_ref[j],
                           sem.at[slot]).start()
            return c

        lax.fori_loop(0, N_EXPERTS, per_expert, 0)

    @pl.when(i == 0)
    def _():
        zero_ref[...] = jnp.zeros(zero_ref.shape, zero_ref.dtype)

        def fill(start):
            def body(e, c):
                @pl.when(fcnt_ref[e] > 0)
                def _():
                    copy = _rows_copy(zero_ref, 0, xs_ref, fstart_ref[e], fcnt_ref[e], zsem)
                    copy.start() if start else copy.wait()
                return c
            lax.fori_loop(0, N_EXPERTS, body, 0)

            def tail(blk, c):
                copy = _rows_copy(zero_ref, 0, xs_ref, blk * MOE_BLOCK, MOE_BLOCK, zsem)
                copy.start() if start else copy.wait()
                return c
            lax.fori_loop(nu_ref[0], xs_ref.shape[0] // MOE_BLOCK, tail, 0)

        fill(True)
        fill(False)

    @pl.when(i == nt - 1)
    def _():
        drain(i, slot)

        @pl.when(i >= 1)
        def _():
            drain(i - 1, 1 - slot)


def _dispatch(h2, route, tables, n_used, n_slots):
    t, d = h2.shape
    step_rows = MOE_TILES * TOK_TILE
    return pl.pallas_call(
        _dispatch_body,
        grid_spec=pltpu.PrefetchScalarGridSpec(
            num_scalar_prefetch=7,
            grid=(t // step_rows,),
            in_specs=[
                pl.BlockSpec((step_rows, d), lambda i, *_: (i, 0)),
                pl.BlockSpec((step_rows, ROUTE_LANES), lambda i, *_: (i, 0)),
            ],
            out_specs=pl.BlockSpec(memory_space=pl.ANY),
            scratch_shapes=[pltpu.VMEM((2, MOE_TILES * SORT_ROWS, d // 2), jnp.uint32),
                            pltpu.VMEM((MOE_BLOCK, d // 2), jnp.uint32),
                            pltpu.SemaphoreType.DMA((2,)), pltpu.SemaphoreType.DMA(())],
        ),
        out_shape=jax.ShapeDtypeStruct((n_slots, d // 2), jnp.uint32),
        compiler_params=_cparams(("arbitrary",)),
        name="dispatch",
    )(tables["cnt"], tables["loff"], tables["gb"], tables["tot"], tables["fill_start"],
      tables["fill_cnt"], n_used, h2, route)


def _experts_body(be_ref, nu_ref, first_ref, slot_ref, nxt_ref, x_ref, wg_hbm, wu_hbm, wd_hbm, y_ref,
                  wg_f, wu_f, wd_f, wg_bf, wu_bf, wd_bf, sem):
    b = pl.program_id(0)
    used = b < nu_ref[0]

    def weights(e, s):
        return [pltpu.make_async_copy(hbm.at[e], buf.at[s], sem.at[s])
                for hbm, buf in ((wg_hbm, wg_f), (wu_hbm, wu_f), (wd_hbm, wd_f))]

    @pl.when(used & (first_ref[b] == 1))
    def _():
        s = slot_ref[b]

        @pl.when(b == 0)
        def _():
            for copy in weights(be_ref[0], 0):
                copy.start()

        for copy in weights(be_ref[b], s):
            copy.wait()
        wg_bf[...] = wg_f[s].astype(BF16)
        wu_bf[...] = wu_f[s].astype(BF16)
        wd_bf[...] = wd_f[s].astype(BF16)

        @pl.when(nxt_ref[b] >= 0)
        def _():
            for copy in weights(nxt_ref[b], 1 - s):
                copy.start()

    @pl.when(used)
    def _():
        x = _unpack_pairs(x_ref[...])
        g = jnp.dot(x, wg_bf[...], preferred_element_type=F32)
        u = jnp.dot(x, wu_bf[...], preferred_element_type=F32)
        a = (g * _sigmoid(g) * u).astype(BF16)
        y = jnp.dot(a, wd_bf[...], preferred_element_type=F32)
        y_ref[...] = _pack_pairs(y.astype(BF16).astype(F32))

    @pl.when(jnp.logical_not(used))
    def _():
        y_ref[...] = jnp.zeros(y_ref.shape, y_ref.dtype)


def _experts(xs, etab, w_gate, w_up, w_down):
    n_slots, dp = xs.shape
    d = w_gate.shape[1]
    n_blocks = n_slots // MOE_BLOCK
    last = lambda b, nu: jnp.maximum(jnp.minimum(b, nu[0] - 1), 0)
    hbm = pl.BlockSpec(memory_space=pl.ANY)
    return pl.pallas_call(
        _experts_body,
        grid_spec=pltpu.PrefetchScalarGridSpec(
            num_scalar_prefetch=5,
            grid=(n_blocks,),
            in_specs=[pl.BlockSpec((MOE_BLOCK, dp), lambda b, be, nu, *_: (last(b, nu), 0)), hbm, hbm, hbm],
            out_specs=pl.BlockSpec((MOE_BLOCK, dp), lambda b, *_: (b, 0)),
            scratch_shapes=[pltpu.VMEM((2, d, D_EXPERT), F32), pltpu.VMEM((2, d, D_EXPERT), F32),
                            pltpu.VMEM((2, D_EXPERT, d), F32),
                            pltpu.VMEM((d, D_EXPERT), BF16), pltpu.VMEM((d, D_EXPERT), BF16),
                            pltpu.VMEM((D_EXPERT, d), BF16), pltpu.SemaphoreType.DMA((2,))],
        ),
        out_shape=jax.ShapeDtypeStruct((n_slots, dp), jnp.uint32),
        compiler_params=_cparams(("arbitrary",)),
        name="experts",
    )(etab["block_e"], etab["n_used"], etab["first"], etab["slot"], etab["nxt"], xs, w_gate, w_up, w_down)


def _combine_body(cnt_ref, loff_ref, gb_ref, tot_ref, x_ref, route_ref, g_ref, ys_ref, o_ref, ybuf_ref, sem):
    i = pl.program_id(0)
    nt = pl.num_programs(0)
    slot = i % 2

    def gather(step, s):
        for k in range(MOE_TILES):
            def per_expert(e, c, k=k):
                j = (step * MOE_TILES + k) * N_EXPERTS + e

                @pl.when(cnt_ref[j] > 0)
                def _():
                    _rows_copy(ys_ref, gb_ref[j], ybuf_ref.at[s], k * SORT_ROWS + loff_ref[j], cnt_ref[j],
                               sem.at[s]).start()
                return c

            lax.fori_loop(0, N_EXPERTS, per_expert, 0)

    @pl.when(i == 0)
    def _():
        ybuf_ref[...] = jnp.zeros(ybuf_ref.shape, ybuf_ref.dtype)
        gather(0, 0)

    @pl.when(i + 1 < nt)
    def _():
        gather(i + 1, 1 - slot)

    def combine_tile(k):
        rows = slice(k * TOK_TILE, (k + 1) * TOK_TILE)
        route = route_ref[rows, :]
        c_i = lax.broadcasted_iota(jnp.int32, (TOK_TILE, SORT_ROWS), 1)
        lp1 = route[:, 4:5].astype(jnp.int32)
        lp2 = route[:, 5:6].astype(jnp.int32)
        pw = jnp.where(c_i == lp1, route[:, 2:3], jnp.where(c_i == lp2, route[:, 3:4], 0.0)).astype(BF16)
        yield
        r_i = lax.broadcasted_iota(jnp.int32, (SORT_ROWS, 1), 0)
        y = _unpack_pairs(jnp.where(r_i < tot_ref[i * MOE_TILES + k],
                                    ybuf_ref[slot, k * SORT_ROWS:(k + 1) * SORT_ROWS, :], jnp.uint32(0)))
        moe = jnp.dot(pw, y, preferred_element_type=F32)
        yield
        o_ref[rows, :] = _rms(x_ref[rows, :] + moe, g_ref[...])

    tiles = [combine_tile(k) for k in range(MOE_TILES)]
    for tile in tiles:
        next(tile)
    n = sum(tot_ref[i * MOE_TILES + k] for k in range(MOE_TILES))
    _rows_copy(ys_ref, 0, ybuf_ref.at[slot], 0, n, sem.at[slot]).wait()
    _interleave(tiles)


def _combine(x2, route, tables, ys, g_final):
    t, d = x2.shape
    step_rows = MOE_TILES * TOK_TILE
    tile = pl.BlockSpec((step_rows, d), lambda i, *_: (i, 0))
    return pl.pallas_call(
        _combine_body,
        grid_spec=pltpu.PrefetchScalarGridSpec(
            num_scalar_prefetch=4,
            grid=(t // step_rows,),
            in_specs=[
                tile,
                pl.BlockSpec((step_rows, ROUTE_LANES), lambda i, *_: (i, 0)),
                pl.BlockSpec((1, d), lambda i, *_: (0, 0)),
                pl.BlockSpec(memory_space=pl.ANY),
            ],
            out_specs=tile,
            scratch_shapes=[pltpu.VMEM((2, MOE_TILES * SORT_ROWS, d // 2), jnp.uint32),
                            pltpu.SemaphoreType.DMA((2,))],
        ),
        out_shape=jax.ShapeDtypeStruct((t, d), F32),
        compiler_params=_cparams(("arbitrary",)),
        name="combine",
    )(tables["cnt"], tables["loff"], tables["gb"], tables["tot"], x2, route, g_final.reshape(1, d), ys)


def _routing_tables(stats, n_tok):
    cnt = stats[:, 0, :N_EXPERTS].astype(jnp.int32)
    loff = stats[:, 1, :N_EXPERTS].astype(jnp.int32)
    n_tiles = cnt.shape[0]
    counts = jnp.sum(cnt, axis=0)
    padded = ((counts + MOE_BLOCK - 1) // MOE_BLOCK) * MOE_BLOCK
    pad_end = jnp.cumsum(padded)
    pad_start = pad_end - padded
    gb = pad_start[None, :] + jnp.cumsum(cnt, axis=0) - cnt
    max_rows = (n_tok * TOP_K + n_tiles * N_EXPERTS * (SUBLANES - 1)
                + N_EXPERTS * (MOE_BLOCK - SUBLANES))
    n_blocks = -(-max_rows // MOE_BLOCK)
    n_slots = n_blocks * MOE_BLOCK
    block_start = jnp.arange(n_blocks, dtype=jnp.int32) * MOE_BLOCK
    block_e = jnp.minimum(jnp.sum((pad_end[None, :] <= block_start[:, None]).astype(jnp.int32), axis=1),
                          N_EXPERTS - 1)
    tables = dict(cnt=cnt.reshape(-1), loff=loff.reshape(-1), gb=gb.reshape(-1).astype(jnp.int32),
                  tot=jnp.sum(cnt, axis=1).astype(jnp.int32),
                  fill_start=(pad_start + counts).astype(jnp.int32),
                  fill_cnt=(padded - counts).astype(jnp.int32))
    n_used = (pad_end[-1:] // MOE_BLOCK).astype(jnp.int32)
    first = jnp.concatenate([jnp.ones((1,), jnp.int32), (block_e[1:] != block_e[:-1]).astype(jnp.int32)])
    seg_slot = (jnp.cumsum(first) - 1) % 2
    experts = jnp.arange(N_EXPERTS, dtype=jnp.int32)
    later = jnp.where((experts[None, :] > experts[:, None]) & (padded[None, :] > 0), experts[None, :], N_EXPERTS)
    next_e = jnp.min(later, axis=1)
    next_e = jnp.where(next_e < N_EXPERTS, next_e, -1)
    nxt = jnp.sum(jnp.where(block_e[:, None] == experts[None, :], next_e[None, :], 0), axis=1)
    etab = dict(block_e=block_e, n_used=n_used, first=first, slot=seg_slot.astype(jnp.int32),
                nxt=nxt.astype(jnp.int32))
    return tables, etab, n_slots


def kernel(x, mem, g_mix, w_in, conv_w, conv_b, ln_g, ln_b, ssd_conv_w, ssd_conv_b, dt_bias, a_log, d_skip, ssd_norm_g, w_out, g_xattn, g_mem, w_q, w_k, w_v, w_o, g_moe, w_router_group, b_router_group, w_router_expert, b_router_expert, w_gate, w_up, w_down, g_final):
    bsz, seqlen, d = x.shape
    n_tok = bsz * seqlen
    xt = x.reshape(n_tok, d)
    assert g_mix.shape[0] == 1, "the combine kernel applies the final norm: single layer only"
    for l in range(1):
        u, xbc, z, dt_raw = _mix_in(xt, bsz, seqlen, g_mix[l], w_in[l], conv_w[l], conv_b[l],
                                    ln_g[l], ln_b[l])
        y = _ssd(xbc, z, dt_raw, bsz, seqlen, ssd_conv_w[l], ssd_conv_b[l], dt_bias[l], a_log[l],
                 d_skip[l], ssd_norm_g[l])
        k, v = _kv_proj(mem, g_mem[l], w_k[l], w_v[l])
        x2, h2, route, stats = _xattn_route(xt, u, y, w_out[l], k, v, bsz, seqlen, g_xattn[l], w_q[l],
                                            w_o[l], g_moe[l], w_router_group[l], b_router_group[l],
                                            w_router_expert[l], b_router_expert[l])
        tables, etab, n_slots = _routing_tables(stats, n_tok)
        xs = _dispatch(h2, route, tables, etab["n_used"], n_slots)
        ys = _experts(xs, etab, w_gate[l], w_up[l], w_down[l])
        xt = _combine(x2, route, tables, ys, g_final)
    return xt.reshape(bsz, seqlen, d)
```

```python
import functools

import jax
import jax.numpy as jnp
from jax import lax
from jax.experimental import pallas as pl
from jax.experimental.pallas import tpu as pltpu

F32 = jnp.float32
BF16 = jnp.bfloat16

D_MODEL = 1024
CONV_CH = 1024
CONV_K = 31
SSD_INNER = 1024
SSD_HEAD_DIM = 64
SSD_HEADS = 16
SSD_STATE = 128
SSD_GROUPS = 2
SSD_GROUP_W = SSD_INNER // SSD_GROUPS
SSD_CONV_K = 4
SSD_CHUNK = 128
SSD_BC = SSD_GROUPS * SSD_STATE
SSD_CONV_CH = SSD_INNER + 2 * SSD_BC
N_MAIN = 2 * CONV_CH + 2 * SSD_INNER + 2 * SSD_BC
XA_HEADS = 4
XA_HEAD_DIM = 256
MEM_LEN = 256
N_GROUPS = 4
EXPERTS_PER_GROUP = 8
N_EXPERTS = 32
TOP_K = 2
D_EXPERT = 512
MOE_BLOCK = 512
RMS_EPS = 1e-6
LN_EPS = 1e-5
LOG2_E = 1.4426950408889634

LANES = 128
SUBLANES = 8
VMEM_LIMIT = 56 * 1024 * 1024

TOK_TILE = 512
CONV_TILE = 512
MM_CHUNK = 512
CONV_HALO = 32
CONV_ROWS = 128
SSD_TILE = 512
MOE_TILES = 2
XA_TILES = 2
SSD_UNROLL = 4
SSD_HALO = 8
ROUTE_LANES = 128
SORT_ROWS = TOP_K * TOK_TILE + N_EXPERTS * SUBLANES


def _cparams(sem):
    return pltpu.CompilerParams(dimension_semantics=sem, vmem_limit_bytes=VMEM_LIMIT)


def _rms(x, g):
    return x * lax.rsqrt(jnp.mean(x * x, axis=-1, keepdims=True) + RMS_EPS) * g


def _sigmoid(x):
    return 1.0 / (1.0 + jnp.exp2(x * (-LOG2_E)))


def _interleave(stages):
    results = {}
    while len(results) < len(stages):
        for k, item in enumerate(stages):
            if k not in results:
                try:
                    next(item)
                except StopIteration as done:
                    results[k] = done.value
    return [results[k] for k in range(len(stages))]


def _kv_body(m_ref, g_ref, wk_ref, wv_ref, k_ref, v_ref):
    m = _rms(m_ref[0], g_ref[...]).astype(BF16)
    k_ref[0] = jnp.dot(m, wk_ref[...], preferred_element_type=F32).astype(BF16)
    v_ref[0] = jnp.dot(m, wv_ref[...], preferred_element_type=F32).astype(BF16)


def _kv_proj(mem, g_mem, w_k, w_v):
    b, s, d = mem.shape
    w_spec = pl.BlockSpec((d, d), lambda i: (0, 0))
    kv_spec = pl.BlockSpec((1, s, d), lambda i: (i, 0, 0))
    return pl.pallas_call(
        _kv_body,
        grid=(b,),
        in_specs=[kv_spec, pl.BlockSpec((1, d), lambda i: (0, 0)), w_spec, w_spec],
        out_specs=[kv_spec, kv_spec],
        out_shape=[jax.ShapeDtypeStruct((b, s, d), BF16)] * 2,
        compiler_params=_cparams(("arbitrary",)),
        name="kv_proj",
    )(mem, g_mem.reshape(1, d), w_k.astype(BF16), w_v.astype(BF16))


def _mix_in_body(x_ref, g_ref, w_ref, wdt_ref, cw_ref, cb_ref, lg_ref, lb_ref,
                 u_ref, xbc_ref, z_ref, dt_ref, ubuf_ref, acc_ref):
    @pl.when(pl.program_id(1) == 0)
    def _():
        ubuf_ref[0:CONV_HALO, :] = jnp.zeros((CONV_HALO, CONV_CH), F32)

    h = _rms(x_ref[...], g_ref[...]).astype(BF16)
    proj = lambda lo: jnp.dot(h, w_ref[:, lo:lo + MM_CHUNK], preferred_element_type=F32)
    for c in range(CONV_CH // MM_CHUNK):
        cols = slice(c * MM_CHUNK, (c + 1) * MM_CHUNK)
        ubuf_ref[CONV_HALO:CONV_HALO + CONV_TILE, cols] = (
            proj(c * MM_CHUNK) * _sigmoid(proj(CONV_CH + c * MM_CHUNK)))
    z0 = 2 * CONV_CH
    for c in range(SSD_INNER // MM_CHUNK):
        z_ref[:, c * MM_CHUNK:(c + 1) * MM_CHUNK] = proj(z0 + c * MM_CHUNK).astype(BF16)
    x0 = z0 + SSD_INNER
    for c in range(SSD_CONV_CH // MM_CHUNK):
        xbc_ref[:, c * MM_CHUNK:(c + 1) * MM_CHUNK] = proj(x0 + c * MM_CHUNK).astype(BF16)
    dt_ref[...] = jnp.dot(h, wdt_ref[...], preferred_element_type=F32)

    first = CONV_HALO - (CONV_K - 1)
    for rc in range(CONV_TILE // CONV_ROWS):
        r0 = rc * CONV_ROWS
        rows = slice(r0, r0 + CONV_ROWS)
        for cb in range(CONV_CH // LANES):
            cols = slice(cb * LANES, (cb + 1) * LANES)
            acc = None
            for res in range(SUBLANES):
                part = None
                for k in range(CONV_K):
                    if (first + k) % SUBLANES == res:
                        term = cw_ref[k:k + 1, cols] * ubuf_ref[r0 + first + k:r0 + first + k + CONV_ROWS, cols]
                        part = term if part is None else part + term
                acc = part if acc is None else acc + part
            acc_ref[rows, cols] = acc
        u = acc_ref[rows, :] + cb_ref[...]
        mu = jnp.mean(u, axis=-1, keepdims=True)
        uc = u - mu
        var = jnp.mean(uc * uc, axis=-1, keepdims=True)
        y = uc * lax.rsqrt(var + LN_EPS) * lg_ref[...] + lb_ref[...]
        u_ref[rows, :] = (y * _sigmoid(y)).astype(BF16)

    ubuf_ref[0:CONV_HALO, :] = ubuf_ref[CONV_TILE:CONV_TILE + CONV_HALO, :]


def _mix_in(x2d, bsz, seqlen, g_mix, w_in, conv_w, conv_b, ln_g, ln_b):
    t, d = x2d.shape
    nt = seqlen // CONV_TILE
    w_main = w_in.astype(BF16)
    w_dt = jnp.pad(w_in[:, N_MAIN:], ((0, 0), (0, LANES - SSD_HEADS))).astype(BF16)
    row = lambda v: v.reshape(1, -1)
    const = lambda shape: pl.BlockSpec(shape, lambda b, j: (0, 0))
    tile = lambda width: pl.BlockSpec((CONV_TILE, width), lambda b, j: (b * nt + j, 0))
    return pl.pallas_call(
        _mix_in_body,
        grid=(bsz, nt),
        in_specs=[
            tile(d), const((1, d)), const(w_in.shape), const((d, LANES)),
            const((CONV_K + 1, CONV_CH)), const((1, CONV_CH)), const((1, CONV_CH)), const((1, CONV_CH)),
        ],
        out_specs=[tile(CONV_CH), tile(SSD_CONV_CH), tile(SSD_INNER), tile(LANES)],
        out_shape=[jax.ShapeDtypeStruct((t, CONV_CH), BF16), jax.ShapeDtypeStruct((t, SSD_CONV_CH), BF16),
                   jax.ShapeDtypeStruct((t, SSD_INNER), BF16), jax.ShapeDtypeStruct((t, LANES), F32)],
        scratch_shapes=[
            pltpu.VMEM((CONV_HALO + CONV_TILE, CONV_CH), F32),
            pltpu.VMEM((CONV_TILE, CONV_CH), F32),
        ],
        compiler_params=_cparams(("arbitrary", "arbitrary")),
        name="mix_in",
    )(x2d, row(g_mix), w_main, w_dt, jnp.pad(conv_w, ((0, 1), (0, 0))), row(conv_b), row(ln_g), row(ln_b))


def _ssd_body(xbc_ref, z_ref, dt_ref, cw_ref, cb_ref, dtb_ref, alog_ref, dsk_ref, ng_ref,
              expand_ref, y_ref, xbuf_ref, act_ref, dts_ref, state_ref):
    @pl.when(pl.program_id(1) == 0)
    def _():
        xbuf_ref[0:SSD_HALO, :] = jnp.zeros((SSD_HALO, SSD_CONV_CH), F32)
        state_ref[...] = jnp.zeros(state_ref.shape, F32)

    xbuf_ref[SSD_HALO:SSD_HALO + SSD_TILE, :] = xbc_ref[...].astype(F32)
    full = xbuf_ref[...]
    conv = cb_ref[...] + cw_ref[SSD_CONV_K - 1:SSD_CONV_K, :] * full[SSD_HALO:, :]
    for back in range(1, SSD_CONV_K):
        past = pltpu.roll(full, back, 0)[SSD_HALO:, :]
        conv = conv + cw_ref[SSD_CONV_K - 1 - back:SSD_CONV_K - back, :] * past
    act_ref[...] = conv * _sigmoid(conv)
    xbuf_ref[0:SSD_HALO, :] = xbuf_ref[SSD_TILE:SSD_TILE + SSD_HALO, :]

    dt_in = dt_ref[...] + dtb_ref[...]
    dts_ref[...] = jnp.maximum(dt_in, 0.0) + jnp.log1p(jnp.exp(-jnp.abs(dt_in)))

    a_neg = -jnp.exp(alog_ref[...]) * LOG2_E
    q = SSD_CHUNK
    row_i = lax.broadcasted_iota(jnp.int32, (q, q), 0)
    col_i = lax.broadcasted_iota(jnp.int32, (q, q), 1)
    causal = row_i >= col_i
    tril = causal.astype(F32)
    lane_i = lax.broadcasted_iota(jnp.int32, (q, LANES), 1)
    low_half = lane_i < SSD_HEAD_DIM
    expand = expand_ref[...]

    def chunk(c, carry):
        r0 = pl.multiple_of(c * q, q)
        rows = pl.ds(r0, q)
        dtc = dts_ref[rows, :]
        a_cs = jnp.dot(tril, dtc * a_neg, preferred_element_type=F32,
                       precision=lax.Precision.HIGHEST)
        a_cs_t = a_cs.T
        dt_t = dtc.T
        a_end = a_cs[q - 1:q, :]
        e_exp = jnp.dot(jnp.exp2(a_cs).astype(BF16), expand, preferred_element_type=F32)
        w_exp = jnp.dot((jnp.exp2(a_end - a_cs) * dtc).astype(BF16), expand,
                        preferred_element_type=F32)
        dec_row = e_exp[q - 1:q, :]
        xc = act_ref[rows, 0:SSD_INNER]
        xw = (xc * w_exp).astype(BF16)
        y_parts = []
        for g in range(SSD_GROUPS):
            b_f = act_ref[rows, SSD_INNER + g * SSD_STATE:SSD_INNER + (g + 1) * SSD_STATE]
            c_f = act_ref[rows, SSD_INNER + SSD_BC + g * SSD_STATE:
                          SSD_INNER + SSD_BC + (g + 1) * SSD_STATE]
            b_g = b_f.astype(BF16)
            c_g = c_f.astype(BF16)
            cb = lax.dot_general(c_g, b_g, (((1,), (1,)), ((), ())),
                                 preferred_element_type=F32)
            gcols = slice(g * SSD_GROUP_W, (g + 1) * SSD_GROUP_W)
            st = state_ref[g]
            y_off = jnp.dot(c_g, st.astype(BF16), preferred_element_type=F32)
            state_ref[g] = st * dec_row[:, gcols] + jnp.dot(
                b_f.T.astype(BF16), xw[:, gcols], preferred_element_type=F32)
            for pair in range(SSD_GROUP_W // LANES):
                ms = []
                for hh in range(2):
                    h = g * (SSD_HEADS // SSD_GROUPS) + 2 * pair + hh
                    seg = a_cs[:, h:h + 1] - a_cs_t[h:h + 1, :]
                    dec = jnp.exp2(jnp.where(causal, seg, -jnp.inf))
                    ms.append((cb * dec * dt_t[h:h + 1, :]).astype(BF16))
                lhs = jnp.concatenate(ms, axis=1)
                xp = xc[:, g * SSD_GROUP_W + pair * LANES:g * SSD_GROUP_W + (pair + 1) * LANES]
                rhs = jnp.concatenate([jnp.where(low_half, xp, 0.0),
                                       jnp.where(low_half, 0.0, xp)], axis=0).astype(BF16)
                y_diag = jnp.dot(lhs, rhs, preferred_element_type=F32)
                lo = pair * LANES
                y_parts.append(y_diag + y_off[:, lo:lo + LANES]
                               * e_exp[:, g * SSD_GROUP_W + lo:g * SSD_GROUP_W + lo + LANES])
        y = jnp.concatenate(y_parts, axis=1) + xc * dsk_ref[...]
        z = z_ref[rows, :].astype(F32)
        y = y * (z * _sigmoid(z))
        outs = []
        for g in range(SSD_GROUPS):
            yg = y[:, g * SSD_GROUP_W:(g + 1) * SSD_GROUP_W]
            outs.append(yg * lax.rsqrt(jnp.mean(yg * yg, axis=-1, keepdims=True) + RMS_EPS))
        y_ref[rows, :] = (jnp.concatenate(outs, axis=1) * ng_ref[...]).astype(BF16)
        return carry

    lax.fori_loop(0, SSD_TILE // q, chunk, 0, unroll=SSD_UNROLL)


def _ssd(xbc, z, dt_raw, bsz, seqlen, ssd_conv_w, ssd_conv_b, dt_bias, a_log, d_skip, ssd_norm_g):
    nt = seqlen // SSD_TILE
    pad_h = lambda v: jnp.pad(v, (0, LANES - SSD_HEADS)).reshape(1, LANES)
    expand = (jnp.arange(LANES)[:, None] == (jnp.arange(SSD_INNER) // SSD_HEAD_DIM)[None, :]).astype(BF16)
    const = lambda shape: pl.BlockSpec(shape, lambda b, j: (0, 0))
    return pl.pallas_call(
        _ssd_body,
        grid=(bsz, nt),
        in_specs=[
            pl.BlockSpec((SSD_TILE, SSD_CONV_CH), lambda b, j: (b * nt + j, 0)),
            pl.BlockSpec((SSD_TILE, SSD_INNER), lambda b, j: (b * nt + j, 0)),
            pl.BlockSpec((SSD_TILE, LANES), lambda b, j: (b * nt + j, 0)),
            const((SSD_CONV_K, SSD_CONV_CH)),
            const((1, SSD_CONV_CH)),
            const((1, LANES)), const((1, LANES)),
            const((1, SSD_INNER)), const((1, SSD_INNER)),
            const((LANES, SSD_INNER)),
        ],
        out_specs=pl.BlockSpec((SSD_TILE, SSD_INNER), lambda b, j: (b * nt + j, 0)),
        out_shape=jax.ShapeDtypeStruct((bsz * seqlen, SSD_INNER), BF16),
        scratch_shapes=[
            pltpu.VMEM((SSD_HALO + SSD_TILE, SSD_CONV_CH), F32),
            pltpu.VMEM((SSD_TILE, SSD_CONV_CH), F32),
            pltpu.VMEM((SSD_TILE, LANES), F32),
            pltpu.VMEM((SSD_GROUPS, SSD_STATE, SSD_GROUP_W), F32),
        ],
        compiler_params=_cparams(("arbitrary", "arbitrary")),
        name="ssd",
    )(xbc, z, dt_raw, ssd_conv_w, ssd_conv_b.reshape(1, SSD_CONV_CH), pad_h(dt_bias),
      pad_h(a_log), jnp.repeat(d_skip, SSD_HEAD_DIM).reshape(1, SSD_INNER),
      ssd_norm_g.reshape(1, SSD_INNER), expand)


def _xattn_body(x_ref, u_ref, y_ref, wu_ref, wy_ref, k_ref, v_ref, gx_ref, wq_ref, wo_ref, gm_ref,
                wr_ref, br_ref, x2_ref, h2_ref, route_ref, stats_ref):
    def tile(s):
        rows = slice(s * TOK_TILE, (s + 1) * TOK_TILE)
        return _xattn_tile(x_ref[rows, :], u_ref[rows, :], y_ref[rows, :], wu_ref, wy_ref, k_ref, v_ref,
                           gx_ref, wq_ref, wo_ref, gm_ref, wr_ref, br_ref)

    results = _interleave([tile(s) for s in range(XA_TILES)])
    for s in range(XA_TILES):
        rows = slice(s * TOK_TILE, (s + 1) * TOK_TILE)
        x2_ref[rows, :], h2_ref[rows, :], route_ref[rows, :], stats_ref[s] = results[s]


def _xattn_tile(x, u, y, wu_ref, wy_ref, k_ref, v_ref, gx_ref, wq_ref, wo_ref, gm_ref, wr_ref, br_ref):
    x = (x + jnp.dot(u, wu_ref[...], preferred_element_type=F32)
         + jnp.dot(y, wy_ref[...], preferred_element_type=F32))
    yield
    h = _rms(x, gx_ref[...]).astype(BF16)
    q = (jnp.dot(h, wq_ref[...], preferred_element_type=F32) * (XA_HEAD_DIM ** -0.5)).astype(BF16)
    yield
    heads = []
    for i in range(XA_HEADS):
        cols = slice(i * XA_HEAD_DIM, (i + 1) * XA_HEAD_DIM)
        s = jnp.dot(q[:, cols], k_ref[0, cols, :], preferred_element_type=F32)
        p = jnp.exp(s - jnp.max(s, axis=-1, keepdims=True))
        p = p / jnp.sum(p, axis=-1, keepdims=True)
        heads.append(jnp.dot(p.astype(BF16), v_ref[0, :, cols], preferred_element_type=F32))
        yield
    o = jnp.concatenate(heads, axis=1).astype(BF16)
    x2 = x + jnp.dot(o, wo_ref[...], preferred_element_type=F32)
    yield

    h2 = _rms(x2, gm_ref[...])
    h_hi = h2.astype(BF16)
    h_lo = (h2 - h_hi.astype(F32)).astype(BF16)
    both = jnp.dot(h_hi, wr_ref[...], preferred_element_type=F32)
    logits = (both[:, :ROUTE_LANES] + both[:, ROUTE_LANES:]
              + jnp.dot(h_lo, wr_ref[:, :ROUTE_LANES], preferred_element_type=F32)) + br_ref[...]
    yield
    lane = lax.broadcasted_iota(jnp.int32, logits.shape, 1)
    neg = -jnp.inf

    def first_argmax(v):
        m = jnp.max(v, axis=-1, keepdims=True)
        return m, jnp.min(jnp.where(v == m, lane, ROUTE_LANES), axis=-1, keepdims=True)

    gl = jnp.where(lane < N_GROUPS, logits, neg)
    g_max, g_sel = first_argmax(gl)
    p_top = 1.0 / jnp.sum(jnp.exp(gl - g_max), axis=-1, keepdims=True)
    e_lo = N_GROUPS + EXPERTS_PER_GROUP * g_sel
    el = jnp.where((lane >= e_lo) & (lane < e_lo + EXPERTS_PER_GROUP), logits, neg)
    m1, i1 = first_argmax(el)
    m2, i2 = first_argmax(jnp.where(lane == i1, neg, el))
    r = jnp.exp(m2 - m1)
    w1 = p_top / (1.0 + r)
    w2 = w1 * r
    yield
    e1 = i1 - N_GROUPS
    e2 = i2 - N_GROUPS
    oh1 = (lane == e1).astype(BF16)
    oh2 = (lane == e2).astype(BF16)
    n_t = logits.shape[0]
    before = (lax.broadcasted_iota(jnp.int32, (n_t, n_t), 0)
              > lax.broadcasted_iota(jnp.int32, (n_t, n_t), 1)).astype(BF16)
    c12 = jnp.dot(before, jnp.concatenate([oh1, oh2], axis=1), preferred_element_type=F32)
    c1 = c12[:, :ROUTE_LANES]
    c2 = c12[:, ROUTE_LANES:]
    tot1 = jnp.sum(oh1.astype(F32), axis=0, keepdims=True)
    cnt = tot1 + jnp.sum(oh2.astype(F32), axis=0, keepdims=True)
    cnt = jnp.floor((cnt + (SUBLANES - 1)) * (1.0 / SUBLANES)) * SUBLANES
    lanes_before = (lax.broadcasted_iota(jnp.int32, (ROUTE_LANES, ROUTE_LANES), 0)
                    < lax.broadcasted_iota(jnp.int32, (ROUTE_LANES, ROUTE_LANES), 1)).astype(F32)
    loff = jnp.dot(jnp.broadcast_to(cnt, (SUBLANES, ROUTE_LANES)), lanes_before,
                   preferred_element_type=F32, precision=lax.Precision.HIGHEST)[0:1, :]
    lp1 = jnp.sum(oh1.astype(F32) * (loff + c1), axis=-1, keepdims=True)
    lp2 = jnp.sum(oh2.astype(F32) * (loff + tot1 + c2), axis=-1, keepdims=True)
    route = jnp.where(lane == 0, e1.astype(F32),
            jnp.where(lane == 1, e2.astype(F32),
            jnp.where(lane == 2, w1,
            jnp.where(lane == 3, w2,
            jnp.where(lane == 4, lp1, jnp.where(lane == 5, lp2, 0.0))))))
    row = lax.broadcasted_iota(jnp.int32, (SUBLANES, ROUTE_LANES), 0)
    return x2, h_hi, route, jnp.where(row == 0, cnt, jnp.where(row == 1, loff, 0.0))


def _xattn_route(x, u, y, w_out, k, v, bsz, seqlen, g_xattn, w_q, w_o, g_moe, w_rg, b_rg, w_re, b_re):
    t, d = x.shape
    w_mix = w_out.astype(BF16)
    w_r = jnp.pad(jnp.concatenate([w_rg, w_re], axis=1), ((0, 0), (0, ROUTE_LANES - N_GROUPS - N_EXPERTS)))
    b_r = jnp.pad(jnp.concatenate([b_rg, b_re]), (0, ROUTE_LANES - N_GROUPS - N_EXPERTS)).reshape(1, ROUTE_LANES)
    wr_hi = w_r.astype(BF16)
    wr = jnp.concatenate([wr_hi, (w_r - wr_hi.astype(F32)).astype(BF16)], axis=1)
    step_rows = XA_TILES * TOK_TILE
    nt = seqlen // step_rows
    tile = pl.BlockSpec((step_rows, d), lambda b, j: (b * nt + j, 0))
    kv_spec = pl.BlockSpec((1, MEM_LEN, d), lambda b, j: (b, 0, 0))
    const = lambda shape: pl.BlockSpec(shape, lambda b, j: (0, 0))
    return pl.pallas_call(
        _xattn_body,
        grid=(bsz, nt),
        in_specs=[tile, tile, tile, const((d, d)), const((d, d)),
                  pl.BlockSpec((1, d, MEM_LEN), lambda b, j: (b, 0, 0)), kv_spec, const((1, d)),
                  const((d, d)), const((d, d)), const((1, d)),
                  const((d, 2 * ROUTE_LANES)), const((1, ROUTE_LANES))],
        out_specs=[tile, tile, pl.BlockSpec((step_rows, ROUTE_LANES), lambda b, j: (b * nt + j, 0)),
                   pl.BlockSpec((XA_TILES, SUBLANES, ROUTE_LANES), lambda b, j: (b * nt + j, 0, 0))],
        out_shape=[jax.ShapeDtypeStruct((t, d), F32), jax.ShapeDtypeStruct((t, d), BF16),
                   jax.ShapeDtypeStruct((t, ROUTE_LANES), F32),
                   jax.ShapeDtypeStruct((t // TOK_TILE, SUBLANES, ROUTE_LANES), F32)],
        compiler_params=_cparams(("arbitrary", "arbitrary")),
        name="xattn_route",
    )(x, u, y, w_mix[:CONV_CH], w_mix[CONV_CH:], jnp.swapaxes(k, 1, 2), v, g_xattn.reshape(1, d), w_q.astype(BF16), w_o.astype(BF16), g_moe.reshape(1, d),
      wr, b_r)


def _rows_copy(src_ref, src0, dst_ref, dst0, n, sem):
    rows = lambda r0: pl.ds(pl.multiple_of(r0, SUBLANES), pl.multiple_of(n, SUBLANES))
    return pltpu.make_async_copy(src_ref.at[rows(src0), :], dst_ref.at[rows(dst0), :], sem)


def _pack_pairs(v):
    half = v.shape[1] // 2
    lo = lax.bitcast_convert_type(v[:, :half], jnp.uint32)
    hi = lax.bitcast_convert_type(v[:, half:], jnp.uint32)
    return (lo >> 16) | (hi & jnp.uint32(0xFFFF0000))


def _unpack_pairs(w):
    lo = lax.bitcast_convert_type(w << 16, F32)
    hi = lax.bitcast_convert_type(w & jnp.uint32(0xFFFF0000), F32)
    return jnp.concatenate([lo, hi], axis=1).astype(BF16)


def _dispatch_body(cnt_ref, loff_ref, gb_ref, tot_ref, fstart_ref, fcnt_ref, nu_ref, h_ref, route_ref,
                   xs_ref, buf_ref, zero_ref, sem, zsem):
    i = pl.program_id(0)
    nt = pl.num_programs(0)
    slot = i % 2
    buf = buf_ref.at[slot]

    def drain(step, s):
        n = sum(tot_ref[step * MOE_TILES + k] for k in range(MOE_TILES))
        _rows_copy(buf_ref.at[s], 0, xs_ref, 0, n, sem.at[s]).wait()

    @pl.when(i >= 2)
    def _():
        drain(i - 2, slot)

    def sort_tile(k):
        rows = slice(k * TOK_TILE, (k + 1) * TOK_TILE)
        rt = route_ref[rows, :].T
        lp1 = rt[4:5, :].astype(jnp.int32)
        lp2 = rt[5:6, :].astype(jnp.int32)
        r_i = lax.broadcasted_iota(jnp.int32, (SORT_ROWS, TOK_TILE), 0)
        perm = jnp.where((r_i == lp1) | (r_i == lp2), 1.0, 0.0).astype(BF16)
        yield
        buf[k * SORT_ROWS:(k + 1) * SORT_ROWS, :] = _pack_pairs(
            jnp.dot(perm, h_ref[rows, :], preferred_element_type=F32))

    _interleave([sort_tile(k) for k in range(MOE_TILES)])

    for k in range(MOE_TILES):
        def per_expert(e, c, k=k):
            j = (i * MOE_TILES + k) * N_EXPERTS + e

            @pl.when(cnt_ref[j] > 0)
            def _():
                _rows_copy(buf, k * SORT_ROWS + loff_ref[j], xs_ref, gb_ref[j], cnt_ref[j],
                           sem.at[slot]).start()
            return c

        lax.fori_loop(0, N_EXPERTS, per_expert, 0)

    @pl.when(i == 0)
    def _():
        zero_ref[...] = jnp.zeros(zero_ref.shape, zero_ref.dtype)

        def fill(start):
            def body(e, c):
                @pl.when(fcnt_ref[e] > 0)
                def _():
                    copy = _rows_copy(zero_ref, 0, xs_ref, fstart_ref[e], fcnt_ref[e], zsem)
                    copy.start() if start else copy.wait()
                return c
            lax.fori_loop(0, N_EXPERTS, body, 0)

            def tail(blk, c):
                copy = _rows_copy(zero_ref, 0, xs_ref, blk * MOE_BLOCK, MOE_BLOCK, zsem)
                copy.start() if start else copy.wait()
                return c
            lax.fori_loop(nu_ref[0], xs_ref.shape[0] // MOE_BLOCK, tail, 0)

        fill(True)
        fill(False)

    @pl.when(i == nt - 1)
    def _():
        drain(i, slot)

        @pl.when(i >= 1)
        def _():
            drain(i - 1, 1 - slot)


def _dispatch(h2, route, tables, n_used, n_slots):
    t, d = h2.shape
    step_rows = MOE_TILES * TOK_TILE
    return pl.pallas_call(
        _dispatch_body,
        grid_spec=pltpu.PrefetchScalarGridSpec(
            num_scalar_prefetch=7,
            grid=(t // step_rows,),
            in_specs=[
                pl.BlockSpec((step_rows, d), lambda i, *_: (i, 0)),
                pl.BlockSpec((step_rows, ROUTE_LANES), lambda i, *_: (i, 0)),
            ],
            out_specs=pl.BlockSpec(memory_space=pl.ANY),
            scratch_shapes=[pltpu.VMEM((2, MOE_TILES * SORT_ROWS, d // 2), jnp.uint32),
                            pltpu.VMEM((MOE_BLOCK, d // 2), jnp.uint32),
                            pltpu.SemaphoreType.DMA((2,)), pltpu.SemaphoreType.DMA(())],
        ),
        out_shape=jax.ShapeDtypeStruct((n_slots, d // 2), jnp.uint32),
        compiler_params=_cparams(("arbitrary",)),
        name="dispatch",
    )(tables["cnt"], tables["loff"], tables["gb"], tables["tot"], tables["fill_start"],
      tables["fill_cnt"], n_used, h2, route)


def _experts_body(be_ref, nu_ref, first_ref, slot_ref, nxt_ref, x_ref, wg_hbm, wu_hbm, wd_hbm, y_ref,
                  wg_f, wu_f, wd_f, wg_bf, wu_bf, wd_bf, sem):
    b = pl.program_id(0)
    used = b < nu_ref[0]

    def weights(e, s):
        return [pltpu.make_async_copy(hbm.at[e], buf.at[s], sem.at[s])
                for hbm, buf in ((wg_hbm, wg_f), (wu_hbm, wu_f), (wd_hbm, wd_f))]

    @pl.when(used & (first_ref[b] == 1))
    def _():
        s = slot_ref[b]

        @pl.when(b == 0)
        def _():
            for copy in weights(be_ref[0], 0):
                copy.start()

        for copy in weights(be_ref[b], s):
            copy.wait()
        wg_bf[...] = wg_f[s].astype(BF16)
        wu_bf[...] = wu_f[s].astype(BF16)
        wd_bf[...] = wd_f[s].astype(BF16)

        @pl.when(nxt_ref[b] >= 0)
        def _():
            for copy in weights(nxt_ref[b], 1 - s):
                copy.start()

    @pl.when(used)
    def _():
        x = _unpack_pairs(x_ref[...])
        g = jnp.dot(x, wg_bf[...], preferred_element_type=F32)
        u = jnp.dot(x, wu_bf[...], preferred_element_type=F32)
        a = (g * _sigmoid(g) * u).astype(BF16)
        y = jnp.dot(a, wd_bf[...], preferred_element_type=F32)
        y_ref[...] = _pack_pairs(y.astype(BF16).astype(F32))

    @pl.when(jnp.logical_not(used))
    def _():
        y_ref[...] = jnp.zeros(y_ref.shape, y_ref.dtype)


def _experts(xs, etab, w_gate, w_up, w_down):
    n_slots, dp = xs.shape
    d = w_gate.shape[1]
    n_blocks = n_slots // MOE_BLOCK
    last = lambda b, nu: jnp.maximum(jnp.minimum(b, nu[0] - 1), 0)
    hbm = pl.BlockSpec(memory_space=pl.ANY)
    return pl.pallas_call(
        _experts_body,
        grid_spec=pltpu.PrefetchScalarGridSpec(
            num_scalar_prefetch=5,
            grid=(n_blocks,),
            in_specs=[pl.BlockSpec((MOE_BLOCK, dp), lambda b, be, nu, *_: (last(b, nu), 0)), hbm, hbm, hbm],
            out_specs=pl.BlockSpec((MOE_BLOCK, dp), lambda b, *_: (b, 0)),
            scratch_shapes=[pltpu.VMEM((2, d, D_EXPERT), F32), pltpu.VMEM((2, d, D_EXPERT), F32),
                            pltpu.VMEM((2, D_EXPERT, d), F32),
                            pltpu.VMEM((d, D_EXPERT), BF16), pltpu.VMEM((d, D_EXPERT), BF16),
                            pltpu.VMEM((D_EXPERT, d), BF16), pltpu.SemaphoreType.DMA((2,))],
        ),
        out_shape=jax.ShapeDtypeStruct((n_slots, dp), jnp.uint32),
        compiler_params=_cparams(("arbitrary",)),
        name="experts",
    )(etab["block_e"], etab["n_used"], etab["first"], etab["slot"], etab["nxt"], xs, w_gate, w_up, w_down)


def _combine_body(cnt_ref, loff_ref, gb_ref, tot_ref, x_ref, route_ref, g_ref, ys_ref, o_ref, ybuf_ref, sem):
    i = pl.program_id(0)
    nt = pl.num_programs(0)
    slot = i % 2

    def gather(step, s):
        for k in range(MOE_TILES):
            def per_expert(e, c, k=k):
                j = (step * MOE_TILES + k) * N_EXPERTS + e

                @pl.when(cnt_ref[j] > 0)
                def _():
                    _rows_copy(ys_ref, gb_ref[j], ybuf_ref.at[s], k * SORT_ROWS + loff_ref[j], cnt_ref[j],
                               sem.at[s]).start()
                return c

            lax.fori_loop(0, N_EXPERTS, per_expert, 0)

    @pl.when(i == 0)
    def _():
        ybuf_ref[...] = jnp.zeros(ybuf_ref.shape, ybuf_ref.dtype)
        gather(0, 0)

    @pl.when(i + 1 < nt)
    def _():
        gather(i + 1, 1 - slot)

    def combine_tile(k):
        rows = slice(k * TOK_TILE, (k + 1) * TOK_TILE)
        route = route_ref[rows, :]
        c_i = lax.broadcasted_iota(jnp.int32, (TOK_TILE, SORT_ROWS), 1)
        lp1 = route[:, 4:5].astype(jnp.int32)
        lp2 = route[:, 5:6].astype(jnp.int32)
        pw = jnp.where(c_i == lp1, route[:, 2:3], jnp.where(c_i == lp2, route[:, 3:4], 0.0)).astype(BF16)
        yield
        r_i = lax.broadcasted_iota(jnp.int32, (SORT_ROWS, 1), 0)
        y = _unpack_pairs(jnp.where(r_i < tot_ref[i * MOE_TILES + k],
                                    ybuf_ref[slot, k * SORT_ROWS:(k + 1) * SORT_ROWS, :], jnp.uint32(0)))
        moe = jnp.dot(pw, y, preferred_element_type=F32)
        yield
        o_ref[rows, :] = _rms(x_ref[rows, :] + moe, g_ref[...])

    tiles = [combine_tile(k) for k in range(MOE_TILES)]
    for tile in tiles:
        next(tile)
    n = sum(tot_ref[i * MOE_TILES + k] for k in range(MOE_TILES))
    _rows_copy(ys_ref, 0, ybuf_ref.at[slot], 0, n, sem.at[slot]).wait()
    _interleave(tiles)


def _combine(x2, route, tables, ys, g_final):
    t, d = x2.shape
    step_rows = MOE_TILES * TOK_TILE
    tile = pl.BlockSpec((step_rows, d), lambda i, *_: (i, 0))
    return pl.pallas_call(
        _combine_body,
        grid_spec=pltpu.PrefetchScalarGridSpec(
            num_scalar_prefetch=4,
            grid=(t // step_rows,),
            in_specs=[
                tile,
                pl.BlockSpec((step_rows, ROUTE_LANES), lambda i, *_: (i, 0)),
                pl.BlockSpec((1, d), lambda i, *_: (0, 0)),
                pl.BlockSpec(memory_space=pl.ANY),
            ],
            out_specs=tile,
            scratch_shapes=[pltpu.VMEM((2, MOE_TILES * SORT_ROWS, d // 2), jnp.uint32),
                            pltpu.SemaphoreType.DMA((2,))],
        ),
        out_shape=jax.ShapeDtypeStruct((t, d), F32),
        compiler_params=_cparams(("arbitrary",)),
        name="combine",
    )(tables["cnt"], tables["loff"], tables["gb"], tables["tot"], x2, route, g_final.reshape(1, d), ys)


def _routing_tables(stats, n_tok):
    cnt = stats[:, 0, :N_EXPERTS].astype(jnp.int32)
    loff = stats[:, 1, :N_EXPERTS].astype(jnp.int32)
    n_tiles = cnt.shape[0]
    counts = jnp.sum(cnt, axis=0)
    padded = ((counts + MOE_BLOCK - 1) // MOE_BLOCK) * MOE_BLOCK
    pad_end = jnp.cumsum(padded)
    pad_start = pad_end - padded
    gb = pad_start[None, :] + jnp.cumsum(cnt, axis=0) - cnt
    max_rows = (n_tok * TOP_K + n_tiles * N_EXPERTS * (SUBLANES - 1)
                + N_EXPERTS * (MOE_BLOCK - SUBLANES))
    n_blocks = -(-max_rows // MOE_BLOCK)
    n_slots = n_blocks * MOE_BLOCK
    block_start = jnp.arange(n_blocks, dtype=jnp.int32) * MOE_BLOCK
    block_e = jnp.minimum(jnp.sum((pad_end[None, :] <= block_start[:, None]).astype(jnp.int32), axis=1),
                          N_EXPERTS - 1)
    tables = dict(cnt=cnt.reshape(-1), loff=loff.reshape(-1), gb=gb.reshape(-1).astype(jnp.int32),
                  tot=jnp.sum(cnt, axis=1).astype(jnp.int32),
                  fill_start=(pad_start + counts).astype(jnp.int32),
                  fill_cnt=(padded - counts).astype(jnp.int32))
    n_used = (pad_end[-1:] // MOE_BLOCK).astype(jnp.int32)
    first = jnp.concatenate([jnp.ones((1,), jnp.int32), (block_e[1:] != block_e[:-1]).astype(jnp.int32)])
    seg_slot = (jnp.cumsum(first) - 1) % 2
    experts = jnp.arange(N_EXPERTS, dtype=jnp.int32)
    later = jnp.where((experts[None, :] > experts[:, None]) & (padded[None, :] > 0), experts[None, :], N_EXPERTS)
    next_e = jnp.min(later, axis=1)
    next_e = jnp.where(next_e < N_EXPERTS, next_e, -1)
    nxt = jnp.sum(jnp.where(block_e[:, None] == experts[None, :], next_e[None, :], 0), axis=1)
    etab = dict(block_e=block_e, n_used=n_used, first=first, slot=seg_slot.astype(jnp.int32),
                nxt=nxt.astype(jnp.int32))
    return tables, etab, n_slots


def kernel(x, mem, g_mix, w_in, conv_w, conv_b, ln_g, ln_b, ssd_conv_w, ssd_conv_b, dt_bias, a_log, d_skip, ssd_norm_g, w_out, g_xattn, g_mem, w_q, w_k, w_v, w_o, g_moe, w_router_group, b_router_group, w_router_expert, b_router_expert, w_gate, w_up, w_down, g_final):
    bsz, seqlen, d = x.shape
    n_tok = bsz * seqlen
    xt = x.reshape(n_tok, d)
    assert g_mix.shape[0] == 1, "the combine kernel applies the final norm: single layer only"
    for l in range(1):
        u, xbc, z, dt_raw = _mix_in(xt, bsz, seqlen, g_mix[l], w_in[l], conv_w[l], conv_b[l],
                                    ln_g[l], ln_b[l])
        y = _ssd(xbc, z, dt_raw, bsz, seqlen, ssd_conv_w[l], ssd_conv_b[l], dt_bias[l], a_log[l],
                 d_skip[l], ssd_norm_g[l])
        k, v = _kv_proj(mem, g_mem[l], w_k[l], w_v[l])
        x2, h2, route, stats = _xattn_route(xt, u, y, w_out[l], k, v, bsz, seqlen, g_xattn[l], w_q[l],
                                            w_o[l], g_moe[l], w_router_group[l], b_router_group[l],
                                            w_router_expert[l], b_router_expert[l])
        tables, etab, n_slots = _routing_tables(stats, n_tok)
        xs = _dispatch(h2, route, tables, etab["n_used"], n_slots)
        ys = _experts(xs, etab, w_gate[l], w_up[l], w_down[l])
        xt = _combine(x2, route, tables, ys, g_final)
    return xt.reshape(bsz, seqlen, d)
```

```python
import functools

import jax
import jax.numpy as jnp
from jax import lax
from jax.experimental import pallas as pl
from jax.experimental.pallas import tpu as pltpu

F32 = jnp.float32
BF16 = jnp.bfloat16

D_MODEL = 1024
CONV_CH = 1024
CONV_K = 31
SSD_INNER = 1024
SSD_HEAD_DIM = 64
SSD_HEADS = 16
SSD_STATE = 128
SSD_GROUPS = 2
SSD_GROUP_W = SSD_INNER // SSD_GROUPS
SSD_CONV_K = 4
SSD_CHUNK = 128
SSD_BC = SSD_GROUPS * SSD_STATE
SSD_CONV_CH = SSD_INNER + 2 * SSD_BC
N_MAIN = 2 * CONV_CH + 2 * SSD_INNER + 2 * SSD_BC
XA_HEADS = 4
XA_HEAD_DIM = 256
MEM_LEN = 256
N_GROUPS = 4
EXPERTS_PER_GROUP = 8
N_EXPERTS = 32
TOP_K = 2
D_EXPERT = 512
MOE_BLOCK = 512
RMS_EPS = 1e-6
LN_EPS = 1e-5
LOG2_E = 1.4426950408889634

LANES = 128
SUBLANES = 8
VMEM_LIMIT = 56 * 1024 * 1024

TOK_TILE = 512
CONV_TILE = 512
MM_CHUNK = 512
CONV_HALO = 32
CONV_ROWS = 128
SSD_TILE = 1024
MOE_TILES = 2
XA_TILES = 2
SSD_UNROLL = 8
SSD_HALO = 8
ROUTE_LANES = 128
SORT_ROWS = TOP_K * TOK_TILE + N_EXPERTS * SUBLANES


def _cparams(sem):
    return pltpu.CompilerParams(dimension_semantics=sem, vmem_limit_bytes=VMEM_LIMIT)


def _rms(x, g):
    return x * lax.rsqrt(jnp.mean(x * x, axis=-1, keepdims=True) + RMS_EPS) * g


def _sigmoid(x):
    return 1.0 / (1.0 + jnp.exp2(x * (-LOG2_E)))


def _interleave(stages):
    results = {}
    while len(results) < len(stages):
        for k, item in enumerate(stages):
            if k not in results:
                try:
                    next(item)
                except StopIteration as done:
                    results[k] = done.value
    return [results[k] for k in range(len(stages))]


def _kv_body(m_ref, g_ref, wk_ref, wv_ref, k_ref, v_ref):
    m = _rms(m_ref[0], g_ref[...]).astype(BF16)
    k_ref[0] = jnp.dot(m, wk_ref[...], preferred_element_type=F32).astype(BF16)
    v_ref[0] = jnp.dot(m, wv_ref[...], preferred_element_type=F32).astype(BF16)


def _kv_proj(mem, g_mem, w_k, w_v):
    b, s, d = mem.shape
    w_spec = pl.BlockSpec((d, d), lambda i: (0, 0))
    kv_spec = pl.BlockSpec((1, s, d), lambda i: (i, 0, 0))
    return pl.pallas_call(
        _kv_body,
        grid=(b,),
        in_specs=[kv_spec, pl.BlockSpec((1, d), lambda i: (0, 0)), w_spec, w_spec],
        out_specs=[kv_spec, kv_spec],
        out_shape=[jax.ShapeDtypeStruct((b, s, d), BF16)] * 2,
        compiler_params=_cparams(("arbitrary",)),
        name="kv_proj",
    )(mem, g_mem.reshape(1, d), w_k.astype(BF16), w_v.astype(BF16))


def _mix_in_body(x_ref, g_ref, w_ref, wdt_ref, cw_ref, cb_ref, lg_ref, lb_ref,
                 u_ref, xbc_ref, z_ref, dt_ref, ubuf_ref, acc_ref):
    @pl.when(pl.program_id(1) == 0)
    def _():
        ubuf_ref[0:CONV_HALO, :] = jnp.zeros((CONV_HALO, CONV_CH), F32)

    h = _rms(x_ref[...], g_ref[...]).astype(BF16)
    proj = lambda lo: jnp.dot(h, w_ref[:, lo:lo + MM_CHUNK], preferred_element_type=F32)
    for c in range(CONV_CH // MM_CHUNK):
        cols = slice(c * MM_CHUNK, (c + 1) * MM_CHUNK)
        ubuf_ref[CONV_HALO:CONV_HALO + CONV_TILE, cols] = (
            proj(c * MM_CHUNK) * _sigmoid(proj(CONV_CH + c * MM_CHUNK)))
    z0 = 2 * CONV_CH
    for c in range(SSD_INNER // MM_CHUNK):
        z_ref[:, c * MM_CHUNK:(c + 1) * MM_CHUNK] = proj(z0 + c * MM_CHUNK).astype(BF16)
    x0 = z0 + SSD_INNER
    for c in range(SSD_CONV_CH // MM_CHUNK):
        xbc_ref[:, c * MM_CHUNK:(c + 1) * MM_CHUNK] = proj(x0 + c * MM_CHUNK).astype(BF16)
    dt_ref[...] = jnp.dot(h, wdt_ref[...], preferred_element_type=F32)

    first = CONV_HALO - (CONV_K - 1)
    for rc in range(CONV_TILE // CONV_ROWS):
        r0 = rc * CONV_ROWS
        rows = slice(r0, r0 + CONV_ROWS)
        for cb in range(CONV_CH // LANES):
            cols = slice(cb * LANES, (cb + 1) * LANES)
            acc = None
            for res in range(SUBLANES):
                part = None
                for k in range(CONV_K):
                    if (first + k) % SUBLANES == res:
                        term = cw_ref[k:k + 1, cols] * ubuf_ref[r0 + first + k:r0 + first + k + CONV_ROWS, cols]
                        part = term if part is None else part + term
                acc = part if acc is None else acc + part
            acc_ref[rows, cols] = acc
        u = acc_ref[rows, :] + cb_ref[...]
        mu = jnp.mean(u, axis=-1, keepdims=True)
        uc = u - mu
        var = jnp.mean(uc * uc, axis=-1, keepdims=True)
        y = uc * lax.rsqrt(var + LN_EPS) * lg_ref[...] + lb_ref[...]
        u_ref[rows, :] = (y * _sigmoid(y)).astype(BF16)

    ubuf_ref[0:CONV_HALO, :] = ubuf_ref[CONV_TILE:CONV_TILE + CONV_HALO, :]


def _mix_in(x2d, bsz, seqlen, g_mix, w_in, conv_w, conv_b, ln_g, ln_b):
    t, d = x2d.shape
    nt = seqlen // CONV_TILE
    w_main = w_in.astype(BF16)
    w_dt = jnp.pad(w_in[:, N_MAIN:], ((0, 0), (0, LANES - SSD_HEADS))).astype(BF16)
    row = lambda v: v.reshape(1, -1)
    const = lambda shape: pl.BlockSpec(shape, lambda b, j: (0, 0))
    tile = lambda width: pl.BlockSpec((CONV_TILE, width), lambda b, j: (b * nt + j, 0))
    return pl.pallas_call(
        _mix_in_body,
        grid=(bsz, nt),
        in_specs=[
            tile(d), const((1, d)), const(w_in.shape), const((d, LANES)),
            const((CONV_K + 1, CONV_CH)), const((1, CONV_CH)), const((1, CONV_CH)), const((1, CONV_CH)),
        ],
        out_specs=[tile(CONV_CH), tile(SSD_CONV_CH), tile(SSD_INNER), tile(LANES)],
        out_shape=[jax.ShapeDtypeStruct((t, CONV_CH), BF16), jax.ShapeDtypeStruct((t, SSD_CONV_CH), BF16),
                   jax.ShapeDtypeStruct((t, SSD_INNER), BF16), jax.ShapeDtypeStruct((t, LANES), F32)],
        scratch_shapes=[
            pltpu.VMEM((CONV_HALO + CONV_TILE, CONV_CH), F32),
            pltpu.VMEM((CONV_TILE, CONV_CH), F32),
        ],
        compiler_params=_cparams(("arbitrary", "arbitrary")),
        name="mix_in",
    )(x2d, row(g_mix), w_main, w_dt, jnp.pad(conv_w, ((0, 1), (0, 0))), row(conv_b), row(ln_g), row(ln_b))


def _ssd_body(xbc_ref, z_ref, dt_ref, cw_ref, cb_ref, dtb_ref, alog_ref, dsk_ref, ng_ref,
              expand_ref, y_ref, xbuf_ref, act_ref, dts_ref, state_ref):
    @pl.when(pl.program_id(1) == 0)
    def _():
        xbuf_ref[0:SSD_HALO, :] = jnp.zeros((SSD_HALO, SSD_CONV_CH), F32)
        state_ref[...] = jnp.zeros(state_ref.shape, F32)

    xbuf_ref[SSD_HALO:SSD_HALO + SSD_TILE, :] = xbc_ref[...].astype(F32)
    full = xbuf_ref[...]
    conv = cb_ref[...] + cw_ref[SSD_CONV_K - 1:SSD_CONV_K, :] * full[SSD_HALO:, :]
    for back in range(1, SSD_CONV_K):
        past = pltpu.roll(full, back, 0)[SSD_HALO:, :]
        conv = conv + cw_ref[SSD_CONV_K - 1 - back:SSD_CONV_K - back, :] * past
    act_ref[...] = conv * _sigmoid(conv)
    xbuf_ref[0:SSD_HALO, :] = xbuf_ref[SSD_TILE:SSD_TILE + SSD_HALO, :]

    dt_in = dt_ref[...] + dtb_ref[...]
    dts_ref[...] = jnp.maximum(dt_in, 0.0) + jnp.log1p(jnp.exp(-jnp.abs(dt_in)))

    a_neg = -jnp.exp(alog_ref[...]) * LOG2_E
    q = SSD_CHUNK
    row_i = lax.broadcasted_iota(jnp.int32, (q, q), 0)
    col_i = lax.broadcasted_iota(jnp.int32, (q, q), 1)
    causal = row_i >= col_i
    tril = causal.astype(F32)
    lane_i = lax.broadcasted_iota(jnp.int32, (q, LANES), 1)
    low_half = lane_i < SSD_HEAD_DIM
    expand = expand_ref[...]

    def chunk(c, carry):
        r0 = pl.multiple_of(c * q, q)
        rows = pl.ds(r0, q)
        dtc = dts_ref[rows, :]
        a_cs = jnp.dot(tril, dtc * a_neg, preferred_element_type=F32,
                       precision=lax.Precision.HIGHEST)
        a_cs_t = a_cs.T
        dt_t = dtc.T
        a_end = a_cs[q - 1:q, :]
        e_exp = jnp.dot(jnp.exp2(a_cs).astype(BF16), expand, preferred_element_type=F32)
        w_exp = jnp.dot((jnp.exp2(a_end - a_cs) * dtc).astype(BF16), expand,
                        preferred_element_type=F32)
        dec_row = e_exp[q - 1:q, :]
        xc = act_ref[rows, 0:SSD_INNER]
        xw = (xc * w_exp).astype(BF16)
        y_parts = []
        for g in range(SSD_GROUPS):
            b_f = act_ref[rows, SSD_INNER + g * SSD_STATE:SSD_INNER + (g + 1) * SSD_STATE]
            c_f = act_ref[rows, SSD_INNER + SSD_BC + g * SSD_STATE:
                          SSD_INNER + SSD_BC + (g + 1) * SSD_STATE]
            b_g = b_f.astype(BF16)
            c_g = c_f.astype(BF16)
            cb = lax.dot_general(c_g, b_g, (((1,), (1,)), ((), ())),
                                 preferred_element_type=F32)
            gcols = slice(g * SSD_GROUP_W, (g + 1) * SSD_GROUP_W)
            st = state_ref[g]
            y_off = jnp.dot(c_g, st.astype(BF16), preferred_element_type=F32)
            state_ref[g] = st * dec_row[:, gcols] + jnp.dot(
                b_f.T.astype(BF16), xw[:, gcols], preferred_element_type=F32)
            for pair in range(SSD_GROUP_W // LANES):
                ms = []
                for hh in range(2):
                    h = g * (SSD_HEADS // SSD_GROUPS) + 2 * pair + hh
                    seg = a_cs[:, h:h + 1] - a_cs_t[h:h + 1, :]
                    dec = jnp.exp2(jnp.where(causal, seg, -jnp.inf))
                    ms.append((cb * dec * dt_t[h:h + 1, :]).astype(BF16))
                lhs = jnp.concatenate(ms, axis=1)
                xp = xc[:, g * SSD_GROUP_W + pair * LANES:g * SSD_GROUP_W + (pair + 1) * LANES]
                rhs = jnp.concatenate([jnp.where(low_half, xp, 0.0),
                                       jnp.where(low_half, 0.0, xp)], axis=0).astype(BF16)
                y_diag = jnp.dot(lhs, rhs, preferred_element_type=F32)
                lo = pair * LANES
                y_parts.append(y_diag + y_off[:, lo:lo + LANES]
                               * e_exp[:, g * SSD_GROUP_W + lo:g * SSD_GROUP_W + lo + LANES])
        y = jnp.concatenate(y_parts, axis=1) + xc * dsk_ref[...]
        z = z_ref[rows, :].astype(F32)
        y = y * (z * _sigmoid(z))
        outs = []
        for g in range(SSD_GROUPS):
            yg = y[:, g * SSD_GROUP_W:(g + 1) * SSD_GROUP_W]
            outs.append(yg * lax.rsqrt(jnp.mean(yg * yg, axis=-1, keepdims=True) + RMS_EPS))
        y_ref[rows, :] = (jnp.concatenate(outs, axis=1) * ng_ref[...]).astype(BF16)
        return carry

    lax.fori_loop(0, SSD_TILE // q, chunk, 0, unroll=SSD_UNROLL)


def _ssd(xbc, z, dt_raw, bsz, seqlen, ssd_conv_w, ssd_conv_b, dt_bias, a_log, d_skip, ssd_norm_g):
    nt = seqlen // SSD_TILE
    pad_h = lambda v: jnp.pad(v, (0, LANES - SSD_HEADS)).reshape(1, LANES)
    expand = (jnp.arange(LANES)[:, None] == (jnp.arange(SSD_INNER) // SSD_HEAD_DIM)[None, :]).astype(BF16)
    const = lambda shape: pl.BlockSpec(shape, lambda b, j: (0, 0))
    return pl.pallas_call(
        _ssd_body,
        grid=(bsz, nt),
        in_specs=[
            pl.BlockSpec((SSD_TILE, SSD_CONV_CH), lambda b, j: (b * nt + j, 0)),
            pl.BlockSpec((SSD_TILE, SSD_INNER), lambda b, j: (b * nt + j, 0)),
            pl.BlockSpec((SSD_TILE, LANES), lambda b, j: (b * nt + j, 0)),
            const((SSD_CONV_K, SSD_CONV_CH)),
            const((1, SSD_CONV_CH)),
            const((1, LANES)), const((1, LANES)),
            const((1, SSD_INNER)), const((1, SSD_INNER)),
            const((LANES, SSD_INNER)),
        ],
        out_specs=pl.BlockSpec((SSD_TILE, SSD_INNER), lambda b, j: (b * nt + j, 0)),
        out_shape=jax.ShapeDtypeStruct((bsz * seqlen, SSD_INNER), BF16),
        scratch_shapes=[
            pltpu.VMEM((SSD_HALO + SSD_TILE, SSD_CONV_CH), F32),
            pltpu.VMEM((SSD_TILE, SSD_CONV_CH), F32),
            pltpu.VMEM((SSD_TILE, LANES), F32),
            pltpu.VMEM((SSD_GROUPS, SSD_STATE, SSD_GROUP_W), F32),
        ],
        compiler_params=_cparams(("arbitrary", "arbitrary")),
        name="ssd",
    )(xbc, z, dt_raw, ssd_conv_w, ssd_conv_b.reshape(1, SSD_CONV_CH), pad_h(dt_bias),
      pad_h(a_log), jnp.repeat(d_skip, SSD_HEAD_DIM).reshape(1, SSD_INNER),
      ssd_norm_g.reshape(1, SSD_INNER), expand)


def _xattn_body(x_ref, u_ref, y_ref, wu_ref, wy_ref, k_ref, v_ref, gx_ref, wq_ref, wo_ref, gm_ref,
                wr_ref, br_ref, x2_ref, h2_ref, route_ref, stats_ref):
    def tile(s):
        rows = slice(s * TOK_TILE, (s + 1) * TOK_TILE)
        return _xattn_tile(x_ref[rows, :], u_ref[rows, :], y_ref[rows, :], wu_ref, wy_ref, k_ref, v_ref,
                           gx_ref, wq_ref, wo_ref, gm_ref, wr_ref, br_ref)

    results = _interleave([tile(s) for s in range(XA_TILES)])
    for s in range(XA_TILES):
        rows = slice(s * TOK_TILE, (s + 1) * TOK_TILE)
        x2_ref[rows, :], h2_ref[rows, :], route_ref[rows, :], stats_ref[s] = results[s]


def _xattn_tile(x, u, y, wu_ref, wy_ref, k_ref, v_ref, gx_ref, wq_ref, wo_ref, gm_ref, wr_ref, br_ref):
    x = (x + jnp.dot(u, wu_ref[...], preferred_element_type=F32)
         + jnp.dot(y, wy_ref[...], preferred_element_type=F32))
    yield
    h = _rms(x, gx_ref[...]).astype(BF16)
    q = (jnp.dot(h, wq_ref[...], preferred_element_type=F32) * (XA_HEAD_DIM ** -0.5)).astype(BF16)
    yield
    heads = []
    for i in range(XA_HEADS):
        cols = slice(i * XA_HEAD_DIM, (i + 1) * XA_HEAD_DIM)
        s = jnp.dot(q[:, cols], k_ref[0, cols, :], preferred_element_type=F32)
        p = jnp.exp(s - jnp.max(s, axis=-1, keepdims=True))
        p = p / jnp.sum(p, axis=-1, keepdims=True)
        heads.append(jnp.dot(p.astype(BF16), v_ref[0, :, cols], preferred_element_type=F32))
        yield
    o = jnp.concatenate(heads, axis=1).astype(BF16)
    x2 = x + jnp.dot(o, wo_ref[...], preferred_element_type=F32)
    yield

    h2 = _rms(x2, gm_ref[...])
    h_hi = h2.astype(BF16)
    h_lo = (h2 - h_hi.astype(F32)).astype(BF16)
    both = jnp.dot(h_hi, wr_ref[...], preferred_element_type=F32)
    logits = (both[:, :ROUTE_LANES] + both[:, ROUTE_LANES:]
              + jnp.dot(h_lo, wr_ref[:, :ROUTE_LANES], preferred_element_type=F32)) + br_ref[...]
    yield
    lane = lax.broadcasted_iota(jnp.int32, logits.shape, 1)
    neg = -jnp.inf

    def first_argmax(v):
        m = jnp.max(v, axis=-1, keepdims=True)
        return m, jnp.min(jnp.where(v == m, lane, ROUTE_LANES), axis=-1, keepdims=True)

    gl = jnp.where(lane < N_GROUPS, logits, neg)
    g_max, g_sel = first_argmax(gl)
    p_top = 1.0 / jnp.sum(jnp.exp(gl - g_max), axis=-1, keepdims=True)
    e_lo = N_GROUPS + EXPERTS_PER_GROUP * g_sel
    el = jnp.where((lane >= e_lo) & (lane < e_lo + EXPERTS_PER_GROUP), logits, neg)
    m1, i1 = first_argmax(el)
    m2, i2 = first_argmax(jnp.where(lane == i1, neg, el))
    r = jnp.exp(m2 - m1)
    w1 = p_top / (1.0 + r)
    w2 = w1 * r
    yield
    e1 = i1 - N_GROUPS
    e2 = i2 - N_GROUPS
    oh1 = (lane == e1).astype(BF16)
    oh2 = (lane == e2).astype(BF16)
    n_t = logits.shape[0]
    before = (lax.broadcasted_iota(jnp.int32, (n_t, n_t), 0)
              > lax.broadcasted_iota(jnp.int32, (n_t, n_t), 1)).astype(BF16)
    c12 = jnp.dot(before, jnp.concatenate([oh1, oh2], axis=1), preferred_element_type=F32)
    c1 = c12[:, :ROUTE_LANES]
    c2 = c12[:, ROUTE_LANES:]
    tot1 = jnp.sum(oh1.astype(F32), axis=0, keepdims=True)
    cnt = tot1 + jnp.sum(oh2.astype(F32), axis=0, keepdims=True)
    cnt = jnp.floor((cnt + (SUBLANES - 1)) * (1.0 / SUBLANES)) * SUBLANES
    lanes_before = (lax.broadcasted_iota(jnp.int32, (ROUTE_LANES, ROUTE_LANES), 0)
                    < lax.broadcasted_iota(jnp.int32, (ROUTE_LANES, ROUTE_LANES), 1)).astype(F32)
    loff = jnp.dot(jnp.broadcast_to(cnt, (SUBLANES, ROUTE_LANES)), lanes_before,
                   preferred_element_type=F32, precision=lax.Precision.HIGHEST)[0:1, :]
    lp1 = jnp.sum(oh1.astype(F32) * (loff + c1), axis=-1, keepdims=True)
    lp2 = jnp.sum(oh2.astype(F32) * (loff + tot1 + c2), axis=-1, keepdims=True)
    route = jnp.where(lane == 0, e1.astype(F32),
            jnp.where(lane == 1, e2.astype(F32),
            jnp.where(lane == 2, w1,
            jnp.where(lane == 3, w2,
            jnp.where(lane == 4, lp1, jnp.where(lane == 5, lp2, 0.0))))))
    row = lax.broadcasted_iota(jnp.int32, (SUBLANES, ROUTE_LANES), 0)
    return x2, h_hi, route, jnp.where(row == 0, cnt, jnp.where(row == 1, loff, 0.0))


def _xattn_route(x, u, y, w_out, k, v, bsz, seqlen, g_xattn, w_q, w_o, g_moe, w_rg, b_rg, w_re, b_re):
    t, d = x.shape
    w_mix = w_out.astype(BF16)
    w_r = jnp.pad(jnp.concatenate([w_rg, w_re], axis=1), ((0, 0), (0, ROUTE_LANES - N_GROUPS - N_EXPERTS)))
    b_r = jnp.pad(jnp.concatenate([b_rg, b_re]), (0, ROUTE_LANES - N_GROUPS - N_EXPERTS)).reshape(1, ROUTE_LANES)
    wr_hi = w_r.astype(BF16)
    wr = jnp.concatenate([wr_hi, (w_r - wr_hi.astype(F32)).astype(BF16)], axis=1)
    step_rows = XA_TILES * TOK_TILE
    nt = seqlen // step_rows
    tile = pl.BlockSpec((step_rows, d), lambda b, j: (b * nt + j, 0))
    kv_spec = pl.BlockSpec((1, MEM_LEN, d), lambda b, j: (b, 0, 0))
    const = lambda shape: pl.BlockSpec(shape, lambda b, j: (0, 0))
    return pl.pallas_call(
        _xattn_body,
        grid=(bsz, nt),
        in_specs=[tile, tile, tile, const((d, d)), const((d, d)),
                  pl.BlockSpec((1, d, MEM_LEN), lambda b, j: (b, 0, 0)), kv_spec, const((1, d)),
                  const((d, d)), const((d, d)), const((1, d)),
                  const((d, 2 * ROUTE_LANES)), const((1, ROUTE_LANES))],
        out_specs=[tile, tile, pl.BlockSpec((step_rows, ROUTE_LANES), lambda b, j: (b * nt + j, 0)),
                   pl.BlockSpec((XA_TILES, SUBLANES, ROUTE_LANES), lambda b, j: (b * nt + j, 0, 0))],
        out_shape=[jax.ShapeDtypeStruct((t, d), F32), jax.ShapeDtypeStruct((t, d), BF16),
                   jax.ShapeDtypeStruct((t, ROUTE_LANES), F32),
                   jax.ShapeDtypeStruct((t // TOK_TILE, SUBLANES, ROUTE_LANES), F32)],
        compiler_params=_cparams(("arbitrary", "arbitrary")),
        name="xattn_route",
    )(x, u, y, w_mix[:CONV_CH], w_mix[CONV_CH:], jnp.swapaxes(k, 1, 2), v, g_xattn.reshape(1, d), w_q.astype(BF16), w_o.astype(BF16), g_moe.reshape(1, d),
      wr, b_r)


def _rows_copy(src_ref, src0, dst_ref, dst0, n, sem):
    rows = lambda r0: pl.ds(pl.multiple_of(r0, SUBLANES), pl.multiple_of(n, SUBLANES))
    return pltpu.make_async_copy(src_ref.at[rows(src0), :], dst_ref.at[rows(dst0), :], sem)


def _pack_pairs(v):
    half = v.shape[1] // 2
    lo = lax.bitcast_convert_type(v[:, :half], jnp.uint32)
    hi = lax.bitcast_convert_type(v[:, half:], jnp.uint32)
    return (lo >> 16) | (hi & jnp.uint32(0xFFFF0000))


def _unpack_pairs(w):
    lo = lax.bitcast_convert_type(w << 16, F32)
    hi = lax.bitcast_convert_type(w & jnp.uint32(0xFFFF0000), F32)
    return jnp.concatenate([lo, hi], axis=1).astype(BF16)


def _dispatch_body(cnt_ref, loff_ref, gb_ref, tot_ref, fstart_ref, fcnt_ref, nu_ref, h_ref, route_ref,
                   xs_ref, buf_ref, zero_ref, sem, zsem):
    i = pl.program_id(0)
    nt = pl.num_programs(0)
    slot = i % 2
    buf = buf_ref.at[slot]

    def drain(step, s):
        n = sum(tot_ref[step * MOE_TILES + k] for k in range(MOE_TILES))
        _rows_copy(buf_ref.at[s], 0, xs_ref, 0, n, sem.at[s]).wait()

    @pl.when(i >= 2)
    def _():
        drain(i - 2, slot)

    def sort_tile(k):
        rows = slice(k * TOK_TILE, (k + 1) * TOK_TILE)
        rt = route_ref[rows, :].T
        lp1 = rt[4:5, :].astype(jnp.int32)
        lp2 = rt[5:6, :].astype(jnp.int32)
        r_i = lax.broadcasted_iota(jnp.int32, (SORT_ROWS, TOK_TILE), 0)
        perm = jnp.where((r_i == lp1) | (r_i == lp2), 1.0, 0.0).astype(BF16)
        yield
        buf[k * SORT_ROWS:(k + 1) * SORT_ROWS, :] = _pack_pairs(
            jnp.dot(perm, h_ref[rows, :], preferred_element_type=F32))

    _interleave([sort_tile(k) for k in range(MOE_TILES)])

    for k in range(MOE_TILES):
        def per_expert(e, c, k=k):
            j = (i * MOE_TILES + k) * N_EXPERTS + e

            @pl.when(cnt_ref[j] > 0)
            def _():
                _rows_copy(buf, k * SORT_ROWS + loff_ref[j], xs_ref, gb_ref[j], cnt_ref[j],
                           sem.at[slot]).start()
            return c

        lax.fori_loop(0, N_EXPERTS, per_expert, 0)

    @pl.when(i == 0)
    def _():
        zero_ref[...] = jnp.zeros(zero_ref.shape, zero_ref.dtype)

        def fill(start):
            def body(e, c):
                @pl.when(fcnt_ref[e] > 0)
                def _():
                    copy = _rows_copy(zero_ref, 0, xs_ref, fstart_ref[e], fcnt_ref[e], zsem)
                    copy.start() if start else copy.wait()
                return c
            lax.fori_loop(0, N_EXPERTS, body, 0)

            def tail(blk, c):
                copy = _rows_copy(zero_ref, 0, xs_ref, blk * MOE_BLOCK, MOE_BLOCK, zsem)
                copy.start() if start else copy.wait()
                return c
            lax.fori_loop(nu_ref[0], xs_ref.shape[0] // MOE_BLOCK, tail, 0)

        fill(True)
        fill(False)

    @pl.when(i == nt - 1)
    def _():
        drain(i, slot)

        @pl.when(i >= 1)
        def _():
            drain(i - 1, 1 - slot)


def _dispatch(h2, route, tables, n_used, n_slots):
    t, d = h2.shape
    step_rows = MOE_TILES * TOK_TILE
    return pl.pallas_call(
        _dispatch_body,
        grid_spec=pltpu.PrefetchScalarGridSpec(
            num_scalar_prefetch=7,
            grid=(t // step_rows,),
            in_specs=[
                pl.BlockSpec((step_rows, d), lambda i, *_: (i, 0)),
                pl.BlockSpec((step_rows, ROUTE_LANES), lambda i, *_: (i, 0)),
            ],
            out_specs=pl.BlockSpec(memory_space=pl.ANY),
            scratch_shapes=[pltpu.VMEM((2, MOE_TILES * SORT_ROWS, d // 2), jnp.uint32),
                            pltpu.VMEM((MOE_BLOCK, d // 2), jnp.uint32),
                            pltpu.SemaphoreType.DMA((2,)), pltpu.SemaphoreType.DMA(())],
        ),
        out_shape=jax.ShapeDtypeStruct((n_slots, d // 2), jnp.uint32),
        compiler_params=_cparams(("arbitrary",)),
        name="dispatch",
    )(tables["cnt"], tables["loff"], tables["gb"], tables["tot"], tables["fill_start"],
      tables["fill_cnt"], n_used, h2, route)


def _experts_body(be_ref, nu_ref, first_ref, slot_ref, nxt_ref, x_ref, wg_hbm, wu_hbm, wd_hbm, y_ref,
                  wg_f, wu_f, wd_f, wg_bf, wu_bf, wd_bf, sem):
    b = pl.program_id(0)
    used = b < nu_ref[0]

    def weights(e, s):
        return [pltpu.make_async_copy(hbm.at[e], buf.at[s], sem.at[s])
                for hbm, buf in ((wg_hbm, wg_f), (wu_hbm, wu_f), (wd_hbm, wd_f))]

    @pl.when(used & (first_ref[b] == 1))
    def _():
        s = slot_ref[b]

        @pl.when(b == 0)
        def _():
            for copy in weights(be_ref[0], 0):
                copy.start()

        for copy in weights(be_ref[b], s):
            copy.wait()
        wg_bf[...] = wg_f[s].astype(BF16)
        wu_bf[...] = wu_f[s].astype(BF16)
        wd_bf[...] = wd_f[s].astype(BF16)

        @pl.when(nxt_ref[b] >= 0)
        def _():
            for copy in weights(nxt_ref[b], 1 - s):
                copy.start()

    @pl.when(used)
    def _():
        x = _unpack_pairs(x_ref[...])
        g = jnp.dot(x, wg_bf[...], preferred_element_type=F32)
        u = jnp.dot(x, wu_bf[...], preferred_element_type=F32)
        a = (g * _sigmoid(g) * u).astype(BF16)
        y = jnp.dot(a, wd_bf[...], preferred_element_type=F32)
        y_ref[...] = _pack_pairs(y.astype(BF16).astype(F32))

    @pl.when(jnp.logical_not(used))
    def _():
        y_ref[...] = jnp.zeros(y_ref.shape, y_ref.dtype)


def _experts(xs, etab, w_gate, w_up, w_down):
    n_slots, dp = xs.shape
    d = w_gate.shape[1]
    n_blocks = n_slots // MOE_BLOCK
    last = lambda b, nu: jnp.maximum(jnp.minimum(b, nu[0] - 1), 0)
    hbm = pl.BlockSpec(memory_space=pl.ANY)
    return pl.pallas_call(
        _experts_body,
        grid_spec=pltpu.PrefetchScalarGridSpec(
            num_scalar_prefetch=5,
            grid=(n_blocks,),
            in_specs=[pl.BlockSpec((MOE_BLOCK, dp), lambda b, be, nu, *_: (last(b, nu), 0)), hbm, hbm, hbm],
            out_specs=pl.BlockSpec((MOE_BLOCK, dp), lambda b, *_: (b, 0)),
            scratch_shapes=[pltpu.VMEM((2, d, D_EXPERT), F32), pltpu.VMEM((2, d, D_EXPERT), F32),
                            pltpu.VMEM((2, D_EXPERT, d), F32),
                            pltpu.VMEM((d, D_EXPERT), BF16), pltpu.VMEM((d, D_EXPERT), BF16),
                            pltpu.VMEM((D_EXPERT, d), BF16), pltpu.SemaphoreType.DMA((2,))],
        ),
        out_shape=jax.ShapeDtypeStruct((n_slots, dp), jnp.uint32),
        compiler_params=_cparams(("arbitrary",)),
        name="experts",
    )(etab["block_e"], etab["n_used"], etab["first"], etab["slot"], etab["nxt"], xs, w_gate, w_up, w_down)


def _combine_body(cnt_ref, loff_ref, gb_ref, tot_ref, x_ref, route_ref, g_ref, ys_ref, o_ref, ybuf_ref, sem):
    i = pl.program_id(0)
    nt = pl.num_programs(0)
    slot = i % 2

    def gather(step, s):
        for k in range(MOE_TILES):
            def per_expert(e, c, k=k):
                j = (step * MOE_TILES + k) * N_EXPERTS + e

                @pl.when(cnt_ref[j] > 0)
                def _():
                    _rows_copy(ys_ref, gb_ref[j], ybuf_ref.at[s], k * SORT_ROWS + loff_ref[j], cnt_ref[j],
                               sem.at[s]).start()
                return c

            lax.fori_loop(0, N_EXPERTS, per_expert, 0)

    @pl.when(i == 0)
    def _():
        ybuf_ref[...] = jnp.zeros(ybuf_ref.shape, ybuf_ref.dtype)
        gather(0, 0)

    @pl.when(i + 1 < nt)
    def _():
        gather(i + 1, 1 - slot)

    def combine_tile(k):
        rows = slice(k * TOK_TILE, (k + 1) * TOK_TILE)
        route = route_ref[rows, :]
        c_i = lax.broadcasted_iota(jnp.int32, (TOK_TILE, SORT_ROWS), 1)
        lp1 = route[:, 4:5].astype(jnp.int32)
        lp2 = route[:, 5:6].astype(jnp.int32)
        pw = jnp.where(c_i == lp1, route[:, 2:3], jnp.where(c_i == lp2, route[:, 3:4], 0.0)).astype(BF16)
        yield
        r_i = lax.broadcasted_iota(jnp.int32, (SORT_ROWS, 1), 0)
        y = _unpack_pairs(jnp.where(r_i < tot_ref[i * MOE_TILES + k],
                                    ybuf_ref[slot, k * SORT_ROWS:(k + 1) * SORT_ROWS, :], jnp.uint32(0)))
        moe = jnp.dot(pw, y, preferred_element_type=F32)
        yield
        o_ref[rows, :] = _rms(x_ref[rows, :] + moe, g_ref[...])

    tiles = [combine_tile(k) for k in range(MOE_TILES)]
    for tile in tiles:
        next(tile)
    n = sum(tot_ref[i * MOE_TILES + k] for k in range(MOE_TILES))
    _rows_copy(ys_ref, 0, ybuf_ref.at[slot], 0, n, sem.at[slot]).wait()
    _interleave(tiles)


def _combine(x2, route, tables, ys, g_final):
    t, d = x2.shape
    step_rows = MOE_TILES * TOK_TILE
    tile = pl.BlockSpec((step_rows, d), lambda i, *_: (i, 0))
    return pl.pallas_call(
        _combine_body,
        grid_spec=pltpu.PrefetchScalarGridSpec(
            num_scalar_prefetch=4,
            grid=(t // step_rows,),
            in_specs=[
                tile,
                pl.BlockSpec((step_rows, ROUTE_LANES), lambda i, *_: (i, 0)),
                pl.BlockSpec((1, d), lambda i, *_: (0, 0)),
                pl.BlockSpec(memory_space=pl.ANY),
            ],
            out_specs=tile,
            scratch_shapes=[pltpu.VMEM((2, MOE_TILES * SORT_ROWS, d // 2), jnp.uint32),
                            pltpu.SemaphoreType.DMA((2,))],
        ),
        out_shape=jax.ShapeDtypeStruct((t, d), F32),
        compiler_params=_cparams(("arbitrary",)),
        name="combine",
    )(tables["cnt"], tables["loff"], tables["gb"], tables["tot"], x2, route, g_final.reshape(1, d), ys)


def _routing_tables(stats, n_tok):
    cnt = stats[:, 0, :N_EXPERTS].astype(jnp.int32)
    loff = stats[:, 1, :N_EXPERTS].astype(jnp.int32)
    n_tiles = cnt.shape[0]
    counts = jnp.sum(cnt, axis=0)
    padded = ((counts + MOE_BLOCK - 1) // MOE_BLOCK) * MOE_BLOCK
    pad_end = jnp.cumsum(padded)
    pad_start = pad_end - padded
    gb = pad_start[None, :] + jnp.cumsum(cnt, axis=0) - cnt
    max_rows = (n_tok * TOP_K + n_tiles * N_EXPERTS * (SUBLANES - 1)
                + N_EXPERTS * (MOE_BLOCK - SUBLANES))
    n_blocks = -(-max_rows // MOE_BLOCK)
    n_slots = n_blocks * MOE_BLOCK
    block_start = jnp.arange(n_blocks, dtype=jnp.int32) * MOE_BLOCK
    block_e = jnp.minimum(jnp.sum((pad_end[None, :] <= block_start[:, None]).astype(jnp.int32), axis=1),
                          N_EXPERTS - 1)
    tables = dict(cnt=cnt.reshape(-1), loff=loff.reshape(-1), gb=gb.reshape(-1).astype(jnp.int32),
                  tot=jnp.sum(cnt, axis=1).astype(jnp.int32),
                  fill_start=(pad_start + counts).astype(jnp.int32),
                  fill_cnt=(padded - counts).astype(jnp.int32))
    n_used = (pad_end[-1:] // MOE_BLOCK).astype(jnp.int32)
    first = jnp.concatenate([jnp.ones((1,), jnp.int32), (block_e[1:] != block_e[:-1]).astype(jnp.int32)])
    seg_slot = (jnp.cumsum(first) - 1) % 2
    experts = jnp.arange(N_EXPERTS, dtype=jnp.int32)
    later = jnp.where((experts[None, :] > experts[:, None]) & (padded[None, :] > 0), experts[None, :], N_EXPERTS)
    next_e = jnp.min(later, axis=1)
    next_e = jnp.where(next_e < N_EXPERTS, next_e, -1)
    nxt = jnp.sum(jnp.where(block_e[:, None] == experts[None, :], next_e[None, :], 0), axis=1)
    etab = dict(block_e=block_e, n_used=n_used, first=first, slot=seg_slot.astype(jnp.int32),
                nxt=nxt.astype(jnp.int32))
    return tables, etab, n_slots


def kernel(x, mem, g_mix, w_in, conv_w, conv_b, ln_g, ln_b, ssd_conv_w, ssd_conv_b, dt_bias, a_log, d_skip, ssd_norm_g, w_out, g_xattn, g_mem, w_q, w_k, w_v, w_o, g_moe, w_router_group, b_router_group, w_router_expert, b_router_expert, w_gate, w_up, w_down, g_final):
    bsz, seqlen, d = x.shape
    n_tok = bsz * seqlen
    xt = x.reshape(n_tok, d)
    assert g_mix.shape[0] == 1, "the combine kernel applies the final norm: single layer only"
    for l in range(1):
        u, xbc, z, dt_raw = _mix_in(xt, bsz, seqlen, g_mix[l], w_in[l], conv_w[l], conv_b[l],
                                    ln_g[l], ln_b[l])
        y = _ssd(xbc, z, dt_raw, bsz, seqlen, ssd_conv_w[l], ssd_conv_b[l], dt_bias[l], a_log[l],
                 d_skip[l], ssd_norm_g[l])
        k, v = _kv_proj(mem, g_mem[l], w_k[l], w_v[l])
        x2, h2, route, stats = _xattn_route(xt, u, y, w_out[l], k, v, bsz, seqlen, g_xattn[l], w_q[l],
                                            w_o[l], g_moe[l], w_router_group[l], b_router_group[l],
                                            w_router_expert[l], b_router_expert[l])
        tables, etab, n_slots = _routing_tables(stats, n_tok)
        xs = _dispatch(h2, route, tables, etab["n_used"], n_slots)
        ys = _experts(xs, etab, w_gate[l], w_up[l], w_down[l])
        xt = _combine(x2, route, tables, ys, g_final)
    return xt.reshape(bsz, seqlen, d)
```

```python
import functools

import jax
import jax.numpy as jnp
from jax import lax
from jax.experimental import pallas as pl
from jax.experimental.pallas import tpu as pltpu

F32 = jnp.float32
BF16 = jnp.bfloat16

D_MODEL = 1024
CONV_CH = 1024
CONV_K = 31
SSD_INNER = 1024
SSD_HEAD_DIM = 64
SSD_HEADS = 16
SSD_STATE = 128
SSD_GROUPS = 2
SSD_GROUP_W = SSD_INNER // SSD_GROUPS
SSD_CONV_K = 4
SSD_CHUNK = 128
SSD_BC = SSD_GROUPS * SSD_STATE
SSD_CONV_CH = SSD_INNER + 2 * SSD_BC
N_MAIN = 2 * CONV_CH + 2 * SSD_INNER + 2 * SSD_BC
XA_HEADS = 4
XA_HEAD_DIM = 256
MEM_LEN = 256
N_GROUPS = 4
EXPERTS_PER_GROUP = 8
N_EXPERTS = 32
TOP_K = 2
D_EXPERT = 512
MOE_BLOCK = 512
RMS_EPS = 1e-6
LN_EPS = 1e-5
LOG2_E = 1.4426950408889634

LANES = 128
SUBLANES = 8
VMEM_LIMIT = 56 * 1024 * 1024

TOK_TILE = 512
CONV_TILE = 512
ZX_Z0 = -(-SSD_CONV_CH // SSD_INNER) * SSD_INNER
ZX_W = ZX_Z0 + SSD_INNER
ZX_SHARES = (0, 512, 512, 512, 512, 512, 512, 0)
assert sum(ZX_SHARES) == ZX_W and len(ZX_SHARES) == CONV_CH // LANES
CONV_HALO = 32
CONV_ROWS = 128
SSD_TILE = 1024
MOE_TILES = 2
XA_TILES = 2
SSD_UNROLL = 8
SSD_HALO = 8
ROUTE_LANES = 128
SORT_ROWS = TOP_K * TOK_TILE + N_EXPERTS * SUBLANES


def _cparams(sem):
    return pltpu.CompilerParams(dimension_semantics=sem, vmem_limit_bytes=VMEM_LIMIT)


def _rms(x, g):
    return x * lax.rsqrt(jnp.mean(x * x, axis=-1, keepdims=True) + RMS_EPS) * g


def _sigmoid(x):
    return 1.0 / (1.0 + jnp.exp2(x * (-LOG2_E)))


def _interleave(stages):
    results = {}
    while len(results) < len(stages):
        for k, item in enumerate(stages):
            if k not in results:
                try:
                    next(item)
                except StopIteration as done:
                    results[k] = done.value
    return [results[k] for k in range(len(stages))]


def _kv_body(m_ref, g_ref, wk_ref, wv_ref, k_ref, v_ref):
    m = _rms(m_ref[0], g_ref[...]).astype(BF16)
    k_ref[0] = jnp.dot(m, wk_ref[...], preferred_element_type=F32).astype(BF16)
    v_ref[0] = jnp.dot(m, wv_ref[...], preferred_element_type=F32).astype(BF16)


def _kv_proj(mem, g_mem, w_k, w_v):
    b, s, d = mem.shape
    w_spec = pl.BlockSpec((d, d), lambda i: (0, 0))
    kv_spec = pl.BlockSpec((1, s, d), lambda i: (i, 0, 0))
    return pl.pallas_call(
        _kv_body,
        grid=(b,),
        in_specs=[kv_spec, pl.BlockSpec((1, d), lambda i: (0, 0)), w_spec, w_spec],
        out_specs=[kv_spec, kv_spec],
        out_shape=[jax.ShapeDtypeStruct((b, s, d), BF16)] * 2,
        compiler_params=_cparams(("arbitrary",)),
        name="kv_proj",
    )(mem, g_mem.reshape(1, d), w_k.astype(BF16), w_v.astype(BF16))


def _mix_in_body(x_ref, g_ref, w_ref, wdt_ref, cw_ref, cb_ref, lg_ref, lb_ref,
                 u_ref, zx_ref, dt_ref, ubuf_ref, acc_ref):
    @pl.when(pl.program_id(1) == 0)
    def _():
        ubuf_ref[0:CONV_HALO, :] = jnp.zeros((CONV_HALO, CONV_CH), F32)

    h = _rms(x_ref[...], g_ref[...]).astype(BF16)
    dt_ref[...] = jnp.dot(h, wdt_ref[...], preferred_element_type=F32)

    first = CONV_HALO - (CONV_K - 1)
    w_lo = zx_lo = 0
    for cb, zx_w in enumerate(ZX_SHARES):
        cols = slice(cb * LANES, (cb + 1) * LANES)
        blk_w = 2 * LANES + zx_w
        r = jnp.dot(h, w_ref[:, w_lo:w_lo + blk_w], preferred_element_type=F32)
        ubuf_ref[CONV_HALO:CONV_HALO + CONV_TILE, cols] = r[:, :LANES] * _sigmoid(r[:, LANES:2 * LANES])
        if zx_w:
            zx_ref[:, zx_lo:zx_lo + zx_w] = r[:, 2 * LANES:].astype(BF16)
        w_lo += blk_w
        zx_lo += zx_w
        for rc in range(CONV_TILE // CONV_ROWS):
            r0 = rc * CONV_ROWS
            acc = None
            for res in range(SUBLANES):
                part = None
                for k in range(CONV_K):
                    if (first + k) % SUBLANES == res:
                        term = cw_ref[k:k + 1, cols] * ubuf_ref[r0 + first + k:r0 + first + k + CONV_ROWS, cols]
                        part = term if part is None else part + term
                acc = part if acc is None else acc + part
            acc_ref[r0:r0 + CONV_ROWS, cols] = acc

    for rc in range(CONV_TILE // CONV_ROWS):
        rows = slice(rc * CONV_ROWS, (rc + 1) * CONV_ROWS)
        u = acc_ref[rows, :] + cb_ref[...]
        mu = jnp.mean(u, axis=-1, keepdims=True)
        uc = u - mu
        var = jnp.mean(uc * uc, axis=-1, keepdims=True)
        y = uc * lax.rsqrt(var + LN_EPS) * lg_ref[...] + lb_ref[...]
        u_ref[rows, :] = (y * _sigmoid(y)).astype(BF16)

    ubuf_ref[0:CONV_HALO, :] = ubuf_ref[CONV_TILE:CONV_TILE + CONV_HALO, :]


def _mix_in(x2d, bsz, seqlen, g_mix, w_in, conv_w, conv_b, ln_g, ln_b):
    t, d = x2d.shape
    nt = seqlen // CONV_TILE
    w = w_in.astype(BF16)
    z0 = 2 * CONV_CH
    x0 = z0 + SSD_INNER
    w_zx = jnp.concatenate([w[:, x0:N_MAIN], jnp.zeros((d, ZX_Z0 - SSD_CONV_CH), BF16), w[:, z0:x0]], axis=1)
    parts, zx_lo = [], 0
    for cb, zx_w in enumerate(ZX_SHARES):
        parts += [w[:, cb * LANES:(cb + 1) * LANES], w[:, CONV_CH + cb * LANES:CONV_CH + (cb + 1) * LANES],
                  w_zx[:, zx_lo:zx_lo + zx_w]]
        zx_lo += zx_w
    w_blocks = jnp.concatenate(parts, axis=1)
    w_dt = jnp.pad(w_in[:, N_MAIN:], ((0, 0), (0, LANES - SSD_HEADS))).astype(BF16)
    row = lambda v: v.reshape(1, -1)
    const = lambda shape: pl.BlockSpec(shape, lambda b, j: (0, 0))
    tile = lambda width: pl.BlockSpec((CONV_TILE, width), lambda b, j: (b * nt + j, 0))
    return pl.pallas_call(
        _mix_in_body,
        grid=(bsz, nt),
        in_specs=[
            tile(d), const((1, d)), const(w_blocks.shape), const((d, LANES)),
            const((CONV_K + 1, CONV_CH)), const((1, CONV_CH)), const((1, CONV_CH)), const((1, CONV_CH)),
        ],
        out_specs=[tile(CONV_CH), tile(ZX_W), tile(LANES)],
        out_shape=[jax.ShapeDtypeStruct((t, CONV_CH), BF16), jax.ShapeDtypeStruct((t, ZX_W), BF16),
                   jax.ShapeDtypeStruct((t, LANES), F32)],
        scratch_shapes=[
            pltpu.VMEM((CONV_HALO + CONV_TILE, CONV_CH), F32),
            pltpu.VMEM((CONV_TILE, CONV_CH), F32),
        ],
        compiler_params=_cparams(("arbitrary", "arbitrary")),
        name="mix_in",
    )(x2d, row(g_mix), w_blocks, w_dt, jnp.pad(conv_w, ((0, 1), (0, 0))), row(conv_b), row(ln_g), row(ln_b))


def _ssd_body(xbc_ref, z_ref, dt_ref, cw_ref, cb_ref, dtb_ref, alog_ref, dsk_ref, ng_ref,
              expand_ref, y_ref, xbuf_ref, act_ref, dts_ref, state_ref):
    @pl.when(pl.program_id(1) == 0)
    def _():
        xbuf_ref[0:SSD_HALO, :] = jnp.zeros((SSD_HALO, SSD_CONV_CH), F32)
        state_ref[...] = jnp.zeros(state_ref.shape, F32)

    xbuf_ref[SSD_HALO:SSD_HALO + SSD_TILE, :] = xbc_ref[...].astype(F32)
    full = xbuf_ref[...]
    conv = cb_ref[...] + cw_ref[SSD_CONV_K - 1:SSD_CONV_K, :] * full[SSD_HALO:, :]
    for back in range(1, SSD_CONV_K):
        past = pltpu.roll(full, back, 0)[SSD_HALO:, :]
        conv = conv + cw_ref[SSD_CONV_K - 1 - back:SSD_CONV_K - back, :] * past
    act_ref[...] = conv * _sigmoid(conv)
    xbuf_ref[0:SSD_HALO, :] = xbuf_ref[SSD_TILE:SSD_TILE + SSD_HALO, :]

    dt_in = dt_ref[...] + dtb_ref[...]
    dts_ref[...] = jnp.maximum(dt_in, 0.0) + jnp.log1p(jnp.exp(-jnp.abs(dt_in)))

    a_neg = -jnp.exp(alog_ref[...]) * LOG2_E
    q = SSD_CHUNK
    row_i = lax.broadcasted_iota(jnp.int32, (q, q), 0)
    col_i = lax.broadcasted_iota(jnp.int32, (q, q), 1)
    causal = row_i >= col_i
    tril = causal.astype(F32)
    lane_i = lax.broadcasted_iota(jnp.int32, (q, LANES), 1)
    low_half = lane_i < SSD_HEAD_DIM
    expand = expand_ref[...]

    def chunk(c, carry):
        r0 = pl.multiple_of(c * q, q)
        rows = pl.ds(r0, q)
        dtc = dts_ref[rows, :]
        a_cs = jnp.dot(tril, dtc * a_neg, preferred_element_type=F32,
                       precision=lax.Precision.HIGHEST)
        a_cs_t = a_cs.T
        dt_t = dtc.T
        a_end = a_cs[q - 1:q, :]
        e_exp = jnp.dot(jnp.exp2(a_cs).astype(BF16), expand, preferred_element_type=F32)
        w_exp = jnp.dot((jnp.exp2(a_end - a_cs) * dtc).astype(BF16), expand,
                        preferred_element_type=F32)
        dec_row = e_exp[q - 1:q, :]
        xc = act_ref[rows, 0:SSD_INNER]
        xw = (xc * w_exp).astype(BF16)
        y_parts = []
        for g in range(SSD_GROUPS):
            b_f = act_ref[rows, SSD_INNER + g * SSD_STATE:SSD_INNER + (g + 1) * SSD_STATE]
            c_f = act_ref[rows, SSD_INNER + SSD_BC + g * SSD_STATE:
                          SSD_INNER + SSD_BC + (g + 1) * SSD_STATE]
            b_g = b_f.astype(BF16)
            c_g = c_f.astype(BF16)
            cb = lax.dot_general(c_g, b_g, (((1,), (1,)), ((), ())),
                                 preferred_element_type=F32)
            gcols = slice(g * SSD_GROUP_W, (g + 1) * SSD_GROUP_W)
            st = state_ref[g]
            y_off = jnp.dot(c_g, st.astype(BF16), preferred_element_type=F32)
            state_ref[g] = st * dec_row[:, gcols] + jnp.dot(
                b_f.T.astype(BF16), xw[:, gcols], preferred_element_type=F32)
            for pair in range(SSD_GROUP_W // LANES):
                ms = []
                for hh in range(2):
                    h = g * (SSD_HEADS // SSD_GROUPS) + 2 * pair + hh
                    seg = a_cs[:, h:h + 1] - a_cs_t[h:h + 1, :]
                    dec = jnp.exp2(jnp.where(causal, seg, -jnp.inf))
                    ms.append((cb * dec * dt_t[h:h + 1, :]).astype(BF16))
                lhs = jnp.concatenate(ms, axis=1)
                xp = xc[:, g * SSD_GROUP_W + pair * LANES:g * SSD_GROUP_W + (pair + 1) * LANES]
                rhs = jnp.concatenate([jnp.where(low_half, xp, 0.0),
                                       jnp.where(low_half, 0.0, xp)], axis=0).astype(BF16)
                y_diag = jnp.dot(lhs, rhs, preferred_element_type=F32)
                lo = pair * LANES
                y_parts.append(y_diag + y_off[:, lo:lo + LANES]
                               * e_exp[:, g * SSD_GROUP_W + lo:g * SSD_GROUP_W + lo + LANES])
        y = jnp.concatenate(y_parts, axis=1) + xc * dsk_ref[...]
        z = z_ref[rows, :].astype(F32)
        y = y * (z * _sigmoid(z))
        outs = []
        for g in range(SSD_GROUPS):
            yg = y[:, g * SSD_GROUP_W:(g + 1) * SSD_GROUP_W]
            outs.append(yg * lax.rsqrt(jnp.mean(yg * yg, axis=-1, keepdims=True) + RMS_EPS))
        y_ref[rows, :] = (jnp.concatenate(outs, axis=1) * ng_ref[...]).astype(BF16)
        return carry

    lax.fori_loop(0, SSD_TILE // q, chunk, 0, unroll=SSD_UNROLL)


def _ssd(zx, dt_raw, bsz, seqlen, ssd_conv_w, ssd_conv_b, dt_bias, a_log, d_skip, ssd_norm_g):
    nt = seqlen // SSD_TILE
    pad_h = lambda v: jnp.pad(v, (0, LANES - SSD_HEADS)).reshape(1, LANES)
    expand = (jnp.arange(LANES)[:, None] == (jnp.arange(SSD_INNER) // SSD_HEAD_DIM)[None, :]).astype(BF16)
    const = lambda shape: pl.BlockSpec(shape, lambda b, j: (0, 0))
    return pl.pallas_call(
        _ssd_body,
        grid=(bsz, nt),
        in_specs=[
            pl.BlockSpec((SSD_TILE, SSD_CONV_CH), lambda b, j: (b * nt + j, 0)),
            pl.BlockSpec((SSD_TILE, SSD_INNER), lambda b, j: (b * nt + j, ZX_Z0 // SSD_INNER)),
            pl.BlockSpec((SSD_TILE, LANES), lambda b, j: (b * nt + j, 0)),
            const((SSD_CONV_K, SSD_CONV_CH)),
            const((1, SSD_CONV_CH)),
            const((1, LANES)), const((1, LANES)),
            const((1, SSD_INNER)), const((1, SSD_INNER)),
            const((LANES, SSD_INNER)),
        ],
        out_specs=pl.BlockSpec((SSD_TILE, SSD_INNER), lambda b, j: (b * nt + j, 0)),
        out_shape=jax.ShapeDtypeStruct((bsz * seqlen, SSD_INNER), BF16),
        scratch_shapes=[
            pltpu.VMEM((SSD_HALO + SSD_TILE, SSD_CONV_CH), F32),
            pltpu.VMEM((SSD_TILE, SSD_CONV_CH), F32),
            pltpu.VMEM((SSD_TILE, LANES), F32),
            pltpu.VMEM((SSD_GROUPS, SSD_STATE, SSD_GROUP_W), F32),
        ],
        compiler_params=_cparams(("arbitrary", "arbitrary")),
        name="ssd",
    )(zx, zx, dt_raw, ssd_conv_w, ssd_conv_b.reshape(1, SSD_CONV_CH), pad_h(dt_bias),
      pad_h(a_log), jnp.repeat(d_skip, SSD_HEAD_DIM).reshape(1, SSD_INNER),
      ssd_norm_g.reshape(1, SSD_INNER), expand)


def _xattn_body(x_ref, u_ref, y_ref, wu_ref, wy_ref, k_ref, v_ref, gx_ref, wq_ref, wo_ref, gm_ref,
                wr_ref, br_ref, x2_ref, h2_ref, route_ref, stats_ref):
    def tile(s):
        rows = slice(s * TOK_TILE, (s + 1) * TOK_TILE)
        return _xattn_tile(x_ref[rows, :], u_ref[rows, :], y_ref[rows, :], wu_ref, wy_ref, k_ref, v_ref,
                           gx_ref, wq_ref, wo_ref, gm_ref, wr_ref, br_ref)

    results = _interleave([tile(s) for s in range(XA_TILES)])
    for s in range(XA_TILES):
        rows = slice(s * TOK_TILE, (s + 1) * TOK_TILE)
        x2_ref[rows, :], h2_ref[rows, :], route_ref[rows, :], stats_ref[s] = results[s]


def _xattn_tile(x, u, y, wu_ref, wy_ref, k_ref, v_ref, gx_ref, wq_ref, wo_ref, gm_ref, wr_ref, br_ref):
    x = (x + jnp.dot(u, wu_ref[...], preferred_element_type=F32)
         + jnp.dot(y, wy_ref[...], preferred_element_type=F32))
    yield
    h = _rms(x, gx_ref[...]).astype(BF16)
    q = (jnp.dot(h, wq_ref[...], preferred_element_type=F32) * (XA_HEAD_DIM ** -0.5)).astype(BF16)
    yield
    heads = []
    for i in range(XA_HEADS):
        cols = slice(i * XA_HEAD_DIM, (i + 1) * XA_HEAD_DIM)
        s = jnp.dot(q[:, cols], k_ref[0, cols, :], preferred_element_type=F32)
        p = jnp.exp(s - jnp.max(s, axis=-1, keepdims=True))
        p = p / jnp.sum(p, axis=-1, keepdims=True)
        heads.append(jnp.dot(p.astype(BF16), v_ref[0, :, cols], preferred_element_type=F32))
        yield
    o = jnp.concatenate(heads, axis=1).astype(BF16)
    x2 = x + jnp.dot(o, wo_ref[...], preferred_element_type=F32)
    yield

    h2 = _rms(x2, gm_ref[...])
    h_hi = h2.astype(BF16)
    h_lo = (h2 - h_hi.astype(F32)).astype(BF16)
    both = jnp.dot(h_hi, wr_ref[...], preferred_element_type=F32)
    logits = (both[:, :ROUTE_LANES] + both[:, ROUTE_LANES:]
              + jnp.dot(h_lo, wr_ref[:, :ROUTE_LANES], preferred_element_type=F32)) + br_ref[...]
    yield
    lane = lax.broadcasted_iota(jnp.int32, logits.shape, 1)
    neg = -jnp.inf

    def first_argmax(v):
        m = jnp.max(v, axis=-1, keepdims=True)
        return m, jnp.min(jnp.where(v == m, lane, ROUTE_LANES), axis=-1, keepdims=True)

    gl = jnp.where(lane < N_GROUPS, logits, neg)
    g_max, g_sel = first_argmax(gl)
    p_top = 1.0 / jnp.sum(jnp.exp(gl - g_max), axis=-1, keepdims=True)
    e_lo = N_GROUPS + EXPERTS_PER_GROUP * g_sel
    el = jnp.where((lane >= e_lo) & (lane < e_lo + EXPERTS_PER_GROUP), logits, neg)
    m1, i1 = first_argmax(el)
    m2, i2 = first_argmax(jnp.where(lane == i1, neg, el))
    r = jnp.exp(m2 - m1)
    w1 = p_top / (1.0 + r)
    w2 = w1 * r
    yield
    e1 = i1 - N_GROUPS
    e2 = i2 - N_GROUPS
    oh1 = (lane == e1).astype(BF16)
    oh2 = (lane == e2).astype(BF16)
    n_t = logits.shape[0]
    before = (lax.broadcasted_iota(jnp.int32, (n_t, n_t), 0)
              > lax.broadcasted_iota(jnp.int32, (n_t, n_t), 1)).astype(BF16)
    c12 = jnp.dot(before, jnp.concatenate([oh1, oh2], axis=1), preferred_element_type=F32)
    c1 = c12[:, :ROUTE_LANES]
    c2 = c12[:, ROUTE_LANES:]
    tot1 = jnp.sum(oh1.astype(F32), axis=0, keepdims=True)
    cnt = tot1 + jnp.sum(oh2.astype(F32), axis=0, keepdims=True)
    cnt = jnp.floor((cnt + (SUBLANES - 1)) * (1.0 / SUBLANES)) * SUBLANES
    lanes_before = (lax.broadcasted_iota(jnp.int32, (ROUTE_LANES, ROUTE_LANES), 0)
                    < lax.broadcasted_iota(jnp.int32, (ROUTE_LANES, ROUTE_LANES), 1)).astype(F32)
    loff = jnp.dot(jnp.broadcast_to(cnt, (SUBLANES, ROUTE_LANES)), lanes_before,
                   preferred_element_type=F32, precision=lax.Precision.HIGHEST)[0:1, :]
    lp1 = jnp.sum(oh1.astype(F32) * (loff + c1), axis=-1, keepdims=True)
    lp2 = jnp.sum(oh2.astype(F32) * (loff + tot1 + c2), axis=-1, keepdims=True)
    route = jnp.where(lane == 0, e1.astype(F32),
            jnp.where(lane == 1, e2.astype(F32),
            jnp.where(lane == 2, w1,
            jnp.where(lane == 3, w2,
            jnp.where(lane == 4, lp1, jnp.where(lane == 5, lp2, 0.0))))))
    row = lax.broadcasted_iota(jnp.int32, (SUBLANES, ROUTE_LANES), 0)
    return x2, h_hi, route, jnp.where(row == 0, cnt, jnp.where(row == 1, loff, 0.0))


def _xattn_route(x, u, y, w_out, k, v, bsz, seqlen, g_xattn, w_q, w_o, g_moe, w_rg, b_rg, w_re, b_re):
    t, d = x.shape
    w_mix = w_out.astype(BF16)
    w_r = jnp.pad(jnp.concatenate([w_rg, w_re], axis=1), ((0, 0), (0, ROUTE_LANES - N_GROUPS - N_EXPERTS)))
    b_r = jnp.pad(jnp.concatenate([b_rg, b_re]), (0, ROUTE_LANES - N_GROUPS - N_EXPERTS)).reshape(1, ROUTE_LANES)
    wr_hi = w_r.astype(BF16)
    wr = jnp.concatenate([wr_hi, (w_r - wr_hi.astype(F32)).astype(BF16)], axis=1)
    step_rows = XA_TILES * TOK_TILE
    nt = seqlen // step_rows
    tile = pl.BlockSpec((step_rows, d), lambda b, j: (b * nt + j, 0))
    kv_spec = pl.BlockSpec((1, MEM_LEN, d), lambda b, j: (b, 0, 0))
    const = lambda shape: pl.BlockSpec(shape, lambda b, j: (0, 0))
    return pl.pallas_call(
        _xattn_body,
        grid=(bsz, nt),
        in_specs=[tile, tile, tile, const((d, d)), const((d, d)),
                  pl.BlockSpec((1, d, MEM_LEN), lambda b, j: (b, 0, 0)), kv_spec, const((1, d)),
                  const((d, d)), const((d, d)), const((1, d)),
                  const((d, 2 * ROUTE_LANES)), const((1, ROUTE_LANES))],
        out_specs=[tile, tile, pl.BlockSpec((step_rows, ROUTE_LANES), lambda b, j: (b * nt + j, 0)),
                   pl.BlockSpec((XA_TILES, SUBLANES, ROUTE_LANES), lambda b, j: (b * nt + j, 0, 0))],
        out_shape=[jax.ShapeDtypeStruct((t, d), F32), jax.ShapeDtypeStruct((t, d), BF16),
                   jax.ShapeDtypeStruct((t, ROUTE_LANES), F32),
                   jax.ShapeDtypeStruct((t // TOK_TILE, SUBLANES, ROUTE_LANES), F32)],
        compiler_params=_cparams(("arbitrary", "arbitrary")),
        name="xattn_route",
    )(x, u, y, w_mix[:CONV_CH], w_mix[CONV_CH:], jnp.swapaxes(k, 1, 2), v, g_xattn.reshape(1, d), w_q.astype(BF16), w_o.astype(BF16), g_moe.reshape(1, d),
      wr, b_r)


def _rows_copy(src_ref, src0, dst_ref, dst0, n, sem):
    rows = lambda r0: pl.ds(pl.multiple_of(r0, SUBLANES), pl.multiple_of(n, SUBLANES))
    return pltpu.make_async_copy(src_ref.at[rows(src0), :], dst_ref.at[rows(dst0), :], sem)


def _pack_pairs(v):
    half = v.shape[1] // 2
    lo = lax.bitcast_convert_type(v[:, :half], jnp.uint32)
    hi = lax.bitcast_convert_type(v[:, half:], jnp.uint32)
    return (lo >> 16) | (hi & jnp.uint32(0xFFFF0000))


def _unpack_pairs(w):
    lo = lax.bitcast_convert_type(w << 16, F32)
    hi = lax.bitcast_convert_type(w & jnp.uint32(0xFFFF0000), F32)
    return jnp.concatenate([lo, hi], axis=1).astype(BF16)


def _dispatch_body(cnt_ref, loff_ref, gb_ref, tot_ref, fstart_ref, fcnt_ref, nu_ref, h_ref, route_ref,
                   xs_ref, buf_ref, zero_ref, sem, zsem):
    i = pl.program_id(0)
    nt = pl.num_programs(0)
    slot = i % 2
    buf = buf_ref.at[slot]

    def drain(step, s):
        n = sum(tot_ref[step * MOE_TILES + k] for k in range(MOE_TILES))
        _rows_copy(buf_ref.at[s], 0, xs_ref, 0, n, sem.at[s]).wait()

    @pl.when(i >= 2)
    def _():
        drain(i - 2, slot)

    def sort_tile(k):
        rows = slice(k * TOK_TILE, (k + 1) * TOK_TILE)
        rt = route_ref[rows, :].T
        lp1 = rt[4:5, :].astype(jnp.int32)
        lp2 = rt[5:6, :].astype(jnp.int32)
        r_i = lax.broadcasted_iota(jnp.int32, (SORT_ROWS, TOK_TILE), 0)
        perm = jnp.where((r_i == lp1) | (r_i == lp2), 1.0, 0.0).astype(BF16)
        yield
        buf[k * SORT_ROWS:(k + 1) * SORT_ROWS, :] = _pack_pairs(
            jnp.dot(perm, h_ref[rows, :], preferred_element_type=F32))

    _interleave([sort_tile(k) for k in range(MOE_TILES)])

    for k in range(MOE_TILES):
        def per_expert(e, c, k=k):
            j = (i * MOE_TILES + k) * N_EXPERTS + e

            @pl.when(cnt_ref[j] > 0)
            def _():
                _rows_copy(buf, k * SORT_ROWS + loff_ref[j], xs_ref, gb_ref[j], cnt_ref[j],
                           sem.at[slot]).start()
            return c

        lax.fori_loop(0, N_EXPERTS, per_expert, 0)

    @pl.when(i == 0)
    def _():
        zero_ref[...] = jnp.zeros(zero_ref.shape, zero_ref.dtype)

        def fill(start):
            def body(e, c):
                @pl.when(fcnt_ref[e] > 0)
                def _():
                    copy = _rows_copy(zero_ref, 0, xs_ref, fstart_ref[e], fcnt_ref[e], zsem)
                    copy.start() if start else copy.wait()
                return c
            lax.fori_loop(0, N_EXPERTS, body, 0)

            def tail(blk, c):
                copy = _rows_copy(zero_ref, 0, xs_ref, blk * MOE_BLOCK, MOE_BLOCK, zsem)
                copy.start() if start else copy.wait()
                return c
            lax.fori_loop(nu_ref[0], xs_ref.shape[0] // MOE_BLOCK, tail, 0)

        fill(True)
        fill(False)

    @pl.when(i == nt - 1)
    def _():
        drain(i, slot)

        @pl.when(i >= 1)
        def _():
            drain(i - 1, 1 - slot)


def _dispatch(h2, route, tables, n_used, n_slots):
    t, d = h2.shape
    step_rows = MOE_TILES * TOK_TILE
    return pl.pallas_call(
        _dispatch_body,
        grid_spec=pltpu.PrefetchScalarGridSpec(
            num_scalar_prefetch=7,
            grid=(t // step_rows,),
            in_specs=[
                pl.BlockSpec((step_rows, d), lambda i, *_: (i, 0)),
                pl.BlockSpec((step_rows, ROUTE_LANES), lambda i, *_: (i, 0)),
            ],
            out_specs=pl.BlockSpec(memory_space=pl.ANY),
            scratch_shapes=[pltpu.VMEM((2, MOE_TILES * SORT_ROWS, d // 2), jnp.uint32),
                            pltpu.VMEM((MOE_BLOCK, d // 2), jnp.uint32),
                            pltpu.SemaphoreType.DMA((2,)), pltpu.SemaphoreType.DMA(())],
        ),
        out_shape=jax.ShapeDtypeStruct((n_slots, d // 2), jnp.uint32),
        compiler_params=_cparams(("arbitrary",)),
        name="dispatch",
    )(tables["cnt"], tables["loff"], tables["gb"], tables["tot"], tables["fill_start"],
      tables["fill_cnt"], n_used, h2, route)


def _experts_body(be_ref, nu_ref, first_ref, slot_ref, nxt_ref, x_ref, wg_hbm, wu_hbm, wd_hbm, y_ref,
                  wg_f, wu_f, wd_f, wg_bf, wu_bf, wd_bf, sem):
    b = pl.program_id(0)
    used = b < nu_ref[0]

    def weights(e, s):
        return [pltpu.make_async_copy(hbm.at[e], buf.at[s], sem.at[s])
                for hbm, buf in ((wg_hbm, wg_f), (wu_hbm, wu_f), (wd_hbm, wd_f))]

    @pl.when(used & (first_ref[b] == 1))
    def _():
        s = slot_ref[b]

        @pl.when(b == 0)
        def _():
            for copy in weights(be_ref[0], 0):
                copy.start()

        for copy in weights(be_ref[b], s):
            copy.wait()
        wg_bf[...] = wg_f[s].astype(BF16)
        wu_bf[...] = wu_f[s].astype(BF16)
        wd_bf[...] = wd_f[s].astype(BF16)

        @pl.when(nxt_ref[b] >= 0)
        def _():
            for copy in weights(nxt_ref[b], 1 - s):
                copy.start()

    @pl.when(used)
    def _():
        x = _unpack_pairs(x_ref[...])
        g = jnp.dot(x, wg_bf[...], preferred_element_type=F32)
        u = jnp.dot(x, wu_bf[...], preferred_element_type=F32)
        a = (g * _sigmoid(g) * u).astype(BF16)
        y = jnp.dot(a, wd_bf[...], preferred_element_type=F32)
        y_ref[...] = _pack_pairs(y.astype(BF16).astype(F32))

    @pl.when(jnp.logical_not(used))
    def _():
        y_ref[...] = jnp.zeros(y_ref.shape, y_ref.dtype)


def _experts(xs, etab, w_gate, w_up, w_down):
    n_slots, dp = xs.shape
    d = w_gate.shape[1]
    n_blocks = n_slots // MOE_BLOCK
    last = lambda b, nu: jnp.maximum(jnp.minimum(b, nu[0] - 1), 0)
    hbm = pl.BlockSpec(memory_space=pl.ANY)
    return pl.pallas_call(
        _experts_body,
        grid_spec=pltpu.PrefetchScalarGridSpec(
            num_scalar_prefetch=5,
            grid=(n_blocks,),
            in_specs=[pl.BlockSpec((MOE_BLOCK, dp), lambda b, be, nu, *_: (last(b, nu), 0)), hbm, hbm, hbm],
            out_specs=pl.BlockSpec((MOE_BLOCK, dp), lambda b, *_: (b, 0)),
            scratch_shapes=[pltpu.VMEM((2, d, D_EXPERT), F32), pltpu.VMEM((2, d, D_EXPERT), F32),
                            pltpu.VMEM((2, D_EXPERT, d), F32),
                            pltpu.VMEM((d, D_EXPERT), BF16), pltpu.VMEM((d, D_EXPERT), BF16),
                            pltpu.VMEM((D_EXPERT, d), BF16), pltpu.SemaphoreType.DMA((2,))],
        ),
        out_shape=jax.ShapeDtypeStruct((n_slots, dp), jnp.uint32),
        compiler_params=_cparams(("arbitrary",)),
        name="experts",
    )(etab["block_e"], etab["n_used"], etab["first"], etab["slot"], etab["nxt"], xs, w_gate, w_up, w_down)


def _combine_body(cnt_ref, loff_ref, gb_ref, tot_ref, x_ref, route_ref, g_ref, ys_ref, o_ref, ybuf_ref, sem):
    i = pl.program_id(0)
    nt = pl.num_programs(0)
    slot = i % 2

    def gather(step, s):
        for k in range(MOE_TILES):
            def per_expert(e, c, k=k):
                j = (step * MOE_TILES + k) * N_EXPERTS + e

                @pl.when(cnt_ref[j] > 0)
                def _():
                    _rows_copy(ys_ref, gb_ref[j], ybuf_ref.at[s], k * SORT_ROWS + loff_ref[j], cnt_ref[j],
                               sem.at[s]).start()
                return c

            lax.fori_loop(0, N_EXPERTS, per_expert, 0)

    @pl.when(i == 0)
    def _():
        ybuf_ref[...] = jnp.zeros(ybuf_ref.shape, ybuf_ref.dtype)
        gather(0, 0)

    @pl.when(i + 1 < nt)
    def _():
        gather(i + 1, 1 - slot)

    def combine_tile(k):
        rows = slice(k * TOK_TILE, (k + 1) * TOK_TILE)
        route = route_ref[rows, :]
        c_i = lax.broadcasted_iota(jnp.int32, (TOK_TILE, SORT_ROWS), 1)
        lp1 = route[:, 4:5].astype(jnp.int32)
        lp2 = route[:, 5:6].astype(jnp.int32)
        pw = jnp.where(c_i == lp1, route[:, 2:3], jnp.where(c_i == lp2, route[:, 3:4], 0.0)).astype(BF16)
        yield
        r_i = lax.broadcasted_iota(jnp.int32, (SORT_ROWS, 1), 0)
        y = _unpack_pairs(jnp.where(r_i < tot_ref[i * MOE_TILES + k],
                                    ybuf_ref[slot, k * SORT_ROWS:(k + 1) * SORT_ROWS, :], jnp.uint32(0)))
        moe = jnp.dot(pw, y, preferred_element_type=F32)
        yield
        o_ref[rows, :] = _rms(x_ref[rows, :] + moe, g_ref[...])

    tiles = [combine_tile(k) for k in range(MOE_TILES)]
    for tile in tiles:
        next(tile)
    n = sum(tot_ref[i * MOE_TILES + k] for k in range(MOE_TILES))
    _rows_copy(ys_ref, 0, ybuf_ref.at[slot], 0, n, sem.at[slot]).wait()
    _interleave(tiles)


def _combine(x2, route, tables, ys, g_final):
    t, d = x2.shape
    step_rows = MOE_TILES * TOK_TILE
    tile = pl.BlockSpec((step_rows, d), lambda i, *_: (i, 0))
    return pl.pallas_call(
        _combine_body,
        grid_spec=pltpu.PrefetchScalarGridSpec(
            num_scalar_prefetch=4,
            grid=(t // step_rows,),
            in_specs=[
                tile,
                pl.BlockSpec((step_rows, ROUTE_LANES), lambda i, *_: (i, 0)),
                pl.BlockSpec((1, d), lambda i, *_: (0, 0)),
                pl.BlockSpec(memory_space=pl.ANY),
            ],
            out_specs=tile,
            scratch_shapes=[pltpu.VMEM((2, MOE_TILES * SORT_ROWS, d // 2), jnp.uint32),
                            pltpu.SemaphoreType.DMA((2,))],
        ),
        out_shape=jax.ShapeDtypeStruct((t, d), F32),
        compiler_params=_cparams(("arbitrary",)),
        name="combine",
    )(tables["cnt"], tables["loff"], tables["gb"], tables["tot"], x2, route, g_final.reshape(1, d), ys)


def _routing_tables(stats, n_tok):
    cnt = stats[:, 0, :N_EXPERTS].astype(jnp.int32)
    loff = stats[:, 1, :N_EXPERTS].astype(jnp.int32)
    n_tiles = cnt.shape[0]
    counts = jnp.sum(cnt, axis=0)
    padded = ((counts + MOE_BLOCK - 1) // MOE_BLOCK) * MOE_BLOCK
    pad_end = jnp.cumsum(padded)
    pad_start = pad_end - padded
    gb = pad_start[None, :] + jnp.cumsum(cnt, axis=0) - cnt
    max_rows = (n_tok * TOP_K + n_tiles * N_EXPERTS * (SUBLANES - 1)
                + N_EXPERTS * (MOE_BLOCK - SUBLANES))
    n_blocks = -(-max_rows // MOE_BLOCK)
    n_slots = n_blocks * MOE_BLOCK
    block_start = jnp.arange(n_blocks, dtype=jnp.int32) * MOE_BLOCK
    block_e = jnp.minimum(jnp.sum((pad_end[None, :] <= block_start[:, None]).astype(jnp.int32), axis=1),
                          N_EXPERTS - 1)
    tables = dict(cnt=cnt.reshape(-1), loff=loff.reshape(-1), gb=gb.reshape(-1).astype(jnp.int32),
                  tot=jnp.sum(cnt, axis=1).astype(jnp.int32),
                  fill_start=(pad_start + counts).astype(jnp.int32),
                  fill_cnt=(padded - counts).astype(jnp.int32))
    n_used = (pad_end[-1:] // MOE_BLOCK).astype(jnp.int32)
    first = jnp.concatenate([jnp.ones((1,), jnp.int32), (block_e[1:] != block_e[:-1]).astype(jnp.int32)])
    seg_slot = (jnp.cumsum(first) - 1) % 2
    experts = jnp.arange(N_EXPERTS, dtype=jnp.int32)
    later = jnp.where((experts[None, :] > experts[:, None]) & (padded[None, :] > 0), experts[None, :], N_EXPERTS)
    next_e = jnp.min(later, axis=1)
    next_e = jnp.where(next_e < N_EXPERTS, next_e, -1)
    nxt = jnp.sum(jnp.where(block_e[:, None] == experts[None, :], next_e[None, :], 0), axis=1)
    etab = dict(block_e=block_e, n_used=n_used, first=first, slot=seg_slot.astype(jnp.int32),
                nxt=nxt.astype(jnp.int32))
    return tables, etab, n_slots


def kernel(x, mem, g_mix, w_in, conv_w, conv_b, ln_g, ln_b, ssd_conv_w, ssd_conv_b, dt_bias, a_log, d_skip, ssd_norm_g, w_out, g_xattn, g_mem, w_q, w_k, w_v, w_o, g_moe, w_router_group, b_router_group, w_router_expert, b_router_expert, w_gate, w_up, w_down, g_final):
    bsz, seqlen, d = x.shape
    n_tok = bsz * seqlen
    xt = x.reshape(n_tok, d)
    assert g_mix.shape[0] == 1, "the combine kernel applies the final norm: single layer only"
    for l in range(1):
        u, zx, dt_raw = _mix_in(xt, bsz, seqlen, g_mix[l], w_in[l], conv_w[l], conv_b[l],
                                    ln_g[l], ln_b[l])
        y = _ssd(zx, dt_raw, bsz, seqlen, ssd_conv_w[l], ssd_conv_b[l], dt_bias[l], a_log[l],
                 d_skip[l], ssd_norm_g[l])
        k, v = _kv_proj(mem, g_mem[l], w_k[l], w_v[l])
        x2, h2, route, stats = _xattn_route(xt, u, y, w_out[l], k, v, bsz, seqlen, g_xattn[l], w_q[l],
                                            w_o[l], g_moe[l], w_router_group[l], b_router_group[l],
                                            w_router_expert[l], b_router_expert[l])
        tables, etab, n_slots = _routing_tables(stats, n_tok)
        xs = _dispatch(h2, route, tables, etab["n_used"], n_slots)
        ys = _experts(xs, etab, w_gate[l], w_up[l], w_down[l])
        xt = _combine(x2, route, tables, ys, g_final)
    return xt.reshape(bsz, seqlen, d)
```

```python
import functools

import jax
import jax.numpy as jnp
from jax import lax
from jax.experimental import pallas as pl
from jax.experimental.pallas import tpu as pltpu

F32 = jnp.float32
BF16 = jnp.bfloat16

D_MODEL = 1024
CONV_CH = 1024
CONV_K = 31
SSD_INNER = 1024
SSD_HEAD_DIM = 64
SSD_HEADS = 16
SSD_STATE = 128
SSD_GROUPS = 2
SSD_GROUP_W = SSD_INNER // SSD_GROUPS
SSD_CONV_K = 4
SSD_CHUNK = 128
SSD_BC = SSD_GROUPS * SSD_STATE
SSD_CONV_CH = SSD_INNER + 2 * SSD_BC
N_MAIN = 2 * CONV_CH + 2 * SSD_INNER + 2 * SSD_BC
XA_HEADS = 4
XA_HEAD_DIM = 256
MEM_LEN = 256
N_GROUPS = 4
EXPERTS_PER_GROUP = 8
N_EXPERTS = 32
TOP_K = 2
D_EXPERT = 512
MOE_BLOCK = 512
RMS_EPS = 1e-6
LN_EPS = 1e-5
LOG2_E = 1.4426950408889634

LANES = 128
SUBLANES = 8
VMEM_LIMIT = 56 * 1024 * 1024

TOK_TILE = 512
CONV_TILE = 512
ZX_Z0 = -(-SSD_CONV_CH // SSD_INNER) * SSD_INNER
ZX_W = ZX_Z0 + SSD_INNER
ZX_SHARES = (0, 512, 512, 512, 512, 512, 512, 0)
assert sum(ZX_SHARES) == ZX_W and len(ZX_SHARES) == CONV_CH // LANES
CONV_HALO = 32
CONV_ROWS = 512
SSD_TILE = 1024
MOE_TILES = 2
XA_TILES = 2
SSD_UNROLL = 8
SSD_HALO = 8
ROUTE_LANES = 128
SORT_ROWS = TOP_K * TOK_TILE + N_EXPERTS * SUBLANES


def _cparams(sem):
    return pltpu.CompilerParams(dimension_semantics=sem, vmem_limit_bytes=VMEM_LIMIT)


def _rms(x, g):
    return x * lax.rsqrt(jnp.mean(x * x, axis=-1, keepdims=True) + RMS_EPS) * g


def _sigmoid(x):
    return 1.0 / (1.0 + jnp.exp2(x * (-LOG2_E)))


def _interleave(stages):
    results = {}
    while len(results) < len(stages):
        for k, item in enumerate(stages):
            if k not in results:
                try:
                    next(item)
                except StopIteration as done:
                    results[k] = done.value
    return [results[k] for k in range(len(stages))]


def _kv_body(m_ref, g_ref, wk_ref, wv_ref, k_ref, v_ref):
    m = _rms(m_ref[0], g_ref[...]).astype(BF16)
    k_ref[0] = jnp.dot(m, wk_ref[...], preferred_element_type=F32).astype(BF16)
    v_ref[0] = jnp.dot(m, wv_ref[...], preferred_element_type=F32).astype(BF16)


def _kv_proj(mem, g_mem, w_k, w_v):
    b, s, d = mem.shape
    w_spec = pl.BlockSpec((d, d), lambda i: (0, 0))
    kv_spec = pl.BlockSpec((1, s, d), lambda i: (i, 0, 0))
    return pl.pallas_call(
        _kv_body,
        grid=(b,),
        in_specs=[kv_spec, pl.BlockSpec((1, d), lambda i: (0, 0)), w_spec, w_spec],
        out_specs=[kv_spec, kv_spec],
        out_shape=[jax.ShapeDtypeStruct((b, s, d), BF16)] * 2,
        compiler_params=_cparams(("arbitrary",)),
        name="kv_proj",
    )(mem, g_mem.reshape(1, d), w_k.astype(BF16), w_v.astype(BF16))


def _mix_in_body(x_ref, g_ref, w_ref, wdt_ref, cw_ref, cb_ref, lg_ref, lb_ref,
                 u_ref, zx_ref, dt_ref, ubuf_ref, acc_ref):
    @pl.when(pl.program_id(1) == 0)
    def _():
        ubuf_ref[0:CONV_HALO, :] = jnp.zeros((CONV_HALO, CONV_CH), F32)

    h = _rms(x_ref[...], g_ref[...]).astype(BF16)
    dt_ref[...] = jnp.dot(h, wdt_ref[...], preferred_element_type=F32)

    first = CONV_HALO - (CONV_K - 1)
    w_lo = zx_lo = 0
    for cb, zx_w in enumerate(ZX_SHARES):
        cols = slice(cb * LANES, (cb + 1) * LANES)
        blk_w = 2 * LANES + zx_w
        r = jnp.dot(h, w_ref[:, w_lo:w_lo + blk_w], preferred_element_type=F32)
        ubuf_ref[CONV_HALO:CONV_HALO + CONV_TILE, cols] = r[:, :LANES] * _sigmoid(r[:, LANES:2 * LANES])
        if zx_w:
            zx_ref[:, zx_lo:zx_lo + zx_w] = r[:, 2 * LANES:].astype(BF16)
        w_lo += blk_w
        zx_lo += zx_w
        for rc in range(CONV_TILE // CONV_ROWS):
            r0 = rc * CONV_ROWS
            acc = None
            for res in range(SUBLANES):
                part = None
                for k in range(CONV_K):
                    if (first + k) % SUBLANES == res:
                        term = cw_ref[k:k + 1, cols] * ubuf_ref[r0 + first + k:r0 + first + k + CONV_ROWS, cols]
                        part = term if part is None else part + term
                acc = part if acc is None else acc + part
            acc_ref[r0:r0 + CONV_ROWS, cols] = acc

    for rc in range(CONV_TILE // CONV_ROWS):
        rows = slice(rc * CONV_ROWS, (rc + 1) * CONV_ROWS)
        u = acc_ref[rows, :] + cb_ref[...]
        mu = jnp.mean(u, axis=-1, keepdims=True)
        uc = u - mu
        var = jnp.mean(uc * uc, axis=-1, keepdims=True)
        y = uc * lax.rsqrt(var + LN_EPS) * lg_ref[...] + lb_ref[...]
        u_ref[rows, :] = (y * _sigmoid(y)).astype(BF16)

    ubuf_ref[0:CONV_HALO, :] = ubuf_ref[CONV_TILE:CONV_TILE + CONV_HALO, :]


def _mix_in(x2d, bsz, seqlen, g_mix, w_in, conv_w, conv_b, ln_g, ln_b):
    t, d = x2d.shape
    nt = seqlen // CONV_TILE
    w = w_in.astype(BF16)
    z0 = 2 * CONV_CH
    x0 = z0 + SSD_INNER
    w_zx = jnp.concatenate([w[:, x0:N_MAIN], jnp.zeros((d, ZX_Z0 - SSD_CONV_CH), BF16), w[:, z0:x0]], axis=1)
    parts, zx_lo = [], 0
    for cb, zx_w in enumerate(ZX_SHARES):
        parts += [w[:, cb * LANES:(cb + 1) * LANES], w[:, CONV_CH + cb * LANES:CONV_CH + (cb + 1) * LANES],
                  w_zx[:, zx_lo:zx_lo + zx_w]]
        zx_lo += zx_w
    w_blocks = jnp.concatenate(parts, axis=1)
    w_dt = jnp.pad(w_in[:, N_MAIN:], ((0, 0), (0, LANES - SSD_HEADS))).astype(BF16)
    row = lambda v: v.reshape(1, -1)
    const = lambda shape: pl.BlockSpec(shape, lambda b, j: (0, 0))
    tile = lambda width: pl.BlockSpec((CONV_TILE, width), lambda b, j: (b * nt + j, 0))
    return pl.pallas_call(
        _mix_in_body,
        grid=(bsz, nt),
        in_specs=[
            tile(d), const((1, d)), const(w_blocks.shape), const((d, LANES)),
            const((CONV_K + 1, CONV_CH)), const((1, CONV_CH)), const((1, CONV_CH)), const((1, CONV_CH)),
        ],
        out_specs=[tile(CONV_CH), tile(ZX_W), tile(LANES)],
        out_shape=[jax.ShapeDtypeStruct((t, CONV_CH), BF16), jax.ShapeDtypeStruct((t, ZX_W), BF16),
                   jax.ShapeDtypeStruct((t, LANES), F32)],
        scratch_shapes=[
            pltpu.VMEM((CONV_HALO + CONV_TILE, CONV_CH), F32),
            pltpu.VMEM((CONV_TILE, CONV_CH), F32),
        ],
        compiler_params=_cparams(("arbitrary", "arbitrary")),
        name="mix_in",
    )(x2d, row(g_mix), w_blocks, w_dt, jnp.pad(conv_w, ((0, 1), (0, 0))), row(conv_b), row(ln_g), row(ln_b))


def _ssd_body(xbc_ref, z_ref, dt_ref, cw_ref, cb_ref, dtb_ref, alog_ref, dsk_ref, ng_ref,
              expand_ref, y_ref, xbuf_ref, act_ref, dts_ref, state_ref):
    @pl.when(pl.program_id(1) == 0)
    def _():
        xbuf_ref[0:SSD_HALO, :] = jnp.zeros((SSD_HALO, SSD_CONV_CH), F32)
        state_ref[...] = jnp.zeros(state_ref.shape, F32)

    xbuf_ref[SSD_HALO:SSD_HALO + SSD_TILE, :] = xbc_ref[...].astype(F32)
    full = xbuf_ref[...]
    conv = cb_ref[...] + cw_ref[SSD_CONV_K - 1:SSD_CONV_K, :] * full[SSD_HALO:, :]
    for back in range(1, SSD_CONV_K):
        past = pltpu.roll(full, back, 0)[SSD_HALO:, :]
        conv = conv + cw_ref[SSD_CONV_K - 1 - back:SSD_CONV_K - back, :] * past
    act_ref[...] = conv * _sigmoid(conv)
    xbuf_ref[0:SSD_HALO, :] = xbuf_ref[SSD_TILE:SSD_TILE + SSD_HALO, :]

    dt_in = dt_ref[...] + dtb_ref[...]
    dts_ref[...] = jnp.maximum(dt_in, 0.0) + jnp.log1p(jnp.exp(-jnp.abs(dt_in)))

    a_neg = -jnp.exp(alog_ref[...]) * LOG2_E
    q = SSD_CHUNK
    row_i = lax.broadcasted_iota(jnp.int32, (q, q), 0)
    col_i = lax.broadcasted_iota(jnp.int32, (q, q), 1)
    causal = row_i >= col_i
    tril = causal.astype(F32)
    lane_i = lax.broadcasted_iota(jnp.int32, (q, LANES), 1)
    low_half = lane_i < SSD_HEAD_DIM
    expand = expand_ref[...]

    def chunk(c, carry):
        r0 = pl.multiple_of(c * q, q)
        rows = pl.ds(r0, q)
        dtc = dts_ref[rows, :]
        a_cs = jnp.dot(tril, dtc * a_neg, preferred_element_type=F32,
                       precision=lax.Precision.HIGHEST)
        a_cs_t = a_cs.T
        dt_t = dtc.T
        a_end = a_cs[q - 1:q, :]
        e_exp = jnp.dot(jnp.exp2(a_cs).astype(BF16), expand, preferred_element_type=F32)
        w_exp = jnp.dot((jnp.exp2(a_end - a_cs) * dtc).astype(BF16), expand,
                        preferred_element_type=F32)
        dec_row = e_exp[q - 1:q, :]
        xc = act_ref[rows, 0:SSD_INNER]
        xw = (xc * w_exp).astype(BF16)
        y_parts = []
        for g in range(SSD_GROUPS):
            b_f = act_ref[rows, SSD_INNER + g * SSD_STATE:SSD_INNER + (g + 1) * SSD_STATE]
            c_f = act_ref[rows, SSD_INNER + SSD_BC + g * SSD_STATE:
                          SSD_INNER + SSD_BC + (g + 1) * SSD_STATE]
            b_g = b_f.astype(BF16)
            c_g = c_f.astype(BF16)
            cb = lax.dot_general(c_g, b_g, (((1,), (1,)), ((), ())),
                                 preferred_element_type=F32)
            gcols = slice(g * SSD_GROUP_W, (g + 1) * SSD_GROUP_W)
            st = state_ref[g]
            y_off = jnp.dot(c_g, st.astype(BF16), preferred_element_type=F32)
            state_ref[g] = st * dec_row[:, gcols] + jnp.dot(
                b_f.T.astype(BF16), xw[:, gcols], preferred_element_type=F32)
            for pair in range(SSD_GROUP_W // LANES):
                ms = []
                for hh in range(2):
                    h = g * (SSD_HEADS // SSD_GROUPS) + 2 * pair + hh
                    seg = a_cs[:, h:h + 1] - a_cs_t[h:h + 1, :]
                    dec = jnp.exp2(jnp.where(causal, seg, -jnp.inf))
                    ms.append((cb * dec * dt_t[h:h + 1, :]).astype(BF16))
                lhs = jnp.concatenate(ms, axis=1)
                xp = xc[:, g * SSD_GROUP_W + pair * LANES:g * SSD_GROUP_W + (pair + 1) * LANES]
                rhs = jnp.concatenate([jnp.where(low_half, xp, 0.0),
                                       jnp.where(low_half, 0.0, xp)], axis=0).astype(BF16)
                y_diag = jnp.dot(lhs, rhs, preferred_element_type=F32)
                lo = pair * LANES
                y_parts.append(y_diag + y_off[:, lo:lo + LANES]
                               * e_exp[:, g * SSD_GROUP_W + lo:g * SSD_GROUP_W + lo + LANES])
        y = jnp.concatenate(y_parts, axis=1) + xc * dsk_ref[...]
        z = z_ref[rows, :].astype(F32)
        y = y * (z * _sigmoid(z))
        outs = []
        for g in range(SSD_GROUPS):
            yg = y[:, g * SSD_GROUP_W:(g + 1) * SSD_GROUP_W]
            outs.append(yg * lax.rsqrt(jnp.mean(yg * yg, axis=-1, keepdims=True) + RMS_EPS))
        y_ref[rows, :] = (jnp.concatenate(outs, axis=1) * ng_ref[...]).astype(BF16)
        return carry

    lax.fori_loop(0, SSD_TILE // q, chunk, 0, unroll=SSD_UNROLL)


def _ssd(zx, dt_raw, bsz, seqlen, ssd_conv_w, ssd_conv_b, dt_bias, a_log, d_skip, ssd_norm_g):
    nt = seqlen // SSD_TILE
    pad_h = lambda v: jnp.pad(v, (0, LANES - SSD_HEADS)).reshape(1, LANES)
    expand = (jnp.arange(LANES)[:, None] == (jnp.arange(SSD_INNER) // SSD_HEAD_DIM)[None, :]).astype(BF16)
    const = lambda shape: pl.BlockSpec(shape, lambda b, j: (0, 0))
    return pl.pallas_call(
        _ssd_body,
        grid=(bsz, nt),
        in_specs=[
            pl.BlockSpec((SSD_TILE, SSD_CONV_CH), lambda b, j: (b * nt + j, 0)),
            pl.BlockSpec((SSD_TILE, SSD_INNER), lambda b, j: (b * nt + j, ZX_Z0 // SSD_INNER)),
            pl.BlockSpec((SSD_TILE, LANES), lambda b, j: (b * nt + j, 0)),
            const((SSD_CONV_K, SSD_CONV_CH)),
            const((1, SSD_CONV_CH)),
            const((1, LANES)), const((1, LANES)),
            const((1, SSD_INNER)), const((1, SSD_INNER)),
            const((LANES, SSD_INNER)),
        ],
        out_specs=pl.BlockSpec((SSD_TILE, SSD_INNER), lambda b, j: (b * nt + j, 0)),
        out_shape=jax.ShapeDtypeStruct((bsz * seqlen, SSD_INNER), BF16),
        scratch_shapes=[
            pltpu.VMEM((SSD_HALO + SSD_TILE, SSD_CONV_CH), F32),
            pltpu.VMEM((SSD_TILE, SSD_CONV_CH), F32),
            pltpu.VMEM((SSD_TILE, LANES), F32),
            pltpu.VMEM((SSD_GROUPS, SSD_STATE, SSD_GROUP_W), F32),
        ],
        compiler_params=_cparams(("arbitrary", "arbitrary")),
        name="ssd",
    )(zx, zx, dt_raw, ssd_conv_w, ssd_conv_b.reshape(1, SSD_CONV_CH), pad_h(dt_bias),
      pad_h(a_log), jnp.repeat(d_skip, SSD_HEAD_DIM).reshape(1, SSD_INNER),
      ssd_norm_g.reshape(1, SSD_INNER), expand)


def _xattn_body(x_ref, u_ref, y_ref, wu_ref, wy_ref, k_ref, v_ref, gx_ref, wq_ref, wo_ref, gm_ref,
                wr_ref, br_ref, x2_ref, h2_ref, route_ref, stats_ref):
    def tile(s):
        rows = slice(s * TOK_TILE, (s + 1) * TOK_TILE)
        return _xattn_tile(x_ref[rows, :], u_ref[rows, :], y_ref[rows, :], wu_ref, wy_ref, k_ref, v_ref,
                           gx_ref, wq_ref, wo_ref, gm_ref, wr_ref, br_ref)

    results = _interleave([tile(s) for s in range(XA_TILES)])
    for s in range(XA_TILES):
        rows = slice(s * TOK_TILE, (s + 1) * TOK_TILE)
        x2_ref[rows, :], h2_ref[rows, :], route_ref[rows, :], stats_ref[s] = results[s]


def _xattn_tile(x, u, y, wu_ref, wy_ref, k_ref, v_ref, gx_ref, wq_ref, wo_ref, gm_ref, wr_ref, br_ref):
    x = (x + jnp.dot(u, wu_ref[...], preferred_element_type=F32)
         + jnp.dot(y, wy_ref[...], preferred_element_type=F32))
    yield
    h = _rms(x, gx_ref[...]).astype(BF16)
    q = (jnp.dot(h, wq_ref[...], preferred_element_type=F32) * (XA_HEAD_DIM ** -0.5)).astype(BF16)
    yield
    heads = []
    for i in range(XA_HEADS):
        cols = slice(i * XA_HEAD_DIM, (i + 1) * XA_HEAD_DIM)
        s = jnp.dot(q[:, cols], k_ref[0, cols, :], preferred_element_type=F32)
        p = jnp.exp(s - jnp.max(s, axis=-1, keepdims=True))
        p = p / jnp.sum(p, axis=-1, keepdims=True)
        heads.append(jnp.dot(p.astype(BF16), v_ref[0, :, cols], preferred_element_type=F32))
        yield
    o = jnp.concatenate(heads, axis=1).astype(BF16)
    x2 = x + jnp.dot(o, wo_ref[...], preferred_element_type=F32)
    yield

    h2 = _rms(x2, gm_ref[...])
    h_hi = h2.astype(BF16)
    h_lo = (h2 - h_hi.astype(F32)).astype(BF16)
    both = jnp.dot(h_hi, wr_ref[...], preferred_element_type=F32)
    logits = (both[:, :ROUTE_LANES] + both[:, ROUTE_LANES:]
              + jnp.dot(h_lo, wr_ref[:, :ROUTE_LANES], preferred_element_type=F32)) + br_ref[...]
    yield
    lane = lax.broadcasted_iota(jnp.int32, logits.shape, 1)
    neg = -jnp.inf

    def first_argmax(v):
        m = jnp.max(v, axis=-1, keepdims=True)
        return m, jnp.min(jnp.where(v == m, lane, ROUTE_LANES), axis=-1, keepdims=True)

    gl = jnp.where(lane < N_GROUPS, logits, neg)
    g_max, g_sel = first_argmax(gl)
    p_top = 1.0 / jnp.sum(jnp.exp(gl - g_max), axis=-1, keepdims=True)
    e_lo = N_GROUPS + EXPERTS_PER_GROUP * g_sel
    el = jnp.where((lane >= e_lo) & (lane < e_lo + EXPERTS_PER_GROUP), logits, neg)
    m1, i1 = first_argmax(el)
    m2, i2 = first_argmax(jnp.where(lane == i1, neg, el))
    r = jnp.exp(m2 - m1)
    w1 = p_top / (1.0 + r)
    w2 = w1 * r
    yield
    e1 = i1 - N_GROUPS
    e2 = i2 - N_GROUPS
    oh1 = (lane == e1).astype(BF16)
    oh2 = (lane == e2).astype(BF16)
    n_t = logits.shape[0]
    before = (lax.broadcasted_iota(jnp.int32, (n_t, n_t), 0)
              > lax.broadcasted_iota(jnp.int32, (n_t, n_t), 1)).astype(BF16)
    c12 = jnp.dot(before, jnp.concatenate([oh1, oh2], axis=1), preferred_element_type=F32)
    c1 = c12[:, :ROUTE_LANES]
    c2 = c12[:, ROUTE_LANES:]
    tot1 = jnp.sum(oh1.astype(F32), axis=0, keepdims=True)
    cnt = tot1 + jnp.sum(oh2.astype(F32), axis=0, keepdims=True)
    cnt = jnp.floor((cnt + (SUBLANES - 1)) * (1.0 / SUBLANES)) * SUBLANES
    lanes_before = (lax.broadcasted_iota(jnp.int32, (ROUTE_LANES, ROUTE_LANES), 0)
                    < lax.broadcasted_iota(jnp.int32, (ROUTE_LANES, ROUTE_LANES), 1)).astype(F32)
    loff = jnp.dot(jnp.broadcast_to(cnt, (SUBLANES, ROUTE_LANES)), lanes_before,
                   preferred_element_type=F32, precision=lax.Precision.HIGHEST)[0:1, :]
    lp1 = jnp.sum(oh1.astype(F32) * (loff + c1), axis=-1, keepdims=True)
    lp2 = jnp.sum(oh2.astype(F32) * (loff + tot1 + c2), axis=-1, keepdims=True)
    route = jnp.where(lane == 0, e1.astype(F32),
            jnp.where(lane == 1, e2.astype(F32),
            jnp.where(lane == 2, w1,
            jnp.where(lane == 3, w2,
            jnp.where(lane == 4, lp1, jnp.where(lane == 5, lp2, 0.0))))))
    row = lax.broadcasted_iota(jnp.int32, (SUBLANES, ROUTE_LANES), 0)
    return x2, h_hi, route, jnp.where(row == 0, cnt, jnp.where(row == 1, loff, 0.0))


def _xattn_route(x, u, y, w_out, k, v, bsz, seqlen, g_xattn, w_q, w_o, g_moe, w_rg, b_rg, w_re, b_re):
    t, d = x.shape
    w_mix = w_out.astype(BF16)
    w_r = jnp.pad(jnp.concatenate([w_rg, w_re], axis=1), ((0, 0), (0, ROUTE_LANES - N_GROUPS - N_EXPERTS)))
    b_r = jnp.pad(jnp.concatenate([b_rg, b_re]), (0, ROUTE_LANES - N_GROUPS - N_EXPERTS)).reshape(1, ROUTE_LANES)
    wr_hi = w_r.astype(BF16)
    wr = jnp.concatenate([wr_hi, (w_r - wr_hi.astype(F32)).astype(BF16)], axis=1)
    step_rows = XA_TILES * TOK_TILE
    nt = seqlen // step_rows
    tile = pl.BlockSpec((step_rows, d), lambda b, j: (b * nt + j, 0))
    kv_spec = pl.BlockSpec((1, MEM_LEN, d), lambda b, j: (b, 0, 0))
    const = lambda shape: pl.BlockSpec(shape, lambda b, j: (0, 0))
    return pl.pallas_call(
        _xattn_body,
        grid=(bsz, nt),
        in_specs=[tile, tile, tile, const((d, d)), const((d, d)),
                  pl.BlockSpec((1, d, MEM_LEN), lambda b, j: (b, 0, 0)), kv_spec, const((1, d)),
                  const((d, d)), const((d, d)), const((1, d)),
                  const((d, 2 * ROUTE_LANES)), const((1, ROUTE_LANES))],
        out_specs=[tile, tile, pl.BlockSpec((step_rows, ROUTE_LANES), lambda b, j: (b * nt + j, 0)),
                   pl.BlockSpec((XA_TILES, SUBLANES, ROUTE_LANES), lambda b, j: (b * nt + j, 0, 0))],
        out_shape=[jax.ShapeDtypeStruct((t, d), F32), jax.ShapeDtypeStruct((t, d), BF16),
                   jax.ShapeDtypeStruct((t, ROUTE_LANES), F32),
                   jax.ShapeDtypeStruct((t // TOK_TILE, SUBLANES, ROUTE_LANES), F32)],
        compiler_params=_cparams(("arbitrary", "arbitrary")),
        name="xattn_route",
    )(x, u, y, w_mix[:CONV_CH], w_mix[CONV_CH:], jnp.swapaxes(k, 1, 2), v, g_xattn.reshape(1, d), w_q.astype(BF16), w_o.astype(BF16), g_moe.reshape(1, d),
      wr, b_r)


def _rows_copy(src_ref, src0, dst_ref, dst0, n, sem):
    rows = lambda r0: pl.ds(pl.multiple_of(r0, SUBLANES), pl.multiple_of(n, SUBLANES))
    return pltpu.make_async_copy(src_ref.at[rows(src0), :], dst_ref.at[rows(dst0), :], sem)


def _pack_pairs(v):
    half = v.shape[1] // 2
    lo = lax.bitcast_convert_type(v[:, :half], jnp.uint32)
    hi = lax.bitcast_convert_type(v[:, half:], jnp.uint32)
    return (lo >> 16) | (hi & jnp.uint32(0xFFFF0000))


def _unpack_pairs(w):
    lo = lax.bitcast_convert_type(w << 16, F32)
    hi = lax.bitcast_convert_type(w & jnp.uint32(0xFFFF0000), F32)
    return jnp.concatenate([lo, hi], axis=1).astype(BF16)


def _dispatch_body(cnt_ref, loff_ref, gb_ref, tot_ref, fstart_ref, fcnt_ref, nu_ref, h_ref, route_ref,
                   xs_ref, buf_ref, zero_ref, sem, zsem):
    i = pl.program_id(0)
    nt = pl.num_programs(0)
    slot = i % 2
    buf = buf_ref.at[slot]

    def drain(step, s):
        n = sum(tot_ref[step * MOE_TILES + k] for k in range(MOE_TILES))
        _rows_copy(buf_ref.at[s], 0, xs_ref, 0, n, sem.at[s]).wait()

    @pl.when(i >= 2)
    def _():
        drain(i - 2, slot)

    def sort_tile(k):
        rows = slice(k * TOK_TILE, (k + 1) * TOK_TILE)
        rt = route_ref[rows, :].T
        lp1 = rt[4:5, :].astype(jnp.int32)
        lp2 = rt[5:6, :].astype(jnp.int32)
        r_i = lax.broadcasted_iota(jnp.int32, (SORT_ROWS, TOK_TILE), 0)
        perm = jnp.where((r_i == lp1) | (r_i == lp2), 1.0, 0.0).astype(BF16)
        yield
        buf[k * SORT_ROWS:(k + 1) * SORT_ROWS, :] = _pack_pairs(
            jnp.dot(perm, h_ref[rows, :], preferred_element_type=F32))

    _interleave([sort_tile(k) for k in range(MOE_TILES)])

    for k in range(MOE_TILES):
        def per_expert(e, c, k=k):
            j = (i * MOE_TILES + k) * N_EXPERTS + e

            @pl.when(cnt_ref[j] > 0)
            def _():
                _rows_copy(buf, k * SORT_ROWS + loff_ref[j], xs_ref, gb_ref[j], cnt_ref[j],
                           sem.at[slot]).start()
            return c

        lax.fori_loop(0, N_EXPERTS, per_expert, 0)

    @pl.when(i == 0)
    def _():
        zero_ref[...] = jnp.zeros(zero_ref.shape, zero_ref.dtype)

        def fill(start):
            def body(e, c):
                @pl.when(fcnt_ref[e] > 0)
                def _():
                    copy = _rows_copy(zero_ref, 0, xs_ref, fstart_ref[e], fcnt_ref[e], zsem)
                    copy.start() if start else copy.wait()
                return c
            lax.fori_loop(0, N_EXPERTS, body, 0)

            def tail(blk, c):
                copy = _rows_copy(zero_ref, 0, xs_ref, blk * MOE_BLOCK, MOE_BLOCK, zsem)
                copy.start() if start else copy.wait()
                return c
            lax.fori_loop(nu_ref[0], xs_ref.shape[0] // MOE_BLOCK, tail, 0)

        fill(True)
        fill(False)

    @pl.when(i == nt - 1)
    def _():
        drain(i, slot)

        @pl.when(i >= 1)
        def _():
            drain(i - 1, 1 - slot)


def _dispatch(h2, route, tables, n_used, n_slots):
    t, d = h2.shape
    step_rows = MOE_TILES * TOK_TILE
    return pl.pallas_call(
        _dispatch_body,
        grid_spec=pltpu.PrefetchScalarGridSpec(
            num_scalar_prefetch=7,
            grid=(t // step_rows,),
            in_specs=[
                pl.BlockSpec((step_rows, d), lambda i, *_: (i, 0)),
                pl.BlockSpec((step_rows, ROUTE_LANES), lambda i, *_: (i, 0)),
            ],
            out_specs=pl.BlockSpec(memory_space=pl.ANY),
            scratch_shapes=[pltpu.VMEM((2, MOE_TILES * SORT_ROWS, d // 2), jnp.uint32),
                            pltpu.VMEM((MOE_BLOCK, d // 2), jnp.uint32),
                            pltpu.SemaphoreType.DMA((2,)), pltpu.SemaphoreType.DMA(())],
        ),
        out_shape=jax.ShapeDtypeStruct((n_slots, d // 2), jnp.uint32),
        compiler_params=_cparams(("arbitrary",)),
        name="dispatch",
    )(tables["cnt"], tables["loff"], tables["gb"], tables["tot"], tables["fill_start"],
      tables["fill_cnt"], n_used, h2, route)


def _experts_body(be_ref, nu_ref, first_ref, slot_ref, nxt_ref, x_ref, wg_hbm, wu_hbm, wd_hbm, y_ref,
                  wg_f, wu_f, wd_f, wg_bf, wu_bf, wd_bf, sem):
    b = pl.program_id(0)
    used = b < nu_ref[0]

    def weights(e, s):
        return [pltpu.make_async_copy(hbm.at[e], buf.at[s], sem.at[s])
                for hbm, buf in ((wg_hbm, wg_f), (wu_hbm, wu_f), (wd_hbm, wd_f))]

    @pl.when(used & (first_ref[b] == 1))
    def _():
        s = slot_ref[b]

        @pl.when(b == 0)
        def _():
            for copy in weights(be_ref[0], 0):
                copy.start()

        for copy in weights(be_ref[b], s):
            copy.wait()
        wg_bf[...] = wg_f[s].astype(BF16)
        wu_bf[...] = wu_f[s].astype(BF16)
        wd_bf[...] = wd_f[s].astype(BF16)

        @pl.when(nxt_ref[b] >= 0)
        def _():
            for copy in weights(nxt_ref[b], 1 - s):
                copy.start()

    @pl.when(used)
    def _():
        x = _unpack_pairs(x_ref[...])
        g = jnp.dot(x, wg_bf[...], preferred_element_type=F32)
        u = jnp.dot(x, wu_bf[...], preferred_element_type=F32)
        a = (g * _sigmoid(g) * u).astype(BF16)
        y = jnp.dot(a, wd_bf[...], preferred_element_type=F32)
        y_ref[...] = _pack_pairs(y.astype(BF16).astype(F32))

    @pl.when(jnp.logical_not(used))
    def _():
        y_ref[...] = jnp.zeros(y_ref.shape, y_ref.dtype)


def _experts(xs, etab, w_gate, w_up, w_down):
    n_slots, dp = xs.shape
    d = w_gate.shape[1]
    n_blocks = n_slots // MOE_BLOCK
    last = lambda b, nu: jnp.maximum(jnp.minimum(b, nu[0] - 1), 0)
    hbm = pl.BlockSpec(memory_space=pl.ANY)
    return pl.pallas_call(
        _experts_body,
        grid_spec=pltpu.PrefetchScalarGridSpec(
            num_scalar_prefetch=5,
            grid=(n_blocks,),
            in_specs=[pl.BlockSpec((MOE_BLOCK, dp), lambda b, be, nu, *_: (last(b, nu), 0)), hbm, hbm, hbm],
            out_specs=pl.BlockSpec((MOE_BLOCK, dp), lambda b, *_: (b, 0)),
            scratch_shapes=[pltpu.VMEM((2, d, D_EXPERT), F32), pltpu.VMEM((2, d, D_EXPERT), F32),
                            pltpu.VMEM((2, D_EXPERT, d), F32),
                            pltpu.VMEM((d, D_EXPERT), BF16), pltpu.VMEM((d, D_EXPERT), BF16),
                            pltpu.VMEM((D_EXPERT, d), BF16), pltpu.SemaphoreType.DMA((2,))],
        ),
        out_shape=jax.ShapeDtypeStruct((n_slots, dp), jnp.uint32),
        compiler_params=_cparams(("arbitrary",)),
        name="experts",
    )(etab["block_e"], etab["n_used"], etab["first"], etab["slot"], etab["nxt"], xs, w_gate, w_up, w_down)


def _combine_body(cnt_ref, loff_ref, gb_ref, tot_ref, x_ref, route_ref, g_ref, ys_ref, o_ref, ybuf_ref, sem):
    i = pl.program_id(0)
    nt = pl.num_programs(0)
    slot = i % 2

    def gather(step, s):
        for k in range(MOE_TILES):
            def per_expert(e, c, k=k):
                j = (step * MOE_TILES + k) * N_EXPERTS + e

                @pl.when(cnt_ref[j] > 0)
                def _():
                    _rows_copy(ys_ref, gb_ref[j], ybuf_ref.at[s], k * SORT_ROWS + loff_ref[j], cnt_ref[j],
                               sem.at[s]).start()
                return c

            lax.fori_loop(0, N_EXPERTS, per_expert, 0)

    @pl.when(i == 0)
    def _():
        ybuf_ref[...] = jnp.zeros(ybuf_ref.shape, ybuf_ref.dtype)
        gather(0, 0)

    @pl.when(i + 1 < nt)
    def _():
        gather(i + 1, 1 - slot)

    def combine_tile(k):
        rows = slice(k * TOK_TILE, (k + 1) * TOK_TILE)
        route = route_ref[rows, :]
        c_i = lax.broadcasted_iota(jnp.int32, (TOK_TILE, SORT_ROWS), 1)
        lp1 = route[:, 4:5].astype(jnp.int32)
        lp2 = route[:, 5:6].astype(jnp.int32)
        pw = jnp.where(c_i == lp1, route[:, 2:3], jnp.where(c_i == lp2, route[:, 3:4], 0.0)).astype(BF16)
        yield
        r_i = lax.broadcasted_iota(jnp.int32, (SORT_ROWS, 1), 0)
        y = _unpack_pairs(jnp.where(r_i < tot_ref[i * MOE_TILES + k],
                                    ybuf_ref[slot, k * SORT_ROWS:(k + 1) * SORT_ROWS, :], jnp.uint32(0)))
        moe = jnp.dot(pw, y, preferred_element_type=F32)
        yield
        o_ref[rows, :] = _rms(x_ref[rows, :] + moe, g_ref[...])

    tiles = [combine_tile(k) for k in range(MOE_TILES)]
    for tile in tiles:
        next(tile)
    n = sum(tot_ref[i * MOE_TILES + k] for k in range(MOE_TILES))
    _rows_copy(ys_ref, 0, ybuf_ref.at[slot], 0, n, sem.at[slot]).wait()
    _interleave(tiles)


def _combine(x2, route, tables, ys, g_final):
    t, d = x2.shape
    step_rows = MOE_TILES * TOK_TILE
    tile = pl.BlockSpec((step_rows, d), lambda i, *_: (i, 0))
    return pl.pallas_call(
        _combine_body,
        grid_spec=pltpu.PrefetchScalarGridSpec(
            num_scalar_prefetch=4,
            grid=(t // step_rows,),
            in_specs=[
                tile,
                pl.BlockSpec((step_rows, ROUTE_LANES), lambda i, *_: (i, 0)),
                pl.BlockSpec((1, d), lambda i, *_: (0, 0)),
                pl.BlockSpec(memory_space=pl.ANY),
            ],
            out_specs=tile,
            scratch_shapes=[pltpu.VMEM((2, MOE_TILES * SORT_ROWS, d // 2), jnp.uint32),
                            pltpu.SemaphoreType.DMA((2,))],
        ),
        out_shape=jax.ShapeDtypeStruct((t, d), F32),
        compiler_params=_cparams(("arbitrary",)),
        name="combine",
    )(tables["cnt"], tables["loff"], tables["gb"], tables["tot"], x2, route, g_final.reshape(1, d), ys)


def _routing_tables(stats, n_tok):
    cnt = stats[:, 0, :N_EXPERTS].astype(jnp.int32)
    loff = stats[:, 1, :N_EXPERTS].astype(jnp.int32)
    n_tiles = cnt.shape[0]
    counts = jnp.sum(cnt, axis=0)
    padded = ((counts + MOE_BLOCK - 1) // MOE_BLOCK) * MOE_BLOCK
    pad_end = jnp.cumsum(padded)
    pad_start = pad_end - padded
    gb = pad_start[None, :] + jnp.cumsum(cnt, axis=0) - cnt
    max_rows = (n_tok * TOP_K + n_tiles * N_EXPERTS * (SUBLANES - 1)
                + N_EXPERTS * (MOE_BLOCK - SUBLANES))
    n_blocks = -(-max_rows // MOE_BLOCK)
    n_slots = n_blocks * MOE_BLOCK
    block_start = jnp.arange(n_blocks, dtype=jnp.int32) * MOE_BLOCK
    block_e = jnp.minimum(jnp.sum((pad_end[None, :] <= block_start[:, None]).astype(jnp.int32), axis=1),
                          N_EXPERTS - 1)
    tables = dict(cnt=cnt.reshape(-1), loff=loff.reshape(-1), gb=gb.reshape(-1).astype(jnp.int32),
                  tot=jnp.sum(cnt, axis=1).astype(jnp.int32),
                  fill_start=(pad_start + counts).astype(jnp.int32),
                  fill_cnt=(padded - counts).astype(jnp.int32))
    n_used = (pad_end[-1:] // MOE_BLOCK).astype(jnp.int32)
    first = jnp.concatenate([jnp.ones((1,), jnp.int32), (block_e[1:] != block_e[:-1]).astype(jnp.int32)])
    seg_slot = (jnp.cumsum(first) - 1) % 2
    experts = jnp.arange(N_EXPERTS, dtype=jnp.int32)
    later = jnp.where((experts[None, :] > experts[:, None]) & (padded[None, :] > 0), experts[None, :], N_EXPERTS)
    next_e = jnp.min(later, axis=1)
    next_e = jnp.where(next_e < N_EXPERTS, next_e, -1)
    nxt = jnp.sum(jnp.where(block_e[:, None] == experts[None, :], next_e[None, :], 0), axis=1)
    etab = dict(block_e=block_e, n_used=n_used, first=first, slot=seg_slot.astype(jnp.int32),
                nxt=nxt.astype(jnp.int32))
    return tables, etab, n_slots


def kernel(x, mem, g_mix, w_in, conv_w, conv_b, ln_g, ln_b, ssd_conv_w, ssd_conv_b, dt_bias, a_log, d_skip, ssd_norm_g, w_out, g_xattn, g_mem, w_q, w_k, w_v, w_o, g_moe, w_router_group, b_router_group, w_router_expert, b_router_expert, w_gate, w_up, w_down, g_final):
    bsz, seqlen, d = x.shape
    n_tok = bsz * seqlen
    xt = x.reshape(n_tok, d)
    assert g_mix.shape[0] == 1, "the combine kernel applies the final norm: single layer only"
    for l in range(1):
        u, zx, dt_raw = _mix_in(xt, bsz, seqlen, g_mix[l], w_in[l], conv_w[l], conv_b[l],
                                    ln_g[l], ln_b[l])
        y = _ssd(zx, dt_raw, bsz, seqlen, ssd_conv_w[l], ssd_conv_b[l], dt_bias[l], a_log[l],
                 d_skip[l], ssd_norm_g[l])
        k, v = _kv_proj(mem, g_mem[l], w_k[l], w_v[l])
        x2, h2, route, stats = _xattn_route(xt, u, y, w_out[l], k, v, bsz, seqlen, g_xattn[l], w_q[l],
                                            w_o[l], g_moe[l], w_router_group[l], b_router_group[l],
                                            w_router_expert[l], b_router_expert[l])
        tables, etab, n_slots = _routing_tables(stats, n_tok)
        xs = _dispatch(h2, route, tables, etab["n_used"], n_slots)
        ys = _experts(xs, etab, w_gate[l], w_up[l], w_down[l])
        xt = _combine(x2, route, tables, ys, g_final)
    return xt.reshape(bsz, seqlen, d)
```

```python
import functools

import jax
import jax.numpy as jnp
from jax import lax
from jax.experimental import pallas as pl
from jax.experimental.pallas import tpu as pltpu

F32 = jnp.float32
BF16 = jnp.bfloat16

D_MODEL = 1024
CONV_CH = 1024
CONV_K = 31
SSD_INNER = 1024
SSD_HEAD_DIM = 64
SSD_HEADS = 16
SSD_STATE = 128
SSD_GROUPS = 2
SSD_GROUP_W = SSD_INNER // SSD_GROUPS
SSD_CONV_K = 4
SSD_CHUNK = 128
SSD_BC = SSD_GROUPS * SSD_STATE
SSD_CONV_CH = SSD_INNER + 2 * SSD_BC
N_MAIN = 2 * CONV_CH + 2 * SSD_INNER + 2 * SSD_BC
XA_HEADS = 4
XA_HEAD_DIM = 256
MEM_LEN = 256
N_GROUPS = 4
EXPERTS_PER_GROUP = 8
N_EXPERTS = 32
TOP_K = 2
D_EXPERT = 512
MOE_BLOCK = 512
RMS_EPS = 1e-6
LN_EPS = 1e-5
LOG2_E = 1.4426950408889634

LANES = 128
SUBLANES = 8
VMEM_LIMIT = 56 * 1024 * 1024

TOK_TILE = 512
CONV_TILE = 512
ZX_Z0 = -(-SSD_CONV_CH // SSD_INNER) * SSD_INNER
ZX_W = ZX_Z0 + SSD_INNER
ZX_SHARES = (0, 512, 512, 512, 512, 512, 512, 0)
assert sum(ZX_SHARES) == ZX_W and len(ZX_SHARES) == CONV_CH // LANES
CONV_HALO = 32
CONV_ROWS = 512
SSD_TILE = 1024
MOE_TILES = 2
XA_TILES = 2
SSD_UNROLL = 8
SSD_HALO = 8
ROUTE_LANES = 128
SORT_ROWS = TOP_K * TOK_TILE + N_EXPERTS * SUBLANES


def _cparams(sem):
    return pltpu.CompilerParams(dimension_semantics=sem, vmem_limit_bytes=VMEM_LIMIT)


def _rms(x, g):
    return x * lax.rsqrt(jnp.mean(x * x, axis=-1, keepdims=True) + RMS_EPS) * g


def _sigmoid(x):
    return 1.0 / (1.0 + jnp.exp2(x * (-LOG2_E)))


def _interleave(stages):
    results = {}
    while len(results) < len(stages):
        for k, item in enumerate(stages):
            if k not in results:
                try:
                    next(item)
                except StopIteration as done:
                    results[k] = done.value
    return [results[k] for k in range(len(stages))]


def _kv_body(m_ref, g_ref, wk_ref, wv_ref, k_ref, v_ref):
    m = _rms(m_ref[0], g_ref[...]).astype(BF16)
    k_ref[0] = jnp.dot(m, wk_ref[...], preferred_element_type=F32).astype(BF16)
    v_ref[0] = jnp.dot(m, wv_ref[...], preferred_element_type=F32).astype(BF16)


def _kv_proj(mem, g_mem, w_k, w_v):
    b, s, d = mem.shape
    w_spec = pl.BlockSpec((d, d), lambda i: (0, 0))
    kv_spec = pl.BlockSpec((1, s, d), lambda i: (i, 0, 0))
    return pl.pallas_call(
        _kv_body,
        grid=(b,),
        in_specs=[kv_spec, pl.BlockSpec((1, d), lambda i: (0, 0)), w_spec, w_spec],
        out_specs=[kv_spec, kv_spec],
        out_shape=[jax.ShapeDtypeStruct((b, s, d), BF16)] * 2,
        compiler_params=_cparams(("arbitrary",)),
        name="kv_proj",
    )(mem, g_mem.reshape(1, d), w_k.astype(BF16), w_v.astype(BF16))


def _mix_in_body(x_ref, g_ref, w_ref, wdt_ref, cw_ref, cb_ref, lg_ref, lb_ref,
                 u_ref, zx_ref, dt_ref, ubuf_ref, acc_ref):
    @pl.when(pl.program_id(1) == 0)
    def _():
        ubuf_ref[0:CONV_HALO, :] = jnp.zeros((CONV_HALO, CONV_CH), F32)

    h = _rms(x_ref[...], g_ref[...]).astype(BF16)
    dt_ref[...] = jnp.dot(h, wdt_ref[...], preferred_element_type=F32)

    first = CONV_HALO - (CONV_K - 1)
    w_lo = zx_lo = 0
    for cb, zx_w in enumerate(ZX_SHARES):
        cols = slice(cb * LANES, (cb + 1) * LANES)
        blk_w = 2 * LANES + zx_w
        r = jnp.dot(h, w_ref[:, w_lo:w_lo + blk_w], preferred_element_type=F32)
        ubuf_ref[CONV_HALO:CONV_HALO + CONV_TILE, cols] = r[:, :LANES] * _sigmoid(r[:, LANES:2 * LANES])
        if zx_w:
            zx_ref[:, zx_lo:zx_lo + zx_w] = r[:, 2 * LANES:].astype(BF16)
        w_lo += blk_w
        zx_lo += zx_w
        for rc in range(CONV_TILE // CONV_ROWS):
            r0 = rc * CONV_ROWS
            acc = None
            for res in range(SUBLANES):
                part = None
                for k in range(CONV_K):
                    if (first + k) % SUBLANES == res:
                        term = cw_ref[k:k + 1, cols] * ubuf_ref[r0 + first + k:r0 + first + k + CONV_ROWS, cols]
                        part = term if part is None else part + term
                acc = part if acc is None else acc + part
            acc_ref[r0:r0 + CONV_ROWS, cols] = acc

    for rc in range(CONV_TILE // CONV_ROWS):
        rows = slice(rc * CONV_ROWS, (rc + 1) * CONV_ROWS)
        u = acc_ref[rows, :] + cb_ref[...]
        mu = jnp.mean(u, axis=-1, keepdims=True)
        uc = u - mu
        var = jnp.mean(uc * uc, axis=-1, keepdims=True)
        y = uc * lax.rsqrt(var + LN_EPS) * lg_ref[...] + lb_ref[...]
        u_ref[rows, :] = (y * _sigmoid(y)).astype(BF16)

    ubuf_ref[0:CONV_HALO, :] = ubuf_ref[CONV_TILE:CONV_TILE + CONV_HALO, :]


def _mix_in(x2d, bsz, seqlen, g_mix, w_in, conv_w, conv_b, ln_g, ln_b):
    t, d = x2d.shape
    nt = seqlen // CONV_TILE
    w = w_in.astype(BF16)
    z0 = 2 * CONV_CH
    x0 = z0 + SSD_INNER
    w_zx = jnp.concatenate([w[:, x0:N_MAIN], jnp.zeros((d, ZX_Z0 - SSD_CONV_CH), BF16), w[:, z0:x0]], axis=1)
    parts, zx_lo = [], 0
    for cb, zx_w in enumerate(ZX_SHARES):
        parts += [w[:, cb * LANES:(cb + 1) * LANES], w[:, CONV_CH + cb * LANES:CONV_CH + (cb + 1) * LANES],
                  w_zx[:, zx_lo:zx_lo + zx_w]]
        zx_lo += zx_w
    w_blocks = jnp.concatenate(parts, axis=1)
    w_dt = jnp.pad(w_in[:, N_MAIN:], ((0, 0), (0, LANES - SSD_HEADS))).astype(BF16)
    row = lambda v: v.reshape(1, -1)
    const = lambda shape: pl.BlockSpec(shape, lambda b, j: (0, 0))
    tile = lambda width: pl.BlockSpec((CONV_TILE, width), lambda b, j: (b * nt + j, 0))
    return pl.pallas_call(
        _mix_in_body,
        grid=(bsz, nt),
        in_specs=[
            tile(d), const((1, d)), const(w_blocks.shape), const((d, LANES)),
            const((CONV_K + 1, CONV_CH)), const((1, CONV_CH)), const((1, CONV_CH)), const((1, CONV_CH)),
        ],
        out_specs=[tile(CONV_CH), tile(ZX_W), tile(LANES)],
        out_shape=[jax.ShapeDtypeStruct((t, CONV_CH), BF16), jax.ShapeDtypeStruct((t, ZX_W), BF16),
                   jax.ShapeDtypeStruct((t, LANES), F32)],
        scratch_shapes=[
            pltpu.VMEM((CONV_HALO + CONV_TILE, CONV_CH), F32),
            pltpu.VMEM((CONV_TILE, CONV_CH), F32),
        ],
        compiler_params=_cparams(("arbitrary", "arbitrary")),
        name="mix_in",
    )(x2d, row(g_mix), w_blocks, w_dt, jnp.pad(conv_w, ((0, 1), (0, 0))), row(conv_b), row(ln_g), row(ln_b))


def _ssd_body(xbc_ref, z_ref, dt_ref, cw_ref, cb_ref, dtb_ref, alog_ref, dsk_ref, ng_ref,
              expand_ref, y_ref, xbuf_ref, act_ref, dts_ref, state_ref):
    @pl.when(pl.program_id(1) == 0)
    def _():
        xbuf_ref[0:SSD_HALO, :] = jnp.zeros((SSD_HALO, SSD_CONV_CH), F32)
        state_ref[...] = jnp.zeros(state_ref.shape, F32)

    xbuf_ref[SSD_HALO:SSD_HALO + SSD_TILE, :] = xbc_ref[...].astype(F32)
    full = xbuf_ref[...]
    conv = cb_ref[...] + cw_ref[SSD_CONV_K - 1:SSD_CONV_K, :] * full[SSD_HALO:, :]
    for back in range(1, SSD_CONV_K):
        past = pltpu.roll(full, back, 0)[SSD_HALO:, :]
        conv = conv + cw_ref[SSD_CONV_K - 1 - back:SSD_CONV_K - back, :] * past
    act_ref[...] = conv * _sigmoid(conv)
    xbuf_ref[0:SSD_HALO, :] = xbuf_ref[SSD_TILE:SSD_TILE + SSD_HALO, :]

    dt_in = dt_ref[...] + dtb_ref[...]
    dts_ref[...] = jnp.maximum(dt_in, 0.0) + jnp.log1p(jnp.exp(-jnp.abs(dt_in)))

    a_neg = -jnp.exp(alog_ref[...]) * LOG2_E
    q = SSD_CHUNK
    row_i = lax.broadcasted_iota(jnp.int32, (q, q), 0)
    col_i = lax.broadcasted_iota(jnp.int32, (q, q), 1)
    causal = row_i >= col_i
    tril = causal.astype(F32)
    lane_i = lax.broadcasted_iota(jnp.int32, (q, LANES), 1)
    low_half = lane_i < SSD_HEAD_DIM
    expand = expand_ref[...]

    def chunk(c, carry):
        r0 = pl.multiple_of(c * q, q)
        rows = pl.ds(r0, q)
        dtc = dts_ref[rows, :]
        a_cs = jnp.dot(tril, dtc * a_neg, preferred_element_type=F32,
                       precision=lax.Precision.HIGHEST)
        a_cs_t = a_cs.T
        dt_t = dtc.T.astype(BF16)
        a_end = a_cs[q - 1:q, :]
        e_exp = jnp.dot(jnp.exp2(a_cs).astype(BF16), expand, preferred_element_type=F32)
        w_exp = jnp.dot((jnp.exp2(a_end - a_cs) * dtc).astype(BF16), expand,
                        preferred_element_type=F32)
        dec_row = e_exp[q - 1:q, :]
        xc = act_ref[rows, 0:SSD_INNER]
        xw = (xc * w_exp).astype(BF16)
        y_parts = []
        for g in range(SSD_GROUPS):
            b_f = act_ref[rows, SSD_INNER + g * SSD_STATE:SSD_INNER + (g + 1) * SSD_STATE]
            c_f = act_ref[rows, SSD_INNER + SSD_BC + g * SSD_STATE:
                          SSD_INNER + SSD_BC + (g + 1) * SSD_STATE]
            b_g = b_f.astype(BF16)
            c_g = c_f.astype(BF16)
            cb = lax.dot_general(c_g, b_g, (((1,), (1,)), ((), ())),
                                 preferred_element_type=F32).astype(BF16)
            gcols = slice(g * SSD_GROUP_W, (g + 1) * SSD_GROUP_W)
            st = state_ref[g]
            y_off = jnp.dot(c_g, st.astype(BF16), preferred_element_type=F32)
            state_ref[g] = st * dec_row[:, gcols] + jnp.dot(
                b_f.T.astype(BF16), xw[:, gcols], preferred_element_type=F32)
            for pair in range(SSD_GROUP_W // LANES):
                ms = []
                for hh in range(2):
                    h = g * (SSD_HEADS // SSD_GROUPS) + 2 * pair + hh
                    seg = a_cs[:, h:h + 1] - a_cs_t[h:h + 1, :]
                    dec = jnp.exp2(jnp.where(causal, seg, -jnp.inf))
                    ms.append(cb * dec.astype(BF16) * dt_t[h:h + 1, :])
                lhs = jnp.concatenate(ms, axis=1)
                xp = xc[:, g * SSD_GROUP_W + pair * LANES:g * SSD_GROUP_W + (pair + 1) * LANES]
                rhs = jnp.concatenate([jnp.where(low_half, xp, 0.0),
                                       jnp.where(low_half, 0.0, xp)], axis=0).astype(BF16)
                y_diag = jnp.dot(lhs, rhs, preferred_element_type=F32)
                lo = pair * LANES
                y_parts.append(y_diag + y_off[:, lo:lo + LANES]
                               * e_exp[:, g * SSD_GROUP_W + lo:g * SSD_GROUP_W + lo + LANES])
        y = jnp.concatenate(y_parts, axis=1) + xc * dsk_ref[...]
        z = z_ref[rows, :].astype(F32)
        y = y * (z * _sigmoid(z))
        outs = []
        for g in range(SSD_GROUPS):
            yg = y[:, g * SSD_GROUP_W:(g + 1) * SSD_GROUP_W]
            outs.append(yg * lax.rsqrt(jnp.mean(yg * yg, axis=-1, keepdims=True) + RMS_EPS))
        y_ref[rows, :] = (jnp.concatenate(outs, axis=1) * ng_ref[...]).astype(BF16)
        return carry

    lax.fori_loop(0, SSD_TILE // q, chunk, 0, unroll=SSD_UNROLL)


def _ssd(zx, dt_raw, bsz, seqlen, ssd_conv_w, ssd_conv_b, dt_bias, a_log, d_skip, ssd_norm_g):
    nt = seqlen // SSD_TILE
    pad_h = lambda v: jnp.pad(v, (0, LANES - SSD_HEADS)).reshape(1, LANES)
    expand = (jnp.arange(LANES)[:, None] == (jnp.arange(SSD_INNER) // SSD_HEAD_DIM)[None, :]).astype(BF16)
    const = lambda shape: pl.BlockSpec(shape, lambda b, j: (0, 0))
    return pl.pallas_call(
        _ssd_body,
        grid=(bsz, nt),
        in_specs=[
            pl.BlockSpec((SSD_TILE, SSD_CONV_CH), lambda b, j: (b * nt + j, 0)),
            pl.BlockSpec((SSD_TILE, SSD_INNER), lambda b, j: (b * nt + j, ZX_Z0 // SSD_INNER)),
            pl.BlockSpec((SSD_TILE, LANES), lambda b, j: (b * nt + j, 0)),
            const((SSD_CONV_K, SSD_CONV_CH)),
            const((1, SSD_CONV_CH)),
            const((1, LANES)), const((1, LANES)),
            const((1, SSD_INNER)), const((1, SSD_INNER)),
            const((LANES, SSD_INNER)),
        ],
        out_specs=pl.BlockSpec((SSD_TILE, SSD_INNER), lambda b, j: (b * nt + j, 0)),
        out_shape=jax.ShapeDtypeStruct((bsz * seqlen, SSD_INNER), BF16),
        scratch_shapes=[
            pltpu.VMEM((SSD_HALO + SSD_TILE, SSD_CONV_CH), F32),
            pltpu.VMEM((SSD_TILE, SSD_CONV_CH), F32),
            pltpu.VMEM((SSD_TILE, LANES), F32),
            pltpu.VMEM((SSD_GROUPS, SSD_STATE, SSD_GROUP_W), F32),
        ],
        compiler_params=_cparams(("arbitrary", "arbitrary")),
        name="ssd",
    )(zx, zx, dt_raw, ssd_conv_w, ssd_conv_b.reshape(1, SSD_CONV_CH), pad_h(dt_bias),
      pad_h(a_log), jnp.repeat(d_skip, SSD_HEAD_DIM).reshape(1, SSD_INNER),
      ssd_norm_g.reshape(1, SSD_INNER), expand)


def _xattn_body(x_ref, u_ref, y_ref, wu_ref, wy_ref, k_ref, v_ref, gx_ref, wq_ref, wo_ref, gm_ref,
                wr_ref, br_ref, x2_ref, h2_ref, route_ref, stats_ref):
    def tile(s):
        rows = slice(s * TOK_TILE, (s + 1) * TOK_TILE)
        return _xattn_tile(x_ref[rows, :], u_ref[rows, :], y_ref[rows, :], wu_ref, wy_ref, k_ref, v_ref,
                           gx_ref, wq_ref, wo_ref, gm_ref, wr_ref, br_ref)

    results = _interleave([tile(s) for s in range(XA_TILES)])
    for s in range(XA_TILES):
        rows = slice(s * TOK_TILE, (s + 1) * TOK_TILE)
        x2_ref[rows, :], h2_ref[rows, :], route_ref[rows, :], stats_ref[s] = results[s]


def _xattn_tile(x, u, y, wu_ref, wy_ref, k_ref, v_ref, gx_ref, wq_ref, wo_ref, gm_ref, wr_ref, br_ref):
    x = (x + jnp.dot(u, wu_ref[...], preferred_element_type=F32)
         + jnp.dot(y, wy_ref[...], preferred_element_type=F32))
    yield
    h = _rms(x, gx_ref[...]).astype(BF16)
    q = (jnp.dot(h, wq_ref[...], preferred_element_type=F32) * (XA_HEAD_DIM ** -0.5)).astype(BF16)
    yield
    heads = []
    for i in range(XA_HEADS):
        cols = slice(i * XA_HEAD_DIM, (i + 1) * XA_HEAD_DIM)
        s = jnp.dot(q[:, cols], k_ref[0, cols, :], preferred_element_type=F32)
        p = jnp.exp(s - jnp.max(s, axis=-1, keepdims=True))
        p = p / jnp.sum(p, axis=-1, keepdims=True)
        heads.append(jnp.dot(p.astype(BF16), v_ref[0, :, cols], preferred_element_type=F32))
        yield
    o = jnp.concatenate(heads, axis=1).astype(BF16)
    x2 = x + jnp.dot(o, wo_ref[...], preferred_element_type=F32)
    yield

    h2 = _rms(x2, gm_ref[...])
    h_hi = h2.astype(BF16)
    h_lo = (h2 - h_hi.astype(F32)).astype(BF16)
    both = jnp.dot(h_hi, wr_ref[...], preferred_element_type=F32)
    logits = (both[:, :ROUTE_LANES] + both[:, ROUTE_LANES:]
              + jnp.dot(h_lo, wr_ref[:, :ROUTE_LANES], preferred_element_type=F32)) + br_ref[...]
    yield
    lane = lax.broadcasted_iota(jnp.int32, logits.shape, 1)
    neg = -jnp.inf

    def first_argmax(v):
        m = jnp.max(v, axis=-1, keepdims=True)
        return m, jnp.min(jnp.where(v == m, lane, ROUTE_LANES), axis=-1, keepdims=True)

    gl = jnp.where(lane < N_GROUPS, logits, neg)
    g_max, g_sel = first_argmax(gl)
    p_top = 1.0 / jnp.sum(jnp.exp(gl - g_max), axis=-1, keepdims=True)
    e_lo = N_GROUPS + EXPERTS_PER_GROUP * g_sel
    el = jnp.where((lane >= e_lo) & (lane < e_lo + EXPERTS_PER_GROUP), logits, neg)
    m1, i1 = first_argmax(el)
    m2, i2 = first_argmax(jnp.where(lane == i1, neg, el))
    r = jnp.exp(m2 - m1)
    w1 = p_top / (1.0 + r)
    w2 = w1 * r
    yield
    e1 = i1 - N_GROUPS
    e2 = i2 - N_GROUPS
    oh1 = (lane == e1).astype(BF16)
    oh2 = (lane == e2).astype(BF16)
    n_t = logits.shape[0]
    before = (lax.broadcasted_iota(jnp.int32, (n_t, n_t), 0)
              > lax.broadcasted_iota(jnp.int32, (n_t, n_t), 1)).astype(BF16)
    c12 = jnp.dot(before, jnp.concatenate([oh1, oh2], axis=1), preferred_element_type=F32)
    c1 = c12[:, :ROUTE_LANES]
    c2 = c12[:, ROUTE_LANES:]
    tot1 = jnp.sum(oh1.astype(F32), axis=0, keepdims=True)
    cnt = tot1 + jnp.sum(oh2.astype(F32), axis=0, keepdims=True)
    cnt = jnp.floor((cnt + (SUBLANES - 1)) * (1.0 / SUBLANES)) * SUBLANES
    lanes_before = (lax.broadcasted_iota(jnp.int32, (ROUTE_LANES, ROUTE_LANES), 0)
                    < lax.broadcasted_iota(jnp.int32, (ROUTE_LANES, ROUTE_LANES), 1)).astype(F32)
    loff = jnp.dot(jnp.broadcast_to(cnt, (SUBLANES, ROUTE_LANES)), lanes_before,
                   preferred_element_type=F32, precision=lax.Precision.HIGHEST)[0:1, :]
    lp1 = jnp.sum(oh1.astype(F32) * (loff + c1), axis=-1, keepdims=True)
    lp2 = jnp.sum(oh2.astype(F32) * (loff + tot1 + c2), axis=-1, keepdims=True)
    route = jnp.where(lane == 0, e1.astype(F32),
            jnp.where(lane == 1, e2.astype(F32),
            jnp.where(lane == 2, w1,
            jnp.where(lane == 3, w2,
            jnp.where(lane == 4, lp1, jnp.where(lane == 5, lp2, 0.0))))))
    row = lax.broadcasted_iota(jnp.int32, (SUBLANES, ROUTE_LANES), 0)
    return x2, h_hi, route, jnp.where(row == 0, cnt, jnp.where(row == 1, loff, 0.0))


def _xattn_route(x, u, y, w_out, k, v, bsz, seqlen, g_xattn, w_q, w_o, g_moe, w_rg, b_rg, w_re, b_re):
    t, d = x.shape
    w_mix = w_out.astype(BF16)
    w_r = jnp.pad(jnp.concatenate([w_rg, w_re], axis=1), ((0, 0), (0, ROUTE_LANES - N_GROUPS - N_EXPERTS)))
    b_r = jnp.pad(jnp.concatenate([b_rg, b_re]), (0, ROUTE_LANES - N_GROUPS - N_EXPERTS)).reshape(1, ROUTE_LANES)
    wr_hi = w_r.astype(BF16)
    wr = jnp.concatenate([wr_hi, (w_r - wr_hi.astype(F32)).astype(BF16)], axis=1)
    step_rows = XA_TILES * TOK_TILE
    nt = seqlen // step_rows
    tile = pl.BlockSpec((step_rows, d), lambda b, j: (b * nt + j, 0))
    kv_spec = pl.BlockSpec((1, MEM_LEN, d), lambda b, j: (b, 0, 0))
    const = lambda shape: pl.BlockSpec(shape, lambda b, j: (0, 0))
    return pl.pallas_call(
        _xattn_body,
        grid=(bsz, nt),
        in_specs=[tile, tile, tile, const((d, d)), const((d, d)),
                  pl.BlockSpec((1, d, MEM_LEN), lambda b, j: (b, 0, 0)), kv_spec, const((1, d)),
                  const((d, d)), const((d, d)), const((1, d)),
                  const((d, 2 * ROUTE_LANES)), const((1, ROUTE_LANES))],
        out_specs=[tile, tile, pl.BlockSpec((step_rows, ROUTE_LANES), lambda b, j: (b * nt + j, 0)),
                   pl.BlockSpec((XA_TILES, SUBLANES, ROUTE_LANES), lambda b, j: (b * nt + j, 0, 0))],
        out_shape=[jax.ShapeDtypeStruct((t, d), F32), jax.ShapeDtypeStruct((t, d), BF16),
                   jax.ShapeDtypeStruct((t, ROUTE_LANES), F32),
                   jax.ShapeDtypeStruct((t // TOK_TILE, SUBLANES, ROUTE_LANES), F32)],
        compiler_params=_cparams(("arbitrary", "arbitrary")),
        name="xattn_route",
    )(x, u, y, w_mix[:CONV_CH], w_mix[CONV_CH:], jnp.swapaxes(k, 1, 2), v, g_xattn.reshape(1, d), w_q.astype(BF16), w_o.astype(BF16), g_moe.reshape(1, d),
      wr, b_r)


def _rows_copy(src_ref, src0, dst_ref, dst0, n, sem):
    rows = lambda r0: pl.ds(pl.multiple_of(r0, SUBLANES), pl.multiple_of(n, SUBLANES))
    return pltpu.make_async_copy(src_ref.at[rows(src0), :], dst_ref.at[rows(dst0), :], sem)


def _pack_pairs(v):
    half = v.shape[1] // 2
    lo = lax.bitcast_convert_type(v[:, :half], jnp.uint32)
    hi = lax.bitcast_convert_type(v[:, half:], jnp.uint32)
    return (lo >> 16) | (hi & jnp.uint32(0xFFFF0000))


def _unpack_pairs(w):
    lo = lax.bitcast_convert_type(w << 16, F32)
    hi = lax.bitcast_convert_type(w & jnp.uint32(0xFFFF0000), F32)
    return jnp.concatenate([lo, hi], axis=1).astype(BF16)


def _dispatch_body(cnt_ref, loff_ref, gb_ref, tot_ref, fstart_ref, fcnt_ref, nu_ref, h_ref, route_ref,
                   xs_ref, buf_ref, zero_ref, sem, zsem):
    i = pl.program_id(0)
    nt = pl.num_programs(0)
    slot = i % 2
    buf = buf_ref.at[slot]

    def drain(step, s):
        n = sum(tot_ref[step * MOE_TILES + k] for k in range(MOE_TILES))
        _rows_copy(buf_ref.at[s], 0, xs_ref, 0, n, sem.at[s]).wait()

    @pl.when(i >= 2)
    def _():
        drain(i - 2, slot)

    def sort_tile(k):
        rows = slice(k * TOK_TILE, (k + 1) * TOK_TILE)
        rt = route_ref[rows, :].T
        lp1 = rt[4:5, :].astype(jnp.int32)
        lp2 = rt[5:6, :].astype(jnp.int32)
        r_i = lax.broadcasted_iota(jnp.int32, (SORT_ROWS, TOK_TILE), 0)
        perm = jnp.where((r_i == lp1) | (r_i == lp2), 1.0, 0.0).astype(BF16)
        yield
        buf[k * SORT_ROWS:(k + 1) * SORT_ROWS, :] = _pack_pairs(
            jnp.dot(perm, h_ref[rows, :], preferred_element_type=F32))

    _interleave([sort_tile(k) for k in range(MOE_TILES)])

    for k in range(MOE_TILES):
        def per_expert(e, c, k=k):
            j = (i * MOE_TILES + k) * N_EXPERTS + e

            @pl.when(cnt_ref[j] > 0)
            def _():
                _rows_copy(buf, k * SORT_ROWS + loff_ref[j], xs_ref, gb_ref[j], cnt_ref[j],
                           sem.at[slot]).start()
            return c

        lax.fori_loop(0, N_EXPERTS, per_expert, 0)

    @pl.when(i == 0)
    def _():
        zero_ref[...] = jnp.zeros(zero_ref.shape, zero_ref.dtype)

        def fill(start):
            def body(e, c):
                @pl.when(fcnt_ref[e] > 0)
                def _():
                    copy = _rows_copy(zero_ref, 0, xs_ref, fstart_ref[e], fcnt_ref[e], zsem)
                    copy.start() if start else copy.wait()
                return c
            lax.fori_loop(0, N_EXPERTS, body, 0)

            def tail(blk, c):
                copy = _rows_copy(zero_ref, 0, xs_ref, blk * MOE_BLOCK, MOE_BLOCK, zsem)
                copy.start() if start else copy.wait()
                return c
            lax.fori_loop(nu_ref[0], xs_ref.shape[0] // MOE_BLOCK, tail, 0)

        fill(True)
        fill(False)

    @pl.when(i == nt - 1)
    def _():
        drain(i, slot)

        @pl.when(i >= 1)
        def _():
            drain(i - 1, 1 - slot)


def _dispatch(h2, route, tables, n_used, n_slots):
    t, d = h2.shape
    step_rows = MOE_TILES * TOK_TILE
    return pl.pallas_call(
        _dispatch_body,
        grid_spec=pltpu.PrefetchScalarGridSpec(
            num_scalar_prefetch=7,
            grid=(t // step_rows,),
            in_specs=[
                pl.BlockSpec((step_rows, d), lambda i, *_: (i, 0)),
                pl.BlockSpec((step_rows, ROUTE_LANES), lambda i, *_: (i, 0)),
            ],
            out_specs=pl.BlockSpec(memory_space=pl.ANY),
            scratch_shapes=[pltpu.VMEM((2, MOE_TILES * SORT_ROWS, d // 2), jnp.uint32),
                            pltpu.VMEM((MOE_BLOCK, d // 2), jnp.uint32),
                            pltpu.SemaphoreType.DMA((2,)), pltpu.SemaphoreType.DMA(())],
        ),
        out_shape=jax.ShapeDtypeStruct((n_slots, d // 2), jnp.uint32),
        compiler_params=_cparams(("arbitrary",)),
        name="dispatch",
    )(tables["cnt"], tables["loff"], tables["gb"], tables["tot"], tables["fill_start"],
      tables["fill_cnt"], n_used, h2, route)


def _experts_body(be_ref, nu_ref, first_ref, slot_ref, nxt_ref, x_ref, wg_hbm, wu_hbm, wd_hbm, y_ref,
                  wg_f, wu_f, wd_f, wg_bf, wu_bf, wd_bf, sem):
    b = pl.program_id(0)
    used = b < nu_ref[0]

    def weights(e, s):
        return [pltpu.make_async_copy(hbm.at[e], buf.at[s], sem.at[s])
                for hbm, buf in ((wg_hbm, wg_f), (wu_hbm, wu_f), (wd_hbm, wd_f))]

    @pl.when(used & (first_ref[b] == 1))
    def _():
        s = slot_ref[b]

        @pl.when(b == 0)
        def _():
            for copy in weights(be_ref[0], 0):
                copy.start()

        for copy in weights(be_ref[b], s):
            copy.wait()
        wg_bf[...] = wg_f[s].astype(BF16)
        wu_bf[...] = wu_f[s].astype(BF16)
        wd_bf[...] = wd_f[s].astype(BF16)

        @pl.when(nxt_ref[b] >= 0)
        def _():
            for copy in weights(nxt_ref[b], 1 - s):
                copy.start()

    @pl.when(used)
    def _():
        x = _unpack_pairs(x_ref[...])
        g = jnp.dot(x, wg_bf[...], preferred_element_type=F32)
        u = jnp.dot(x, wu_bf[...], preferred_element_type=F32)
        a = (g * _sigmoid(g) * u).astype(BF16)
        y = jnp.dot(a, wd_bf[...], preferred_element_type=F32)
        y_ref[...] = _pack_pairs(y.astype(BF16).astype(F32))

    @pl.when(jnp.logical_not(used))
    def _():
        y_ref[...] = jnp.zeros(y_ref.shape, y_ref.dtype)


def _experts(xs, etab, w_gate, w_up, w_down):
    n_slots, dp = xs.shape
    d = w_gate.shape[1]
    n_blocks = n_slots // MOE_BLOCK
    last = lambda b, nu: jnp.maximum(jnp.minimum(b, nu[0] - 1), 0)
    hbm = pl.BlockSpec(memory_space=pl.ANY)
    return pl.pallas_call(
        _experts_body,
        grid_spec=pltpu.PrefetchScalarGridSpec(
            num_scalar_prefetch=5,
            grid=(n_blocks,),
            in_specs=[pl.BlockSpec((MOE_BLOCK, dp), lambda b, be, nu, *_: (last(b, nu), 0)), hbm, hbm, hbm],
            out_specs=pl.BlockSpec((MOE_BLOCK, dp), lambda b, *_: (b, 0)),
            scratch_shapes=[pltpu.VMEM((2, d, D_EXPERT), F32), pltpu.VMEM((2, d, D_EXPERT), F32),
                            pltpu.VMEM((2, D_EXPERT, d), F32),
                            pltpu.VMEM((d, D_EXPERT), BF16), pltpu.VMEM((d, D_EXPERT), BF16),
                            pltpu.VMEM((D_EXPERT, d), BF16), pltpu.SemaphoreType.DMA((2,))],
        ),
        out_shape=jax.ShapeDtypeStruct((n_slots, dp), jnp.uint32),
        compiler_params=_cparams(("arbitrary",)),
        name="experts",
    )(etab["block_e"], etab["n_used"], etab["first"], etab["slot"], etab["nxt"], xs, w_gate, w_up, w_down)


def _combine_body(cnt_ref, loff_ref, gb_ref, tot_ref, x_ref, route_ref, g_ref, ys_ref, o_ref, ybuf_ref, sem):
    i = pl.program_id(0)
    nt = pl.num_programs(0)
    slot = i % 2

    def gather(step, s):
        for k in range(MOE_TILES):
            def per_expert(e, c, k=k):
                j = (step * MOE_TILES + k) * N_EXPERTS + e

                @pl.when(cnt_ref[j] > 0)
                def _():
                    _rows_copy(ys_ref, gb_ref[j], ybuf_ref.at[s], k * SORT_ROWS + loff_ref[j], cnt_ref[j],
                               sem.at[s]).start()
                return c

            lax.fori_loop(0, N_EXPERTS, per_expert, 0)

    @pl.when(i == 0)
    def _():
        ybuf_ref[...] = jnp.zeros(ybuf_ref.shape, ybuf_ref.dtype)
        gather(0, 0)

    @pl.when(i + 1 < nt)
    def _():
        gather(i + 1, 1 - slot)

    def combine_tile(k):
        rows = slice(k * TOK_TILE, (k + 1) * TOK_TILE)
        route = route_ref[rows, :]
        c_i = lax.broadcasted_iota(jnp.int32, (TOK_TILE, SORT_ROWS), 1)
        lp1 = route[:, 4:5].astype(jnp.int32)
        lp2 = route[:, 5:6].astype(jnp.int32)
        pw = jnp.where(c_i == lp1, route[:, 2:3], jnp.where(c_i == lp2, route[:, 3:4], 0.0)).astype(BF16)
        yield
        r_i = lax.broadcasted_iota(jnp.int32, (SORT_ROWS, 1), 0)
        y = _unpack_pairs(jnp.where(r_i < tot_ref[i * MOE_TILES + k],
                                    ybuf_ref[slot, k * SORT_ROWS:(k + 1) * SORT_ROWS, :], jnp.uint32(0)))
        moe = jnp.dot(pw, y, preferred_element_type=F32)
        yield
        o_ref[rows, :] = _rms(x_ref[rows, :] + moe, g_ref[...])

    tiles = [combine_tile(k) for k in range(MOE_TILES)]
    for tile in tiles:
        next(tile)
    n = sum(tot_ref[i * MOE_TILES + k] for k in range(MOE_TILES))
    _rows_copy(ys_ref, 0, ybuf_ref.at[slot], 0, n, sem.at[slot]).wait()
    _interleave(tiles)


def _combine(x2, route, tables, ys, g_final):
    t, d = x2.shape
    step_rows = MOE_TILES * TOK_TILE
    tile = pl.BlockSpec((step_rows, d), lambda i, *_: (i, 0))
    return pl.pallas_call(
        _combine_body,
        grid_spec=pltpu.PrefetchScalarGridSpec(
            num_scalar_prefetch=4,
            grid=(t // step_rows,),
            in_specs=[
                tile,
                pl.BlockSpec((step_rows, ROUTE_LANES), lambda i, *_: (i, 0)),
                pl.BlockSpec((1, d), lambda i, *_: (0, 0)),
                pl.BlockSpec(memory_space=pl.ANY),
            ],
            out_specs=tile,
            scratch_shapes=[pltpu.VMEM((2, MOE_TILES * SORT_ROWS, d // 2), jnp.uint32),
                            pltpu.SemaphoreType.DMA((2,))],
        ),
        out_shape=jax.ShapeDtypeStruct((t, d), F32),
        compiler_params=_cparams(("arbitrary",)),
        name="combine",
    )(tables["cnt"], tables["loff"], tables["gb"], tables["tot"], x2, route, g_final.reshape(1, d), ys)


def _routing_tables(stats, n_tok):
    cnt = stats[:, 0, :N_EXPERTS].astype(jnp.int32)
    loff = stats[:, 1, :N_EXPERTS].astype(jnp.int32)
    n_tiles = cnt.shape[0]
    counts = jnp.sum(cnt, axis=0)
    padded = ((counts + MOE_BLOCK - 1) // MOE_BLOCK) * MOE_BLOCK
    pad_end = jnp.cumsum(padded)
    pad_start = pad_end - padded
    gb = pad_start[None, :] + jnp.cumsum(cnt, axis=0) - cnt
    max_rows = (n_tok * TOP_K + n_tiles * N_EXPERTS * (SUBLANES - 1)
                + N_EXPERTS * (MOE_BLOCK - SUBLANES))
    n_blocks = -(-max_rows // MOE_BLOCK)
    n_slots = n_blocks * MOE_BLOCK
    block_start = jnp.arange(n_blocks, dtype=jnp.int32) * MOE_BLOCK
    block_e = jnp.minimum(jnp.sum((pad_end[None, :] <= block_start[:, None]).astype(jnp.int32), axis=1),
                          N_EXPERTS - 1)
    tables = dict(cnt=cnt.reshape(-1), loff=loff.reshape(-1), gb=gb.reshape(-1).astype(jnp.int32),
                  tot=jnp.sum(cnt, axis=1).astype(jnp.int32),
                  fill_start=(pad_start + counts).astype(jnp.int32),
                  fill_cnt=(padded - counts).astype(jnp.int32))
    n_used = (pad_end[-1:] // MOE_BLOCK).astype(jnp.int32)
    first = jnp.concatenate([jnp.ones((1,), jnp.int32), (block_e[1:] != block_e[:-1]).astype(jnp.int32)])
    seg_slot = (jnp.cumsum(first) - 1) % 2
    experts = jnp.arange(N_EXPERTS, dtype=jnp.int32)
    later = jnp.where((experts[None, :] > experts[:, None]) & (padded[None, :] > 0), experts[None, :], N_EXPERTS)
    next_e = jnp.min(later, axis=1)
    next_e = jnp.where(next_e < N_EXPERTS, next_e, -1)
    nxt = jnp.sum(jnp.where(block_e[:, None] == experts[None, :], next_e[None, :], 0), axis=1)
    etab = dict(block_e=block_e, n_used=n_used, first=first, slot=seg_slot.astype(jnp.int32),
                nxt=nxt.astype(jnp.int32))
    return tables, etab, n_slots


def kernel(x, mem, g_mix, w_in, conv_w, conv_b, ln_g, ln_b, ssd_conv_w, ssd_conv_b, dt_bias, a_log, d_skip, ssd_norm_g, w_out, g_xattn, g_mem, w_q, w_k, w_v, w_o, g_moe, w_router_group, b_router_group, w_router_expert, b_router_expert, w_gate, w_up, w_down, g_final):
    bsz, seqlen, d = x.shape
    n_tok = bsz * seqlen
    xt = x.reshape(n_tok, d)
    assert g_mix.shape[0] == 1, "the combine kernel applies the final norm: single layer only"
    for l in range(1):
        u, zx, dt_raw = _mix_in(xt, bsz, seqlen, g_mix[l], w_in[l], conv_w[l], conv_b[l],
                                    ln_g[l], ln_b[l])
        y = _ssd(zx, dt_raw, bsz, seqlen, ssd_conv_w[l], ssd_conv_b[l], dt_bias[l], a_log[l],
                 d_skip[l], ssd_norm_g[l])
        k, v = _kv_proj(mem, g_mem[l], w_k[l], w_v[l])
        x2, h2, route, stats = _xattn_route(xt, u, y, w_out[l], k, v, bsz, seqlen, g_xattn[l], w_q[l],
                                            w_o[l], g_moe[l], w_router_group[l], b_router_group[l],
                                            w_router_expert[l], b_router_expert[l])
        tables, etab, n_slots = _routing_tables(stats, n_tok)
        xs = _dispatch(h2, route, tables, etab["n_used"], n_slots)
        ys = _experts(xs, etab, w_gate[l], w_up[l], w_down[l])
        xt = _combine(x2, route, tables, ys, g_final)
    return xt.reshape(bsz, seqlen, d)
```

```python
import functools

import jax
import jax.numpy as jnp
from jax import lax
from jax.experimental import pallas as pl
from jax.experimental.pallas import tpu as pltpu

F32 = jnp.float32
BF16 = jnp.bfloat16

D_MODEL = 1024
CONV_CH = 1024
CONV_K = 31
SSD_INNER = 1024
SSD_HEAD_DIM = 64
SSD_HEADS = 16
SSD_STATE = 128
SSD_GROUPS = 2
SSD_GROUP_W = SSD_INNER // SSD_GROUPS
SSD_CONV_K = 4
SSD_CHUNK = 128
SSD_BC = SSD_GROUPS * SSD_STATE
SSD_CONV_CH = SSD_INNER + 2 * SSD_BC
N_MAIN = 2 * CONV_CH + 2 * SSD_INNER + 2 * SSD_BC
XA_HEADS = 4
XA_HEAD_DIM = 256
MEM_LEN = 256
N_GROUPS = 4
EXPERTS_PER_GROUP = 8
N_EXPERTS = 32
TOP_K = 2
D_EXPERT = 512
MOE_BLOCK = 512
RMS_EPS = 1e-6
LN_EPS = 1e-5
LOG2_E = 1.4426950408889634

LANES = 128
SUBLANES = 8
VMEM_LIMIT = 56 * 1024 * 1024

TOK_TILE = 512
CONV_TILE = 512
ZX_Z0 = -(-SSD_CONV_CH // SSD_INNER) * SSD_INNER
ZX_W = ZX_Z0 + SSD_INNER
ZX_SHARES = (0, 512, 512, 512, 512, 512, 512, 0)
assert sum(ZX_SHARES) == ZX_W and len(ZX_SHARES) == CONV_CH // LANES
CONV_HALO = 32
CONV_ROWS = 512
SSD_TILE = 1024
MOE_TILES = 2
XA_TILES = 2
SSD_UNROLL = 8
SSD_HALO = 8
ROUTE_LANES = 128
SORT_ROWS = TOP_K * TOK_TILE + N_EXPERTS * SUBLANES


def _cparams(sem):
    return pltpu.CompilerParams(dimension_semantics=sem, vmem_limit_bytes=VMEM_LIMIT)


def _rms(x, g):
    return x * lax.rsqrt(jnp.mean(x * x, axis=-1, keepdims=True) + RMS_EPS) * g


def _sigmoid(x):
    return 1.0 / (1.0 + jnp.exp2(x * (-LOG2_E)))


def _interleave(stages):
    results = {}
    while len(results) < len(stages):
        for k, item in enumerate(stages):
            if k not in results:
                try:
                    next(item)
                except StopIteration as done:
                    results[k] = done.value
    return [results[k] for k in range(len(stages))]


def _kv_body(m_ref, g_ref, wk_ref, wv_ref, k_ref, v_ref):
    m = _rms(m_ref[0], g_ref[...]).astype(BF16)
    k_ref[0] = jnp.dot(m, wk_ref[...], preferred_element_type=F32).astype(BF16)
    v_ref[0] = jnp.dot(m, wv_ref[...], preferred_element_type=F32).astype(BF16)


def _kv_proj(mem, g_mem, w_k, w_v):
    b, s, d = mem.shape
    w_spec = pl.BlockSpec((d, d), lambda i: (0, 0))
    kv_spec = pl.BlockSpec((1, s, d), lambda i: (i, 0, 0))
    return pl.pallas_call(
        _kv_body,
        grid=(b,),
        in_specs=[kv_spec, pl.BlockSpec((1, d), lambda i: (0, 0)), w_spec, w_spec],
        out_specs=[kv_spec, kv_spec],
        out_shape=[jax.ShapeDtypeStruct((b, s, d), BF16)] * 2,
        compiler_params=_cparams(("arbitrary",)),
        name="kv_proj",
    )(mem, g_mem.reshape(1, d), w_k.astype(BF16), w_v.astype(BF16))


def _mix_in_body(x_ref, g_ref, w_ref, wdt_ref, cw_ref, cb_ref, lg_ref, lb_ref,
                 u_ref, zx_ref, dt_ref, ubuf_ref, acc_ref):
    @pl.when(pl.program_id(1) == 0)
    def _():
        ubuf_ref[0:CONV_HALO, :] = jnp.zeros((CONV_HALO, CONV_CH), F32)

    h = _rms(x_ref[...], g_ref[...]).astype(BF16)
    dt_ref[...] = jnp.dot(h, wdt_ref[...], preferred_element_type=F32)

    first = CONV_HALO - (CONV_K - 1)
    w_lo = zx_lo = 0
    for cb, zx_w in enumerate(ZX_SHARES):
        cols = slice(cb * LANES, (cb + 1) * LANES)
        blk_w = 2 * LANES + zx_w
        r = jnp.dot(h, w_ref[:, w_lo:w_lo + blk_w], preferred_element_type=F32)
        ubuf_ref[CONV_HALO:CONV_HALO + CONV_TILE, cols] = r[:, :LANES] * _sigmoid(r[:, LANES:2 * LANES])
        if zx_w:
            zx_ref[:, zx_lo:zx_lo + zx_w] = r[:, 2 * LANES:].astype(BF16)
        w_lo += blk_w
        zx_lo += zx_w
        for rc in range(CONV_TILE // CONV_ROWS):
            r0 = rc * CONV_ROWS
            acc = None
            for res in range(SUBLANES):
                part = None
                for k in range(CONV_K):
                    if (first + k) % SUBLANES == res:
                        term = cw_ref[k:k + 1, cols] * ubuf_ref[r0 + first + k:r0 + first + k + CONV_ROWS, cols]
                        part = term if part is None else part + term
                acc = part if acc is None else acc + part
            acc_ref[r0:r0 + CONV_ROWS, cols] = acc

    for rc in range(CONV_TILE // CONV_ROWS):
        rows = slice(rc * CONV_ROWS, (rc + 1) * CONV_ROWS)
        u = acc_ref[rows, :] + cb_ref[...]
        mu = jnp.mean(u, axis=-1, keepdims=True)
        uc = u - mu
        var = jnp.mean(uc * uc, axis=-1, keepdims=True)
        y = uc * lax.rsqrt(var + LN_EPS) * lg_ref[...] + lb_ref[...]
        u_ref[rows, :] = (y * _sigmoid(y)).astype(BF16)

    ubuf_ref[0:CONV_HALO, :] = ubuf_ref[CONV_TILE:CONV_TILE + CONV_HALO, :]


def _mix_in(x2d, bsz, seqlen, g_mix, w_in, conv_w, conv_b, ln_g, ln_b):
    t, d = x2d.shape
    nt = seqlen // CONV_TILE
    w = w_in.astype(BF16)
    z0 = 2 * CONV_CH
    x0 = z0 + SSD_INNER
    w_zx = jnp.concatenate([w[:, x0:N_MAIN], jnp.zeros((d, ZX_Z0 - SSD_CONV_CH), BF16), w[:, z0:x0]], axis=1)
    parts, zx_lo = [], 0
    for cb, zx_w in enumerate(ZX_SHARES):
        parts += [w[:, cb * LANES:(cb + 1) * LANES], w[:, CONV_CH + cb * LANES:CONV_CH + (cb + 1) * LANES],
                  w_zx[:, zx_lo:zx_lo + zx_w]]
        zx_lo += zx_w
    w_blocks = jnp.concatenate(parts, axis=1)
    w_dt = jnp.pad(w_in[:, N_MAIN:], ((0, 0), (0, LANES - SSD_HEADS))).astype(BF16)
    row = lambda v: v.reshape(1, -1)
    const = lambda shape: pl.BlockSpec(shape, lambda b, j: (0, 0))
    tile = lambda width: pl.BlockSpec((CONV_TILE, width), lambda b, j: (b * nt + j, 0))
    return pl.pallas_call(
        _mix_in_body,
        grid=(bsz, nt),
        in_specs=[
            tile(d), const((1, d)), const(w_blocks.shape), const((d, LANES)),
            const((CONV_K + 1, CONV_CH)), const((1, CONV_CH)), const((1, CONV_CH)), const((1, CONV_CH)),
        ],
        out_specs=[tile(CONV_CH), tile(ZX_W), tile(LANES)],
        out_shape=[jax.ShapeDtypeStruct((t, CONV_CH), BF16), jax.ShapeDtypeStruct((t, ZX_W), BF16),
                   jax.ShapeDtypeStruct((t, LANES), F32)],
        scratch_shapes=[
            pltpu.VMEM((CONV_HALO + CONV_TILE, CONV_CH), F32),
            pltpu.VMEM((CONV_TILE, CONV_CH), F32),
        ],
        compiler_params=_cparams(("arbitrary", "arbitrary")),
        name="mix_in",
    )(x2d, row(g_mix), w_blocks, w_dt, jnp.pad(conv_w, ((0, 1), (0, 0))), row(conv_b), row(ln_g), row(ln_b))


def _ssd_body(xbc_ref, z_ref, dt_ref, cw_ref, cb_ref, dtb_ref, alog_ref, dsk_ref, ng_ref,
              expand_ref, y_ref, xbuf_ref, act_ref, dts_ref, state_ref):
    @pl.when(pl.program_id(1) == 0)
    def _():
        xbuf_ref[0:SSD_HALO, :] = jnp.zeros((SSD_HALO, SSD_CONV_CH), F32)
        state_ref[...] = jnp.zeros(state_ref.shape, F32)

    xbuf_ref[SSD_HALO:SSD_HALO + SSD_TILE, :] = xbc_ref[...].astype(F32)
    full = xbuf_ref[...]
    conv = cb_ref[...] + cw_ref[SSD_CONV_K - 1:SSD_CONV_K, :] * full[SSD_HALO:, :]
    for back in range(1, SSD_CONV_K):
        past = pltpu.roll(full, back, 0)[SSD_HALO:, :]
        conv = conv + cw_ref[SSD_CONV_K - 1 - back:SSD_CONV_K - back, :] * past
    act_ref[...] = conv * _sigmoid(conv)
    xbuf_ref[0:SSD_HALO, :] = xbuf_ref[SSD_TILE:SSD_TILE + SSD_HALO, :]

    dt_in = dt_ref[...] + dtb_ref[...]
    dts_ref[...] = jnp.maximum(dt_in, 0.0) + jnp.log1p(jnp.exp(-jnp.abs(dt_in)))

    a_neg = -jnp.exp(alog_ref[...]) * LOG2_E
    q = SSD_CHUNK
    row_i = lax.broadcasted_iota(jnp.int32, (q, q), 0)
    col_i = lax.broadcasted_iota(jnp.int32, (q, q), 1)
    causal = row_i >= col_i
    tril = causal.astype(F32)
    lane_i = lax.broadcasted_iota(jnp.int32, (q, LANES), 1)
    low_half = lane_i < SSD_HEAD_DIM
    expand = expand_ref[...]

    def chunk(c, carry):
        r0 = pl.multiple_of(c * q, q)
        rows = pl.ds(r0, q)
        dtc = dts_ref[rows, :]
        a_cs = jnp.dot(tril, dtc * a_neg, preferred_element_type=F32,
                       precision=lax.Precision.HIGHEST)
        a_cs_t = a_cs.T
        dt_t = dtc.T.astype(BF16)
        a_end = a_cs[q - 1:q, :]
        e_exp = jnp.dot(jnp.exp2(a_cs).astype(BF16), expand, preferred_element_type=F32)
        w_exp = jnp.dot((jnp.exp2(a_end - a_cs) * dtc).astype(BF16), expand,
                        preferred_element_type=F32)
        dec_row = e_exp[q - 1:q, :]
        xc = act_ref[rows, 0:SSD_INNER]
        xw = (xc * w_exp).astype(BF16)
        y_parts = []
        for g in range(SSD_GROUPS):
            b_f = act_ref[rows, SSD_INNER + g * SSD_STATE:SSD_INNER + (g + 1) * SSD_STATE]
            c_f = act_ref[rows, SSD_INNER + SSD_BC + g * SSD_STATE:
                          SSD_INNER + SSD_BC + (g + 1) * SSD_STATE]
            b_g = b_f.astype(BF16)
            c_g = c_f.astype(BF16)
            cb = lax.dot_general(c_g, b_g, (((1,), (1,)), ((), ())),
                                 preferred_element_type=F32).astype(BF16)
            gcols = slice(g * SSD_GROUP_W, (g + 1) * SSD_GROUP_W)
            st = state_ref[g]
            y_off = jnp.dot(c_g, st.astype(BF16), preferred_element_type=F32)
            state_ref[g] = st * dec_row[:, gcols] + jnp.dot(
                b_f.T.astype(BF16), xw[:, gcols], preferred_element_type=F32)
            for pair in range(SSD_GROUP_W // LANES):
                ms = []
                for hh in range(2):
                    h = g * (SSD_HEADS // SSD_GROUPS) + 2 * pair + hh
                    seg = a_cs[:, h:h + 1] - a_cs_t[h:h + 1, :]
                    dec = jnp.exp2(jnp.where(causal, seg, -jnp.inf))
                    ms.append(cb * dec.astype(BF16) * dt_t[h:h + 1, :])
                lhs = jnp.concatenate(ms, axis=1)
                xp = xc[:, g * SSD_GROUP_W + pair * LANES:g * SSD_GROUP_W + (pair + 1) * LANES]
                rhs = jnp.concatenate([jnp.where(low_half, xp, 0.0),
                                       jnp.where(low_half, 0.0, xp)], axis=0).astype(BF16)
                y_diag = jnp.dot(lhs, rhs, preferred_element_type=F32)
                lo = pair * LANES
                y_parts.append(y_diag + y_off[:, lo:lo + LANES]
                               * e_exp[:, g * SSD_GROUP_W + lo:g * SSD_GROUP_W + lo + LANES])
        y = jnp.concatenate(y_parts, axis=1) + xc * dsk_ref[...]
        z = z_ref[rows, :].astype(F32)
        y = y * (z * _sigmoid(z))
        outs = []
        for g in range(SSD_GROUPS):
            yg = y[:, g * SSD_GROUP_W:(g + 1) * SSD_GROUP_W]
            outs.append(yg * lax.rsqrt(jnp.mean(yg * yg, axis=-1, keepdims=True) + RMS_EPS))
        y_ref[rows, :] = (jnp.concatenate(outs, axis=1) * ng_ref[...]).astype(BF16)
        return carry

    lax.fori_loop(0, SSD_TILE // q, chunk, 0, unroll=SSD_UNROLL)


def _ssd(zx, dt_raw, bsz, seqlen, ssd_conv_w, ssd_conv_b, dt_bias, a_log, d_skip, ssd_norm_g):
    nt = seqlen // SSD_TILE
    pad_h = lambda v: jnp.pad(v, (0, LANES - SSD_HEADS)).reshape(1, LANES)
    expand = (jnp.arange(LANES)[:, None] == (jnp.arange(SSD_INNER) // SSD_HEAD_DIM)[None, :]).astype(BF16)
    const = lambda shape: pl.BlockSpec(shape, lambda b, j: (0, 0))
    return pl.pallas_call(
        _ssd_body,
        grid=(bsz, nt),
        in_specs=[
            pl.BlockSpec((SSD_TILE, SSD_CONV_CH), lambda b, j: (b * nt + j, 0)),
            pl.BlockSpec((SSD_TILE, SSD_INNER), lambda b, j: (b * nt + j, ZX_Z0 // SSD_INNER)),
            pl.BlockSpec((SSD_TILE, LANES), lambda b, j: (b * nt + j, 0)),
            const((SSD_CONV_K, SSD_CONV_CH)),
            const((1, SSD_CONV_CH)),
            const((1, LANES)), const((1, LANES)),
            const((1, SSD_INNER)), const((1, SSD_INNER)),
            const((LANES, SSD_INNER)),
        ],
        out_specs=pl.BlockSpec((SSD_TILE, SSD_INNER), lambda b, j: (b * nt + j, 0)),
        out_shape=jax.ShapeDtypeStruct((bsz * seqlen, SSD_INNER), BF16),
        scratch_shapes=[
            pltpu.VMEM((SSD_HALO + SSD_TILE, SSD_CONV_CH), F32),
            pltpu.VMEM((SSD_TILE, SSD_CONV_CH), F32),
            pltpu.VMEM((SSD_TILE, LANES), F32),
            pltpu.VMEM((SSD_GROUPS, SSD_STATE, SSD_GROUP_W), F32),
        ],
        compiler_params=_cparams(("arbitrary", "arbitrary")),
        name="ssd",
    )(zx, zx, dt_raw, ssd_conv_w, ssd_conv_b.reshape(1, SSD_CONV_CH), pad_h(dt_bias),
      pad_h(a_log), jnp.repeat(d_skip, SSD_HEAD_DIM).reshape(1, SSD_INNER),
      ssd_norm_g.reshape(1, SSD_INNER), expand)


def _xattn_body(x_ref, u_ref, y_ref, wu_ref, wy_ref, k_ref, v_ref, gx_ref, wq_ref, wo_ref, gm_ref,
                wr_ref, br_ref, x2_ref, h2_ref, route_ref, stats_ref):
    def tile(s):
        rows = slice(s * TOK_TILE, (s + 1) * TOK_TILE)
        return _xattn_tile(x_ref[rows, :], u_ref[rows, :], y_ref[rows, :], wu_ref, wy_ref, k_ref, v_ref,
                           gx_ref, wq_ref, wo_ref, gm_ref, wr_ref, br_ref)

    results = _interleave([tile(s) for s in range(XA_TILES)])
    for s in range(XA_TILES):
        rows = slice(s * TOK_TILE, (s + 1) * TOK_TILE)
        x2_ref[rows, :], h2_ref[rows, :], route_ref[rows, :], stats_ref[s] = results[s]


def _xattn_tile(x, u, y, wu_ref, wy_ref, k_ref, v_ref, gx_ref, wq_ref, wo_ref, gm_ref, wr_ref, br_ref):
    x = (x + jnp.dot(u, wu_ref[...], preferred_element_type=F32)
         + jnp.dot(y, wy_ref[...], preferred_element_type=F32))
    yield
    h = _rms(x, gx_ref[...]).astype(BF16)
    q = (jnp.dot(h, wq_ref[...], preferred_element_type=F32) * (XA_HEAD_DIM ** -0.5)).astype(BF16)
    yield
    heads = []
    for i in range(XA_HEADS):
        cols = slice(i * XA_HEAD_DIM, (i + 1) * XA_HEAD_DIM)
        s = jnp.dot(q[:, cols], k_ref[0, cols, :], preferred_element_type=F32)
        p = jnp.exp(s - jnp.max(s, axis=-1, keepdims=True))
        p = p / jnp.sum(p, axis=-1, keepdims=True)
        heads.append(jnp.dot(p.astype(BF16), v_ref[0, :, cols], preferred_element_type=F32))
        yield
    o = jnp.concatenate(heads, axis=1).astype(BF16)
    x2 = x + jnp.dot(o, wo_ref[...], preferred_element_type=F32)
    yield

    h2 = _rms(x2, gm_ref[...])
    h_hi = h2.astype(BF16)
    h_lo = (h2 - h_hi.astype(F32)).astype(BF16)
    both = jnp.dot(h_hi, wr_ref[...], preferred_element_type=F32)
    logits = (both[:, :ROUTE_LANES] + both[:, ROUTE_LANES:]
              + jnp.dot(h_lo, wr_ref[:, :ROUTE_LANES], preferred_element_type=F32)) + br_ref[...]
    yield
    lt = logits.T
    n_t = logits.shape[0]
    neg = -jnp.inf

    def first_max(rows):
        m = functools.reduce(jnp.maximum, rows)
        idx = jnp.full_like(m, len(rows) - 1)
        for j in reversed(range(len(rows) - 1)):
            idx = jnp.where(rows[j] == m, float(j), idx)
        return m, idx

    g_rows = [lt[g:g + 1, :] for g in range(N_GROUPS)]
    g_max, g_sel = first_max(g_rows)
    p_top = 1.0 / functools.reduce(lambda a, b: a + b, [jnp.exp(r - g_max) for r in g_rows])
    e_rows = []
    for j in range(EXPERTS_PER_GROUP):
        r = lt[N_GROUPS + j:N_GROUPS + j + 1, :]
        for g in range(1, N_GROUPS):
            lo = N_GROUPS + g * EXPERTS_PER_GROUP + j
            r = jnp.where(g_sel == float(g), lt[lo:lo + 1, :], r)
        e_rows.append(r)
    m1, i1 = first_max(e_rows)
    m2, i2 = first_max([jnp.where(i1 == float(j), neg, r) for j, r in enumerate(e_rows)])
    r = jnp.exp(m2 - m1)
    w1 = p_top / (1.0 + r)
    w2 = w1 * r
    e1 = g_sel * EXPERTS_PER_GROUP + i1
    e2 = g_sel * EXPERTS_PER_GROUP + i2
    yield
    expert = lax.broadcasted_iota(jnp.int32, (ROUTE_LANES, n_t), 0).astype(F32)
    oh1 = jnp.where(expert == e1, 1.0, 0.0)
    oh2 = jnp.where(expert == e2, 1.0, 0.0)
    earlier = (lax.broadcasted_iota(jnp.int32, (n_t, n_t), 0)
               < lax.broadcasted_iota(jnp.int32, (n_t, n_t), 1)).astype(BF16)
    c12 = jnp.dot(jnp.concatenate([oh1, oh2], axis=0).astype(BF16), earlier, preferred_element_type=F32)
    c1 = c12[:ROUTE_LANES, :]
    c2 = c12[ROUTE_LANES:, :]
    tot1 = c1[:, n_t - 1:n_t] + oh1[:, n_t - 1:n_t]
    cnt = tot1 + c2[:, n_t - 1:n_t] + oh2[:, n_t - 1:n_t]
    cnt = jnp.floor((cnt + (SUBLANES - 1)) * (1.0 / SUBLANES)) * SUBLANES
    cnt_b = jnp.broadcast_to(cnt, (ROUTE_LANES, ROUTE_LANES))
    experts_before = (lax.broadcasted_iota(jnp.int32, (ROUTE_LANES, ROUTE_LANES), 1)
                      < lax.broadcasted_iota(jnp.int32, (ROUTE_LANES, ROUTE_LANES), 0)).astype(F32)
    loff_b = jnp.dot(experts_before, cnt_b, preferred_element_type=F32, precision=lax.Precision.HIGHEST)
    loff = loff_b[:, 0:1]
    lp1 = jnp.sum(oh1 * (loff + c1), axis=0, keepdims=True)
    lp2 = jnp.sum(oh2 * (loff + tot1 + c2), axis=0, keepdims=True)
    route_t = jnp.where(expert == 0.0, e1,
              jnp.where(expert == 1.0, e2,
              jnp.where(expert == 2.0, w1,
              jnp.where(expert == 3.0, w2,
              jnp.where(expert == 4.0, lp1, jnp.where(expert == 5.0, lp2, 0.0))))))
    row = lax.broadcasted_iota(jnp.int32, (SUBLANES, ROUTE_LANES), 0)
    stats = jnp.where(row == 0, cnt_b.T[0:1, :], jnp.where(row == 1, loff_b.T[0:1, :], 0.0))
    return x2, h_hi, route_t.T, stats


def _xattn_route(x, u, y, w_out, k, v, bsz, seqlen, g_xattn, w_q, w_o, g_moe, w_rg, b_rg, w_re, b_re):
    t, d = x.shape
    w_mix = w_out.astype(BF16)
    w_r = jnp.pad(jnp.concatenate([w_rg, w_re], axis=1), ((0, 0), (0, ROUTE_LANES - N_GROUPS - N_EXPERTS)))
    b_r = jnp.pad(jnp.concatenate([b_rg, b_re]), (0, ROUTE_LANES - N_GROUPS - N_EXPERTS)).reshape(1, ROUTE_LANES)
    wr_hi = w_r.astype(BF16)
    wr = jnp.concatenate([wr_hi, (w_r - wr_hi.astype(F32)).astype(BF16)], axis=1)
    step_rows = XA_TILES * TOK_TILE
    nt = seqlen // step_rows
    tile = pl.BlockSpec((step_rows, d), lambda b, j: (b * nt + j, 0))
    kv_spec = pl.BlockSpec((1, MEM_LEN, d), lambda b, j: (b, 0, 0))
    const = lambda shape: pl.BlockSpec(shape, lambda b, j: (0, 0))
    return pl.pallas_call(
        _xattn_body,
        grid=(bsz, nt),
        in_specs=[tile, tile, tile, const((d, d)), const((d, d)),
                  pl.BlockSpec((1, d, MEM_LEN), lambda b, j: (b, 0, 0)), kv_spec, const((1, d)),
                  const((d, d)), const((d, d)), const((1, d)),
                  const((d, 2 * ROUTE_LANES)), const((1, ROUTE_LANES))],
        out_specs=[tile, tile, pl.BlockSpec((step_rows, ROUTE_LANES), lambda b, j: (b * nt + j, 0)),
                   pl.BlockSpec((XA_TILES, SUBLANES, ROUTE_LANES), lambda b, j: (b * nt + j, 0, 0))],
        out_shape=[jax.ShapeDtypeStruct((t, d), F32), jax.ShapeDtypeStruct((t, d), BF16),
                   jax.ShapeDtypeStruct((t, ROUTE_LANES), F32),
                   jax.ShapeDtypeStruct((t // TOK_TILE, SUBLANES, ROUTE_LANES), F32)],
        compiler_params=_cparams(("arbitrary", "arbitrary")),
        name="xattn_route",
    )(x, u, y, w_mix[:CONV_CH], w_mix[CONV_CH:], jnp.swapaxes(k, 1, 2), v, g_xattn.reshape(1, d), w_q.astype(BF16), w_o.astype(BF16), g_moe.reshape(1, d),
      wr, b_r)


def _rows_copy(src_ref, src0, dst_ref, dst0, n, sem):
    rows = lambda r0: pl.ds(pl.multiple_of(r0, SUBLANES), pl.multiple_of(n, SUBLANES))
    return pltpu.make_async_copy(src_ref.at[rows(src0), :], dst_ref.at[rows(dst0), :], sem)


def _pack_pairs(v):
    half = v.shape[1] // 2
    lo = lax.bitcast_convert_type(v[:, :half], jnp.uint32)
    hi = lax.bitcast_convert_type(v[:, half:], jnp.uint32)
    return (lo >> 16) | (hi & jnp.uint32(0xFFFF0000))


def _unpack_pairs(w):
    lo = lax.bitcast_convert_type(w << 16, F32)
    hi = lax.bitcast_convert_type(w & jnp.uint32(0xFFFF0000), F32)
    return jnp.concatenate([lo, hi], axis=1).astype(BF16)


def _dispatch_body(cnt_ref, loff_ref, gb_ref, tot_ref, fstart_ref, fcnt_ref, nu_ref, h_ref, route_ref,
                   xs_ref, buf_ref, zero_ref, sem, zsem):
    i = pl.program_id(0)
    nt = pl.num_programs(0)
    slot = i % 2
    buf = buf_ref.at[slot]

    def drain(step, s):
        n = sum(tot_ref[step * MOE_TILES + k] for k in range(MOE_TILES))
        _rows_copy(buf_ref.at[s], 0, xs_ref, 0, n, sem.at[s]).wait()

    @pl.when(i >= 2)
    def _():
        drain(i - 2, slot)

    def sort_tile(k):
        rows = slice(k * TOK_TILE, (k + 1) * TOK_TILE)
        rt = route_ref[rows, :].T
        lp1 = rt[4:5, :].astype(jnp.int32)
        lp2 = rt[5:6, :].astype(jnp.int32)
        r_i = lax.broadcasted_iota(jnp.int32, (SORT_ROWS, TOK_TILE), 0)
        perm = jnp.where((r_i == lp1) | (r_i == lp2), 1.0, 0.0).astype(BF16)
        yield
        buf[k * SORT_ROWS:(k + 1) * SORT_ROWS, :] = _pack_pairs(
            jnp.dot(perm, h_ref[rows, :], preferred_element_type=F32))

    _interleave([sort_tile(k) for k in range(MOE_TILES)])

    for k in range(MOE_TILES):
        def per_expert(e, c, k=k):
            j = (i * MOE_TILES + k) * N_EXPERTS + e

            @pl.when(cnt_ref[j] > 0)
            def _():
                _rows_copy(buf, k * SORT_ROWS + loff_ref[j], xs_ref, gb_ref[j], cnt_ref[j],
                           sem.at[slot]).start()
            return c

        lax.fori_loop(0, N_EXPERTS, per_expert, 0)

    @pl.when(i == 0)
    def _():
        zero_ref[...] = jnp.zeros(zero_ref.shape, zero_ref.dtype)

        def fill(start):
            def body(e, c):
                @pl.when(fcnt_ref[e] > 0)
                def _():
                    copy = _rows_copy(zero_ref, 0, xs_ref, fstart_ref[e], fcnt_ref[e], zsem)
                    copy.start() if start else copy.wait()
                return c
            lax.fori_loop(0, N_EXPERTS, body, 0)

            def tail(blk, c):
                copy = _rows_copy(zero_ref, 0, xs_ref, blk * MOE_BLOCK, MOE_BLOCK, zsem)
                copy.start() if start else copy.wait()
                return c
            lax.fori_loop(nu_ref[0], xs_ref.shape[0] // MOE_BLOCK, tail, 0)

        fill(True)
        fill(False)

    @pl.when(i == nt - 1)
    def _():
        drain(i, slot)

        @pl.when(i >= 1)
        def _():
            drain(i - 1, 1 - slot)


def _dispatch(h2, route, tables, n_used, n_slots):
    t, d = h2.shape
    step_rows = MOE_TILES * TOK_TILE
    return pl.pallas_call(
        _dispatch_body,
        grid_spec=pltpu.PrefetchScalarGridSpec(
            num_scalar_prefetch=7,
            grid=(t // step_rows,),
            in_specs=[
                pl.BlockSpec((step_rows, d), lambda i, *_: (i, 0)),
                pl.BlockSpec((step_rows, ROUTE_LANES), lambda i, *_: (i, 0)),
            ],
            out_specs=pl.BlockSpec(memory_space=pl.ANY),
            scratch_shapes=[pltpu.VMEM((2, MOE_TILES * SORT_ROWS, d // 2), jnp.uint32),
                            pltpu.VMEM((MOE_BLOCK, d // 2), jnp.uint32),
                            pltpu.SemaphoreType.DMA((2,)), pltpu.SemaphoreType.DMA(())],
        ),
        out_shape=jax.ShapeDtypeStruct((n_slots, d // 2), jnp.uint32),
        compiler_params=_cparams(("arbitrary",)),
        name="dispatch",
    )(tables["cnt"], tables["loff"], tables["gb"], tables["tot"], tables["fill_start"],
      tables["fill_cnt"], n_used, h2, route)


def _experts_body(be_ref, nu_ref, first_ref, slot_ref, nxt_ref, x_ref, wg_hbm, wu_hbm, wd_hbm, y_ref,
                  wg_f, wu_f, wd_f, wg_bf, wu_bf, wd_bf, sem):
    b = pl.program_id(0)
    used = b < nu_ref[0]

    def weights(e, s):
        return [pltpu.make_async_copy(hbm.at[e], buf.at[s], sem.at[s])
                for hbm, buf in ((wg_hbm, wg_f), (wu_hbm, wu_f), (wd_hbm, wd_f))]

    @pl.when(used & (first_ref[b] == 1))
    def _():
        s = slot_ref[b]

        @pl.when(b == 0)
        def _():
            for copy in weights(be_ref[0], 0):
                copy.start()

        for copy in weights(be_ref[b], s):
            copy.wait()
        wg_bf[...] = wg_f[s].astype(BF16)
        wu_bf[...] = wu_f[s].astype(BF16)
        wd_bf[...] = wd_f[s].astype(BF16)

        @pl.when(nxt_ref[b] >= 0)
        def _():
            for copy in weights(nxt_ref[b], 1 - s):
                copy.start()

    @pl.when(used)
    def _():
        x = _unpack_pairs(x_ref[...])
        g = jnp.dot(x, wg_bf[...], preferred_element_type=F32)
        u = jnp.dot(x, wu_bf[...], preferred_element_type=F32)
        a = (g * _sigmoid(g) * u).astype(BF16)
        y = jnp.dot(a, wd_bf[...], preferred_element_type=F32)
        y_ref[...] = _pack_pairs(y.astype(BF16).astype(F32))

    @pl.when(jnp.logical_not(used))
    def _():
        y_ref[...] = jnp.zeros(y_ref.shape, y_ref.dtype)


def _experts(xs, etab, w_gate, w_up, w_down):
    n_slots, dp = xs.shape
    d = w_gate.shape[1]
    n_blocks = n_slots // MOE_BLOCK
    last = lambda b, nu: jnp.maximum(jnp.minimum(b, nu[0] - 1), 0)
    hbm = pl.BlockSpec(memory_space=pl.ANY)
    return pl.pallas_call(
        _experts_body,
        grid_spec=pltpu.PrefetchScalarGridSpec(
            num_scalar_prefetch=5,
            grid=(n_blocks,),
            in_specs=[pl.BlockSpec((MOE_BLOCK, dp), lambda b, be, nu, *_: (last(b, nu), 0)), hbm, hbm, hbm],
            out_specs=pl.BlockSpec((MOE_BLOCK, dp), lambda b, *_: (b, 0)),
            scratch_shapes=[pltpu.VMEM((2, d, D_EXPERT), F32), pltpu.VMEM((2, d, D_EXPERT), F32),
                            pltpu.VMEM((2, D_EXPERT, d), F32),
                            pltpu.VMEM((d, D_EXPERT), BF16), pltpu.VMEM((d, D_EXPERT), BF16),
                            pltpu.VMEM((D_EXPERT, d), BF16), pltpu.SemaphoreType.DMA((2,))],
        ),
        out_shape=jax.ShapeDtypeStruct((n_slots, dp), jnp.uint32),
        compiler_params=_cparams(("arbitrary",)),
        name="experts",
    )(etab["block_e"], etab["n_used"], etab["first"], etab["slot"], etab["nxt"], xs, w_gate, w_up, w_down)


def _combine_body(cnt_ref, loff_ref, gb_ref, tot_ref, x_ref, route_ref, g_ref, ys_ref, o_ref, ybuf_ref, sem):
    i = pl.program_id(0)
    nt = pl.num_programs(0)
    slot = i % 2

    def gather(step, s):
        for k in range(MOE_TILES):
            def per_expert(e, c, k=k):
                j = (step * MOE_TILES + k) * N_EXPERTS + e

                @pl.when(cnt_ref[j] > 0)
                def _():
                    _rows_copy(ys_ref, gb_ref[j], ybuf_ref.at[s], k * SORT_ROWS + loff_ref[j], cnt_ref[j],
                               sem.at[s]).start()
                return c

            lax.fori_loop(0, N_EXPERTS, per_expert, 0)

    @pl.when(i == 0)
    def _():
        ybuf_ref[...] = jnp.zeros(ybuf_ref.shape, ybuf_ref.dtype)
        gather(0, 0)

    @pl.when(i + 1 < nt)
    def _():
        gather(i + 1, 1 - slot)

    def combine_tile(k):
        rows = slice(k * TOK_TILE, (k + 1) * TOK_TILE)
        route = route_ref[rows, :]
        c_i = lax.broadcasted_iota(jnp.int32, (TOK_TILE, SORT_ROWS), 1)
        lp1 = route[:, 4:5].astype(jnp.int32)
        lp2 = route[:, 5:6].astype(jnp.int32)
        pw = jnp.where(c_i == lp1, route[:, 2:3], jnp.where(c_i == lp2, route[:, 3:4], 0.0)).astype(BF16)
        yield
        r_i = lax.broadcasted_iota(jnp.int32, (SORT_ROWS, 1), 0)
        y = _unpack_pairs(jnp.where(r_i < tot_ref[i * MOE_TILES + k],
                                    ybuf_ref[slot, k * SORT_ROWS:(k + 1) * SORT_ROWS, :], jnp.uint32(0)))
        moe = jnp.dot(pw, y, preferred_element_type=F32)
        yield
        o_ref[rows, :] = _rms(x_ref[rows, :] + moe, g_ref[...])

    tiles = [combine_tile(k) for k in range(MOE_TILES)]
    for tile in tiles:
        next(tile)
    n = sum(tot_ref[i * MOE_TILES + k] for k in range(MOE_TILES))
    _rows_copy(ys_ref, 0, ybuf_ref.at[slot], 0, n, sem.at[slot]).wait()
    _interleave(tiles)


def _combine(x2, route, tables, ys, g_final):
    t, d = x2.shape
    step_rows = MOE_TILES * TOK_TILE
    tile = pl.BlockSpec((step_rows, d), lambda i, *_: (i, 0))
    return pl.pallas_call(
        _combine_body,
        grid_spec=pltpu.PrefetchScalarGridSpec(
            num_scalar_prefetch=4,
            grid=(t // step_rows,),
            in_specs=[
                tile,
                pl.BlockSpec((step_rows, ROUTE_LANES), lambda i, *_: (i, 0)),
                pl.BlockSpec((1, d), lambda i, *_: (0, 0)),
                pl.BlockSpec(memory_space=pl.ANY),
            ],
            out_specs=tile,
            scratch_shapes=[pltpu.VMEM((2, MOE_TILES * SORT_ROWS, d // 2), jnp.uint32),
                            pltpu.SemaphoreType.DMA((2,))],
        ),
        out_shape=jax.ShapeDtypeStruct((t, d), F32),
        compiler_params=_cparams(("arbitrary",)),
        name="combine",
    )(tables["cnt"], tables["loff"], tables["gb"], tables["tot"], x2, route, g_final.reshape(1, d), ys)


def _routing_tables(stats, n_tok):
    cnt = stats[:, 0, :N_EXPERTS].astype(jnp.int32)
    loff = stats[:, 1, :N_EXPERTS].astype(jnp.int32)
    n_tiles = cnt.shape[0]
    counts = jnp.sum(cnt, axis=0)
    padded = ((counts + MOE_BLOCK - 1) // MOE_BLOCK) * MOE_BLOCK
    pad_end = jnp.cumsum(padded)
    pad_start = pad_end - padded
    gb = pad_start[None, :] + jnp.cumsum(cnt, axis=0) - cnt
    max_rows = (n_tok * TOP_K + n_tiles * N_EXPERTS * (SUBLANES - 1)
                + N_EXPERTS * (MOE_BLOCK - SUBLANES))
    n_blocks = -(-max_rows // MOE_BLOCK)
    n_slots = n_blocks * MOE_BLOCK
    block_start = jnp.arange(n_blocks, dtype=jnp.int32) * MOE_BLOCK
    block_e = jnp.minimum(jnp.sum((pad_end[None, :] <= block_start[:, None]).astype(jnp.int32), axis=1),
                          N_EXPERTS - 1)
    tables = dict(cnt=cnt.reshape(-1), loff=loff.reshape(-1), gb=gb.reshape(-1).astype(jnp.int32),
                  tot=jnp.sum(cnt, axis=1).astype(jnp.int32),
                  fill_start=(pad_start + counts).astype(jnp.int32),
                  fill_cnt=(padded - counts).astype(jnp.int32))
    n_used = (pad_end[-1:] // MOE_BLOCK).astype(jnp.int32)
    first = jnp.concatenate([jnp.ones((1,), jnp.int32), (block_e[1:] != block_e[:-1]).astype(jnp.int32)])
    seg_slot = (jnp.cumsum(first) - 1) % 2
    experts = jnp.arange(N_EXPERTS, dtype=jnp.int32)
    later = jnp.where((experts[None, :] > experts[:, None]) & (padded[None, :] > 0), experts[None, :], N_EXPERTS)
    next_e = jnp.min(later, axis=1)
    next_e = jnp.where(next_e < N_EXPERTS, next_e, -1)
    nxt = jnp.sum(jnp.where(block_e[:, None] == experts[None, :], next_e[None, :], 0), axis=1)
    etab = dict(block_e=block_e, n_used=n_used, first=first, slot=seg_slot.astype(jnp.int32),
                nxt=nxt.astype(jnp.int32))
    return tables, etab, n_slots


def kernel(x, mem, g_mix, w_in, conv_w, conv_b, ln_g, ln_b, ssd_conv_w, ssd_conv_b, dt_bias, a_log, d_skip, ssd_norm_g, w_out, g_xattn, g_mem, w_q, w_k, w_v, w_o, g_moe, w_router_group, b_router_group, w_router_expert, b_router_expert, w_gate, w_up, w_down, g_final):
    bsz, seqlen, d = x.shape
    n_tok = bsz * seqlen
    xt = x.reshape(n_tok, d)
    assert g_mix.shape[0] == 1, "the combine kernel applies the final norm: single layer only"
    for l in range(1):
        u, zx, dt_raw = _mix_in(xt, bsz, seqlen, g_mix[l], w_in[l], conv_w[l], conv_b[l],
                                    ln_g[l], ln_b[l])
        y = _ssd(zx, dt_raw, bsz, seqlen, ssd_conv_w[l], ssd_conv_b[l], dt_bias[l], a_log[l],
                 d_skip[l], ssd_norm_g[l])
        k, v = _kv_proj(mem, g_mem[l], w_k[l], w_v[l])
        x2, h2, route, stats = _xattn_route(xt, u, y, w_out[l], k, v, bsz, seqlen, g_xattn[l], w_q[l],
                                            w_o[l], g_moe[l], w_router_group[l], b_router_group[l],
                                            w_router_expert[l], b_router_expert[l])
        tables, etab, n_slots = _routing_tables(stats, n_tok)
        xs = _dispatch(h2, route, tables, etab["n_used"], n_slots)
        ys = _experts(xs, etab, w_gate[l], w_up[l], w_down[l])
        xt = _combine(x2, route, tables, ys, g_final)
    return xt.reshape(bsz, seqlen, d)
```

```python
import functools

import jax
import jax.numpy as jnp
from jax import lax
from jax.experimental import pallas as pl
from jax.experimental.pallas import tpu as pltpu

F32 = jnp.float32
BF16 = jnp.bfloat16

D_MODEL = 1024
CONV_CH = 1024
CONV_K = 31
SSD_INNER = 1024
SSD_HEAD_DIM = 64
SSD_HEADS = 16
SSD_STATE = 128
SSD_GROUPS = 2
SSD_GROUP_W = SSD_INNER // SSD_GROUPS
SSD_CONV_K = 4
SSD_CHUNK = 128
SSD_BC = SSD_GROUPS * SSD_STATE
SSD_CONV_CH = SSD_INNER + 2 * SSD_BC
N_MAIN = 2 * CONV_CH + 2 * SSD_INNER + 2 * SSD_BC
XA_HEADS = 4
XA_HEAD_DIM = 256
MEM_LEN = 256
N_GROUPS = 4
EXPERTS_PER_GROUP = 8
N_EXPERTS = 32
TOP_K = 2
D_EXPERT = 512
MOE_BLOCK = 512
RMS_EPS = 1e-6
LN_EPS = 1e-5
LOG2_E = 1.4426950408889634

LANES = 128
SUBLANES = 8
VMEM_LIMIT = 56 * 1024 * 1024

TOK_TILE = 512
CONV_TILE = 512
ZX_Z0 = -(-SSD_CONV_CH // SSD_INNER) * SSD_INNER
ZX_W = ZX_Z0 + SSD_INNER
ZX_SHARES = (0, 512, 512, 512, 512, 512, 512, 0)
assert sum(ZX_SHARES) == ZX_W and len(ZX_SHARES) == CONV_CH // LANES
CONV_HALO = 32
CONV_ROWS = 512
SSD_TILE = 1024
MOE_TILES = 2
XA_TILES = 2
SSD_UNROLL = 8
SSD_HALO = 8
ROUTE_LANES = 128
SORT_ROWS = TOP_K * TOK_TILE + N_EXPERTS * SUBLANES


def _cparams(sem):
    return pltpu.CompilerParams(dimension_semantics=sem, vmem_limit_bytes=VMEM_LIMIT)


def _rms(x, g):
    return x * lax.rsqrt(jnp.mean(x * x, axis=-1, keepdims=True) + RMS_EPS) * g


def _sigmoid(x):
    return 1.0 / (1.0 + jnp.exp2(x * (-LOG2_E)))


def _interleave(stages):
    results = {}
    while len(results) < len(stages):
        for k, item in enumerate(stages):
            if k not in results:
                try:
                    next(item)
                except StopIteration as done:
                    results[k] = done.value
    return [results[k] for k in range(len(stages))]


def _kv_body(m_ref, g_ref, wk_ref, wv_ref, k_ref, v_ref):
    m = _rms(m_ref[0], g_ref[...]).astype(BF16)
    k_ref[0] = jnp.dot(m, wk_ref[...], preferred_element_type=F32).astype(BF16)
    v_ref[0] = jnp.dot(m, wv_ref[...], preferred_element_type=F32).astype(BF16)


def _kv_proj(mem, g_mem, w_k, w_v):
    b, s, d = mem.shape
    w_spec = pl.BlockSpec((d, d), lambda i: (0, 0))
    kv_spec = pl.BlockSpec((1, s, d), lambda i: (i, 0, 0))
    return pl.pallas_call(
        _kv_body,
        grid=(b,),
        in_specs=[kv_spec, pl.BlockSpec((1, d), lambda i: (0, 0)), w_spec, w_spec],
        out_specs=[kv_spec, kv_spec],
        out_shape=[jax.ShapeDtypeStruct((b, s, d), BF16)] * 2,
        compiler_params=_cparams(("arbitrary",)),
        name="kv_proj",
    )(mem, g_mem.reshape(1, d), w_k.astype(BF16), w_v.astype(BF16))


def _mix_in_body(x_ref, g_ref, w_ref, wdt_ref, cw_ref, cb_ref, lg_ref, lb_ref,
                 u_ref, zx_ref, dt_ref, ubuf_ref, acc_ref):
    @pl.when(pl.program_id(1) == 0)
    def _():
        ubuf_ref[0:CONV_HALO, :] = jnp.zeros((CONV_HALO, CONV_CH), F32)

    h = _rms(x_ref[...], g_ref[...]).astype(BF16)
    dt_ref[...] = jnp.dot(h, wdt_ref[...], preferred_element_type=F32)

    first = CONV_HALO - (CONV_K - 1)
    w_lo = zx_lo = 0
    for cb, zx_w in enumerate(ZX_SHARES):
        cols = slice(cb * LANES, (cb + 1) * LANES)
        blk_w = 2 * LANES + zx_w
        r = jnp.dot(h, w_ref[:, w_lo:w_lo + blk_w], preferred_element_type=F32)
        ubuf_ref[CONV_HALO:CONV_HALO + CONV_TILE, cols] = r[:, :LANES] * _sigmoid(r[:, LANES:2 * LANES])
        if zx_w:
            zx_ref[:, zx_lo:zx_lo + zx_w] = r[:, 2 * LANES:].astype(BF16)
        w_lo += blk_w
        zx_lo += zx_w
        for rc in range(CONV_TILE // CONV_ROWS):
            r0 = rc * CONV_ROWS
            acc = None
            for res in range(SUBLANES):
                part = None
                for k in range(CONV_K):
                    if (first + k) % SUBLANES == res:
                        term = cw_ref[k:k + 1, cols] * ubuf_ref[r0 + first + k:r0 + first + k + CONV_ROWS, cols]
                        part = term if part is None else part + term
                acc = part if acc is None else acc + part
            acc_ref[r0:r0 + CONV_ROWS, cols] = acc

    for rc in range(CONV_TILE // CONV_ROWS):
        rows = slice(rc * CONV_ROWS, (rc + 1) * CONV_ROWS)
        u = acc_ref[rows, :] + cb_ref[...]
        mu = jnp.mean(u, axis=-1, keepdims=True)
        uc = u - mu
        var = jnp.mean(uc * uc, axis=-1, keepdims=True)
        y = uc * lax.rsqrt(var + LN_EPS) * lg_ref[...] + lb_ref[...]
        u_ref[rows, :] = (y * _sigmoid(y)).astype(BF16)

    ubuf_ref[0:CONV_HALO, :] = ubuf_ref[CONV_TILE:CONV_TILE + CONV_HALO, :]


def _mix_in(x2d, bsz, seqlen, g_mix, w_in, conv_w, conv_b, ln_g, ln_b):
    t, d = x2d.shape
    nt = seqlen // CONV_TILE
    w = w_in.astype(BF16)
    z0 = 2 * CONV_CH
    x0 = z0 + SSD_INNER
    w_zx = jnp.concatenate([w[:, x0:N_MAIN], jnp.zeros((d, ZX_Z0 - SSD_CONV_CH), BF16), w[:, z0:x0]], axis=1)
    parts, zx_lo = [], 0
    for cb, zx_w in enumerate(ZX_SHARES):
        parts += [w[:, cb * LANES:(cb + 1) * LANES], w[:, CONV_CH + cb * LANES:CONV_CH + (cb + 1) * LANES],
                  w_zx[:, zx_lo:zx_lo + zx_w]]
        zx_lo += zx_w
    w_blocks = jnp.concatenate(parts, axis=1)
    w_dt = jnp.pad(w_in[:, N_MAIN:], ((0, 0), (0, LANES - SSD_HEADS))).astype(BF16)
    row = lambda v: v.reshape(1, -1)
    const = lambda shape: pl.BlockSpec(shape, lambda b, j: (0, 0))
    tile = lambda width: pl.BlockSpec((CONV_TILE, width), lambda b, j: (b * nt + j, 0))
    return pl.pallas_call(
        _mix_in_body,
        grid=(bsz, nt),
        in_specs=[
            tile(d), const((1, d)), const(w_blocks.shape), const((d, LANES)),
            const((CONV_K + 1, CONV_CH)), const((1, CONV_CH)), const((1, CONV_CH)), const((1, CONV_CH)),
        ],
        out_specs=[tile(CONV_CH), tile(ZX_W), tile(LANES)],
        out_shape=[jax.ShapeDtypeStruct((t, CONV_CH), BF16), jax.ShapeDtypeStruct((t, ZX_W), BF16),
                   jax.ShapeDtypeStruct((t, LANES), F32)],
        scratch_shapes=[
            pltpu.VMEM((CONV_HALO + CONV_TILE, CONV_CH), F32),
            pltpu.VMEM((CONV_TILE, CONV_CH), F32),
        ],
        compiler_params=_cparams(("arbitrary", "arbitrary")),
        name="mix_in",
    )(x2d, row(g_mix), w_blocks, w_dt, jnp.pad(conv_w, ((0, 1), (0, 0))), row(conv_b), row(ln_g), row(ln_b))


def _ssd_body(xbc_ref, z_ref, dt_ref, cw_ref, cb_ref, dtb_ref, alog_ref, dsk_ref, ng_ref,
              expand_ref, y_ref, xbuf_ref, act_ref, dts_ref, state_ref):
    @pl.when(pl.program_id(1) == 0)
    def _():
        xbuf_ref[0:SSD_HALO, :] = jnp.zeros((SSD_HALO, SSD_CONV_CH), F32)
        state_ref[...] = jnp.zeros(state_ref.shape, F32)

    xbuf_ref[SSD_HALO:SSD_HALO + SSD_TILE, :] = xbc_ref[...].astype(F32)
    full = xbuf_ref[...]
    conv = cb_ref[...] + cw_ref[SSD_CONV_K - 1:SSD_CONV_K, :] * full[SSD_HALO:, :]
    for back in range(1, SSD_CONV_K):
        past = pltpu.roll(full, back, 0)[SSD_HALO:, :]
        conv = conv + cw_ref[SSD_CONV_K - 1 - back:SSD_CONV_K - back, :] * past
    act_ref[...] = conv * _sigmoid(conv)
    xbuf_ref[0:SSD_HALO, :] = xbuf_ref[SSD_TILE:SSD_TILE + SSD_HALO, :]

    dt_in = dt_ref[...] + dtb_ref[...]
    dts_ref[...] = jnp.maximum(dt_in, 0.0) + jnp.log1p(jnp.exp(-jnp.abs(dt_in)))

    a_neg = -jnp.exp(alog_ref[...]) * LOG2_E
    q = SSD_CHUNK
    row_i = lax.broadcasted_iota(jnp.int32, (q, q), 0)
    col_i = lax.broadcasted_iota(jnp.int32, (q, q), 1)
    causal = row_i >= col_i
    tril = causal.astype(F32)
    lane_i = lax.broadcasted_iota(jnp.int32, (q, LANES), 1)
    low_half = lane_i < SSD_HEAD_DIM
    expand = expand_ref[...]

    def chunk(c, carry):
        r0 = pl.multiple_of(c * q, q)
        rows = pl.ds(r0, q)
        dtc = dts_ref[rows, :]
        a_cs = jnp.dot(tril, dtc * a_neg, preferred_element_type=F32,
                       precision=lax.Precision.HIGHEST)
        a_cs_t = a_cs.T
        dt_t = dtc.T.astype(BF16)
        a_end = a_cs[q - 1:q, :]
        e_exp = jnp.dot(jnp.exp2(a_cs).astype(BF16), expand, preferred_element_type=F32)
        w_exp = jnp.dot((jnp.exp2(a_end - a_cs) * dtc).astype(BF16), expand,
                        preferred_element_type=F32)
        dec_row = e_exp[q - 1:q, :]
        xc = act_ref[rows, 0:SSD_INNER]
        xw = (xc * w_exp).astype(BF16)
        y_parts = []
        for g in range(SSD_GROUPS):
            b_f = act_ref[rows, SSD_INNER + g * SSD_STATE:SSD_INNER + (g + 1) * SSD_STATE]
            c_f = act_ref[rows, SSD_INNER + SSD_BC + g * SSD_STATE:
                          SSD_INNER + SSD_BC + (g + 1) * SSD_STATE]
            b_g = b_f.astype(BF16)
            c_g = c_f.astype(BF16)
            cb = lax.dot_general(c_g, b_g, (((1,), (1,)), ((), ())),
                                 preferred_element_type=F32).astype(BF16)
            gcols = slice(g * SSD_GROUP_W, (g + 1) * SSD_GROUP_W)
            st = state_ref[g]
            y_off = jnp.dot(c_g, st.astype(BF16), preferred_element_type=F32)
            state_ref[g] = st * dec_row[:, gcols] + jnp.dot(
                b_f.T.astype(BF16), xw[:, gcols], preferred_element_type=F32)
            for pair in range(SSD_GROUP_W // LANES):
                ms = []
                for hh in range(2):
                    h = g * (SSD_HEADS // SSD_GROUPS) + 2 * pair + hh
                    seg = a_cs[:, h:h + 1] - a_cs_t[h:h + 1, :]
                    dec = jnp.exp2(jnp.where(causal, seg, -jnp.inf))
                    ms.append(cb * dec.astype(BF16) * dt_t[h:h + 1, :])
                lhs = jnp.concatenate(ms, axis=1)
                xp = xc[:, g * SSD_GROUP_W + pair * LANES:g * SSD_GROUP_W + (pair + 1) * LANES]
                rhs = jnp.concatenate([jnp.where(low_half, xp, 0.0),
                                       jnp.where(low_half, 0.0, xp)], axis=0).astype(BF16)
                y_diag = jnp.dot(lhs, rhs, preferred_element_type=F32)
                lo = pair * LANES
                y_parts.append(y_diag + y_off[:, lo:lo + LANES]
                               * e_exp[:, g * SSD_GROUP_W + lo:g * SSD_GROUP_W + lo + LANES])
        y = jnp.concatenate(y_parts, axis=1) + xc * dsk_ref[...]
        z = z_ref[rows, :].astype(F32)
        y = y * (z * _sigmoid(z))
        outs = []
        for g in range(SSD_GROUPS):
            yg = y[:, g * SSD_GROUP_W:(g + 1) * SSD_GROUP_W]
            outs.append(yg * lax.rsqrt(jnp.mean(yg * yg, axis=-1, keepdims=True) + RMS_EPS))
        y_ref[rows, :] = (jnp.concatenate(outs, axis=1) * ng_ref[...]).astype(BF16)
        return carry

    lax.fori_loop(0, SSD_TILE // q, chunk, 0, unroll=SSD_UNROLL)


def _ssd(zx, dt_raw, bsz, seqlen, ssd_conv_w, ssd_conv_b, dt_bias, a_log, d_skip, ssd_norm_g):
    nt = seqlen // SSD_TILE
    pad_h = lambda v: jnp.pad(v, (0, LANES - SSD_HEADS)).reshape(1, LANES)
    expand = (jnp.arange(LANES)[:, None] == (jnp.arange(SSD_INNER) // SSD_HEAD_DIM)[None, :]).astype(BF16)
    const = lambda shape: pl.BlockSpec(shape, lambda b, j: (0, 0))
    return pl.pallas_call(
        _ssd_body,
        grid=(bsz, nt),
        in_specs=[
            pl.BlockSpec((SSD_TILE, SSD_CONV_CH), lambda b, j: (b * nt + j, 0)),
            pl.BlockSpec((SSD_TILE, SSD_INNER), lambda b, j: (b * nt + j, ZX_Z0 // SSD_INNER)),
            pl.BlockSpec((SSD_TILE, LANES), lambda b, j: (b * nt + j, 0)),
            const((SSD_CONV_K, SSD_CONV_CH)),
            const((1, SSD_CONV_CH)),
            const((1, LANES)), const((1, LANES)),
            const((1, SSD_INNER)), const((1, SSD_INNER)),
            const((LANES, SSD_INNER)),
        ],
        out_specs=pl.BlockSpec((SSD_TILE, SSD_INNER), lambda b, j: (b * nt + j, 0)),
        out_shape=jax.ShapeDtypeStruct((bsz * seqlen, SSD_INNER), BF16),
        scratch_shapes=[
            pltpu.VMEM((SSD_HALO + SSD_TILE, SSD_CONV_CH), F32),
            pltpu.VMEM((SSD_TILE, SSD_CONV_CH), F32),
            pltpu.VMEM((SSD_TILE, LANES), F32),
            pltpu.VMEM((SSD_GROUPS, SSD_STATE, SSD_GROUP_W), F32),
        ],
        compiler_params=_cparams(("arbitrary", "arbitrary")),
        name="ssd",
    )(zx, zx, dt_raw, ssd_conv_w, ssd_conv_b.reshape(1, SSD_CONV_CH), pad_h(dt_bias),
      pad_h(a_log), jnp.repeat(d_skip, SSD_HEAD_DIM).reshape(1, SSD_INNER),
      ssd_norm_g.reshape(1, SSD_INNER), expand)


def _xattn_body(x_ref, u_ref, y_ref, wu_ref, wy_ref, k_ref, v_ref, gx_ref, wq_ref, wo_ref, gm_ref,
                wr_ref, br_ref, x2_ref, h2_ref, route_ref, stats_ref):
    def tile(s):
        rows = slice(s * TOK_TILE, (s + 1) * TOK_TILE)
        return _xattn_tile(x_ref[rows, :], u_ref[rows, :], y_ref[rows, :], wu_ref, wy_ref, k_ref, v_ref,
                           gx_ref, wq_ref, wo_ref, gm_ref, wr_ref, br_ref)

    results = _interleave([tile(s) for s in range(XA_TILES)])
    for s in range(XA_TILES):
        rows = slice(s * TOK_TILE, (s + 1) * TOK_TILE)
        x2_ref[rows, :], h2_ref[rows, :], route_ref[rows, :], stats_ref[s] = results[s]


def _xattn_tile(x, u, y, wu_ref, wy_ref, k_ref, v_ref, gx_ref, wq_ref, wo_ref, gm_ref, wr_ref, br_ref):
    x = (x + jnp.dot(u, wu_ref[...], preferred_element_type=F32)
         + jnp.dot(y, wy_ref[...], preferred_element_type=F32))
    yield
    h = _rms(x, gx_ref[...]).astype(BF16)
    q = (jnp.dot(h, wq_ref[...], preferred_element_type=F32) * (XA_HEAD_DIM ** -0.5)).astype(BF16)
    yield
    heads = []
    for i in range(XA_HEADS):
        cols = slice(i * XA_HEAD_DIM, (i + 1) * XA_HEAD_DIM)
        s = jnp.dot(q[:, cols], k_ref[0, cols, :], preferred_element_type=F32)
        p = jnp.exp(s - jnp.max(s, axis=-1, keepdims=True))
        p = p / jnp.sum(p, axis=-1, keepdims=True)
        yield
        heads.append(jnp.dot(p.astype(BF16), v_ref[0, :, cols], preferred_element_type=F32))
        yield
    o = jnp.concatenate(heads, axis=1).astype(BF16)
    x2 = x + jnp.dot(o, wo_ref[...], preferred_element_type=F32)
    yield

    h2 = _rms(x2, gm_ref[...])
    h_hi = h2.astype(BF16)
    h_lo = (h2 - h_hi.astype(F32)).astype(BF16)
    both = jnp.dot(h_hi, wr_ref[...], preferred_element_type=F32)
    logits = (both[:, :ROUTE_LANES] + both[:, ROUTE_LANES:]
              + jnp.dot(h_lo, wr_ref[:, :ROUTE_LANES], preferred_element_type=F32)) + br_ref[...]
    yield
    lt = logits.T
    n_t = logits.shape[0]
    neg = -jnp.inf

    def first_max(rows):
        m = functools.reduce(jnp.maximum, rows)
        idx = jnp.full_like(m, len(rows) - 1)
        for j in reversed(range(len(rows) - 1)):
            idx = jnp.where(rows[j] == m, float(j), idx)
        return m, idx

    g_rows = [lt[g:g + 1, :] for g in range(N_GROUPS)]
    g_max, g_sel = first_max(g_rows)
    p_top = 1.0 / functools.reduce(lambda a, b: a + b, [jnp.exp(r - g_max) for r in g_rows])
    e_rows = []
    for j in range(EXPERTS_PER_GROUP):
        r = lt[N_GROUPS + j:N_GROUPS + j + 1, :]
        for g in range(1, N_GROUPS):
            lo = N_GROUPS + g * EXPERTS_PER_GROUP + j
            r = jnp.where(g_sel == float(g), lt[lo:lo + 1, :], r)
        e_rows.append(r)
    yield
    m1, i1 = first_max(e_rows)
    m2, i2 = first_max([jnp.where(i1 == float(j), neg, r) for j, r in enumerate(e_rows)])
    r = jnp.exp(m2 - m1)
    w1 = p_top / (1.0 + r)
    w2 = w1 * r
    e1 = g_sel * EXPERTS_PER_GROUP + i1
    e2 = g_sel * EXPERTS_PER_GROUP + i2
    yield
    expert = lax.broadcasted_iota(jnp.int32, (ROUTE_LANES, n_t), 0).astype(F32)
    oh1 = jnp.where(expert == e1, 1.0, 0.0)
    oh2 = jnp.where(expert == e2, 1.0, 0.0)
    earlier = (lax.broadcasted_iota(jnp.int32, (n_t, n_t), 0)
               < lax.broadcasted_iota(jnp.int32, (n_t, n_t), 1)).astype(BF16)
    c12 = jnp.dot(jnp.concatenate([oh1, oh2], axis=0).astype(BF16), earlier, preferred_element_type=F32)
    yield
    c1 = c12[:ROUTE_LANES, :]
    c2 = c12[ROUTE_LANES:, :]
    tot1 = c1[:, n_t - 1:n_t] + oh1[:, n_t - 1:n_t]
    cnt = tot1 + c2[:, n_t - 1:n_t] + oh2[:, n_t - 1:n_t]
    cnt = jnp.floor((cnt + (SUBLANES - 1)) * (1.0 / SUBLANES)) * SUBLANES
    cnt_b = jnp.broadcast_to(cnt, (ROUTE_LANES, ROUTE_LANES))
    experts_before = (lax.broadcasted_iota(jnp.int32, (ROUTE_LANES, ROUTE_LANES), 1)
                      < lax.broadcasted_iota(jnp.int32, (ROUTE_LANES, ROUTE_LANES), 0)).astype(F32)
    loff_b = jnp.dot(experts_before, cnt_b, preferred_element_type=F32, precision=lax.Precision.HIGHEST)
    loff = loff_b[:, 0:1]
    lp1 = jnp.sum(oh1 * (loff + c1), axis=0, keepdims=True)
    lp2 = jnp.sum(oh2 * (loff + tot1 + c2), axis=0, keepdims=True)
    route_t = jnp.where(expert == 0.0, e1,
              jnp.where(expert == 1.0, e2,
              jnp.where(expert == 2.0, w1,
              jnp.where(expert == 3.0, w2,
              jnp.where(expert == 4.0, lp1, jnp.where(expert == 5.0, lp2, 0.0))))))
    row = lax.broadcasted_iota(jnp.int32, (SUBLANES, ROUTE_LANES), 0)
    stats = jnp.where(row == 0, cnt_b.T[0:1, :], jnp.where(row == 1, loff_b.T[0:1, :], 0.0))
    return x2, h_hi, route_t.T, stats


def _xattn_route(x, u, y, w_out, k, v, bsz, seqlen, g_xattn, w_q, w_o, g_moe, w_rg, b_rg, w_re, b_re):
    t, d = x.shape
    w_mix = w_out.astype(BF16)
    w_r = jnp.pad(jnp.concatenate([w_rg, w_re], axis=1), ((0, 0), (0, ROUTE_LANES - N_GROUPS - N_EXPERTS)))
    b_r = jnp.pad(jnp.concatenate([b_rg, b_re]), (0, ROUTE_LANES - N_GROUPS - N_EXPERTS)).reshape(1, ROUTE_LANES)
    wr_hi = w_r.astype(BF16)
    wr = jnp.concatenate([wr_hi, (w_r - wr_hi.astype(F32)).astype(BF16)], axis=1)
    step_rows = XA_TILES * TOK_TILE
    nt = seqlen // step_rows
    tile = pl.BlockSpec((step_rows, d), lambda b, j: (b * nt + j, 0))
    kv_spec = pl.BlockSpec((1, MEM_LEN, d), lambda b, j: (b, 0, 0))
    const = lambda shape: pl.BlockSpec(shape, lambda b, j: (0, 0))
    return pl.pallas_call(
        _xattn_body,
        grid=(bsz, nt),
        in_specs=[tile, tile, tile, const((d, d)), const((d, d)),
                  pl.BlockSpec((1, d, MEM_LEN), lambda b, j: (b, 0, 0)), kv_spec, const((1, d)),
                  const((d, d)), const((d, d)), const((1, d)),
                  const((d, 2 * ROUTE_LANES)), const((1, ROUTE_LANES))],
        out_specs=[tile, tile, pl.BlockSpec((step_rows, ROUTE_LANES), lambda b, j: (b * nt + j, 0)),
                   pl.BlockSpec((XA_TILES, SUBLANES, ROUTE_LANES), lambda b, j: (b * nt + j, 0, 0))],
        out_shape=[jax.ShapeDtypeStruct((t, d), F32), jax.ShapeDtypeStruct((t, d), BF16),
                   jax.ShapeDtypeStruct((t, ROUTE_LANES), F32),
                   jax.ShapeDtypeStruct((t // TOK_TILE, SUBLANES, ROUTE_LANES), F32)],
        compiler_params=_cparams(("arbitrary", "arbitrary")),
        name="xattn_route",
    )(x, u, y, w_mix[:CONV_CH], w_mix[CONV_CH:], jnp.swapaxes(k, 1, 2), v, g_xattn.reshape(1, d), w_q.astype(BF16), w_o.astype(BF16), g_moe.reshape(1, d),
      wr, b_r)


def _rows_copy(src_ref, src0, dst_ref, dst0, n, sem):
    rows = lambda r0: pl.ds(pl.multiple_of(r0, SUBLANES), pl.multiple_of(n, SUBLANES))
    return pltpu.make_async_copy(src_ref.at[rows(src0), :], dst_ref.at[rows(dst0), :], sem)


def _pack_pairs(v):
    half = v.shape[1] // 2
    lo = lax.bitcast_convert_type(v[:, :half], jnp.uint32)
    hi = lax.bitcast_convert_type(v[:, half:], jnp.uint32)
    return (lo >> 16) | (hi & jnp.uint32(0xFFFF0000))


def _unpack_pairs(w):
    lo = lax.bitcast_convert_type(w << 16, F32)
    hi = lax.bitcast_convert_type(w & jnp.uint32(0xFFFF0000), F32)
    return jnp.concatenate([lo, hi], axis=1).astype(BF16)


def _dispatch_body(cnt_ref, loff_ref, gb_ref, tot_ref, fstart_ref, fcnt_ref, nu_ref, h_ref, route_ref,
                   xs_ref, buf_ref, zero_ref, sem, zsem):
    i = pl.program_id(0)
    nt = pl.num_programs(0)
    slot = i % 2
    buf = buf_ref.at[slot]

    def drain(step, s):
        n = sum(tot_ref[step * MOE_TILES + k] for k in range(MOE_TILES))
        _rows_copy(buf_ref.at[s], 0, xs_ref, 0, n, sem.at[s]).wait()

    @pl.when(i >= 2)
    def _():
        drain(i - 2, slot)

    def sort_tile(k):
        rows = slice(k * TOK_TILE, (k + 1) * TOK_TILE)
        rt = route_ref[rows, :].T
        lp1 = rt[4:5, :].astype(jnp.int32)
        lp2 = rt[5:6, :].astype(jnp.int32)
        r_i = lax.broadcasted_iota(jnp.int32, (SORT_ROWS, TOK_TILE), 0)
        perm = jnp.where((r_i == lp1) | (r_i == lp2), 1.0, 0.0).astype(BF16)
        yield
        buf[k * SORT_ROWS:(k + 1) * SORT_ROWS, :] = _pack_pairs(
            jnp.dot(perm, h_ref[rows, :], preferred_element_type=F32))

    _interleave([sort_tile(k) for k in range(MOE_TILES)])

    for k in range(MOE_TILES):
        def per_expert(e, c, k=k):
            j = (i * MOE_TILES + k) * N_EXPERTS + e

            @pl.when(cnt_ref[j] > 0)
            def _():
                _rows_copy(buf, k * SORT_ROWS + loff_ref[j], xs_ref, gb_ref[j], cnt_ref[j],
                           sem.at[slot]).start()
            return c

        lax.fori_loop(0, N_EXPERTS, per_expert, 0)

    @pl.when(i == 0)
    def _():
        zero_ref[...] = jnp.zeros(zero_ref.shape, zero_ref.dtype)

        def fill(start):
            def body(e, c):
                @pl.when(fcnt_ref[e] > 0)
                def _():
                    copy = _rows_copy(zero_ref, 0, xs_ref, fstart_ref[e], fcnt_ref[e], zsem)
                    copy.start() if start else copy.wait()
                return c
            lax.fori_loop(0, N_EXPERTS, body, 0)

            def tail(blk, c):
                copy = _rows_copy(zero_ref, 0, xs_ref, blk * MOE_BLOCK, MOE_BLOCK, zsem)
                copy.start() if start else copy.wait()
                return c
            lax.fori_loop(nu_ref[0], xs_ref.shape[0] // MOE_BLOCK, tail, 0)

        fill(True)
        fill(False)

    @pl.when(i == nt - 1)
    def _():
        drain(i, slot)

        @pl.when(i >= 1)
        def _():
            drain(i - 1, 1 - slot)


def _dispatch(h2, route, tables, n_used, n_slots):
    t, d = h2.shape
    step_rows = MOE_TILES * TOK_TILE
    return pl.pallas_call(
        _dispatch_body,
        grid_spec=pltpu.PrefetchScalarGridSpec(
            num_scalar_prefetch=7,
            grid=(t // step_rows,),
            in_specs=[
                pl.BlockSpec((step_rows, d), lambda i, *_: (i, 0)),
                pl.BlockSpec((step_rows, ROUTE_LANES), lambda i, *_: (i, 0)),
            ],
            out_specs=pl.BlockSpec(memory_space=pl.ANY),
            scratch_shapes=[pltpu.VMEM((2, MOE_TILES * SORT_ROWS, d // 2), jnp.uint32),
                            pltpu.VMEM((MOE_BLOCK, d // 2), jnp.uint32),
                            pltpu.SemaphoreType.DMA((2,)), pltpu.SemaphoreType.DMA(())],
        ),
        out_shape=jax.ShapeDtypeStruct((n_slots, d // 2), jnp.uint32),
        compiler_params=_cparams(("arbitrary",)),
        name="dispatch",
    )(tables["cnt"], tables["loff"], tables["gb"], tables["tot"], tables["fill_start"],
      tables["fill_cnt"], n_used, h2, route)


def _experts_body(be_ref, nu_ref, first_ref, slot_ref, nxt_ref, x_ref, wg_hbm, wu_hbm, wd_hbm, y_ref,
                  wg_f, wu_f, wd_f, wg_bf, wu_bf, wd_bf, sem):
    b = pl.program_id(0)
    used = b < nu_ref[0]

    def weights(e, s):
        return [pltpu.make_async_copy(hbm.at[e], buf.at[s], sem.at[s])
                for hbm, buf in ((wg_hbm, wg_f), (wu_hbm, wu_f), (wd_hbm, wd_f))]

    @pl.when(used & (first_ref[b] == 1))
    def _():
        s = slot_ref[b]

        @pl.when(b == 0)
        def _():
            for copy in weights(be_ref[0], 0):
                copy.start()

        for copy in weights(be_ref[b], s):
            copy.wait()
        wg_bf[...] = wg_f[s].astype(BF16)
        wu_bf[...] = wu_f[s].astype(BF16)
        wd_bf[...] = wd_f[s].astype(BF16)

        @pl.when(nxt_ref[b] >= 0)
        def _():
            for copy in weights(nxt_ref[b], 1 - s):
                copy.start()

    @pl.when(used)
    def _():
        x = _unpack_pairs(x_ref[...])
        g = jnp.dot(x, wg_bf[...], preferred_element_type=F32)
        u = jnp.dot(x, wu_bf[...], preferred_element_type=F32)
        a = (g * _sigmoid(g) * u).astype(BF16)
        y = jnp.dot(a, wd_bf[...], preferred_element_type=F32)
        y_ref[...] = _pack_pairs(y.astype(BF16).astype(F32))

    @pl.when(jnp.logical_not(used))
    def _():
        y_ref[...] = jnp.zeros(y_ref.shape, y_ref.dtype)


def _experts(xs, etab, w_gate, w_up, w_down):
    n_slots, dp = xs.shape
    d = w_gate.shape[1]
    n_blocks = n_slots // MOE_BLOCK
    last = lambda b, nu: jnp.maximum(jnp.minimum(b, nu[0] - 1), 0)
    hbm = pl.BlockSpec(memory_space=pl.ANY)
    return pl.pallas_call(
        _experts_body,
        grid_spec=pltpu.PrefetchScalarGridSpec(
            num_scalar_prefetch=5,
            grid=(n_blocks,),
            in_specs=[pl.BlockSpec((MOE_BLOCK, dp), lambda b, be, nu, *_: (last(b, nu), 0)), hbm, hbm, hbm],
            out_specs=pl.BlockSpec((MOE_BLOCK, dp), lambda b, *_: (b, 0)),
            scratch_shapes=[pltpu.VMEM((2, d, D_EXPERT), F32), pltpu.VMEM((2, d, D_EXPERT), F32),
                            pltpu.VMEM((2, D_EXPERT, d), F32),
                            pltpu.VMEM((d, D_EXPERT), BF16), pltpu.VMEM((d, D_EXPERT), BF16),
                            pltpu.VMEM((D_EXPERT, d), BF16), pltpu.SemaphoreType.DMA((2,))],
        ),
        out_shape=jax.ShapeDtypeStruct((n_slots, dp), jnp.uint32),
        compiler_params=_cparams(("arbitrary",)),
        name="experts",
    )(etab["block_e"], etab["n_used"], etab["first"], etab["slot"], etab["nxt"], xs, w_gate, w_up, w_down)


def _combine_body(cnt_ref, loff_ref, gb_ref, tot_ref, x_ref, route_ref, g_ref, ys_ref, o_ref, ybuf_ref, sem):
    i = pl.program_id(0)
    nt = pl.num_programs(0)
    slot = i % 2

    def gather(step, s):
        for k in range(MOE_TILES):
            def per_expert(e, c, k=k):
                j = (step * MOE_TILES + k) * N_EXPERTS + e

                @pl.when(cnt_ref[j] > 0)
                def _():
                    _rows_copy(ys_ref, gb_ref[j], ybuf_ref.at[s], k * SORT_ROWS + loff_ref[j], cnt_ref[j],
                               sem.at[s]).start()
                return c

            lax.fori_loop(0, N_EXPERTS, per_expert, 0)

    @pl.when(i == 0)
    def _():
        ybuf_ref[...] = jnp.zeros(ybuf_ref.shape, ybuf_ref.dtype)
        gather(0, 0)

    @pl.when(i + 1 < nt)
    def _():
        gather(i + 1, 1 - slot)

    def combine_tile(k):
        rows = slice(k * TOK_TILE, (k + 1) * TOK_TILE)
        route = route_ref[rows, :]
        c_i = lax.broadcasted_iota(jnp.int32, (TOK_TILE, SORT_ROWS), 1)
        lp1 = route[:, 4:5].astype(jnp.int32)
        lp2 = route[:, 5:6].astype(jnp.int32)
        pw = jnp.where(c_i == lp1, route[:, 2:3], jnp.where(c_i == lp2, route[:, 3:4], 0.0)).astype(BF16)
        yield
        r_i = lax.broadcasted_iota(jnp.int32, (SORT_ROWS, 1), 0)
        y = _unpack_pairs(jnp.where(r_i < tot_ref[i * MOE_TILES + k],
                                    ybuf_ref[slot, k * SORT_ROWS:(k + 1) * SORT_ROWS, :], jnp.uint32(0)))
        moe = jnp.dot(pw, y, preferred_element_type=F32)
        yield
        o_ref[rows, :] = _rms(x_ref[rows, :] + moe, g_ref[...])

    tiles = [combine_tile(k) for k in range(MOE_TILES)]
    for tile in tiles:
        next(tile)
    n = sum(tot_ref[i * MOE_TILES + k] for k in range(MOE_TILES))
    _rows_copy(ys_ref, 0, ybuf_ref.at[slot], 0, n, sem.at[slot]).wait()
    _interleave(tiles)


def _combine(x2, route, tables, ys, g_final):
    t, d = x2.shape
    step_rows = MOE_TILES * TOK_TILE
    tile = pl.BlockSpec((step_rows, d), lambda i, *_: (i, 0))
    return pl.pallas_call(
        _combine_body,
        grid_spec=pltpu.PrefetchScalarGridSpec(
            num_scalar_prefetch=4,
            grid=(t // step_rows,),
            in_specs=[
                tile,
                pl.BlockSpec((step_rows, ROUTE_LANES), lambda i, *_: (i, 0)),
                pl.BlockSpec((1, d), lambda i, *_: (0, 0)),
                pl.BlockSpec(memory_space=pl.ANY),
            ],
            out_specs=tile,
            scratch_shapes=[pltpu.VMEM((2, MOE_TILES * SORT_ROWS, d // 2), jnp.uint32),
                            pltpu.SemaphoreType.DMA((2,))],
        ),
        out_shape=jax.ShapeDtypeStruct((t, d), F32),
        compiler_params=_cparams(("arbitrary",)),
        name="combine",
    )(tables["cnt"], tables["loff"], tables["gb"], tables["tot"], x2, route, g_final.reshape(1, d), ys)


def _routing_tables(stats, n_tok):
    cnt = stats[:, 0, :N_EXPERTS].astype(jnp.int32)
    loff = stats[:, 1, :N_EXPERTS].astype(jnp.int32)
    n_tiles = cnt.shape[0]
    counts = jnp.sum(cnt, axis=0)
    padded = ((counts + MOE_BLOCK - 1) // MOE_BLOCK) * MOE_BLOCK
    pad_end = jnp.cumsum(padded)
    pad_start = pad_end - padded
    gb = pad_start[None, :] + jnp.cumsum(cnt, axis=0) - cnt
    max_rows = (n_tok * TOP_K + n_tiles * N_EXPERTS * (SUBLANES - 1)
                + N_EXPERTS * (MOE_BLOCK - SUBLANES))
    n_blocks = -(-max_rows // MOE_BLOCK)
    n_slots = n_blocks * MOE_BLOCK
    block_start = jnp.arange(n_blocks, dtype=jnp.int32) * MOE_BLOCK
    block_e = jnp.minimum(jnp.sum((pad_end[None, :] <= block_start[:, None]).astype(jnp.int32), axis=1),
                          N_EXPERTS - 1)
    tables = dict(cnt=cnt.reshape(-1), loff=loff.reshape(-1), gb=gb.reshape(-1).astype(jnp.int32),
                  tot=jnp.sum(cnt, axis=1).astype(jnp.int32),
                  fill_start=(pad_start + counts).astype(jnp.int32),
                  fill_cnt=(padded - counts).astype(jnp.int32))
    n_used = (pad_end[-1:] // MOE_BLOCK).astype(jnp.int32)
    first = jnp.concatenate([jnp.ones((1,), jnp.int32), (block_e[1:] != block_e[:-1]).astype(jnp.int32)])
    seg_slot = (jnp.cumsum(first) - 1) % 2
    experts = jnp.arange(N_EXPERTS, dtype=jnp.int32)
    later = jnp.where((experts[None, :] > experts[:, None]) & (padded[None, :] > 0), experts[None, :], N_EXPERTS)
    next_e = jnp.min(later, axis=1)
    next_e = jnp.where(next_e < N_EXPERTS, next_e, -1)
    nxt = jnp.sum(jnp.where(block_e[:, None] == experts[None, :], next_e[None, :], 0), axis=1)
    etab = dict(block_e=block_e, n_used=n_used, first=first, slot=seg_slot.astype(jnp.int32),
                nxt=nxt.astype(jnp.int32))
    return tables, etab, n_slots


def kernel(x, mem, g_mix, w_in, conv_w, conv_b, ln_g, ln_b, ssd_conv_w, ssd_conv_b, dt_bias, a_log, d_skip, ssd_norm_g, w_out, g_xattn, g_mem, w_q, w_k, w_v, w_o, g_moe, w_router_group, b_router_group, w_router_expert, b_router_expert, w_gate, w_up, w_down, g_final):
    bsz, seqlen, d = x.shape
    n_tok = bsz * seqlen
    xt = x.reshape(n_tok, d)
    assert g_mix.shape[0] == 1, "the combine kernel applies the final norm: single layer only"
    for l in range(1):
        u, zx, dt_raw = _mix_in(xt, bsz, seqlen, g_mix[l], w_in[l], conv_w[l], conv_b[l],
                                    ln_g[l], ln_b[l])
        y = _ssd(zx, dt_raw, bsz, seqlen, ssd_conv_w[l], ssd_conv_b[l], dt_bias[l], a_log[l],
                 d_skip[l], ssd_norm_g[l])
        k, v = _kv_proj(mem, g_mem[l], w_k[l], w_v[l])
        x2, h2, route, stats = _xattn_route(xt, u, y, w_out[l], k, v, bsz, seqlen, g_xattn[l], w_q[l],
                                            w_o[l], g_moe[l], w_router_group[l], b_router_group[l],
                                            w_router_expert[l], b_router_expert[l])
        tables, etab, n_slots = _routing_tables(stats, n_tok)
        xs = _dispatch(h2, route, tables, etab["n_used"], n_slots)
        ys = _experts(xs, etab, w_gate[l], w_up[l], w_down[l])
        xt = _combine(x2, route, tables, ys, g_final)
    return xt.reshape(bsz, seqlen, d)
```

```python
import functools

import jax
import jax.numpy as jnp
from jax import lax
from jax.experimental import pallas as pl
from jax.experimental.pallas import tpu as pltpu

F32 = jnp.float32
BF16 = jnp.bfloat16

D_MODEL = 1024
CONV_CH = 1024
CONV_K = 31
SSD_INNER = 1024
SSD_HEAD_DIM = 64
SSD_HEADS = 16
SSD_STATE = 128
SSD_GROUPS = 2
SSD_GROUP_W = SSD_INNER // SSD_GROUPS
SSD_CONV_K = 4
SSD_CHUNK = 128
SSD_BC = SSD_GROUPS * SSD_STATE
SSD_CONV_CH = SSD_INNER + 2 * SSD_BC
N_MAIN = 2 * CONV_CH + 2 * SSD_INNER + 2 * SSD_BC
XA_HEADS = 4
XA_HEAD_DIM = 256
MEM_LEN = 256
N_GROUPS = 4
EXPERTS_PER_GROUP = 8
N_EXPERTS = 32
TOP_K = 2
D_EXPERT = 512
MOE_BLOCK = 512
RMS_EPS = 1e-6
LN_EPS = 1e-5
LOG2_E = 1.4426950408889634

LANES = 128
SUBLANES = 8
VMEM_LIMIT = 56 * 1024 * 1024

TOK_TILE = 512
CONV_TILE = 512
ZX_Z0 = -(-SSD_CONV_CH // SSD_INNER) * SSD_INNER
ZX_W = ZX_Z0 + SSD_INNER
ZX_SHARES = (0, 512, 512, 512, 512, 512, 512, 0)
assert sum(ZX_SHARES) == ZX_W and len(ZX_SHARES) == CONV_CH // LANES
CONV_HALO = 32
CONV_ROWS = 512
SSD_TILE = 1024
MOE_TILES = 2
XA_TILES = 2
SSD_UNROLL = 8
SSD_HALO = 8
ROUTE_LANES = 128
SORT_ROWS = TOP_K * TOK_TILE + N_EXPERTS * SUBLANES


def _cparams(sem):
    return pltpu.CompilerParams(dimension_semantics=sem, vmem_limit_bytes=VMEM_LIMIT)


def _rms(x, g):
    return x * lax.rsqrt(jnp.mean(x * x, axis=-1, keepdims=True) + RMS_EPS) * g


def _sigmoid(x):
    return 1.0 / (1.0 + jnp.exp2(x * (-LOG2_E)))


def _interleave(stages):
    results = {}
    while len(results) < len(stages):
        for k, item in enumerate(stages):
            if k not in results:
                try:
                    next(item)
                except StopIteration as done:
                    results[k] = done.value
    return [results[k] for k in range(len(stages))]


def _kv_body(m_ref, g_ref, wk_ref, wv_ref, k_ref, v_ref):
    m = _rms(m_ref[0], g_ref[...]).astype(BF16)
    k_ref[0] = jnp.dot(m, wk_ref[...], preferred_element_type=F32).astype(BF16)
    v_ref[0] = jnp.dot(m, wv_ref[...], preferred_element_type=F32).astype(BF16)


def _kv_proj(mem, g_mem, w_k, w_v):
    b, s, d = mem.shape
    w_spec = pl.BlockSpec((d, d), lambda i: (0, 0))
    kv_spec = pl.BlockSpec((1, s, d), lambda i: (i, 0, 0))
    return pl.pallas_call(
        _kv_body,
        grid=(b,),
        in_specs=[kv_spec, pl.BlockSpec((1, d), lambda i: (0, 0)), w_spec, w_spec],
        out_specs=[kv_spec, kv_spec],
        out_shape=[jax.ShapeDtypeStruct((b, s, d), BF16)] * 2,
        compiler_params=_cparams(("arbitrary",)),
        name="kv_proj",
    )(mem, g_mem.reshape(1, d), w_k.astype(BF16), w_v.astype(BF16))


def _mix_in_body(x_ref, g_ref, w_ref, wdt_ref, cw_ref, cb_ref, lg_ref, lb_ref,
                 u_ref, zx_ref, dt_ref, ubuf_ref, acc_ref):
    @pl.when(pl.program_id(1) == 0)
    def _():
        ubuf_ref[0:CONV_HALO, :] = jnp.zeros((CONV_HALO, CONV_CH), F32)

    h = _rms(x_ref[...], g_ref[...]).astype(BF16)
    dt_ref[...] = jnp.dot(h, wdt_ref[...], preferred_element_type=F32)

    first = CONV_HALO - (CONV_K - 1)
    w_lo = zx_lo = 0
    for cb, zx_w in enumerate(ZX_SHARES):
        cols = slice(cb * LANES, (cb + 1) * LANES)
        blk_w = 2 * LANES + zx_w
        r = jnp.dot(h, w_ref[:, w_lo:w_lo + blk_w], preferred_element_type=F32)
        ubuf_ref[CONV_HALO:CONV_HALO + CONV_TILE, cols] = r[:, :LANES] * _sigmoid(r[:, LANES:2 * LANES])
        if zx_w:
            zx_ref[:, zx_lo:zx_lo + zx_w] = r[:, 2 * LANES:].astype(BF16)
        w_lo += blk_w
        zx_lo += zx_w
        for rc in range(CONV_TILE // CONV_ROWS):
            r0 = rc * CONV_ROWS
            acc = None
            for res in range(SUBLANES):
                part = None
                for k in range(CONV_K):
                    if (first + k) % SUBLANES == res:
                        term = cw_ref[k:k + 1, cols] * ubuf_ref[r0 + first + k:r0 + first + k + CONV_ROWS, cols]
                        part = term if part is None else part + term
                acc = part if acc is None else acc + part
            acc_ref[r0:r0 + CONV_ROWS, cols] = acc

    for rc in range(CONV_TILE // CONV_ROWS):
        rows = slice(rc * CONV_ROWS, (rc + 1) * CONV_ROWS)
        u = acc_ref[rows, :] + cb_ref[...]
        mu = jnp.mean(u, axis=-1, keepdims=True)
        uc = u - mu
        var = jnp.mean(uc * uc, axis=-1, keepdims=True)
        y = uc * lax.rsqrt(var + LN_EPS) * lg_ref[...] + lb_ref[...]
        u_ref[rows, :] = (y * _sigmoid(y)).astype(BF16)

    ubuf_ref[0:CONV_HALO, :] = ubuf_ref[CONV_TILE:CONV_TILE + CONV_HALO, :]


def _mix_in(x2d, bsz, seqlen, g_mix, w_in, conv_w, conv_b, ln_g, ln_b):
    t, d = x2d.shape
    nt = seqlen // CONV_TILE
    w = w_in.astype(BF16)
    z0 = 2 * CONV_CH
    x0 = z0 + SSD_INNER
    w_zx = jnp.concatenate([w[:, x0:N_MAIN], jnp.zeros((d, ZX_Z0 - SSD_CONV_CH), BF16), w[:, z0:x0]], axis=1)
    parts, zx_lo = [], 0
    for cb, zx_w in enumerate(ZX_SHARES):
        parts += [w[:, cb * LANES:(cb + 1) * LANES], w[:, CONV_CH + cb * LANES:CONV_CH + (cb + 1) * LANES],
                  w_zx[:, zx_lo:zx_lo + zx_w]]
        zx_lo += zx_w
    w_blocks = jnp.concatenate(parts, axis=1)
    w_dt = jnp.pad(w_in[:, N_MAIN:], ((0, 0), (0, LANES - SSD_HEADS))).astype(BF16)
    row = lambda v: v.reshape(1, -1)
    const = lambda shape: pl.BlockSpec(shape, lambda b, j: (0, 0))
    tile = lambda width: pl.BlockSpec((CONV_TILE, width), lambda b, j: (b * nt + j, 0))
    return pl.pallas_call(
        _mix_in_body,
        grid=(bsz, nt),
        in_specs=[
            tile(d), const((1, d)), const(w_blocks.shape), const((d, LANES)),
            const((CONV_K + 1, CONV_CH)), const((1, CONV_CH)), const((1, CONV_CH)), const((1, CONV_CH)),
        ],
        out_specs=[tile(CONV_CH), tile(ZX_W), tile(LANES)],
        out_shape=[jax.ShapeDtypeStruct((t, CONV_CH), BF16), jax.ShapeDtypeStruct((t, ZX_W), BF16),
                   jax.ShapeDtypeStruct((t, LANES), F32)],
        scratch_shapes=[
            pltpu.VMEM((CONV_HALO + CONV_TILE, CONV_CH), F32),
            pltpu.VMEM((CONV_TILE, CONV_CH), F32),
        ],
        compiler_params=_cparams(("arbitrary", "arbitrary")),
        name="mix_in",
    )(x2d, row(g_mix), w_blocks, w_dt, jnp.pad(conv_w, ((0, 1), (0, 0))), row(conv_b), row(ln_g), row(ln_b))


def _ssd_body(xbc_ref, z_ref, dt_ref, cw_ref, cb_ref, dtb_ref, alog_ref, dsk_ref, ng_ref,
              expand_ref, y_ref, xbuf_ref, act_ref, dts_ref, state_ref):
    @pl.when(pl.program_id(1) == 0)
    def _():
        xbuf_ref[0:SSD_HALO, :] = jnp.zeros((SSD_HALO, SSD_CONV_CH), F32)
        state_ref[...] = jnp.zeros(state_ref.shape, F32)

    xbuf_ref[SSD_HALO:SSD_HALO + SSD_TILE, :] = xbc_ref[...].astype(F32)
    full = xbuf_ref[...]
    conv = cb_ref[...] + cw_ref[SSD_CONV_K - 1:SSD_CONV_K, :] * full[SSD_HALO:, :]
    for back in range(1, SSD_CONV_K):
        past = pltpu.roll(full, back, 0)[SSD_HALO:, :]
        conv = conv + cw_ref[SSD_CONV_K - 1 - back:SSD_CONV_K - back, :] * past
    act_ref[...] = conv * _sigmoid(conv)
    xbuf_ref[0:SSD_HALO, :] = xbuf_ref[SSD_TILE:SSD_TILE + SSD_HALO, :]

    dt_in = dt_ref[...] + dtb_ref[...]
    dts_ref[...] = jnp.maximum(dt_in, 0.0) + jnp.log1p(jnp.exp(-jnp.abs(dt_in)))

    a_neg = -jnp.exp(alog_ref[...]) * LOG2_E
    q = SSD_CHUNK
    row_i = lax.broadcasted_iota(jnp.int32, (q, q), 0)
    col_i = lax.broadcasted_iota(jnp.int32, (q, q), 1)
    causal = row_i >= col_i
    tril = causal.astype(F32)
    lane_i = lax.broadcasted_iota(jnp.int32, (q, LANES), 1)
    low_half = lane_i < SSD_HEAD_DIM
    expand = expand_ref[...]

    def chunk(c, carry):
        r0 = pl.multiple_of(c * q, q)
        rows = pl.ds(r0, q)
        dtc = dts_ref[rows, :]
        a_cs = jnp.dot(tril, dtc * a_neg, preferred_element_type=F32,
                       precision=lax.Precision.HIGHEST)
        a_cs_t = a_cs.T
        dt_t = dtc.T.astype(BF16)
        a_end = a_cs[q - 1:q, :]
        e_exp = jnp.dot(jnp.exp2(a_cs).astype(BF16), expand, preferred_element_type=F32)
        w_exp = jnp.dot((jnp.exp2(a_end - a_cs) * dtc).astype(BF16), expand,
                        preferred_element_type=F32)
        dec_row = e_exp[q - 1:q, :]
        xc = act_ref[rows, 0:SSD_INNER]
        xw = (xc * w_exp).astype(BF16)
        y_parts = []
        for g in range(SSD_GROUPS):
            b_f = act_ref[rows, SSD_INNER + g * SSD_STATE:SSD_INNER + (g + 1) * SSD_STATE]
            c_f = act_ref[rows, SSD_INNER + SSD_BC + g * SSD_STATE:
                          SSD_INNER + SSD_BC + (g + 1) * SSD_STATE]
            b_g = b_f.astype(BF16)
            c_g = c_f.astype(BF16)
            cb = lax.dot_general(c_g, b_g, (((1,), (1,)), ((), ())),
                                 preferred_element_type=F32).astype(BF16)
            gcols = slice(g * SSD_GROUP_W, (g + 1) * SSD_GROUP_W)
            st = state_ref[g]
            y_off = jnp.dot(c_g, st.astype(BF16), preferred_element_type=F32)
            state_ref[g] = st * dec_row[:, gcols] + jnp.dot(
                b_f.T.astype(BF16), xw[:, gcols], preferred_element_type=F32)
            for pair in range(SSD_GROUP_W // LANES):
                ms = []
                for hh in range(2):
                    h = g * (SSD_HEADS // SSD_GROUPS) + 2 * pair + hh
                    seg = a_cs[:, h:h + 1] - a_cs_t[h:h + 1, :]
                    dec = jnp.exp2(jnp.where(causal, seg, -jnp.inf))
                    ms.append(cb * dec.astype(BF16) * dt_t[h:h + 1, :])
                lhs = jnp.concatenate(ms, axis=1)
                xp = xc[:, g * SSD_GROUP_W + pair * LANES:g * SSD_GROUP_W + (pair + 1) * LANES]
                rhs = jnp.concatenate([jnp.where(low_half, xp, 0.0),
                                       jnp.where(low_half, 0.0, xp)], axis=0).astype(BF16)
                y_diag = jnp.dot(lhs, rhs, preferred_element_type=F32)
                lo = pair * LANES
                y_parts.append(y_diag + y_off[:, lo:lo + LANES]
                               * e_exp[:, g * SSD_GROUP_W + lo:g * SSD_GROUP_W + lo + LANES])
        y = jnp.concatenate(y_parts, axis=1) + xc * dsk_ref[...]
        z = z_ref[rows, :].astype(F32)
        y = y * (z * _sigmoid(z))
        outs = []
        for g in range(SSD_GROUPS):
            yg = y[:, g * SSD_GROUP_W:(g + 1) * SSD_GROUP_W]
            outs.append(yg * lax.rsqrt(jnp.mean(yg * yg, axis=-1, keepdims=True) + RMS_EPS))
        y_ref[rows, :] = (jnp.concatenate(outs, axis=1) * ng_ref[...]).astype(BF16)
        return carry

    lax.fori_loop(0, SSD_TILE // q, chunk, 0, unroll=SSD_UNROLL)


def _ssd(zx, dt_raw, bsz, seqlen, ssd_conv_w, ssd_conv_b, dt_bias, a_log, d_skip, ssd_norm_g):
    nt = seqlen // SSD_TILE
    pad_h = lambda v: jnp.pad(v, (0, LANES - SSD_HEADS)).reshape(1, LANES)
    expand = (jnp.arange(LANES)[:, None] == (jnp.arange(SSD_INNER) // SSD_HEAD_DIM)[None, :]).astype(BF16)
    const = lambda shape: pl.BlockSpec(shape, lambda b, j: (0, 0))
    return pl.pallas_call(
        _ssd_body,
        grid=(bsz, nt),
        in_specs=[
            pl.BlockSpec((SSD_TILE, SSD_CONV_CH), lambda b, j: (b * nt + j, 0)),
            pl.BlockSpec((SSD_TILE, SSD_INNER), lambda b, j: (b * nt + j, ZX_Z0 // SSD_INNER)),
            pl.BlockSpec((SSD_TILE, LANES), lambda b, j: (b * nt + j, 0)),
            const((SSD_CONV_K, SSD_CONV_CH)),
            const((1, SSD_CONV_CH)),
            const((1, LANES)), const((1, LANES)),
            const((1, SSD_INNER)), const((1, SSD_INNER)),
            const((LANES, SSD_INNER)),
        ],
        out_specs=pl.BlockSpec((SSD_TILE, SSD_INNER), lambda b, j: (b * nt + j, 0)),
        out_shape=jax.ShapeDtypeStruct((bsz * seqlen, SSD_INNER), BF16),
        scratch_shapes=[
            pltpu.VMEM((SSD_HALO + SSD_TILE, SSD_CONV_CH), F32),
            pltpu.VMEM((SSD_TILE, SSD_CONV_CH), F32),
            pltpu.VMEM((SSD_TILE, LANES), F32),
            pltpu.VMEM((SSD_GROUPS, SSD_STATE, SSD_GROUP_W), F32),
        ],
        compiler_params=_cparams(("arbitrary", "arbitrary")),
        name="ssd",
    )(zx, zx, dt_raw, ssd_conv_w, ssd_conv_b.reshape(1, SSD_CONV_CH), pad_h(dt_bias),
      pad_h(a_log), jnp.repeat(d_skip, SSD_HEAD_DIM).reshape(1, SSD_INNER),
      ssd_norm_g.reshape(1, SSD_INNER), expand)


def _xattn_body(x_ref, u_ref, y_ref, wu_ref, wy_ref, k_ref, v_ref, gx_ref, wq_ref, wo_ref, gm_ref,
                wr_ref, br_ref, x2_ref, h2_ref, route_ref, stats_ref):
    def tile(s):
        rows = slice(s * TOK_TILE, (s + 1) * TOK_TILE)
        return _xattn_tile(x_ref[rows, :], u_ref[rows, :], y_ref[rows, :], wu_ref, wy_ref, k_ref, v_ref,
                           gx_ref, wq_ref, wo_ref, gm_ref, wr_ref, br_ref)

    results = _interleave([tile(s) for s in range(XA_TILES)])
    for s in range(XA_TILES):
        rows = slice(s * TOK_TILE, (s + 1) * TOK_TILE)
        x2_ref[rows, :], h2_ref[rows, :], route_ref[rows, :], stats_ref[s] = results[s]


def _xattn_tile(x, u, y, wu_ref, wy_ref, k_ref, v_ref, gx_ref, wq_ref, wo_ref, gm_ref, wr_ref, br_ref):
    x = (x + jnp.dot(u, wu_ref[...], preferred_element_type=F32)
         + jnp.dot(y, wy_ref[...], preferred_element_type=F32))
    yield
    h = _rms(x, gx_ref[...]).astype(BF16)
    q = (jnp.dot(h, wq_ref[...], preferred_element_type=F32) * (XA_HEAD_DIM ** -0.5)).astype(BF16)
    yield
    heads = []
    for i in range(XA_HEADS):
        cols = slice(i * XA_HEAD_DIM, (i + 1) * XA_HEAD_DIM)
        s = jnp.dot(q[:, cols], k_ref[0, cols, :], preferred_element_type=F32)
        p = jnp.exp(s - jnp.max(s, axis=-1, keepdims=True))
        p = p / jnp.sum(p, axis=-1, keepdims=True)
        yield
        heads.append(jnp.dot(p.astype(BF16), v_ref[0, :, cols], preferred_element_type=F32))
        yield
    o = jnp.concatenate(heads, axis=1).astype(BF16)
    x2 = x + jnp.dot(o, wo_ref[...], preferred_element_type=F32)
    yield

    h2 = _rms(x2, gm_ref[...])
    h_hi = h2.astype(BF16)
    h_lo = (h2 - h_hi.astype(F32)).astype(BF16)
    both = jnp.dot(h_hi, wr_ref[...], preferred_element_type=F32)
    logits = (both[:, :ROUTE_LANES] + both[:, ROUTE_LANES:]
              + jnp.dot(h_lo, wr_ref[:, :ROUTE_LANES], preferred_element_type=F32)) + br_ref[...]
    yield
    lt = logits.T
    n_t = logits.shape[0]
    neg = -jnp.inf

    def first_max(rows):
        m = functools.reduce(jnp.maximum, rows)
        idx = jnp.full_like(m, len(rows) - 1)
        for j in reversed(range(len(rows) - 1)):
            idx = jnp.where(rows[j] == m, float(j), idx)
        return m, idx

    g_rows = [lt[g:g + 1, :] for g in range(N_GROUPS)]
    g_max, g_sel = first_max(g_rows)
    p_top = 1.0 / functools.reduce(lambda a, b: a + b, [jnp.exp(r - g_max) for r in g_rows])
    e_rows = []
    for j in range(EXPERTS_PER_GROUP):
        r = lt[N_GROUPS + j:N_GROUPS + j + 1, :]
        for g in range(1, N_GROUPS):
            lo = N_GROUPS + g * EXPERTS_PER_GROUP + j
            r = jnp.where(g_sel == float(g), lt[lo:lo + 1, :], r)
        e_rows.append(r)
    yield
    m1, i1 = first_max(e_rows)
    m2, i2 = first_max([jnp.where(i1 == float(j), neg, r) for j, r in enumerate(e_rows)])
    r = jnp.exp(m2 - m1)
    w1 = p_top / (1.0 + r)
    w2 = w1 * r
    e1 = g_sel * EXPERTS_PER_GROUP + i1
    e2 = g_sel * EXPERTS_PER_GROUP + i2
    yield
    expert = lax.broadcasted_iota(jnp.int32, (ROUTE_LANES, n_t), 0).astype(F32)
    oh1 = jnp.where(expert == e1, 1.0, 0.0)
    oh2 = jnp.where(expert == e2, 1.0, 0.0)
    earlier = (lax.broadcasted_iota(jnp.int32, (n_t, n_t), 0)
               < lax.broadcasted_iota(jnp.int32, (n_t, n_t), 1)).astype(BF16)
    c12 = jnp.dot(jnp.concatenate([oh1, oh2], axis=0).astype(BF16), earlier, preferred_element_type=F32)
    yield
    c1 = c12[:ROUTE_LANES, :]
    c2 = c12[ROUTE_LANES:, :]
    tot1 = c1[:, n_t - 1:n_t] + oh1[:, n_t - 1:n_t]
    cnt = tot1 + c2[:, n_t - 1:n_t] + oh2[:, n_t - 1:n_t]
    cnt = jnp.floor((cnt + (SUBLANES - 1)) * (1.0 / SUBLANES)) * SUBLANES
    cnt_b = jnp.broadcast_to(cnt, (ROUTE_LANES, ROUTE_LANES))
    experts_before = (lax.broadcasted_iota(jnp.int32, (ROUTE_LANES, ROUTE_LANES), 1)
                      < lax.broadcasted_iota(jnp.int32, (ROUTE_LANES, ROUTE_LANES), 0)).astype(F32)
    loff_b = jnp.dot(experts_before, cnt_b, preferred_element_type=F32, precision=lax.Precision.HIGHEST)
    loff = loff_b[:, 0:1]
    lp1 = jnp.sum(oh1 * (loff + c1), axis=0, keepdims=True)
    lp2 = jnp.sum(oh2 * (loff + tot1 + c2), axis=0, keepdims=True)
    route_t = jnp.where(expert == 0.0, e1,
              jnp.where(expert == 1.0, e2,
              jnp.where(expert == 2.0, w1,
              jnp.where(expert == 3.0, w2,
              jnp.where(expert == 4.0, lp1, jnp.where(expert == 5.0, lp2, 0.0))))))
    row = lax.broadcasted_iota(jnp.int32, (SUBLANES, ROUTE_LANES), 0)
    stats = jnp.where(row == 0, cnt_b.T[0:1, :], jnp.where(row == 1, loff_b.T[0:1, :], 0.0))
    return x2, h_hi, route_t.T, stats


def _xattn_route(x, u, y, w_out, k, v, bsz, seqlen, g_xattn, w_q, w_o, g_moe, w_rg, b_rg, w_re, b_re):
    t, d = x.shape
    w_mix = w_out.astype(BF16)
    w_r = jnp.pad(jnp.concatenate([w_rg, w_re], axis=1), ((0, 0), (0, ROUTE_LANES - N_GROUPS - N_EXPERTS)))
    b_r = jnp.pad(jnp.concatenate([b_rg, b_re]), (0, ROUTE_LANES - N_GROUPS - N_EXPERTS)).reshape(1, ROUTE_LANES)
    wr_hi = w_r.astype(BF16)
    wr = jnp.concatenate([wr_hi, (w_r - wr_hi.astype(F32)).astype(BF16)], axis=1)
    step_rows = XA_TILES * TOK_TILE
    nt = seqlen // step_rows
    tile = pl.BlockSpec((step_rows, d), lambda b, j: (b * nt + j, 0))
    kv_spec = pl.BlockSpec((1, MEM_LEN, d), lambda b, j: (b, 0, 0))
    const = lambda shape: pl.BlockSpec(shape, lambda b, j: (0, 0))
    return pl.pallas_call(
        _xattn_body,
        grid=(bsz, nt),
        in_specs=[tile, tile, tile, const((d, d)), const((d, d)),
                  pl.BlockSpec((1, d, MEM_LEN), lambda b, j: (b, 0, 0)), kv_spec, const((1, d)),
                  const((d, d)), const((d, d)), const((1, d)),
                  const((d, 2 * ROUTE_LANES)), const((1, ROUTE_LANES))],
        out_specs=[tile, tile, pl.BlockSpec((step_rows, ROUTE_LANES), lambda b, j: (b * nt + j, 0)),
                   pl.BlockSpec((XA_TILES, SUBLANES, ROUTE_LANES), lambda b, j: (b * nt + j, 0, 0))],
        out_shape=[jax.ShapeDtypeStruct((t, d), F32), jax.ShapeDtypeStruct((t, d), BF16),
                   jax.ShapeDtypeStruct((t, ROUTE_LANES), F32),
                   jax.ShapeDtypeStruct((t // TOK_TILE, SUBLANES, ROUTE_LANES), F32)],
        compiler_params=_cparams(("arbitrary", "arbitrary")),
        name="xattn_route",
    )(x, u, y, w_mix[:CONV_CH], w_mix[CONV_CH:], jnp.swapaxes(k, 1, 2), v, g_xattn.reshape(1, d), w_q.astype(BF16), w_o.astype(BF16), g_moe.reshape(1, d),
      wr, b_r)


def _rows_copy(src_ref, src0, dst_ref, dst0, n, sem):
    rows = lambda r0: pl.ds(pl.multiple_of(r0, SUBLANES), pl.multiple_of(n, SUBLANES))
    return pltpu.make_async_copy(src_ref.at[rows(src0), :], dst_ref.at[rows(dst0), :], sem)


def _pack_pairs(v):
    half = v.shape[1] // 2
    lo = lax.bitcast_convert_type(v[:, :half], jnp.uint32)
    hi = lax.bitcast_convert_type(v[:, half:], jnp.uint32)
    return (lo >> 16) | (hi & jnp.uint32(0xFFFF0000))


def _unpack_pairs(w):
    lo = lax.bitcast_convert_type(w << 16, F32)
    hi = lax.bitcast_convert_type(w & jnp.uint32(0xFFFF0000), F32)
    return jnp.concatenate([lo, hi], axis=1).astype(BF16)


def _dispatch_body(cnt_ref, loff_ref, gb_ref, tot_ref, fstart_ref, fcnt_ref, nu_ref, h_ref, route_ref,
                   xs_ref, buf_ref, zero_ref, sem, zsem):
    i = pl.program_id(0)
    nt = pl.num_programs(0)
    slot = i % 2
    buf = buf_ref.at[slot]

    def drain(step, s):
        n = sum(tot_ref[step * MOE_TILES + k] for k in range(MOE_TILES))
        _rows_copy(buf_ref.at[s], 0, xs_ref, 0, n, sem.at[s]).wait()

    @pl.when(i >= 2)
    def _():
        drain(i - 2, slot)

    def sort_tile(k):
        rows = slice(k * TOK_TILE, (k + 1) * TOK_TILE)
        rt = route_ref[rows, :].T
        lp1 = rt[4:5, :].astype(jnp.int32)
        lp2 = rt[5:6, :].astype(jnp.int32)
        r_i = lax.broadcasted_iota(jnp.int32, (SORT_ROWS, TOK_TILE), 0)
        perm = jnp.where((r_i == lp1) | (r_i == lp2), 1.0, 0.0).astype(BF16)
        yield
        buf[k * SORT_ROWS:(k + 1) * SORT_ROWS, :] = _pack_pairs(
            jnp.dot(perm, h_ref[rows, :], preferred_element_type=F32))

    _interleave([sort_tile(k) for k in range(MOE_TILES)])

    for k in range(MOE_TILES):
        def per_expert(e, c, k=k):
            j = (i * MOE_TILES + k) * N_EXPERTS + e

            @pl.when(cnt_ref[j] > 0)
            def _():
                _rows_copy(buf, k * SORT_ROWS + loff_ref[j], xs_ref, gb_ref[j], cnt_ref[j],
                           sem.at[slot]).start()
            return c

        lax.fori_loop(0, N_EXPERTS, per_expert, 0)

    @pl.when(i == 0)
    def _():
        zero_ref[...] = jnp.zeros(zero_ref.shape, zero_ref.dtype)

        def fill(start):
            def body(e, c):
                @pl.when(fcnt_ref[e] > 0)
                def _():
                    copy = _rows_copy(zero_ref, 0, xs_ref, fstart_ref[e], fcnt_ref[e], zsem)
                    copy.start() if start else copy.wait()
                return c
            lax.fori_loop(0, N_EXPERTS, body, 0)

            def tail(blk, c):
                copy = _rows_copy(zero_ref, 0, xs_ref, blk * MOE_BLOCK, MOE_BLOCK, zsem)
                copy.start() if start else copy.wait()
                return c
            lax.fori_loop(nu_ref[0], xs_ref.shape[0] // MOE_BLOCK, tail, 0)

        fill(True)
        fill(False)

    @pl.when(i == nt - 1)
    def _():
        drain(i, slot)

        @pl.when(i >= 1)
        def _():
            drain(i - 1, 1 - slot)


def _dispatch(h2, route, tables, n_used, n_slots):
    t, d = h2.shape
    step_rows = MOE_TILES * TOK_TILE
    return pl.pallas_call(
        _dispatch_body,
        grid_spec=pltpu.PrefetchScalarGridSpec(
            num_scalar_prefetch=7,
            grid=(t // step_rows,),
            in_specs=[
                pl.BlockSpec((step_rows, d), lambda i, *_: (i, 0)),
                pl.BlockSpec((step_rows, ROUTE_LANES), lambda i, *_: (i, 0)),
            ],
            out_specs=pl.BlockSpec(memory_space=pl.ANY),
            scratch_shapes=[pltpu.VMEM((2, MOE_TILES * SORT_ROWS, d // 2), jnp.uint32),
                            pltpu.VMEM((MOE_BLOCK, d // 2), jnp.uint32),
                            pltpu.SemaphoreType.DMA((2,)), pltpu.SemaphoreType.DMA(())],
        ),
        out_shape=jax.ShapeDtypeStruct((n_slots, d // 2), jnp.uint32),
        compiler_params=_cparams(("arbitrary",)),
        name="dispatch",
    )(tables["cnt"], tables["loff"], tables["gb"], tables["tot"], tables["fill_start"],
      tables["fill_cnt"], n_used, h2, route)


def _experts_body(be_ref, nu_ref, first_ref, slot_ref, nxt_ref, half_ref, x_ref, wg_hbm, wu_hbm, wd_hbm, y_ref,
                  wg_f, wu_f, wd_f, wg_bf, wu_bf, wd_bf, sem):
    b = pl.program_id(0)
    used = b < nu_ref[0]

    def weights(e, s):
        return [pltpu.make_async_copy(hbm.at[e], buf.at[s], sem.at[s])
                for hbm, buf in ((wg_hbm, wg_f), (wu_hbm, wu_f), (wd_hbm, wd_f))]

    @pl.when(used & (first_ref[b] == 1))
    def _():
        s = slot_ref[b]

        @pl.when(b == 0)
        def _():
            for copy in weights(be_ref[0], 0):
                copy.start()

        for copy in weights(be_ref[b], s):
            copy.wait()
        wg_bf[...] = wg_f[s].astype(BF16)
        wu_bf[...] = wu_f[s].astype(BF16)
        wd_bf[...] = wd_f[s].astype(BF16)

        @pl.when(nxt_ref[b] >= 0)
        def _():
            for copy in weights(nxt_ref[b], 1 - s):
                copy.start()

    def ffn(rows):
        x = _unpack_pairs(x_ref[rows, :])
        g = jnp.dot(x, wg_bf[...], preferred_element_type=F32)
        u = jnp.dot(x, wu_bf[...], preferred_element_type=F32)
        a = (g * _sigmoid(g) * u).astype(BF16)
        y = jnp.dot(a, wd_bf[...], preferred_element_type=F32)
        y_ref[rows, :] = _pack_pairs(y.astype(BF16).astype(F32))

    half = half_ref[b] == 1

    @pl.when(used & jnp.logical_not(half))
    def _():
        ffn(slice(0, MOE_BLOCK))

    @pl.when(used & half)
    def _():
        ffn(slice(0, MOE_BLOCK // 2))
        y_ref[MOE_BLOCK // 2:, :] = jnp.zeros((MOE_BLOCK // 2, y_ref.shape[1]), y_ref.dtype)

    @pl.when(jnp.logical_not(used))
    def _():
        y_ref[...] = jnp.zeros(y_ref.shape, y_ref.dtype)


def _experts(xs, etab, w_gate, w_up, w_down):
    n_slots, dp = xs.shape
    d = w_gate.shape[1]
    n_blocks = n_slots // MOE_BLOCK
    last = lambda b, nu: jnp.maximum(jnp.minimum(b, nu[0] - 1), 0)
    hbm = pl.BlockSpec(memory_space=pl.ANY)
    return pl.pallas_call(
        _experts_body,
        grid_spec=pltpu.PrefetchScalarGridSpec(
            num_scalar_prefetch=6,
            grid=(n_blocks,),
            in_specs=[pl.BlockSpec((MOE_BLOCK, dp), lambda b, be, nu, *_: (last(b, nu), 0)), hbm, hbm, hbm],
            out_specs=pl.BlockSpec((MOE_BLOCK, dp), lambda b, *_: (b, 0)),
            scratch_shapes=[pltpu.VMEM((2, d, D_EXPERT), F32), pltpu.VMEM((2, d, D_EXPERT), F32),
                            pltpu.VMEM((2, D_EXPERT, d), F32),
                            pltpu.VMEM((d, D_EXPERT), BF16), pltpu.VMEM((d, D_EXPERT), BF16),
                            pltpu.VMEM((D_EXPERT, d), BF16), pltpu.SemaphoreType.DMA((2,))],
        ),
        out_shape=jax.ShapeDtypeStruct((n_slots, dp), jnp.uint32),
        compiler_params=_cparams(("arbitrary",)),
        name="experts",
    )(etab["block_e"], etab["n_used"], etab["first"], etab["slot"], etab["nxt"], etab["half"], xs, w_gate,
      w_up, w_down)


def _combine_body(cnt_ref, loff_ref, gb_ref, tot_ref, x_ref, route_ref, g_ref, ys_ref, o_ref, ybuf_ref, sem):
    i = pl.program_id(0)
    nt = pl.num_programs(0)
    slot = i % 2

    def gather(step, s):
        for k in range(MOE_TILES):
            def per_expert(e, c, k=k):
                j = (step * MOE_TILES + k) * N_EXPERTS + e

                @pl.when(cnt_ref[j] > 0)
                def _():
                    _rows_copy(ys_ref, gb_ref[j], ybuf_ref.at[s], k * SORT_ROWS + loff_ref[j], cnt_ref[j],
                               sem.at[s]).start()
                return c

            lax.fori_loop(0, N_EXPERTS, per_expert, 0)

    @pl.when(i == 0)
    def _():
        ybuf_ref[...] = jnp.zeros(ybuf_ref.shape, ybuf_ref.dtype)
        gather(0, 0)

    @pl.when(i + 1 < nt)
    def _():
        gather(i + 1, 1 - slot)

    def combine_tile(k):
        rows = slice(k * TOK_TILE, (k + 1) * TOK_TILE)
        route = route_ref[rows, :]
        c_i = lax.broadcasted_iota(jnp.int32, (TOK_TILE, SORT_ROWS), 1)
        lp1 = route[:, 4:5].astype(jnp.int32)
        lp2 = route[:, 5:6].astype(jnp.int32)
        pw = jnp.where(c_i == lp1, route[:, 2:3], jnp.where(c_i == lp2, route[:, 3:4], 0.0)).astype(BF16)
        yield
        r_i = lax.broadcasted_iota(jnp.int32, (SORT_ROWS, 1), 0)
        y = _unpack_pairs(jnp.where(r_i < tot_ref[i * MOE_TILES + k],
                                    ybuf_ref[slot, k * SORT_ROWS:(k + 1) * SORT_ROWS, :], jnp.uint32(0)))
        moe = jnp.dot(pw, y, preferred_element_type=F32)
        yield
        o_ref[rows, :] = _rms(x_ref[rows, :] + moe, g_ref[...])

    tiles = [combine_tile(k) for k in range(MOE_TILES)]
    for tile in tiles:
        next(tile)
    n = sum(tot_ref[i * MOE_TILES + k] for k in range(MOE_TILES))
    _rows_copy(ys_ref, 0, ybuf_ref.at[slot], 0, n, sem.at[slot]).wait()
    _interleave(tiles)


def _combine(x2, route, tables, ys, g_final):
    t, d = x2.shape
    step_rows = MOE_TILES * TOK_TILE
    tile = pl.BlockSpec((step_rows, d), lambda i, *_: (i, 0))
    return pl.pallas_call(
        _combine_body,
        grid_spec=pltpu.PrefetchScalarGridSpec(
            num_scalar_prefetch=4,
            grid=(t // step_rows,),
            in_specs=[
                tile,
                pl.BlockSpec((step_rows, ROUTE_LANES), lambda i, *_: (i, 0)),
                pl.BlockSpec((1, d), lambda i, *_: (0, 0)),
                pl.BlockSpec(memory_space=pl.ANY),
            ],
            out_specs=tile,
            scratch_shapes=[pltpu.VMEM((2, MOE_TILES * SORT_ROWS, d // 2), jnp.uint32),
                            pltpu.SemaphoreType.DMA((2,))],
        ),
        out_shape=jax.ShapeDtypeStruct((t, d), F32),
        compiler_params=_cparams(("arbitrary",)),
        name="combine",
    )(tables["cnt"], tables["loff"], tables["gb"], tables["tot"], x2, route, g_final.reshape(1, d), ys)


def _routing_tables(stats, n_tok):
    cnt = stats[:, 0, :N_EXPERTS].astype(jnp.int32)
    loff = stats[:, 1, :N_EXPERTS].astype(jnp.int32)
    n_tiles = cnt.shape[0]
    counts = jnp.sum(cnt, axis=0)
    padded = ((counts + MOE_BLOCK - 1) // MOE_BLOCK) * MOE_BLOCK
    pad_end = jnp.cumsum(padded)
    pad_start = pad_end - padded
    gb = pad_start[None, :] + jnp.cumsum(cnt, axis=0) - cnt
    max_rows = (n_tok * TOP_K + n_tiles * N_EXPERTS * (SUBLANES - 1)
                + N_EXPERTS * (MOE_BLOCK - SUBLANES))
    n_blocks = -(-max_rows // MOE_BLOCK)
    n_slots = n_blocks * MOE_BLOCK
    block_start = jnp.arange(n_blocks, dtype=jnp.int32) * MOE_BLOCK
    block_e = jnp.minimum(jnp.sum((pad_end[None, :] <= block_start[:, None]).astype(jnp.int32), axis=1),
                          N_EXPERTS - 1)
    tables = dict(cnt=cnt.reshape(-1), loff=loff.reshape(-1), gb=gb.reshape(-1).astype(jnp.int32),
                  tot=jnp.sum(cnt, axis=1).astype(jnp.int32),
                  fill_start=(pad_start + counts).astype(jnp.int32),
                  fill_cnt=(padded - counts).astype(jnp.int32))
    n_used = (pad_end[-1:] // MOE_BLOCK).astype(jnp.int32)
    first = jnp.concatenate([jnp.ones((1,), jnp.int32), (block_e[1:] != block_e[:-1]).astype(jnp.int32)])
    seg_slot = (jnp.cumsum(first) - 1) % 2
    experts = jnp.arange(N_EXPERTS, dtype=jnp.int32)
    later = jnp.where((experts[None, :] > experts[:, None]) & (padded[None, :] > 0), experts[None, :], N_EXPERTS)
    next_e = jnp.min(later, axis=1)
    next_e = jnp.where(next_e < N_EXPERTS, next_e, -1)
    of_block = block_e[:, None] == experts[None, :]
    nxt = jnp.sum(jnp.where(of_block, next_e[None, :], 0), axis=1)
    filled = jnp.sum(jnp.where(of_block, (pad_start + counts)[None, :], 0), axis=1) - block_start
    half = (filled <= MOE_BLOCK // 2).astype(jnp.int32)
    etab = dict(block_e=block_e, n_used=n_used, first=first, slot=seg_slot.astype(jnp.int32),
                nxt=nxt.astype(jnp.int32), half=half)
    return tables, etab, n_slots


def kernel(x, mem, g_mix, w_in, conv_w, conv_b, ln_g, ln_b, ssd_conv_w, ssd_conv_b, dt_bias, a_log, d_skip, ssd_norm_g, w_out, g_xattn, g_mem, w_q, w_k, w_v, w_o, g_moe, w_router_group, b_router_group, w_router_expert, b_router_expert, w_gate, w_up, w_down, g_final):
    bsz, seqlen, d = x.shape
    n_tok = bsz * seqlen
    xt = x.reshape(n_tok, d)
    assert g_mix.shape[0] == 1, "the combine kernel applies the final norm: single layer only"
    for l in range(1):
        u, zx, dt_raw = _mix_in(xt, bsz, seqlen, g_mix[l], w_in[l], conv_w[l], conv_b[l],
                                    ln_g[l], ln_b[l])
        y = _ssd(zx, dt_raw, bsz, seqlen, ssd_conv_w[l], ssd_conv_b[l], dt_bias[l], a_log[l],
                 d_skip[l], ssd_norm_g[l])
        k, v = _kv_proj(mem, g_mem[l], w_k[l], w_v[l])
        x2, h2, route, stats = _xattn_route(xt, u, y, w_out[l], k, v, bsz, seqlen, g_xattn[l], w_q[l],
                                            w_o[l], g_moe[l], w_router_group[l], b_router_group[l],
                                            w_router_expert[l], b_router_expert[l])
        tables, etab, n_slots = _routing_tables(stats, n_tok)
        xs = _dispatch(h2, route, tables, etab["n_used"], n_slots)
        ys = _experts(xs, etab, w_gate[l], w_up[l], w_down[l])
        xt = _combine(x2, route, tables, ys, g_final)
    return xt.reshape(bsz, seqlen, d)
```

```python
import functools

import jax
import jax.numpy as jnp
from jax import lax
from jax.experimental import pallas as pl
from jax.experimental.pallas import tpu as pltpu

F32 = jnp.float32
BF16 = jnp.bfloat16

D_MODEL = 1024
CONV_CH = 1024
CONV_K = 31
SSD_INNER = 1024
SSD_HEAD_DIM = 64
SSD_HEADS = 16
SSD_STATE = 128
SSD_GROUPS = 2
SSD_GROUP_W = SSD_INNER // SSD_GROUPS
SSD_CONV_K = 4
SSD_CHUNK = 128
SSD_BC = SSD_GROUPS * SSD_STATE
SSD_CONV_CH = SSD_INNER + 2 * SSD_BC
N_MAIN = 2 * CONV_CH + 2 * SSD_INNER + 2 * SSD_BC
XA_HEADS = 4
XA_HEAD_DIM = 256
MEM_LEN = 256
N_GROUPS = 4
EXPERTS_PER_GROUP = 8
N_EXPERTS = 32
TOP_K = 2
D_EXPERT = 512
MOE_BLOCK = 512
RMS_EPS = 1e-6
LN_EPS = 1e-5
LOG2_E = 1.4426950408889634

LANES = 128
SUBLANES = 8
VMEM_LIMIT = 56 * 1024 * 1024

TOK_TILE = 512
CONV_TILE = 512
ZX_Z0 = -(-SSD_CONV_CH // SSD_INNER) * SSD_INNER
ZX_W = ZX_Z0 + SSD_INNER
ZX_SHARES = (0, 512, 512, 512, 512, 512, 512, 0)
assert sum(ZX_SHARES) == ZX_W and len(ZX_SHARES) == CONV_CH // LANES
CONV_HALO = 32
CONV_ROWS = 512
SSD_TILE = 1024
MOE_TILES = 2
XA_TILES = 2
SSD_UNROLL = 8
SSD_HALO = 8
ROUTE_LANES = 128
SORT_ROWS = TOP_K * TOK_TILE + N_EXPERTS * SUBLANES
DMA_PRIORITIES = 2


def _cparams(sem):
    return pltpu.CompilerParams(dimension_semantics=sem, vmem_limit_bytes=VMEM_LIMIT)


def _rms(x, g):
    return x * lax.rsqrt(jnp.mean(x * x, axis=-1, keepdims=True) + RMS_EPS) * g


def _sigmoid(x):
    return 1.0 / (1.0 + jnp.exp2(x * (-LOG2_E)))


def _interleave(stages):
    results = {}
    while len(results) < len(stages):
        for k, item in enumerate(stages):
            if k not in results:
                try:
                    next(item)
                except StopIteration as done:
                    results[k] = done.value
    return [results[k] for k in range(len(stages))]


def _kv_body(m_ref, g_ref, wk_ref, wv_ref, k_ref, v_ref):
    m = _rms(m_ref[0], g_ref[...]).astype(BF16)
    k_ref[0] = jnp.dot(m, wk_ref[...], preferred_element_type=F32).astype(BF16)
    v_ref[0] = jnp.dot(m, wv_ref[...], preferred_element_type=F32).astype(BF16)


def _kv_proj(mem, g_mem, w_k, w_v):
    b, s, d = mem.shape
    w_spec = pl.BlockSpec((d, d), lambda i: (0, 0))
    kv_spec = pl.BlockSpec((1, s, d), lambda i: (i, 0, 0))
    return pl.pallas_call(
        _kv_body,
        grid=(b,),
        in_specs=[kv_spec, pl.BlockSpec((1, d), lambda i: (0, 0)), w_spec, w_spec],
        out_specs=[kv_spec, kv_spec],
        out_shape=[jax.ShapeDtypeStruct((b, s, d), BF16)] * 2,
        compiler_params=_cparams(("arbitrary",)),
        name="kv_proj",
    )(mem, g_mem.reshape(1, d), w_k.astype(BF16), w_v.astype(BF16))


def _mix_in_body(x_ref, g_ref, w_ref, wdt_ref, cw_ref, cb_ref, lg_ref, lb_ref,
                 u_ref, zx_ref, dt_ref, ubuf_ref, acc_ref):
    @pl.when(pl.program_id(1) == 0)
    def _():
        ubuf_ref[0:CONV_HALO, :] = jnp.zeros((CONV_HALO, CONV_CH), F32)

    h = _rms(x_ref[...], g_ref[...]).astype(BF16)
    dt_ref[...] = jnp.dot(h, wdt_ref[...], preferred_element_type=F32)

    first = CONV_HALO - (CONV_K - 1)
    w_lo = zx_lo = 0
    for cb, zx_w in enumerate(ZX_SHARES):
        cols = slice(cb * LANES, (cb + 1) * LANES)
        blk_w = 2 * LANES + zx_w
        r = jnp.dot(h, w_ref[:, w_lo:w_lo + blk_w], preferred_element_type=F32)
        ubuf_ref[CONV_HALO:CONV_HALO + CONV_TILE, cols] = r[:, :LANES] * _sigmoid(r[:, LANES:2 * LANES])
        if zx_w:
            zx_ref[:, zx_lo:zx_lo + zx_w] = r[:, 2 * LANES:].astype(BF16)
        w_lo += blk_w
        zx_lo += zx_w
        for rc in range(CONV_TILE // CONV_ROWS):
            r0 = rc * CONV_ROWS
            acc = None
            for res in range(SUBLANES):
                part = None
                for k in range(CONV_K):
                    if (first + k) % SUBLANES == res:
                        term = cw_ref[k:k + 1, cols] * ubuf_ref[r0 + first + k:r0 + first + k + CONV_ROWS, cols]
                        part = term if part is None else part + term
                acc = part if acc is None else acc + part
            acc_ref[r0:r0 + CONV_ROWS, cols] = acc

    for rc in range(CONV_TILE // CONV_ROWS):
        rows = slice(rc * CONV_ROWS, (rc + 1) * CONV_ROWS)
        u = acc_ref[rows, :] + cb_ref[...]
        mu = jnp.mean(u, axis=-1, keepdims=True)
        uc = u - mu
        var = jnp.mean(uc * uc, axis=-1, keepdims=True)
        y = uc * lax.rsqrt(var + LN_EPS) * lg_ref[...] + lb_ref[...]
        u_ref[rows, :] = (y * _sigmoid(y)).astype(BF16)

    ubuf_ref[0:CONV_HALO, :] = ubuf_ref[CONV_TILE:CONV_TILE + CONV_HALO, :]


def _mix_in(x2d, bsz, seqlen, g_mix, w_in, conv_w, conv_b, ln_g, ln_b):
    t, d = x2d.shape
    nt = seqlen // CONV_TILE
    w = w_in.astype(BF16)
    z0 = 2 * CONV_CH
    x0 = z0 + SSD_INNER
    w_zx = jnp.concatenate([w[:, x0:N_MAIN], jnp.zeros((d, ZX_Z0 - SSD_CONV_CH), BF16), w[:, z0:x0]], axis=1)
    parts, zx_lo = [], 0
    for cb, zx_w in enumerate(ZX_SHARES):
        parts += [w[:, cb * LANES:(cb + 1) * LANES], w[:, CONV_CH + cb * LANES:CONV_CH + (cb + 1) * LANES],
                  w_zx[:, zx_lo:zx_lo + zx_w]]
        zx_lo += zx_w
    w_blocks = jnp.concatenate(parts, axis=1)
    w_dt = jnp.pad(w_in[:, N_MAIN:], ((0, 0), (0, LANES - SSD_HEADS))).astype(BF16)
    row = lambda v: v.reshape(1, -1)
    const = lambda shape: pl.BlockSpec(shape, lambda b, j: (0, 0))
    tile = lambda width: pl.BlockSpec((CONV_TILE, width), lambda b, j: (b * nt + j, 0))
    return pl.pallas_call(
        _mix_in_body,
        grid=(bsz, nt),
        in_specs=[
            tile(d), const((1, d)), const(w_blocks.shape), const((d, LANES)),
            const((CONV_K + 1, CONV_CH)), const((1, CONV_CH)), const((1, CONV_CH)), const((1, CONV_CH)),
        ],
        out_specs=[tile(CONV_CH), tile(ZX_W), tile(LANES)],
        out_shape=[jax.ShapeDtypeStruct((t, CONV_CH), BF16), jax.ShapeDtypeStruct((t, ZX_W), BF16),
                   jax.ShapeDtypeStruct((t, LANES), F32)],
        scratch_shapes=[
            pltpu.VMEM((CONV_HALO + CONV_TILE, CONV_CH), F32),
            pltpu.VMEM((CONV_TILE, CONV_CH), F32),
        ],
        compiler_params=_cparams(("arbitrary", "arbitrary")),
        name="mix_in",
    )(x2d, row(g_mix), w_blocks, w_dt, jnp.pad(conv_w, ((0, 1), (0, 0))), row(conv_b), row(ln_g), row(ln_b))


def _ssd_body(xbc_ref, z_ref, dt_ref, cw_ref, cb_ref, dtb_ref, alog_ref, dsk_ref, ng_ref,
              expand_ref, y_ref, xbuf_ref, act_ref, dts_ref, state_ref):
    @pl.when(pl.program_id(1) == 0)
    def _():
        xbuf_ref[0:SSD_HALO, :] = jnp.zeros((SSD_HALO, SSD_CONV_CH), F32)
        state_ref[...] = jnp.zeros(state_ref.shape, F32)

    xbuf_ref[SSD_HALO:SSD_HALO + SSD_TILE, :] = xbc_ref[...].astype(F32)
    full = xbuf_ref[...]
    conv = cb_ref[...] + cw_ref[SSD_CONV_K - 1:SSD_CONV_K, :] * full[SSD_HALO:, :]
    for back in range(1, SSD_CONV_K):
        past = pltpu.roll(full, back, 0)[SSD_HALO:, :]
        conv = conv + cw_ref[SSD_CONV_K - 1 - back:SSD_CONV_K - back, :] * past
    act_ref[...] = conv * _sigmoid(conv)
    xbuf_ref[0:SSD_HALO, :] = xbuf_ref[SSD_TILE:SSD_TILE + SSD_HALO, :]

    dt_in = dt_ref[...] + dtb_ref[...]
    dts_ref[...] = jnp.maximum(dt_in, 0.0) + jnp.log1p(jnp.exp(-jnp.abs(dt_in)))

    a_neg = -jnp.exp(alog_ref[...]) * LOG2_E
    q = SSD_CHUNK
    row_i = lax.broadcasted_iota(jnp.int32, (q, q), 0)
    col_i = lax.broadcasted_iota(jnp.int32, (q, q), 1)
    causal = row_i >= col_i
    tril = causal.astype(F32)
    lane_i = lax.broadcasted_iota(jnp.int32, (q, LANES), 1)
    low_half = lane_i < SSD_HEAD_DIM
    expand = expand_ref[...]

    def chunk(c, carry):
        r0 = pl.multiple_of(c * q, q)
        rows = pl.ds(r0, q)
        dtc = dts_ref[rows, :]
        a_cs = jnp.dot(tril, dtc * a_neg, preferred_element_type=F32,
                       precision=lax.Precision.HIGHEST)
        a_cs_t = a_cs.T
        dt_t = dtc.T.astype(BF16)
        a_end = a_cs[q - 1:q, :]
        e_exp = jnp.dot(jnp.exp2(a_cs).astype(BF16), expand, preferred_element_type=F32)
        w_exp = jnp.dot((jnp.exp2(a_end - a_cs) * dtc).astype(BF16), expand,
                        preferred_element_type=F32)
        dec_row = e_exp[q - 1:q, :]
        xc = act_ref[rows, 0:SSD_INNER]
        xw = (xc * w_exp).astype(BF16)
        y_parts = []
        for g in range(SSD_GROUPS):
            b_f = act_ref[rows, SSD_INNER + g * SSD_STATE:SSD_INNER + (g + 1) * SSD_STATE]
            c_f = act_ref[rows, SSD_INNER + SSD_BC + g * SSD_STATE:
                          SSD_INNER + SSD_BC + (g + 1) * SSD_STATE]
            b_g = b_f.astype(BF16)
            c_g = c_f.astype(BF16)
            cb = lax.dot_general(c_g, b_g, (((1,), (1,)), ((), ())),
                                 preferred_element_type=F32).astype(BF16)
            gcols = slice(g * SSD_GROUP_W, (g + 1) * SSD_GROUP_W)
            st = state_ref[g]
            y_off = jnp.dot(c_g, st.astype(BF16), preferred_element_type=F32)
            state_ref[g] = st * dec_row[:, gcols] + jnp.dot(
                b_f.T.astype(BF16), xw[:, gcols], preferred_element_type=F32)
            for pair in range(SSD_GROUP_W // LANES):
                ms = []
                for hh in range(2):
                    h = g * (SSD_HEADS // SSD_GROUPS) + 2 * pair + hh
                    seg = a_cs[:, h:h + 1] - a_cs_t[h:h + 1, :]
                    dec = jnp.exp2(jnp.where(causal, seg, -jnp.inf))
                    ms.append(cb * dec.astype(BF16) * dt_t[h:h + 1, :])
                lhs = jnp.concatenate(ms, axis=1)
                xp = xc[:, g * SSD_GROUP_W + pair * LANES:g * SSD_GROUP_W + (pair + 1) * LANES]
                rhs = jnp.concatenate([jnp.where(low_half, xp, 0.0),
                                       jnp.where(low_half, 0.0, xp)], axis=0).astype(BF16)
                y_diag = jnp.dot(lhs, rhs, preferred_element_type=F32)
                lo = pair * LANES
                y_parts.append(y_diag + y_off[:, lo:lo + LANES]
                               * e_exp[:, g * SSD_GROUP_W + lo:g * SSD_GROUP_W + lo + LANES])
        y = jnp.concatenate(y_parts, axis=1) + xc * dsk_ref[...]
        z = z_ref[rows, :].astype(F32)
        y = y * (z * _sigmoid(z))
        outs = []
        for g in range(SSD_GROUPS):
            yg = y[:, g * SSD_GROUP_W:(g + 1) * SSD_GROUP_W]
            outs.append(yg * lax.rsqrt(jnp.mean(yg * yg, axis=-1, keepdims=True) + RMS_EPS))
        y_ref[rows, :] = (jnp.concatenate(outs, axis=1) * ng_ref[...]).astype(BF16)
        return carry

    lax.fori_loop(0, SSD_TILE // q, chunk, 0, unroll=SSD_UNROLL)


def _ssd(zx, dt_raw, bsz, seqlen, ssd_conv_w, ssd_conv_b, dt_bias, a_log, d_skip, ssd_norm_g):
    nt = seqlen // SSD_TILE
    pad_h = lambda v: jnp.pad(v, (0, LANES - SSD_HEADS)).reshape(1, LANES)
    expand = (jnp.arange(LANES)[:, None] == (jnp.arange(SSD_INNER) // SSD_HEAD_DIM)[None, :]).astype(BF16)
    const = lambda shape: pl.BlockSpec(shape, lambda b, j: (0, 0))
    return pl.pallas_call(
        _ssd_body,
        grid=(bsz, nt),
        in_specs=[
            pl.BlockSpec((SSD_TILE, SSD_CONV_CH), lambda b, j: (b * nt + j, 0)),
            pl.BlockSpec((SSD_TILE, SSD_INNER), lambda b, j: (b * nt + j, ZX_Z0 // SSD_INNER)),
            pl.BlockSpec((SSD_TILE, LANES), lambda b, j: (b * nt + j, 0)),
            const((SSD_CONV_K, SSD_CONV_CH)),
            const((1, SSD_CONV_CH)),
            const((1, LANES)), const((1, LANES)),
            const((1, SSD_INNER)), const((1, SSD_INNER)),
            const((LANES, SSD_INNER)),
        ],
        out_specs=pl.BlockSpec((SSD_TILE, SSD_INNER), lambda b, j: (b * nt + j, 0)),
        out_shape=jax.ShapeDtypeStruct((bsz * seqlen, SSD_INNER), BF16),
        scratch_shapes=[
            pltpu.VMEM((SSD_HALO + SSD_TILE, SSD_CONV_CH), F32),
            pltpu.VMEM((SSD_TILE, SSD_CONV_CH), F32),
            pltpu.VMEM((SSD_TILE, LANES), F32),
            pltpu.VMEM((SSD_GROUPS, SSD_STATE, SSD_GROUP_W), F32),
        ],
        compiler_params=_cparams(("arbitrary", "arbitrary")),
        name="ssd",
    )(zx, zx, dt_raw, ssd_conv_w, ssd_conv_b.reshape(1, SSD_CONV_CH), pad_h(dt_bias),
      pad_h(a_log), jnp.repeat(d_skip, SSD_HEAD_DIM).reshape(1, SSD_INNER),
      ssd_norm_g.reshape(1, SSD_INNER), expand)


def _xattn_body(x_ref, u_ref, y_ref, wu_ref, wy_ref, k_ref, v_ref, gx_ref, wq_ref, wo_ref, gm_ref,
                wr_ref, br_ref, x2_ref, h2_ref, route_ref, stats_ref):
    def tile(s):
        rows = slice(s * TOK_TILE, (s + 1) * TOK_TILE)
        return _xattn_tile(x_ref[rows, :], u_ref[rows, :], y_ref[rows, :], wu_ref, wy_ref, k_ref, v_ref,
                           gx_ref, wq_ref, wo_ref, gm_ref, wr_ref, br_ref)

    results = _interleave([tile(s) for s in range(XA_TILES)])
    for s in range(XA_TILES):
        rows = slice(s * TOK_TILE, (s + 1) * TOK_TILE)
        x2_ref[rows, :], h2_ref[rows, :], route_ref[rows, :], stats_ref[s] = results[s]


def _xattn_tile(x, u, y, wu_ref, wy_ref, k_ref, v_ref, gx_ref, wq_ref, wo_ref, gm_ref, wr_ref, br_ref):
    x = (x + jnp.dot(u, wu_ref[...], preferred_element_type=F32)
         + jnp.dot(y, wy_ref[...], preferred_element_type=F32))
    yield
    h = _rms(x, gx_ref[...]).astype(BF16)
    q = (jnp.dot(h, wq_ref[...], preferred_element_type=F32) * (XA_HEAD_DIM ** -0.5)).astype(BF16)
    yield
    heads = []
    for i in range(XA_HEADS):
        cols = slice(i * XA_HEAD_DIM, (i + 1) * XA_HEAD_DIM)
        s = jnp.dot(q[:, cols], k_ref[0, cols, :], preferred_element_type=F32)
        p = jnp.exp(s - jnp.max(s, axis=-1, keepdims=True))
        p = p / jnp.sum(p, axis=-1, keepdims=True)
        yield
        heads.append(jnp.dot(p.astype(BF16), v_ref[0, :, cols], preferred_element_type=F32))
        yield
    o = jnp.concatenate(heads, axis=1).astype(BF16)
    x2 = x + jnp.dot(o, wo_ref[...], preferred_element_type=F32)
    yield

    h2 = _rms(x2, gm_ref[...])
    h_hi = h2.astype(BF16)
    h_lo = (h2 - h_hi.astype(F32)).astype(BF16)
    both = jnp.dot(h_hi, wr_ref[...], preferred_element_type=F32)
    logits = (both[:, :ROUTE_LANES] + both[:, ROUTE_LANES:]
              + jnp.dot(h_lo, wr_ref[:, :ROUTE_LANES], preferred_element_type=F32)) + br_ref[...]
    yield
    lt = logits.T
    n_t = logits.shape[0]
    neg = -jnp.inf

    def first_max(rows):
        m = functools.reduce(jnp.maximum, rows)
        idx = jnp.full_like(m, len(rows) - 1)
        for j in reversed(range(len(rows) - 1)):
            idx = jnp.where(rows[j] == m, float(j), idx)
        return m, idx

    g_rows = [lt[g:g + 1, :] for g in range(N_GROUPS)]
    g_max, g_sel = first_max(g_rows)
    p_top = 1.0 / functools.reduce(lambda a, b: a + b, [jnp.exp(r - g_max) for r in g_rows])
    e_rows = []
    for j in range(EXPERTS_PER_GROUP):
        r = lt[N_GROUPS + j:N_GROUPS + j + 1, :]
        for g in range(1, N_GROUPS):
            lo = N_GROUPS + g * EXPERTS_PER_GROUP + j
            r = jnp.where(g_sel == float(g), lt[lo:lo + 1, :], r)
        e_rows.append(r)
    yield
    m1, i1 = first_max(e_rows)
    m2, i2 = first_max([jnp.where(i1 == float(j), neg, r) for j, r in enumerate(e_rows)])
    r = jnp.exp(m2 - m1)
    w1 = p_top / (1.0 + r)
    w2 = w1 * r
    e1 = g_sel * EXPERTS_PER_GROUP + i1
    e2 = g_sel * EXPERTS_PER_GROUP + i2
    yield
    expert = lax.broadcasted_iota(jnp.int32, (ROUTE_LANES, n_t), 0).astype(F32)
    oh1 = jnp.where(expert == e1, 1.0, 0.0)
    oh2 = jnp.where(expert == e2, 1.0, 0.0)
    earlier = (lax.broadcasted_iota(jnp.int32, (n_t, n_t), 0)
               < lax.broadcasted_iota(jnp.int32, (n_t, n_t), 1)).astype(BF16)
    c12 = jnp.dot(jnp.concatenate([oh1, oh2], axis=0).astype(BF16), earlier, preferred_element_type=F32)
    yield
    c1 = c12[:ROUTE_LANES, :]
    c2 = c12[ROUTE_LANES:, :]
    tot1 = c1[:, n_t - 1:n_t] + oh1[:, n_t - 1:n_t]
    cnt = tot1 + c2[:, n_t - 1:n_t] + oh2[:, n_t - 1:n_t]
    cnt = jnp.floor((cnt + (SUBLANES - 1)) * (1.0 / SUBLANES)) * SUBLANES
    cnt_b = jnp.broadcast_to(cnt, (ROUTE_LANES, ROUTE_LANES))
    experts_before = (lax.broadcasted_iota(jnp.int32, (ROUTE_LANES, ROUTE_LANES), 1)
                      < lax.broadcasted_iota(jnp.int32, (ROUTE_LANES, ROUTE_LANES), 0)).astype(F32)
    loff_b = jnp.dot(experts_before, cnt_b, preferred_element_type=F32, precision=lax.Precision.HIGHEST)
    loff = loff_b[:, 0:1]
    lp1 = jnp.sum(oh1 * (loff + c1), axis=0, keepdims=True)
    lp2 = jnp.sum(oh2 * (loff + tot1 + c2), axis=0, keepdims=True)
    route_t = jnp.where(expert == 0.0, e1,
              jnp.where(expert == 1.0, e2,
              jnp.where(expert == 2.0, w1,
              jnp.where(expert == 3.0, w2,
              jnp.where(expert == 4.0, lp1, jnp.where(expert == 5.0, lp2, 0.0))))))
    row = lax.broadcasted_iota(jnp.int32, (SUBLANES, ROUTE_LANES), 0)
    stats = jnp.where(row == 0, cnt_b.T[0:1, :], jnp.where(row == 1, loff_b.T[0:1, :], 0.0))
    return x2, h_hi, route_t.T, stats


def _xattn_route(x, u, y, w_out, k, v, bsz, seqlen, g_xattn, w_q, w_o, g_moe, w_rg, b_rg, w_re, b_re):
    t, d = x.shape
    w_mix = w_out.astype(BF16)
    w_r = jnp.pad(jnp.concatenate([w_rg, w_re], axis=1), ((0, 0), (0, ROUTE_LANES - N_GROUPS - N_EXPERTS)))
    b_r = jnp.pad(jnp.concatenate([b_rg, b_re]), (0, ROUTE_LANES - N_GROUPS - N_EXPERTS)).reshape(1, ROUTE_LANES)
    wr_hi = w_r.astype(BF16)
    wr = jnp.concatenate([wr_hi, (w_r - wr_hi.astype(F32)).astype(BF16)], axis=1)
    step_rows = XA_TILES * TOK_TILE
    nt = seqlen // step_rows
    tile = pl.BlockSpec((step_rows, d), lambda b, j: (b * nt + j, 0))
    kv_spec = pl.BlockSpec((1, MEM_LEN, d), lambda b, j: (b, 0, 0))
    const = lambda shape: pl.BlockSpec(shape, lambda b, j: (0, 0))
    return pl.pallas_call(
        _xattn_body,
        grid=(bsz, nt),
        in_specs=[tile, tile, tile, const((d, d)), const((d, d)),
                  pl.BlockSpec((1, d, MEM_LEN), lambda b, j: (b, 0, 0)), kv_spec, const((1, d)),
                  const((d, d)), const((d, d)), const((1, d)),
                  const((d, 2 * ROUTE_LANES)), const((1, ROUTE_LANES))],
        out_specs=[tile, tile, pl.BlockSpec((step_rows, ROUTE_LANES), lambda b, j: (b * nt + j, 0)),
                   pl.BlockSpec((XA_TILES, SUBLANES, ROUTE_LANES), lambda b, j: (b * nt + j, 0, 0))],
        out_shape=[jax.ShapeDtypeStruct((t, d), F32), jax.ShapeDtypeStruct((t, d), BF16),
                   jax.ShapeDtypeStruct((t, ROUTE_LANES), F32),
                   jax.ShapeDtypeStruct((t // TOK_TILE, SUBLANES, ROUTE_LANES), F32)],
        compiler_params=_cparams(("arbitrary", "arbitrary")),
        name="xattn_route",
    )(x, u, y, w_mix[:CONV_CH], w_mix[CONV_CH:], jnp.swapaxes(k, 1, 2), v, g_xattn.reshape(1, d), w_q.astype(BF16), w_o.astype(BF16), g_moe.reshape(1, d),
      wr, b_r)


def _rows_copy(src_ref, src0, dst_ref, dst0, n, sem):
    rows = lambda r0: pl.ds(pl.multiple_of(r0, SUBLANES), pl.multiple_of(n, SUBLANES))
    return pltpu.make_async_copy(src_ref.at[rows(src0), :], dst_ref.at[rows(dst0), :], sem)


def _pack_pairs(v):
    half = v.shape[1] // 2
    lo = lax.bitcast_convert_type(v[:, :half], jnp.uint32)
    hi = lax.bitcast_convert_type(v[:, half:], jnp.uint32)
    return (lo >> 16) | (hi & jnp.uint32(0xFFFF0000))


def _unpack_pairs(w):
    lo = lax.bitcast_convert_type(w << 16, F32)
    hi = lax.bitcast_convert_type(w & jnp.uint32(0xFFFF0000), F32)
    return jnp.concatenate([lo, hi], axis=1).astype(BF16)


def _dispatch_body(cnt_ref, loff_ref, gb_ref, tot_ref, fstart_ref, fcnt_ref, nu_ref, h_ref, route_ref,
                   xs_ref, buf_ref, zero_ref, sem, zsem):
    i = pl.program_id(0)
    nt = pl.num_programs(0)
    slot = i % 2
    buf = buf_ref.at[slot]

    def drain(step, s):
        n = sum(tot_ref[step * MOE_TILES + k] for k in range(MOE_TILES))
        _rows_copy(buf_ref.at[s], 0, xs_ref, 0, n, sem.at[s]).wait()

    @pl.when(i >= 2)
    def _():
        drain(i - 2, slot)

    def sort_tile(k):
        rows = slice(k * TOK_TILE, (k + 1) * TOK_TILE)
        rt = route_ref[rows, :].T
        lp1 = rt[4:5, :].astype(jnp.int32)
        lp2 = rt[5:6, :].astype(jnp.int32)
        r_i = lax.broadcasted_iota(jnp.int32, (SORT_ROWS, TOK_TILE), 0)
        perm = jnp.where((r_i == lp1) | (r_i == lp2), 1.0, 0.0).astype(BF16)
        yield
        buf[k * SORT_ROWS:(k + 1) * SORT_ROWS, :] = _pack_pairs(
            jnp.dot(perm, h_ref[rows, :], preferred_element_type=F32))

    _interleave([sort_tile(k) for k in range(MOE_TILES)])

    for k in range(MOE_TILES):
        def per_expert_pair(p, c, k=k):
            for prio in range(DMA_PRIORITIES):
                j = (i * MOE_TILES + k) * N_EXPERTS + p * DMA_PRIORITIES + prio

                @pl.when(cnt_ref[j] > 0)
                def _(j=j, prio=prio):
                    _rows_copy(buf, k * SORT_ROWS + loff_ref[j], xs_ref, gb_ref[j], cnt_ref[j],
                               sem.at[slot]).start(priority=prio)
            return c

        lax.fori_loop(0, N_EXPERTS // DMA_PRIORITIES, per_expert_pair, 0)

    @pl.when(i == 0)
    def _():
        zero_ref[...] = jnp.zeros(zero_ref.shape, zero_ref.dtype)

        def fill(start):
            def body(e, c):
                @pl.when(fcnt_ref[e] > 0)
                def _():
                    copy = _rows_copy(zero_ref, 0, xs_ref, fstart_ref[e], fcnt_ref[e], zsem)
                    copy.start() if start else copy.wait()
                return c
            lax.fori_loop(0, N_EXPERTS, body, 0)

            def tail(blk, c):
                copy = _rows_copy(zero_ref, 0, xs_ref, blk * MOE_BLOCK, MOE_BLOCK, zsem)
                copy.start() if start else copy.wait()
                return c
            lax.fori_loop(nu_ref[0], xs_ref.shape[0] // MOE_BLOCK, tail, 0)

        fill(True)
        fill(False)

    @pl.when(i == nt - 1)
    def _():
        drain(i, slot)

        @pl.when(i >= 1)
        def _():
            drain(i - 1, 1 - slot)


def _dispatch(h2, route, tables, n_used, n_slots):
    t, d = h2.shape
    step_rows = MOE_TILES * TOK_TILE
    return pl.pallas_call(
        _dispatch_body,
        grid_spec=pltpu.PrefetchScalarGridSpec(
            num_scalar_prefetch=7,
            grid=(t // step_rows,),
            in_specs=[
                pl.BlockSpec((step_rows, d), lambda i, *_: (i, 0)),
                pl.BlockSpec((step_rows, ROUTE_LANES), lambda i, *_: (i, 0)),
            ],
            out_specs=pl.BlockSpec(memory_space=pl.ANY),
            scratch_shapes=[pltpu.VMEM((2, MOE_TILES * SORT_ROWS, d // 2), jnp.uint32),
                            pltpu.VMEM((MOE_BLOCK, d // 2), jnp.uint32),
                            pltpu.SemaphoreType.DMA((2,)), pltpu.SemaphoreType.DMA(())],
        ),
        out_shape=jax.ShapeDtypeStruct((n_slots, d // 2), jnp.uint32),
        compiler_params=_cparams(("arbitrary",)),
        name="dispatch",
    )(tables["cnt"], tables["loff"], tables["gb"], tables["tot"], tables["fill_start"],
      tables["fill_cnt"], n_used, h2, route)


def _experts_body(be_ref, nu_ref, first_ref, slot_ref, nxt_ref, x_ref, wg_hbm, wu_hbm, wd_hbm, y_ref,
                  wg_f, wu_f, wd_f, wg_bf, wu_bf, wd_bf, sem):
    b = pl.program_id(0)
    used = b < nu_ref[0]

    def weights(e, s):
        return [pltpu.make_async_copy(hbm.at[e], buf.at[s], sem.at[s])
                for hbm, buf in ((wg_hbm, wg_f), (wu_hbm, wu_f), (wd_hbm, wd_f))]

    @pl.when(used & (first_ref[b] == 1))
    def _():
        s = slot_ref[b]

        @pl.when(b == 0)
        def _():
            for copy in weights(be_ref[0], 0):
                copy.start()

        for copy in weights(be_ref[b], s):
            copy.wait()
        wg_bf[...] = wg_f[s].astype(BF16)
        wu_bf[...] = wu_f[s].astype(BF16)
        wd_bf[...] = wd_f[s].astype(BF16)

        @pl.when(nxt_ref[b] >= 0)
        def _():
            for copy in weights(nxt_ref[b], 1 - s):
                copy.start()

    @pl.when(used)
    def _():
        x = _unpack_pairs(x_ref[...])
        g = jnp.dot(x, wg_bf[...], preferred_element_type=F32)
        u = jnp.dot(x, wu_bf[...], preferred_element_type=F32)
        a = (g * _sigmoid(g) * u).astype(BF16)
        y = jnp.dot(a, wd_bf[...], preferred_element_type=F32)
        y_ref[...] = _pack_pairs(y.astype(BF16).astype(F32))

    @pl.when(jnp.logical_not(used))
    def _():
        y_ref[...] = jnp.zeros(y_ref.shape, y_ref.dtype)


def _experts(xs, etab, w_gate, w_up, w_down):
    n_slots, dp = xs.shape
    d = w_gate.shape[1]
    n_blocks = n_slots // MOE_BLOCK
    last = lambda b, nu: jnp.maximum(jnp.minimum(b, nu[0] - 1), 0)
    hbm = pl.BlockSpec(memory_space=pl.ANY)
    return pl.pallas_call(
        _experts_body,
        grid_spec=pltpu.PrefetchScalarGridSpec(
            num_scalar_prefetch=5,
            grid=(n_blocks,),
            in_specs=[pl.BlockSpec((MOE_BLOCK, dp), lambda b, be, nu, *_: (last(b, nu), 0)), hbm, hbm, hbm],
            out_specs=pl.BlockSpec((MOE_BLOCK, dp), lambda b, *_: (b, 0)),
            scratch_shapes=[pltpu.VMEM((2, d, D_EXPERT), F32), pltpu.VMEM((2, d, D_EXPERT), F32),
                            pltpu.VMEM((2, D_EXPERT, d), F32),
                            pltpu.VMEM((d, D_EXPERT), BF16), pltpu.VMEM((d, D_EXPERT), BF16),
                            pltpu.VMEM((D_EXPERT, d), BF16), pltpu.SemaphoreType.DMA((2,))],
        ),
        out_shape=jax.ShapeDtypeStruct((n_slots, dp), jnp.uint32),
        compiler_params=_cparams(("arbitrary",)),
        name="experts",
    )(etab["block_e"], etab["n_used"], etab["first"], etab["slot"], etab["nxt"], xs, w_gate, w_up, w_down)


def _combine_body(cnt_ref, loff_ref, gb_ref, tot_ref, x_ref, route_ref, g_ref, ys_ref, o_ref, ybuf_ref, sem):
    i = pl.program_id(0)
    nt = pl.num_programs(0)
    slot = i % 2

    def gather(step, s):
        for k in range(MOE_TILES):
            def per_expert_pair(p, c, k=k):
                for prio in range(DMA_PRIORITIES):
                    j = (step * MOE_TILES + k) * N_EXPERTS + p * DMA_PRIORITIES + prio

                    @pl.when(cnt_ref[j] > 0)
                    def _(j=j, prio=prio):
                        _rows_copy(ys_ref, gb_ref[j], ybuf_ref.at[s], k * SORT_ROWS + loff_ref[j], cnt_ref[j],
                                   sem.at[s]).start(priority=prio)
                return c

            lax.fori_loop(0, N_EXPERTS // DMA_PRIORITIES, per_expert_pair, 0)

    @pl.when(i == 0)
    def _():
        ybuf_ref[...] = jnp.zeros(ybuf_ref.shape, ybuf_ref.dtype)
        gather(0, 0)

    @pl.when(i + 1 < nt)
    def _():
        gather(i + 1, 1 - slot)

    def combine_tile(k):
        rows = slice(k * TOK_TILE, (k + 1) * TOK_TILE)
        route = route_ref[rows, :]
        c_i = lax.broadcasted_iota(jnp.int32, (TOK_TILE, SORT_ROWS), 1)
        lp1 = route[:, 4:5].astype(jnp.int32)
        lp2 = route[:, 5:6].astype(jnp.int32)
        pw = jnp.where(c_i == lp1, route[:, 2:3], jnp.where(c_i == lp2, route[:, 3:4], 0.0)).astype(BF16)
        yield
        r_i = lax.broadcasted_iota(jnp.int32, (SORT_ROWS, 1), 0)
        y = _unpack_pairs(jnp.where(r_i < tot_ref[i * MOE_TILES + k],
                                    ybuf_ref[slot, k * SORT_ROWS:(k + 1) * SORT_ROWS, :], jnp.uint32(0)))
        moe = jnp.dot(pw, y, preferred_element_type=F32)
        yield
        o_ref[rows, :] = _rms(x_ref[rows, :] + moe, g_ref[...])

    tiles = [combine_tile(k) for k in range(MOE_TILES)]
    for tile in tiles:
        next(tile)
    n = sum(tot_ref[i * MOE_TILES + k] for k in range(MOE_TILES))
    _rows_copy(ys_ref, 0, ybuf_ref.at[slot], 0, n, sem.at[slot]).wait()
    _interleave(tiles)


def _combine(x2, route, tables, ys, g_final):
    t, d = x2.shape
    step_rows = MOE_TILES * TOK_TILE
    tile = pl.BlockSpec((step_rows, d), lambda i, *_: (i, 0))
    return pl.pallas_call(
        _combine_body,
        grid_spec=pltpu.PrefetchScalarGridSpec(
            num_scalar_prefetch=4,
            grid=(t // step_rows,),
            in_specs=[
                tile,
                pl.BlockSpec((step_rows, ROUTE_LANES), lambda i, *_: (i, 0)),
                pl.BlockSpec((1, d), lambda i, *_: (0, 0)),
                pl.BlockSpec(memory_space=pl.ANY),
            ],
            out_specs=tile,
            scratch_shapes=[pltpu.VMEM((2, MOE_TILES * SORT_ROWS, d // 2), jnp.uint32),
                            pltpu.SemaphoreType.DMA((2,))],
        ),
        out_shape=jax.ShapeDtypeStruct((t, d), F32),
        compiler_params=_cparams(("arbitrary",)),
        name="combine",
    )(tables["cnt"], tables["loff"], tables["gb"], tables["tot"], x2, route, g_final.reshape(1, d), ys)


def _routing_tables(stats, n_tok):
    cnt = stats[:, 0, :N_EXPERTS].astype(jnp.int32)
    loff = stats[:, 1, :N_EXPERTS].astype(jnp.int32)
    n_tiles = cnt.shape[0]
    counts = jnp.sum(cnt, axis=0)
    padded = ((counts + MOE_BLOCK - 1) // MOE_BLOCK) * MOE_BLOCK
    pad_end = jnp.cumsum(padded)
    pad_start = pad_end - padded
    gb = pad_start[None, :] + jnp.cumsum(cnt, axis=0) - cnt
    max_rows = (n_tok * TOP_K + n_tiles * N_EXPERTS * (SUBLANES - 1)
                + N_EXPERTS * (MOE_BLOCK - SUBLANES))
    n_blocks = -(-max_rows // MOE_BLOCK)
    n_slots = n_blocks * MOE_BLOCK
    block_start = jnp.arange(n_blocks, dtype=jnp.int32) * MOE_BLOCK
    block_e = jnp.minimum(jnp.sum((pad_end[None, :] <= block_start[:, None]).astype(jnp.int32), axis=1),
                          N_EXPERTS - 1)
    tables = dict(cnt=cnt.reshape(-1), loff=loff.reshape(-1), gb=gb.reshape(-1).astype(jnp.int32),
                  tot=jnp.sum(cnt, axis=1).astype(jnp.int32),
                  fill_start=(pad_start + counts).astype(jnp.int32),
                  fill_cnt=(padded - counts).astype(jnp.int32))
    n_used = (pad_end[-1:] // MOE_BLOCK).astype(jnp.int32)
    first = jnp.concatenate([jnp.ones((1,), jnp.int32), (block_e[1:] != block_e[:-1]).astype(jnp.int32)])
    seg_slot = (jnp.cumsum(first) - 1) % 2
    experts = jnp.arange(N_EXPERTS, dtype=jnp.int32)
    later = jnp.where((experts[None, :] > experts[:, None]) & (padded[None, :] > 0), experts[None, :], N_EXPERTS)
    next_e = jnp.min(later, axis=1)
    next_e = jnp.where(next_e < N_EXPERTS, next_e, -1)
    nxt = jnp.sum(jnp.where(block_e[:, None] == experts[None, :], next_e[None, :], 0), axis=1)
    etab = dict(block_e=block_e, n_used=n_used, first=first, slot=seg_slot.astype(jnp.int32),
                nxt=nxt.astype(jnp.int32))
    return tables, etab, n_slots


def kernel(x, mem, g_mix, w_in, conv_w, conv_b, ln_g, ln_b, ssd_conv_w, ssd_conv_b, dt_bias, a_log, d_skip, ssd_norm_g, w_out, g_xattn, g_mem, w_q, w_k, w_v, w_o, g_moe, w_router_group, b_router_group, w_router_expert, b_router_expert, w_gate, w_up, w_down, g_final):
    bsz, seqlen, d = x.shape
    n_tok = bsz * seqlen
    xt = x.reshape(n_tok, d)
    assert g_mix.shape[0] == 1, "the combine kernel applies the final norm: single layer only"
    for l in range(1):
        u, zx, dt_raw = _mix_in(xt, bsz, seqlen, g_mix[l], w_in[l], conv_w[l], conv_b[l],
                                    ln_g[l], ln_b[l])
        y = _ssd(zx, dt_raw, bsz, seqlen, ssd_conv_w[l], ssd_conv_b[l], dt_bias[l], a_log[l],
                 d_skip[l], ssd_norm_g[l])
        k, v = _kv_proj(mem, g_mem[l], w_k[l], w_v[l])
        x2, h2, route, stats = _xattn_route(xt, u, y, w_out[l], k, v, bsz, seqlen, g_xattn[l], w_q[l],
                                            w_o[l], g_moe[l], w_router_group[l], b_router_group[l],
                                            w_router_expert[l], b_router_expert[l])
        tables, etab, n_slots = _routing_tables(stats, n_tok)
        xs = _dispatch(h2, route, tables, etab["n_used"], n_slots)
        ys = _experts(xs, etab, w_gate[l], w_up[l], w_down[l])
        xt = _combine(x2, route, tables, ys, g_final)
    return xt.reshape(bsz, seqlen, d)
```
